```python
import jax, jax.numpy as jnp
from jax import lax
import numpy as np

D_MODEL = 2048
BATCH = 8
SEQ = 4096
DEPTH = 4
DEC_BATCH = 32
DEC_SEQ = 32
PAST_LEN = 1024

CHUNK = 64
N_MIXERS = 2
N_A = (DEPTH + 1) // 2
N_B = DEPTH // 2
A_HEADS = 8
A_DK = 128
A_DV = D_MODEL // A_HEADS
GATE_CAP = 15.0
B_HEADS = 16
B_DH = D_MODEL // B_HEADS
PREV_CHUNKS = 8
REACH = PREV_CHUNKS * CHUNK
BAND = REACH + CHUNK
REL_CLIP = 256
N_REL = 2 * REL_CLIP + 1
N_GROUPS = 4
EXP_PER_GROUP = 4
N_EXPERTS = N_GROUPS * EXP_PER_GROUP
TOP_K = 2
D_EXPERT = D_MODEL // 4
ALPHA = (2 * DEPTH) ** 0.25
BETA = (8 * DEPTH) ** -0.25
LN_EPS = 1e-5
RMS_EPS = 1e-6

kernel_name = "hybrid_mlstm_chunkattn_hmoe_step"


def layer_norm(x, g, b):
    xf = x.astype(jnp.float32)
    mu = jnp.mean(xf, -1, keepdims=True)
    var = jnp.mean(jnp.square(xf - mu), -1, keepdims=True)
    return ((xf - mu) * lax.rsqrt(var + LN_EPS) * g.astype(jnp.float32) + b.astype(jnp.float32)).astype(x.dtype)


def mlstm_block(carry, blk):
    C, n, m = carry
    q, k, v, ig, lf = blk
    L = q.shape[1]
    bt = jnp.transpose(jnp.cumsum(lf, axis=1), (0, 2, 1))
    igt = jnp.transpose(ig, (0, 2, 1))
    causal = jnp.tril(jnp.ones((L, L), dtype=bool))
    log_d = jnp.where(causal, bt[..., :, None] - bt[..., None, :] + igt[..., None, :], -jnp.inf)
    log_inter = bt + m[..., None]
    m_t = jnp.maximum(log_inter, jnp.max(log_d, -1))
    d_mat = jnp.exp(log_d - m_t[..., None])
    inter = jnp.exp(log_inter - m_t)
    s = jnp.einsum('blhk,bshk->bhls', q, k) * d_mat
    inter_l = jnp.transpose(inter, (0, 2, 1))[..., None]
    num = jnp.einsum('bhls,bshv->blhv', s, v) + jnp.einsum('blhk,bhkv->blhv', q, C) * inter_l
    den = jnp.sum(s, -1) + inter * jnp.einsum('blhk,bhk->bhl', q, n)
    den = jnp.maximum(jnp.abs(den), jnp.exp(-m_t))
    h = num / jnp.transpose(den, (0, 2, 1))[..., None]
    m_new = m_t[..., -1]
    w = jnp.exp(bt[..., -1:] - bt + igt - m_new[..., None])
    decay = jnp.exp(bt[..., -1] + m - m_new)
    wk = k * jnp.transpose(w, (0, 2, 1))[..., None]
    C_new = decay[..., None, None] * C + jnp.einsum('bshk,bshv->bhkv', wk, v)
    n_new = decay[..., None] * n + jnp.sum(wk, axis=1)
    return (C_new, n_new, m_new), h


def mlstm_mixer(x, C0, n0, m0, w_in, b_gate, mh_gain, w_out):
    B, T, _ = x.shape
    hk, hv = A_HEADS * A_DK, A_HEADS * A_DV
    proj = x @ w_in
    q, k, v, o, g = jnp.split(proj, [hk, 2 * hk, 2 * hk + hv, 2 * hk + 2 * hv], axis=-1)
    f32 = jnp.float32
    q = q.reshape(B, T, A_HEADS, A_DK).astype(f32)
    k = k.reshape(B, T, A_HEADS, A_DK).astype(f32) * (A_DK ** -0.5)
    v = v.reshape(B, T, A_HEADS, A_DV).astype(f32)
    g = GATE_CAP * jnp.tanh((g.astype(f32) + b_gate.astype(f32)) / GATE_CAP)
    ig = g[..., :A_HEADS]
    lf = jax.nn.log_sigmoid(g[..., A_HEADS:])
    L = min(CHUNK, T)
    nblk = T // L
    blocks = lambda a: jnp.swapaxes(a.reshape(B, nblk, L, *a.shape[2:]), 0, 1)
    carry0 = (C0.astype(f32), n0.astype(f32), m0.astype(f32))
    (C, n, m), h = lax.scan(mlstm_block, carry0, (blocks(q), blocks(k), blocks(v), blocks(ig), blocks(lf)))
    h = jnp.swapaxes(h, 0, 1).reshape(B, T, A_HEADS, A_DV)
    h = h * lax.rsqrt(jnp.mean(jnp.square(h), -1, keepdims=True) + RMS_EPS)
    h = h.reshape(B, T, hv) * mh_gain.astype(f32) * jax.nn.sigmoid(o.astype(f32))
    y = h.astype(x.dtype) @ w_out
    return y, C.astype(x.dtype), n.astype(x.dtype), m.astype(x.dtype)


def band_attend(q, k, v, q_pos, k_pos, rel_table):
    qc = q_pos // CHUNK
    kc = k_pos // CHUNK
    allowed = (k_pos[None, :] >= 0) & (kc[None, :] <= qc[:, None]) & (kc[None, :] >= qc[:, None] - PREV_CHUNKS)
    rel = jnp.clip(q_pos[:, None] - k_pos[None, :], -REL_CLIP, REL_CLIP) + REL_CLIP
    bias = rel_table[:, rel].astype(jnp.float32)
    s = jnp.einsum('bqhd,bkhd->bhqk', q, k).astype(jnp.float32) * (B_DH ** -0.5) + bias
    p = jax.nn.softmax(jnp.where(allowed, s, -jnp.inf), axis=-1)
    return jnp.einsum('bhqk,bkhd->bqhd', p.astype(v.dtype), v)


def attn_qkv(x, w_in):
    B, T, _ = x.shape
    qkv = (x @ w_in).reshape(B, T, 3, B_HEADS, B_DH)
    return qkv[:, :, 0], qkv[:, :, 1], qkv[:, :, 2]


def chunk_attn_prompt(x, w_in, rel_table, w_out):
    B, T, _ = x.shape
    q, k, v = attn_qkv(x, w_in)
    n_chunks = T // CHUNK
    pad = ((0, 0), (REACH, 0), (0, 0), (0, 0))
    kp = jnp.pad(k, pad)
    vp = jnp.pad(v, pad)
    q_chunks = jnp.swapaxes(q.reshape(B, n_chunks, CHUNK, B_HEADS, B_DH), 0, 1)

    def one_chunk(args):
        c, qc = args
        start = c * CHUNK
        kb = lax.dynamic_slice_in_dim(kp, start, BAND, axis=1)
        vb = lax.dynamic_slice_in_dim(vp, start, BAND, axis=1)
        q_pos = start + jnp.arange(CHUNK)
        k_pos = start - REACH + jnp.arange(BAND)
        return band_attend(qc, kb, vb, q_pos, k_pos, rel_table)

    o = lax.map(one_chunk, (jnp.arange(n_chunks), q_chunks))
    o = jnp.swapaxes(o, 0, 1).reshape(B, T, B_HEADS * B_DH)
    rows = min(REACH, T)
    return o @ w_out, k[:, T - rows:], v[:, T - rows:]


def chunk_attn_sample(x, cache_k, cache_v, w_in, rel_table, w_out):
    B, T, _ = x.shape
    q, k, v = attn_qkv(x, w_in)
    rows = cache_k.shape[1]
    kk = jnp.concatenate([cache_k.astype(k.dtype), k], axis=1)
    vv = jnp.concatenate([cache_v.astype(v.dtype), v], axis=1)
    q_pos = PAST_LEN + jnp.arange(T)
    k_pos = PAST_LEN - rows + jnp.arange(rows + T)
    o = band_attend(q, kk, vv, q_pos, k_pos, rel_table).reshape(B, T, B_HEADS * B_DH)
    return o @ w_out, k, v


def hier_moe(x, w_group, b_group, w_expert_r, b_expert_r, w_gate, w_up, w_down):
    shp = x.shape
    xf = x.reshape(-1, D_MODEL)
    g_logits = (xf @ w_group).astype(jnp.float32) + b_group.astype(jnp.float32)
    g_prob = jax.nn.softmax(g_logits, axis=-1)
    g_top, g_sel = lax.top_k(g_logits, 1)
    g_w = jnp.take_along_axis(g_prob, g_sel, axis=-1)
    e_logits = ((xf @ w_expert_r).astype(jnp.float32) + b_expert_r.astype(jnp.float32)).reshape(-1, N_GROUPS, EXP_PER_GROUP)
    e_in = jnp.take_along_axis(e_logits, g_sel[:, :, None], axis=1)[:, 0]
    e_top, e_sel = lax.top_k(e_in, TOP_K)
    e_w = jax.nn.softmax(e_top, axis=-1) * g_w
    expert_id = g_sel * EXP_PER_GROUP + e_sel
    gates = jnp.sum(jax.nn.one_hot(expert_id, N_EXPERTS, dtype=jnp.float32) * e_w[..., None], axis=1)
    y = jnp.zeros(xf.shape, jnp.float32)
    for e in range(N_EXPERTS):
        h = jax.nn.silu(xf @ w_gate[e]) * (xf @ w_up[e])
        y = y + gates[:, e:e + 1] * (h @ w_down[e]).astype(jnp.float32)
    return y.astype(x.dtype).reshape(shp)


def setup_inputs(seed: int = 0) -> dict:
    key = jax.random.key(seed)
    ks = iter(jax.random.split(key, 40))
    nrm = lambda shape, scale: scale * jax.random.normal(next(ks), shape, jnp.float32)
    d_in = D_MODEL ** -0.5
    kv_rows = min(REACH, PAST_LEN)
    hk, hv = A_HEADS * A_DK, A_HEADS * A_DV
    x_prompt = nrm((BATCH, SEQ, D_MODEL), 1.0)
    x_sample = nrm((DEC_BATCH, DEC_SEQ, D_MODEL), 1.0)
    state_C = nrm((N_A, DEC_BATCH, A_HEADS, A_DK, A_DV), 0.1)
    state_n = nrm((N_A, DEC_BATCH, A_HEADS, A_DK), 0.3)
    state_m = nrm((N_A, DEC_BATCH, A_HEADS), 1.0)
    cache_k = nrm((N_B, DEC_BATCH, kv_rows, B_HEADS, B_DH), 1.0)
    cache_v = nrm((N_B, DEC_BATCH, kv_rows, B_HEADS, B_DH), 0.5)
    a_w_in = jnp.concatenate([
        nrm((N_A, D_MODEL, hk), d_in),
        nrm((N_A, D_MODEL, hk), d_in),
        nrm((N_A, D_MODEL, hv), d_in * BETA),
        nrm((N_A, D_MODEL, hv), d_in),
        nrm((N_A, D_MODEL, 2 * A_HEADS), d_in),
    ], axis=-1)
    a_b_gate = jnp.concatenate([nrm((N_A, A_HEADS), 0.5), 3.0 + nrm((N_A, A_HEADS), 0.5)], axis=-1)
    a_norm = 1.0 + nrm((N_A, hv), 0.02)
    a_w_out = nrm((N_A, hv, D_MODEL), (hv ** -0.5) * BETA)
    b_w_in = jnp.concatenate([
        nrm((N_B, D_MODEL, D_MODEL), d_in),
        nrm((N_B, D_MODEL, D_MODEL), d_in),
        nrm((N_B, D_MODEL, D_MODEL), d_in * BETA),
    ], axis=-1)
    b_rel = nrm((N_B, B_HEADS, N_REL), 0.5)
    b_w_out = nrm((N_B, D_MODEL, D_MODEL), d_in * BETA)
    ln1_g = 1.0 + nrm((DEPTH, D_MODEL), 0.02)
    ln1_b = nrm((DEPTH, D_MODEL), 0.02)
    ln2_g = 1.0 + nrm((DEPTH, D_MODEL), 0.02)
    ln2_b = nrm((DEPTH, D_MODEL), 0.02)
    r_w_group = nrm((DEPTH, D_MODEL, N_GROUPS), d_in)
    r_b_group = nrm((DEPTH, N_GROUPS), 0.01)
    r_w_expert = nrm((DEPTH, D_MODEL, N_EXPERTS), d_in)
    r_b_expert = nrm((DEPTH, N_EXPERTS), 0.01)
    e_w_gate = nrm((DEPTH, N_EXPERTS, D_MODEL, D_EXPERT), d_in)
    e_w_up = nrm((DEPTH, N_EXPERTS, D_MODEL, D_EXPERT), d_in)
    e_w_down = nrm((DEPTH, N_EXPERTS, D_EXPERT, D_MODEL), (D_EXPERT ** -0.5) * BETA)
    return {"x_prompt": x_prompt, "x_sample": x_sample,
            "state_C": state_C, "state_n": state_n, "state_m": state_m,
            "cache_k": cache_k, "cache_v": cache_v,
            "a_w_in": a_w_in, "a_b_gate": a_b_gate, "a_norm": a_norm, "a_w_out": a_w_out,
            "b_w_in": b_w_in, "b_rel": b_rel, "b_w_out": b_w_out,
            "ln1_g": ln1_g, "ln1_b": ln1_b, "ln2_g": ln2_g, "ln2_b": ln2_b,
            "r_w_group": r_w_group, "r_b_group": r_b_group, "r_w_expert": r_w_expert, "r_b_expert": r_b_expert,
            "e_w_gate": e_w_gate, "e_w_up": e_w_up, "e_w_down": e_w_down}


def reference(x_prompt, x_sample, state_C, state_n, state_m, cache_k, cache_v,
              a_w_in, a_b_gate, a_norm, a_w_out, b_w_in, b_rel, b_w_out,
              ln1_g, ln1_b, ln2_g, ln2_b, r_w_group, r_b_group, r_w_expert, r_b_expert,
              e_w_gate, e_w_up, e_w_down):
    xp, xs = x_prompt, x_sample
    bp = xp.shape[0]
    Cp_l, np_l, mp_l, kp_l, vp_l = [], [], [], [], []
    Cs_l, ns_l, ms_l, ks_l, vs_l = [], [], [], [], []
    for layer in range(DEPTH):
        j = layer // N_MIXERS
        if layer % N_MIXERS == 0:
            zC = jnp.zeros((bp, A_HEADS, A_DK, A_DV), jnp.float32)
            zn = jnp.zeros((bp, A_HEADS, A_DK), jnp.float32)
            zm = jnp.zeros((bp, A_HEADS), jnp.float32)
            yp, Cp, n_p, mp = mlstm_mixer(xp, zC, zn, zm, a_w_in[j], a_b_gate[j], a_norm[j], a_w_out[j])
            ys, Cs, n_s, ms = mlstm_mixer(xs, state_C[j], state_n[j], state_m[j], a_w_in[j], a_b_gate[j], a_norm[j], a_w_out[j])
            Cp_l.append(Cp); np_l.append(n_p); mp_l.append(mp)
            Cs_l.append(Cs); ns_l.append(n_s); ms_l.append(ms)
        else:
            yp, kp, vp = chunk_attn_prompt(xp, b_w_in[j], b_rel[j], b_w_out[j])
            ys, ks_, vs_ = chunk_attn_sample(xs, cache_k[j], cache_v[j], b_w_in[j], b_rel[j], b_w_out[j])
            kp_l.append(kp); vp_l.append(vp)
            ks_l.append(ks_); vs_l.append(vs_)
        xp = layer_norm(ALPHA * xp + yp, ln1_g[layer], ln1_b[layer])
        xs = layer_norm(ALPHA * xs + ys, ln1_g[layer], ln1_b[layer])
        moe_args = (r_w_group[layer], r_b_group[layer], r_w_expert[layer], r_b_expert[layer],
                    e_w_gate[layer], e_w_up[layer], e_w_down[layer])
        xp = layer_norm(ALPHA * xp + hier_moe(xp, *moe_args), ln2_g[layer], ln2_b[layer])
        xs = layer_norm(ALPHA * xs + hier_moe(xs, *moe_args), ln2_g[layer], ln2_b[layer])
    C_p = jnp.stack(Cp_l); n_p_all = jnp.stack(np_l); m_p = jnp.stack(mp_l)
    k_p = jnp.stack(kp_l); v_p = jnp.stack(vp_l)
    C_s = jnp.stack(Cs_l); n_s_all = jnp.stack(ns_l); m_s = jnp.stack(ms_l)
    k_s = jnp.stack(ks_l); v_s = jnp.stack(vs_l)
    return (xp, xs, C_p, n_p_all, m_p, k_p, v_p, C_s, n_s_all, m_s, k_s, v_s)
```

```python
import functools

import jax
import jax.numpy as jnp
from jax import lax
from jax.experimental import pallas as pl
from jax.experimental.pallas import tpu as pltpu
from jax.experimental.pallas import tpu_sc as plsc

F32 = jnp.float32
BF16 = jnp.bfloat16

D_MODEL = 2048
BATCH = 8
SEQ = 4096
DEPTH = 4
DEC_BATCH = 32
DEC_SEQ = 32
PAST_LEN = 1024
CHUNK = 64
A_HEADS = 8
A_DK = 128
A_DV = D_MODEL // A_HEADS
GATE_CAP = 15.0
B_HEADS = 16
B_DH = D_MODEL // B_HEADS
PREV_CHUNKS = 8
REACH = PREV_CHUNKS * CHUNK
REL_CLIP = 256
N_GROUPS = 4
EXP_PER_GROUP = 4
N_EXPERTS = N_GROUPS * EXP_PER_GROUP
D_EXPERT = D_MODEL // 4
ALPHA = (2 * DEPTH) ** 0.25
LN_EPS = 1e-5
RMS_EPS = 1e-6

N_PROMPT = BATCH * SEQ
N_SAMPLE = DEC_BATCH * DEC_SEQ
N_TOK = N_PROMPT + N_SAMPLE

VMEM_LIMIT = 56 * 1024 * 1024
LANES = 128

PAIRS = ((0, 1), (0, 2), (0, 3), (1, 2), (1, 3), (2, 3))
N_CLASSES = N_GROUPS * len(PAIRS)
MOE_TILE = 256
MOE_TILES = -(-(N_TOK // MOE_TILE + N_CLASSES) // 32) * 32
MOE_ROWS = MOE_TILES * MOE_TILE

MLSTM_CHUNK = 128
ATT_TQ = 256
ATT_TK = ATT_TQ + REACH
NEG = -1e30

SC_WINDOW = 128
SC_CHUNK = 256


def _params(*sem):
    return pltpu.CompilerParams(dimension_semantics=sem, vmem_limit_bytes=VMEM_LIMIT)


def _mm_body(x_ref, w_ref, o_ref):
    o_ref[...] = jnp.dot(x_ref[...], w_ref[...], preferred_element_type=F32).astype(o_ref.dtype)


def matmul(x, w, out_dtype, tm=512, tn=1024):
    n, k = x.shape
    m = w.shape[1]
    return pl.pallas_call(
        _mm_body,
        grid=(m // tn, n // tm),
        in_specs=[pl.BlockSpec((tm, k), lambda j, i: (i, 0)),
                  pl.BlockSpec((k, tn), lambda j, i: (0, j))],
        out_specs=pl.BlockSpec((tm, tn), lambda j, i: (i, j)),
        out_shape=jax.ShapeDtypeStruct((n, m), out_dtype),
        compiler_params=_params("parallel", "parallel"),
        name="matmul",
    )(x, w)


def _split2(x):
    hi = x.astype(BF16)
    lo = (x - hi.astype(F32)).astype(BF16)
    return hi, lo


def _dot3(xh, xl, wh, wl, dims):
    dg = functools.partial(lax.dot_general, dimension_numbers=dims, preferred_element_type=F32)
    return dg(xh, wh) + (dg(xh, wl) + dg(xl, wh))


_NN = (((1,), (0,)), ((), ()))
_NT = (((1,), (1,)), ((), ()))
_TN = (((0,), (0,)), ((), ()))


def _gate_body(x_ref, w_ref, wt_ref, g_ref, gt_ref):
    xh, xl = _split2(x_ref[...])
    wh, wl = _split2(w_ref[...])
    wth, wtl = _split2(wt_ref[...])
    g_ref[...] = _dot3(xh, xl, wh, wl, _NN)
    gt_ref[...] = _dot3(wth, wtl, xh, xl, _NT)


def gate_preacts(x, w_gate, tm=512):
    n = x.shape[0]
    w = jnp.pad(w_gate, ((0, 0), (0, LANES - w_gate.shape[1])))
    return pl.pallas_call(
        _gate_body,
        grid=(n // tm,),
        in_specs=[pl.BlockSpec((tm, D_MODEL), lambda i: (i, 0)),
                  pl.BlockSpec((D_MODEL, LANES), lambda i: (0, 0)),
                  pl.BlockSpec((LANES, D_MODEL), lambda i: (0, 0))],
        out_specs=[pl.BlockSpec((tm, LANES), lambda i: (i, 0)),
                   pl.BlockSpec((LANES, tm), lambda i: (0, i))],
        out_shape=[jax.ShapeDtypeStruct((n, LANES), F32),
                   jax.ShapeDtypeStruct((LANES, n), F32)],
        compiler_params=_params("parallel"),
        name="gate_preacts",
    )(x, w, w.T)


def _log_sigmoid(x):
    return jnp.minimum(x, 0.0) - jnp.log(1.0 + jnp.exp(-jnp.abs(x)))


def _split3(x):
    a = x.astype(BF16)
    r = x - a.astype(F32)
    b = r.astype(BF16)
    c = (r - b.astype(F32)).astype(BF16)
    return a, b, c


def _mlstm_body(q_ref, k_ref, v_ref, o_ref, g_ref, gt_ref, bc_ref, br_ref, gain_ref,
                c0_ref, n0_ref, m0_ref, h_ref, c_ref, n_ref, m_ref, *, L):
    step = pl.program_id(1)

    @pl.when(step == 0)
    def _():
        c_ref[...] = c0_ref[...]
        n_ref[...] = n0_ref[...]
        m_ref[...] = m0_ref[...]

    row = lax.broadcasted_iota(jnp.int32, (L, L), 0)
    col = lax.broadcasted_iota(jnp.int32, (L, L), 1)
    causal = col <= row
    tril = jnp.where(causal, 1.0, 0.0).astype(BF16)
    triu = jnp.where(row <= col, 1.0, 0.0).astype(BF16)

    pre_c = GATE_CAP * jnp.tanh((g_ref[...] + bc_ref[...]) * (1.0 / GATE_CAP))
    lf_c = _log_sigmoid(pre_c)
    a, b, c = _split3(lf_c)
    dot = functools.partial(jnp.dot, preferred_element_type=F32)
    bt_c = dot(tril, a) + (dot(tril, b) + dot(tril, c))
    pre_r = GATE_CAP * jnp.tanh((gt_ref[...] + br_ref[...]) * (1.0 / GATE_CAP))
    lf_r = _log_sigmoid(pre_r)
    a, b, c = _split3(lf_r)
    bt_r = dot(a, triu) + (dot(b, triu) + dot(c, triu))

    scale = A_DK ** -0.5
    for h in range(A_HEADS):
        btc = bt_c[:, A_HEADS + h:A_HEADS + h + 1]
        igc = pre_c[:, h:h + 1]
        btr = bt_r[A_HEADS + h:A_HEADS + h + 1, :]
        igr = pre_r[h:h + 1, :]
        m_prev = m_ref[0, h:h + 1, 0:1]
        log_d = jnp.where(causal, btc - btr + igr, -jnp.inf)
        log_inter = btc + m_prev
        m_t = jnp.maximum(log_inter, jnp.max(log_d, axis=-1, keepdims=True))
        d_mat = jnp.exp(log_d - m_t)
        inter = jnp.exp(log_inter - m_t)
        qh = q_ref[:, h * A_DK:(h + 1) * A_DK]
        kh = k_ref[:, h * A_DK:(h + 1) * A_DK]
        vh = v_ref[:, h * A_DV:(h + 1) * A_DV]
        s = lax.dot_general(qh, kh, _NT, preferred_element_type=F32) * (d_mat * scale)
        c_h = c_ref[0, h]
        n_h = n_ref[0, h:h + 1, :]
        num = dot(s.astype(BF16), vh) + dot(qh, c_h.astype(BF16)) * inter
        qn = jnp.sum(qh.astype(F32) * n_h, axis=-1, keepdims=True)
        den = jnp.sum(s, axis=-1, keepdims=True) + inter * qn
        den = jnp.maximum(jnp.abs(den), jnp.exp(-m_t))
        hh = num / den
        hh = hh * lax.rsqrt(jnp.mean(hh * hh, axis=-1, keepdims=True) + RMS_EPS)
        og = o_ref[:, h * A_DV:(h + 1) * A_DV].astype(F32)
        hh = hh * gain_ref[:, h * A_DV:(h + 1) * A_DV] * jax.nn.sigmoid(og)
        h_ref[:, h * A_DV:(h + 1) * A_DV] = hh.astype(h_ref.dtype)

        m_new = m_t[L - 1:L, :]
        bt_last = btc[L - 1:L, :]
        w_c = jnp.exp(bt_last - btc + igc - m_new)
        decay = jnp.exp(bt_last + m_prev - m_new)
        wk = kh.astype(F32) * (w_c * scale)
        c_ref[0, h] = decay * c_h + lax.dot_general(wk.astype(BF16), vh, _TN, preferred_element_type=F32)
        n_ref[0, h:h + 1, :] = decay * n_h + jnp.sum(wk, axis=0, keepdims=True)
        m_ref[0, h:h + 1, :] = jnp.broadcast_to(m_new, (1, LANES))


def mlstm(proj, g, gt, b_gate, gain, c0, n0, m0, *, batch, seq, L):
    nc = seq // L
    rows = batch * seq
    bias = jnp.pad(b_gate.astype(F32), (0, LANES - 2 * A_HEADS))
    row_blk = lambda b, c: b * nc + c
    gt16 = gt[:2 * A_HEADS]
    if L % LANES:
        gt16 = gt16.reshape(2 * A_HEADS, batch, L).transpose(1, 0, 2)
        gt_spec = pl.BlockSpec((None, 2 * A_HEADS, L), lambda b, c: (b, 0, 0))
    else:
        gt_spec = pl.BlockSpec((2 * A_HEADS, L), lambda b, c: (0, row_blk(b, c)))
    st = lambda *s: pl.BlockSpec((1,) + s, lambda b, c: (b,) + (0,) * len(s))
    return pl.pallas_call(
        functools.partial(_mlstm_body, L=L),
        grid=(batch, nc),
        in_specs=[pl.BlockSpec((L, A_HEADS * A_DK), lambda b, c: (row_blk(b, c), 0)),
                  pl.BlockSpec((L, A_HEADS * A_DK), lambda b, c: (row_blk(b, c), 1)),
                  pl.BlockSpec((L, D_MODEL), lambda b, c: (row_blk(b, c), 1)),
                  pl.BlockSpec((L, D_MODEL), lambda b, c: (row_blk(b, c), 2)),
                  pl.BlockSpec((L, LANES), lambda b, c: (row_blk(b, c), 0)),
                  gt_spec,
                  pl.BlockSpec((1, LANES), lambda b, c: (0, 0)),
                  pl.BlockSpec((2 * A_HEADS, 1), lambda b, c: (0, 0)),
                  pl.BlockSpec((1, D_MODEL), lambda b, c: (0, 0)),
                  st(A_HEADS, A_DK, A_DV), st(A_HEADS, A_DK), st(A_HEADS, LANES)],
        out_specs=[pl.BlockSpec((L, D_MODEL), lambda b, c: (row_blk(b, c), 0)),
                   st(A_HEADS, A_DK, A_DV), st(A_HEADS, A_DK), st(A_HEADS, LANES)],
        out_shape=[jax.ShapeDtypeStruct((rows, D_MODEL), BF16),
                   jax.ShapeDtypeStruct((batch, A_HEADS, A_DK, A_DV), F32),
                   jax.ShapeDtypeStruct((batch, A_HEADS, A_DK), F32),
                   jax.ShapeDtypeStruct((batch, A_HEADS, LANES), F32)],
        compiler_params=_params("parallel", "arbitrary"),
        name="mlstm",
    )(proj, proj, proj, proj, g, gt16, bias[None, :], bias[:2 * A_HEADS, None], gain.astype(F32)[None, :],
      c0, n0, jnp.broadcast_to(m0[..., None], m0.shape + (LANES,)))


def _softmax_pv(s, v):
    m = jnp.max(s, axis=-1, keepdims=True)
    p = jnp.exp(s - m)
    l = jnp.sum(p, axis=-1, keepdims=True)
    return jnp.dot(p.astype(BF16), v, preferred_element_type=F32) / l


def _attn_prompt_body(q_ref, k_ref, v_ref, bias_ref, o_ref, kpad, vpad):
    kpad[0:REACH, :] = jnp.zeros((REACH, B_DH), BF16)
    vpad[0:REACH, :] = jnp.zeros((REACH, B_DH), BF16)
    kpad[REACH:, :] = k_ref[...]
    vpad[REACH:, :] = v_ref[...]
    scale = B_DH ** -0.5

    def tile(t, mask_front):
        r0 = t * ATT_TQ if isinstance(t, int) else pl.multiple_of(t * ATT_TQ, ATT_TQ)
        q = q_ref[pl.ds(r0, ATT_TQ), :]
        kb = kpad[pl.ds(r0, ATT_TK), :]
        vb = vpad[pl.ds(r0, ATT_TK), :]
        s = lax.dot_general(q, kb, _NT, preferred_element_type=F32) * scale + bias_ref[0]
        if mask_front:
            j = lax.broadcasted_iota(jnp.int32, (ATT_TQ, ATT_TK), 1)
            s = jnp.where(j + r0 >= REACH, s, NEG)
        o_ref[pl.ds(r0, ATT_TQ), :] = _softmax_pv(s, vb).astype(o_ref.dtype)

    n_front = REACH // ATT_TQ
    for t in range(n_front):
        tile(t, True)

    def loop(t, carry):
        tile(t, False)
        return carry

    lax.fori_loop(n_front, SEQ // ATT_TQ, loop, 0)


def band_bias(rel_table, nq, nk):
    i = jnp.arange(nq)[:, None]
    j = jnp.arange(nk)[None, :]
    rel = jnp.clip(REACH + i - j, -REL_CLIP, REL_CLIP) + REL_CLIP
    dc = j // CHUNK - i // CHUNK
    allowed = (dc >= 0) & (dc <= PREV_CHUNKS)
    return jnp.where(allowed[None], rel_table.astype(F32)[:, rel], NEG)


def attn_prompt(qkv, rel_table):
    bias = band_bias(rel_table, ATT_TQ, ATT_TK)
    return pl.pallas_call(
        _attn_prompt_body,
        grid=(BATCH, B_HEADS),
        in_specs=[pl.BlockSpec((SEQ, B_DH), lambda b, h: (b, h)),
                  pl.BlockSpec((SEQ, B_DH), lambda b, h: (b, B_HEADS + h)),
                  pl.BlockSpec((SEQ, B_DH), lambda b, h: (b, 2 * B_HEADS + h)),
                  pl.BlockSpec((1, ATT_TQ, ATT_TK), lambda b, h: (h, 0, 0))],
        out_specs=pl.BlockSpec((SEQ, B_DH), lambda b, h: (b, h)),
        out_shape=jax.ShapeDtypeStruct((N_PROMPT, D_MODEL), BF16),
        scratch_shapes=[pltpu.VMEM((SEQ + REACH, B_DH), BF16),
                        pltpu.VMEM((SEQ + REACH, B_DH), BF16)],
        compiler_params=_params("parallel", "parallel"),
        name="attn_prompt",
    )(qkv, qkv, qkv, bias)


SAMPLE_KEYS = REACH + DEC_SEQ
SAMPLE_KEYS_PAD = -(-SAMPLE_KEYS // LANES) * LANES


def _attn_sample_body(q_ref, k_ref, v_ref, bias_ref, o_ref):
    scale = B_DH ** -0.5
    for h in range(B_HEADS):
        sl = slice(h * B_DH, (h + 1) * B_DH)
        s = lax.dot_general(q_ref[:, sl], k_ref[0, :, sl], _NT, preferred_element_type=F32) * scale + bias_ref[h]
        o_ref[:, sl] = _softmax_pv(s, v_ref[0, :, sl]).astype(o_ref.dtype)


def attn_sample(q, kk, vv, rel_table):
    bias = band_bias(rel_table, DEC_SEQ, SAMPLE_KEYS_PAD)
    bias = jnp.where(jnp.arange(SAMPLE_KEYS_PAD)[None, None, :] < SAMPLE_KEYS, bias, NEG)
    return pl.pallas_call(
        _attn_sample_body,
        grid=(DEC_BATCH,),
        in_specs=[pl.BlockSpec((DEC_SEQ, D_MODEL), lambda b: (b, 0)),
                  pl.BlockSpec((1, SAMPLE_KEYS_PAD, D_MODEL), lambda b: (b, 0, 0)),
                  pl.BlockSpec((1, SAMPLE_KEYS_PAD, D_MODEL), lambda b: (b, 0, 0)),
                  pl.BlockSpec((B_HEADS, DEC_SEQ, SAMPLE_KEYS_PAD), lambda b: (0, 0, 0))],
        out_specs=pl.BlockSpec((DEC_SEQ, D_MODEL), lambda b: (b, 0)),
        out_shape=jax.ShapeDtypeStruct((N_SAMPLE, D_MODEL), BF16),
        compiler_params=_params("parallel"),
        name="attn_sample",
    )(q, kk, vv, bias)


def _ln(z, g, b):
    mu = jnp.mean(z, axis=-1, keepdims=True)
    zc = z - mu
    var = jnp.mean(zc * zc, axis=-1, keepdims=True)
    return zc * lax.rsqrt(var + LN_EPS) * g + b


def _mm_ln_body(h_ref, w_ref, x_ref, g_ref, b_ref, xo_ref, xbo_ref):
    y = jnp.dot(h_ref[...], w_ref[...], preferred_element_type=F32)
    xn = _ln(ALPHA * x_ref[...] + y, g_ref[...], b_ref[...])
    xo_ref[...] = xn
    xbo_ref[...] = xn.astype(BF16)


def matmul_residual_ln(h, w, x, g, b, tm=256):
    n = x.shape[0]
    row = pl.BlockSpec((tm, D_MODEL), lambda i: (i, 0))
    vec = pl.BlockSpec((1, D_MODEL), lambda i: (0, 0))
    return pl.pallas_call(
        _mm_ln_body,
        grid=(n // tm,),
        in_specs=[row, pl.BlockSpec((D_MODEL, D_MODEL), lambda i: (0, 0)), row, vec, vec],
        out_specs=[row, row],
        out_shape=[jax.ShapeDtypeStruct((n, D_MODEL), F32), jax.ShapeDtypeStruct((n, D_MODEL), BF16)],
        compiler_params=_params("parallel"),
        name="matmul_residual_ln",
    )(h, w, x, g[None, :], b[None, :])


def _add_ln_body(x_ref, y_ref, g_ref, b_ref, xo_ref, xbo_ref):
    xn = _ln(ALPHA * x_ref[...] + y_ref[...], g_ref[...], b_ref[...])
    xo_ref[...] = xn
    xbo_ref[...] = xn.astype(BF16)


def residual_ln(x, y, g, b, tm=512):
    n = x.shape[0]
    row = pl.BlockSpec((tm, D_MODEL), lambda i: (i, 0))
    vec = pl.BlockSpec((1, D_MODEL), lambda i: (0, 0))
    return pl.pallas_call(
        _add_ln_body,
        grid=(n // tm,),
        in_specs=[row, row, vec, vec],
        out_specs=[row, row],
        out_shape=[jax.ShapeDtypeStruct((n, D_MODEL), F32), jax.ShapeDtypeStruct((n, D_MODEL), BF16)],
        compiler_params=_params("parallel"),
        name="residual_ln",
    )(x, y, g[None, :], b[None, :])


ROUTER_E0 = N_GROUPS


def _router_body(x_ref, w_ref, b_ref, wt_ref, cls_ref):
    xh, xl = _split2(x_ref[...])
    wh, wl = _split2(w_ref[...])
    logits = _dot3(xh, xl, wh, wl, _NN) + b_ref[...]
    lane = lax.broadcasted_iota(jnp.int32, logits.shape, 1).astype(F32)
    big = float(LANES)
    rmax = lambda a: jnp.max(a, axis=-1, keepdims=True)
    rmin = lambda a: jnp.min(a, axis=-1, keepdims=True)
    is_g = lane < N_GROUPS
    gl = jnp.where(is_g, logits, -jnp.inf)
    gmax = rmax(gl)
    gsel = rmin(jnp.where(gl == gmax, lane, big))
    gsum = jnp.sum(jnp.where(is_g, jnp.exp(logits - gmax), 0.0), axis=-1, keepdims=True)
    g_w = 1.0 / gsum
    e_lo = ROUTER_E0 + EXP_PER_GROUP * gsel
    in_grp = (lane >= e_lo) & (lane < e_lo + EXP_PER_GROUP)
    el = jnp.where(in_grp, logits, -jnp.inf)
    e1 = rmax(el)
    i1 = rmin(jnp.where(el == e1, lane, big))
    el2 = jnp.where(lane == i1, -jnp.inf, el)
    e2 = rmax(el2)
    i2 = rmin(jnp.where(el2 == e2, lane, big))
    t = jnp.exp(e2 - e1)
    p1 = 1.0 / (1.0 + t)
    w1 = p1 * g_w
    w2 = (t * p1) * g_w
    a1 = i1 - e_lo
    a2 = i2 - e_lo
    first_low = a1 < a2
    lo = jnp.where(first_low, a1, a2)
    hi = jnp.where(first_low, a2, a1)
    w_lo = jnp.where(first_low, w1, w2)
    w_hi = jnp.where(first_low, w2, w1)
    off = jnp.where(lo == 0.0, 0.0, jnp.where(lo == 1.0, 3.0, 5.0))
    cls = gsel * float(len(PAIRS)) + off + (hi - lo - 1.0)
    wt_ref[...] = jnp.where(lane < LANES // 2, w_lo, w_hi)
    cls_ref[...] = jnp.broadcast_to(cls, logits.shape).astype(jnp.int32)


def router(x, w_group, b_group, w_expert, b_expert, tm=512):
    n = x.shape[0]
    w = jnp.pad(jnp.concatenate([w_group, w_expert], axis=1), ((0, 0), (0, LANES - N_GROUPS - N_EXPERTS)))
    b = jnp.pad(jnp.concatenate([b_group, b_expert]).astype(F32), (0, LANES - N_GROUPS - N_EXPERTS))
    return pl.pallas_call(
        _router_body,
        grid=(n // tm,),
        in_specs=[pl.BlockSpec((tm, D_MODEL), lambda i: (i, 0)),
                  pl.BlockSpec((D_MODEL, LANES), lambda i: (0, 0)),
                  pl.BlockSpec((1, LANES), lambda i: (0, 0))],
        out_specs=[pl.BlockSpec((tm, LANES), lambda i: (i, 0)),
                   pl.BlockSpec((tm, LANES), lambda i: (i, 0))],
        out_shape=[jax.ShapeDtypeStruct((n, LANES), F32), jax.ShapeDtypeStruct((n, LANES), jnp.int32)],
        compiler_params=_params("parallel"),
        name="router",
    )(x, w, b[None, :])


def _expert_body(ea_ref, eb_ref, valid_ref, x_ref, wt_ref, wga_ref, wua_ref, wda_ref,
                 wgb_ref, wub_ref, wdb_ref, y_ref):
    i = pl.program_id(0)

    @pl.when(valid_ref[i] == 0)
    def _():
        y_ref[...] = jnp.zeros_like(y_ref)

    @pl.when(valid_ref[i] != 0)
    def _():
        x = x_ref[...].astype(BF16)
        dot = functools.partial(jnp.dot, preferred_element_type=F32)

        def ffn(wg, wu, wd):
            h = jax.nn.silu(dot(x, wg[...])) * dot(x, wu[...])
            return dot(h.astype(BF16), wd[...])

        w_lo = wt_ref[:, 0:1]
        w_hi = wt_ref[:, LANES // 2:LANES // 2 + 1]
        y_ref[...] = w_lo * ffn(wga_ref, wua_ref, wda_ref) + w_hi * ffn(wgb_ref, wub_ref, wdb_ref)


def experts(xs, wts, tile_ea, tile_eb, tile_valid, w_gate, w_up, w_down):
    wspec = lambda shape, which: pl.BlockSpec(
        (None,) + shape, lambda i, ea, eb, va: ((ea, eb)[which][i], 0, 0))
    up = (D_MODEL, D_EXPERT)
    down = (D_EXPERT, D_MODEL)
    grid_spec = pltpu.PrefetchScalarGridSpec(
        num_scalar_prefetch=3,
        grid=(MOE_TILES,),
        in_specs=[pl.BlockSpec((MOE_TILE, D_MODEL), lambda i, ea, eb, va: (i, 0)),
                  pl.BlockSpec((MOE_TILE, LANES), lambda i, ea, eb, va: (i, 0)),
                  wspec(up, 0), wspec(up, 0), wspec(down, 0),
                  wspec(up, 1), wspec(up, 1), wspec(down, 1)],
        out_specs=pl.BlockSpec((MOE_TILE, D_MODEL), lambda i, ea, eb, va: (i, 0)),
    )
    return pl.pallas_call(
        _expert_body,
        grid_spec=grid_spec,
        out_shape=jax.ShapeDtypeStruct((MOE_ROWS, D_MODEL), F32),
        compiler_params=_params("arbitrary"),
        name="experts",
    )(tile_ea, tile_eb, tile_valid, xs, wts, w_gate, w_up, w_down, w_gate, w_up, w_down)


def sc_gather_rows(x, idx):
    n, d = x.shape
    chunk = min(d, SC_CHUNK)
    c = d // chunk
    n_out = idx.shape[0] * c
    x2 = x.reshape(n * c, chunk)
    idx2 = (idx[:, None] * c + jnp.arange(c, dtype=jnp.int32)[None, :]).reshape(1, n_out)
    mesh = plsc.VectorSubcoreMesh(core_axis_name="core", subcore_axis_name="subcore")

    @functools.partial(
        pl.kernel,
        out_type=jax.ShapeDtypeStruct((n_out, chunk), x.dtype),
        mesh=mesh,
    )
    def gather(x_hbm, i_hbm, o_hbm):
        def body(i_vmem, o_vmem):
            pltpu.sync_copy(x_hbm.at[i_vmem.at[0]], o_vmem)

        pltpu.emit_pipeline(
            body,
            grid=(n_out // SC_WINDOW,),
            in_specs=[pl.BlockSpec((1, SC_WINDOW), lambda i: (0, i))],
            out_specs=[pl.BlockSpec((SC_WINDOW, chunk), lambda i: (i, 0))],
            core_axis_name=("core", "subcore"),
            dimension_semantics=(pltpu.PARALLEL,),
        )(i_hbm, o_hbm)

    return gather(x2, idx2).reshape(idx.shape[0], d)


_CLASS_EA = [g * EXP_PER_GROUP + lo for g in range(N_GROUPS) for lo, hi in PAIRS]
_CLASS_EB = [g * EXP_PER_GROUP + hi for g in range(N_GROUPS) for lo, hi in PAIRS]


def dispatch_plan(cls):
    onehot = (cls[:, None] == jnp.arange(N_CLASSES, dtype=jnp.int32)[None, :]).astype(jnp.int32)
    csum = jnp.cumsum(onehot, axis=0)
    rank = jnp.sum(csum * onehot, axis=1) - 1
    counts = csum[-1]
    tiles_per = (counts + MOE_TILE - 1) // MOE_TILE
    tile_end = jnp.cumsum(tiles_per)
    tile_start = tile_end - tiles_per
    pos = jnp.sum(onehot * tile_start[None, :], axis=1) * MOE_TILE + rank
    src = jnp.zeros((MOE_ROWS,), jnp.int32).at[pos].set(jnp.arange(cls.shape[0], dtype=jnp.int32))
    t = jnp.arange(MOE_TILES, dtype=jnp.int32)
    tile_cls = jnp.sum((t[:, None] >= tile_end[None, :]).astype(jnp.int32), axis=1)
    valid = (tile_cls < N_CLASSES).astype(jnp.int32)
    last_cls = jnp.max(jnp.where(counts > 0, jnp.arange(N_CLASSES, dtype=jnp.int32), 0))
    tile_cls = jnp.where(valid == 1, tile_cls, last_cls)
    ea = jnp.asarray(_CLASS_EA, jnp.int32)[tile_cls]
    eb = jnp.asarray(_CLASS_EB, jnp.int32)[tile_cls]
    return pos, src, ea, eb, valid


def moe(x, w_group, b_group, w_expert, b_expert, w_gate, w_up, w_down):
    wts, cls = router(x, w_group, b_group, w_expert, b_expert)
    pos, src, ea, eb, valid = dispatch_plan(cls[:, 0])
    xs = sc_gather_rows(x, src)
    ws = sc_gather_rows(wts, src)
    ys = experts(xs, ws, ea, eb, valid, w_gate, w_up, w_down)
    return sc_gather_rows(ys, pos)


def _rows(a):
    return a.reshape(-1, a.shape[-1])


def kernel(x_prompt, x_sample, state_C, state_n, state_m, cache_k, cache_v, a_w_in, a_b_gate, a_norm, a_w_out, b_w_in, b_rel, b_w_out, ln1_g, ln1_b, ln2_g, ln2_b, r_w_group, r_b_group, r_w_expert, r_b_expert, e_w_gate, e_w_up, e_w_down):
    x = jnp.concatenate([_rows(x_prompt), _rows(x_sample)], axis=0)
    xb = x.astype(BF16)
    hk = A_HEADS * A_DK
    n_main = 2 * hk + 2 * D_MODEL
    outs = {k: [] for k in ("Cp", "np", "mp", "kp", "vp", "Cs", "ns", "ms", "ks", "vs")}
    for layer in range(DEPTH):
        j = layer // 2
        if layer % 2 == 0:
            w_in = a_w_in[j]
            proj = matmul(xb, w_in[:, :n_main].astype(BF16), BF16)
            g, gt = gate_preacts(x, w_in[:, n_main:])
            zc = jnp.zeros((BATCH, A_HEADS, A_DK, A_DV), F32)
            zn = jnp.zeros((BATCH, A_HEADS, A_DK), F32)
            zm = jnp.zeros((BATCH, A_HEADS), F32)
            h_p, c_p, n_p, m_p = mlstm(proj, g, gt, a_b_gate[j], a_norm[j], zc, zn, zm,
                                       batch=BATCH, seq=SEQ, L=MLSTM_CHUNK)
            h_s, c_s, n_s, m_s = mlstm(proj[N_PROMPT:], g[N_PROMPT:], gt[:, N_PROMPT:], a_b_gate[j], a_norm[j],
                                       state_C[j], state_n[j], state_m[j],
                                       batch=DEC_BATCH, seq=DEC_SEQ, L=DEC_SEQ)
            h = jnp.concatenate([h_p[:N_PROMPT], h_s], axis=0)
            outs["Cp"].append(c_p); outs["np"].append(n_p); outs["mp"].append(m_p[..., 0])
            outs["Cs"].append(c_s); outs["ns"].append(n_s); outs["ms"].append(m_s[..., 0])
            w_out = a_w_out[j]
        else:
            qkv = matmul(xb, b_w_in[j].astype(BF16), BF16)
            o_p = attn_prompt(qkv, b_rel[j])
            qkv_s = qkv[N_PROMPT:]
            k_new = qkv_s[:, D_MODEL:2 * D_MODEL].reshape(DEC_BATCH, DEC_SEQ, D_MODEL)
            v_new = qkv_s[:, 2 * D_MODEL:].reshape(DEC_BATCH, DEC_SEQ, D_MODEL)
            zpad = jnp.zeros((DEC_BATCH, SAMPLE_KEYS_PAD - SAMPLE_KEYS, D_MODEL), BF16)
            kk = jnp.concatenate([cache_k[j].reshape(DEC_BATCH, REACH, D_MODEL).astype(BF16), k_new, zpad], axis=1)
            vv = jnp.concatenate([cache_v[j].reshape(DEC_BATCH, REACH, D_MODEL).astype(BF16), v_new, zpad], axis=1)
            o_s = attn_sample(qkv_s[:, :D_MODEL], kk, vv, b_rel[j])
            h = jnp.concatenate([o_p, o_s], axis=0)
            kv_p = qkv[:N_PROMPT].reshape(BATCH, SEQ, 3, B_HEADS, B_DH)[:, SEQ - REACH:]
            outs["kp"].append(kv_p[:, :, 1].astype(F32)); outs["vp"].append(kv_p[:, :, 2].astype(F32))
            outs["ks"].append(k_new.reshape(DEC_BATCH, DEC_SEQ, B_HEADS, B_DH).astype(F32))
            outs["vs"].append(v_new.reshape(DEC_BATCH, DEC_SEQ, B_HEADS, B_DH).astype(F32))
            w_out = b_w_out[j]
        x, xb = matmul_residual_ln(h, w_out.astype(BF16), x, ln1_g[layer], ln1_b[layer])
        y = moe(x, r_w_group[layer], r_b_group[layer], r_w_expert[layer], r_b_expert[layer],
                e_w_gate[layer].astype(BF16), e_w_up[layer].astype(BF16), e_w_down[layer].astype(BF16))
        x, xb = residual_ln(x, y, ln2_g[layer], ln2_b[layer])
    st = lambda k: jnp.stack(outs[k])
    return (x[:N_PROMPT].reshape(BATCH, SEQ, D_MODEL), x[N_PROMPT:].reshape(DEC_BATCH, DEC_SEQ, D_MODEL),
            st("Cp"), st("np"), st("mp"), st("kp"), st("vp"),
            st("Cs"), st("ns"), st("ms"), st("ks"), st("vs"))
```

```python
import functools

import jax
import jax.numpy as jnp
from jax import lax
from jax.experimental import pallas as pl
from jax.experimental.pallas import tpu as pltpu
from jax.experimental.pallas import tpu_sc as plsc

F32 = jnp.float32
BF16 = jnp.bfloat16

D_MODEL = 2048
BATCH = 8
SEQ = 4096
DEPTH = 4
DEC_BATCH = 32
DEC_SEQ = 32
PAST_LEN = 1024
CHUNK = 64
A_HEADS = 8
A_DK = 128
A_DV = D_MODEL // A_HEADS
GATE_CAP = 15.0
B_HEADS = 16
B_DH = D_MODEL // B_HEADS
PREV_CHUNKS = 8
REACH = PREV_CHUNKS * CHUNK
REL_CLIP = 256
N_GROUPS = 4
EXP_PER_GROUP = 4
N_EXPERTS = N_GROUPS * EXP_PER_GROUP
D_EXPERT = D_MODEL // 4
ALPHA = (2 * DEPTH) ** 0.25
LN_EPS = 1e-5
RMS_EPS = 1e-6

N_PROMPT = BATCH * SEQ
N_SAMPLE = DEC_BATCH * DEC_SEQ
N_TOK = N_PROMPT + N_SAMPLE

VMEM_LIMIT = 56 * 1024 * 1024
LANES = 128

PAIRS = ((0, 1), (0, 2), (0, 3), (1, 2), (1, 3), (2, 3))
N_CLASSES = N_GROUPS * len(PAIRS)
MOE_TILE = 256
MOE_TILES = -(-(N_TOK // MOE_TILE + N_CLASSES) // 32) * 32
MOE_ROWS = MOE_TILES * MOE_TILE

MLSTM_CHUNK = 128
ATT_TQ = 256
ATT_TK = ATT_TQ + REACH
NEG = -1e30

SC_WINDOW = 128
SC_TOKENS = 32


def _params(*sem):
    return pltpu.CompilerParams(dimension_semantics=sem, vmem_limit_bytes=VMEM_LIMIT)


def _mm_body(x_ref, w_ref, o_ref):
    o_ref[...] = jnp.dot(x_ref[...], w_ref[...], preferred_element_type=F32).astype(o_ref.dtype)


def matmul(x, w, out_dtype, tm=512, tn=1024):
    n, k = x.shape
    m = w.shape[1]
    return pl.pallas_call(
        _mm_body,
        grid=(m // tn, n // tm),
        in_specs=[pl.BlockSpec((tm, k), lambda j, i: (i, 0)),
                  pl.BlockSpec((k, tn), lambda j, i: (0, j))],
        out_specs=pl.BlockSpec((tm, tn), lambda j, i: (i, j)),
        out_shape=jax.ShapeDtypeStruct((n, m), out_dtype),
        compiler_params=_params("parallel", "parallel"),
        name="matmul",
    )(x, w)


def _split2(x):
    hi = x.astype(BF16)
    lo = (x - hi.astype(F32)).astype(BF16)
    return hi, lo


def _dot3(xh, xl, wh, wl, dims):
    dg = functools.partial(lax.dot_general, dimension_numbers=dims, preferred_element_type=F32)
    return dg(xh, wh) + (dg(xh, wl) + dg(xl, wh))


_NN = (((1,), (0,)), ((), ()))
_NT = (((1,), (1,)), ((), ()))
_TN = (((0,), (0,)), ((), ()))


def _gate_body(x_ref, w_ref, wt_ref, g_ref, gt_ref):
    xh, xl = _split2(x_ref[...])
    wh, wl = _split2(w_ref[...])
    wth, wtl = _split2(wt_ref[...])
    g_ref[...] = _dot3(xh, xl, wh, wl, _NN)
    gt_ref[...] = _dot3(wth, wtl, xh, xl, _NT)


def gate_preacts(x, w_gate, tm=512):
    n = x.shape[0]
    w = jnp.pad(w_gate, ((0, 0), (0, LANES - w_gate.shape[1])))
    return pl.pallas_call(
        _gate_body,
        grid=(n // tm,),
        in_specs=[pl.BlockSpec((tm, D_MODEL), lambda i: (i, 0)),
                  pl.BlockSpec((D_MODEL, LANES), lambda i: (0, 0)),
                  pl.BlockSpec((LANES, D_MODEL), lambda i: (0, 0))],
        out_specs=[pl.BlockSpec((tm, LANES), lambda i: (i, 0)),
                   pl.BlockSpec((LANES, tm), lambda i: (0, i))],
        out_shape=[jax.ShapeDtypeStruct((n, LANES), F32),
                   jax.ShapeDtypeStruct((LANES, n), F32)],
        compiler_params=_params("parallel"),
        name="gate_preacts",
    )(x, w, w.T)


def _log_sigmoid(x):
    return jnp.minimum(x, 0.0) - jnp.log(1.0 + jnp.exp(-jnp.abs(x)))


def _split3(x):
    a = x.astype(BF16)
    r = x - a.astype(F32)
    b = r.astype(BF16)
    c = (r - b.astype(F32)).astype(BF16)
    return a, b, c


def _mlstm_body(q_ref, k_ref, v_ref, o_ref, g_ref, gt_ref, bc_ref, br_ref, gain_ref,
                c0_ref, n0_ref, m0_ref, h_ref, c_ref, n_ref, m_ref, *, L):
    step = pl.program_id(1)

    @pl.when(step == 0)
    def _():
        c_ref[...] = c0_ref[...]
        n_ref[...] = n0_ref[...]
        m_ref[...] = m0_ref[...]

    row = lax.broadcasted_iota(jnp.int32, (L, L), 0)
    col = lax.broadcasted_iota(jnp.int32, (L, L), 1)
    causal = col <= row
    tril = jnp.where(causal, 1.0, 0.0).astype(BF16)
    triu = jnp.where(row <= col, 1.0, 0.0).astype(BF16)

    pre_c = GATE_CAP * jnp.tanh((g_ref[...] + bc_ref[...]) * (1.0 / GATE_CAP))
    lf_c = _log_sigmoid(pre_c)
    a, b, c = _split3(lf_c)
    dot = functools.partial(jnp.dot, preferred_element_type=F32)
    bt_c = dot(tril, a) + (dot(tril, b) + dot(tril, c))
    pre_r = GATE_CAP * jnp.tanh((gt_ref[...] + br_ref[...]) * (1.0 / GATE_CAP))
    lf_r = _log_sigmoid(pre_r)
    a, b, c = _split3(lf_r)
    bt_r = dot(a, triu) + (dot(b, triu) + dot(c, triu))

    scale = A_DK ** -0.5
    for h in range(A_HEADS):
        btc = bt_c[:, A_HEADS + h:A_HEADS + h + 1]
        igc = pre_c[:, h:h + 1]
        btr = bt_r[A_HEADS + h:A_HEADS + h + 1, :]
        igr = pre_r[h:h + 1, :]
        m_prev = m_ref[0, h:h + 1, 0:1]
        log_d = jnp.where(causal, btc - btr + igr, -jnp.inf)
        log_inter = btc + m_prev
        m_t = jnp.maximum(log_inter, jnp.max(log_d, axis=-1, keepdims=True))
        d_mat = jnp.exp(log_d - m_t)
        inter = jnp.exp(log_inter - m_t)
        qh = q_ref[:, h * A_DK:(h + 1) * A_DK]
        kh = k_ref[:, h * A_DK:(h + 1) * A_DK]
        vh = v_ref[:, h * A_DV:(h + 1) * A_DV]
        s = lax.dot_general(qh, kh, _NT, preferred_element_type=F32) * (d_mat * scale)
        c_h = c_ref[0, h]
        n_h = n_ref[0, h:h + 1, :]
        num = dot(s.astype(BF16), vh) + dot(qh, c_h.astype(BF16)) * inter
        qn = jnp.sum(qh.astype(F32) * n_h, axis=-1, keepdims=True)
        den = jnp.sum(s, axis=-1, keepdims=True) + inter * qn
        den = jnp.maximum(jnp.abs(den), jnp.exp(-m_t))
        hh = num / den
        hh = hh * lax.rsqrt(jnp.mean(hh * hh, axis=-1, keepdims=True) + RMS_EPS)
        og = o_ref[:, h * A_DV:(h + 1) * A_DV].astype(F32)
        hh = hh * gain_ref[:, h * A_DV:(h + 1) * A_DV] * jax.nn.sigmoid(og)
        h_ref[:, h * A_DV:(h + 1) * A_DV] = hh.astype(h_ref.dtype)

        m_new = m_t[L - 1:L, :]
        bt_last = btc[L - 1:L, :]
        w_c = jnp.exp(bt_last - btc + igc - m_new)
        decay = jnp.exp(bt_last + m_prev - m_new)
        wk = kh.astype(F32) * (w_c * scale)
        c_ref[0, h] = decay * c_h + lax.dot_general(wk.astype(BF16), vh, _TN, preferred_element_type=F32)
        n_ref[0, h:h + 1, :] = decay * n_h + jnp.sum(wk, axis=0, keepdims=True)
        m_ref[0, h:h + 1, :] = jnp.broadcast_to(m_new, (1, LANES))


def mlstm(proj, g, gt, b_gate, gain, c0, n0, m0, *, batch, seq, L, row0):
    nc = seq // L
    bias = jnp.pad(b_gate.astype(F32), (0, LANES - 2 * A_HEADS))
    row_blk = lambda b, c: row0 // L + b * nc + c
    if L % LANES:
        gt_spec = pl.BlockSpec((None, 2 * A_HEADS, L), lambda b, c: (b, 0, 0))
    else:
        gt_spec = pl.BlockSpec((2 * A_HEADS, L), lambda b, c: (0, row_blk(b, c)))
    st = lambda *s: pl.BlockSpec((1,) + s, lambda b, c: (b,) + (0,) * len(s))
    in_specs = [pl.BlockSpec((L, A_HEADS * A_DK), lambda b, c: (row_blk(b, c), 0)),
                pl.BlockSpec((L, A_HEADS * A_DK), lambda b, c: (row_blk(b, c), 1)),
                pl.BlockSpec((L, D_MODEL), lambda b, c: (row_blk(b, c), 1)),
                pl.BlockSpec((L, D_MODEL), lambda b, c: (row_blk(b, c), 2)),
                pl.BlockSpec((L, LANES), lambda b, c: (row_blk(b, c), 0)),
                gt_spec,
                pl.BlockSpec((1, LANES), lambda b, c: (0, 0)),
                pl.BlockSpec((2 * A_HEADS, 1), lambda b, c: (0, 0)),
                pl.BlockSpec((1, D_MODEL), lambda b, c: (0, 0)),
                st(A_HEADS, A_DK, A_DV), st(A_HEADS, A_DK), st(A_HEADS, LANES)]
    args = [proj, proj, proj, proj, g, gt, bias[None, :], bias[:2 * A_HEADS, None], gain.astype(F32)[None, :],
            c0, n0, jnp.broadcast_to(m0[..., None], m0.shape + (LANES,))]
    return pl.pallas_call(
        functools.partial(_mlstm_body, L=L),
        grid=(batch, nc),
        in_specs=in_specs,
        out_specs=[pl.BlockSpec((L, D_MODEL), lambda b, c: (b * nc + c, 0)),
                   st(A_HEADS, A_DK, A_DV), st(A_HEADS, A_DK), st(A_HEADS, LANES)],
        out_shape=[jax.ShapeDtypeStruct((batch * seq, D_MODEL), BF16),
                   jax.ShapeDtypeStruct((batch, A_HEADS, A_DK, A_DV), F32),
                   jax.ShapeDtypeStruct((batch, A_HEADS, A_DK), F32),
                   jax.ShapeDtypeStruct((batch, A_HEADS, LANES), F32)],
        compiler_params=_params("parallel", "arbitrary"),
        name="mlstm",
    )(*args)


def _softmax_pv(s, v):
    m = jnp.max(s, axis=-1, keepdims=True)
    p = jnp.exp(s - m)
    l = jnp.sum(p, axis=-1, keepdims=True)
    return jnp.dot(p.astype(BF16), v, preferred_element_type=F32) / l


def _attn_prompt_body(q_ref, k_ref, v_ref, bias_ref, o_ref, kpad, vpad):
    kpad[0:REACH, :] = jnp.zeros((REACH, B_DH), BF16)
    vpad[0:REACH, :] = jnp.zeros((REACH, B_DH), BF16)
    kpad[REACH:, :] = k_ref[...]
    vpad[REACH:, :] = v_ref[...]
    scale = B_DH ** -0.5

    def tile(t, mask_front):
        r0 = t * ATT_TQ if isinstance(t, int) else pl.multiple_of(t * ATT_TQ, ATT_TQ)
        q = q_ref[pl.ds(r0, ATT_TQ), :]
        kb = kpad[pl.ds(r0, ATT_TK), :]
        vb = vpad[pl.ds(r0, ATT_TK), :]
        s = lax.dot_general(q, kb, _NT, preferred_element_type=F32) * scale + bias_ref[0]
        if mask_front:
            j = lax.broadcasted_iota(jnp.int32, (ATT_TQ, ATT_TK), 1)
            s = jnp.where(j + r0 >= REACH, s, NEG)
        o_ref[pl.ds(r0, ATT_TQ), :] = _softmax_pv(s, vb).astype(o_ref.dtype)

    n_front = REACH // ATT_TQ
    for t in range(n_front):
        tile(t, True)

    def loop(t, carry):
        tile(t, False)
        return carry

    lax.fori_loop(n_front, SEQ // ATT_TQ, loop, 0)


def band_bias(rel_table, nq, nk):
    w = nq + nk
    rel = jnp.clip(REACH + nq - 1 - jnp.arange(w), -REL_CLIP, REL_CLIP) + REL_CLIP
    r = rel_table.astype(F32)[:, rel]
    heads = r.shape[0]
    skew = jnp.broadcast_to(r[:, None, :], (heads, nq, w)).reshape(heads, nq * w)
    skew = skew[:, :nq * (w - 1)].reshape(heads, nq, w - 1)
    bias = skew[:, :, nq - 1:nq - 1 + nk]
    i = jnp.arange(nq)[:, None]
    j = jnp.arange(nk)[None, :]
    dc = j // CHUNK - i // CHUNK
    allowed = (dc >= 0) & (dc <= PREV_CHUNKS)
    return jnp.where(allowed[None], bias, NEG)


def attn_prompt(qkv, rel_table):
    bias = band_bias(rel_table, ATT_TQ, ATT_TK)
    return pl.pallas_call(
        _attn_prompt_body,
        grid=(BATCH, B_HEADS),
        in_specs=[pl.BlockSpec((SEQ, B_DH), lambda b, h: (b, h)),
                  pl.BlockSpec((SEQ, B_DH), lambda b, h: (b, B_HEADS + h)),
                  pl.BlockSpec((SEQ, B_DH), lambda b, h: (b, 2 * B_HEADS + h)),
                  pl.BlockSpec((1, ATT_TQ, ATT_TK), lambda b, h: (h, 0, 0))],
        out_specs=pl.BlockSpec((SEQ, B_DH), lambda b, h: (b, h)),
        out_shape=jax.ShapeDtypeStruct((N_PROMPT, D_MODEL), BF16),
        scratch_shapes=[pltpu.VMEM((SEQ + REACH, B_DH), BF16),
                        pltpu.VMEM((SEQ + REACH, B_DH), BF16)],
        compiler_params=_params("parallel", "parallel"),
        name="attn_prompt",
    )(qkv, qkv, qkv, bias)


SAMPLE_KEYS = REACH + DEC_SEQ
SAMPLE_KEYS_PAD = -(-SAMPLE_KEYS // LANES) * LANES


def _attn_sample_body(q_ref, k_ref, v_ref, bias_ref, o_ref):
    scale = B_DH ** -0.5
    for h in range(B_HEADS):
        sl = slice(h * B_DH, (h + 1) * B_DH)
        s = lax.dot_general(q_ref[:, sl], k_ref[0, :, sl], _NT, preferred_element_type=F32) * scale + bias_ref[h]
        o_ref[:, sl] = _softmax_pv(s, v_ref[0, :, sl]).astype(o_ref.dtype)


def attn_sample(qkv, kk, vv, rel_table):
    bias = band_bias(rel_table, DEC_SEQ, SAMPLE_KEYS_PAD)
    bias = jnp.where(jnp.arange(SAMPLE_KEYS_PAD)[None, None, :] < SAMPLE_KEYS, bias, NEG)
    return pl.pallas_call(
        _attn_sample_body,
        grid=(DEC_BATCH,),
        in_specs=[pl.BlockSpec((DEC_SEQ, D_MODEL), lambda b: (N_PROMPT // DEC_SEQ + b, 0)),
                  pl.BlockSpec((1, SAMPLE_KEYS_PAD, D_MODEL), lambda b: (b, 0, 0)),
                  pl.BlockSpec((1, SAMPLE_KEYS_PAD, D_MODEL), lambda b: (b, 0, 0)),
                  pl.BlockSpec((B_HEADS, DEC_SEQ, SAMPLE_KEYS_PAD), lambda b: (0, 0, 0))],
        out_specs=pl.BlockSpec((DEC_SEQ, D_MODEL), lambda b: (b, 0)),
        out_shape=jax.ShapeDtypeStruct((N_SAMPLE, D_MODEL), BF16),
        compiler_params=_params("parallel"),
        name="attn_sample",
    )(qkv, kk, vv, bias)


def _ln(z, g, b):
    mu = jnp.mean(z, axis=-1, keepdims=True)
    zc = z - mu
    var = jnp.mean(zc * zc, axis=-1, keepdims=True)
    return zc * lax.rsqrt(var + LN_EPS) * g + b


PACK_ROWS = D_MODEL // 2 // LANES
HI_MASK = -65536


def _store_packed(ref, v):
    t = v.shape[0]
    half = D_MODEL // 2
    hi = lax.bitcast_convert_type(v[:, :half].astype(BF16).astype(F32), jnp.int32)
    lo = lax.bitcast_convert_type(v[:, half:].astype(BF16).astype(F32), jnp.int32)
    words = (hi & HI_MASK) | lax.shift_right_logical(lo, 16)
    for a in range(PACK_ROWS):
        ref[pl.ds(a, t, stride=PACK_ROWS), :] = words[:, a * LANES:(a + 1) * LANES]


def _load_packed(ref, t):
    words = jnp.concatenate([ref[pl.ds(a, t, stride=PACK_ROWS), :] for a in range(PACK_ROWS)], axis=1)
    hi = lax.bitcast_convert_type(words & HI_MASK, F32)
    lo = lax.bitcast_convert_type(lax.shift_left(words, 16), F32)
    return jnp.concatenate([hi, lo], axis=1)


def _mix_ln_router_body(hp_ref, hs_ref, w_ref, x_ref, g_ref, b_ref, rw_ref, rb_ref,
                        xo_ref, xbo_ref, xp_ref, wt_ref, cls_ref, *, prompt_tiles):
    h = jnp.where(pl.program_id(0) < prompt_tiles, hp_ref[...], hs_ref[...])
    y = jnp.dot(h, w_ref[...], preferred_element_type=F32)
    xn = _ln(ALPHA * x_ref[...] + y, g_ref[...], b_ref[...])
    xo_ref[...] = xn
    xbo_ref[...] = xn.astype(BF16)
    _store_packed(xp_ref, xn)
    _route(xn, rw_ref, rb_ref, wt_ref, cls_ref)


def mix_ln_router(h_p, h_s, w, x, g, b, w_group, b_group, w_expert, b_expert, tm=256):
    n = x.shape[0]
    prompt_tiles = h_p.shape[0] // tm
    sample_tiles = h_s.shape[0] // tm
    assert prompt_tiles * tm == h_p.shape[0] and (prompt_tiles + sample_tiles) * tm == n
    hp_spec = pl.BlockSpec((tm, D_MODEL), lambda i: (jnp.minimum(i, prompt_tiles - 1), 0))
    hs_spec = pl.BlockSpec((tm, D_MODEL), lambda i: (jnp.maximum(i - prompt_tiles, 0), 0))
    rw = jnp.pad(jnp.concatenate([w_group, w_expert], axis=1), ((0, 0), (0, LANES - N_GROUPS - N_EXPERTS)))
    rb = jnp.pad(jnp.concatenate([b_group, b_expert]).astype(F32), (0, LANES - N_GROUPS - N_EXPERTS))
    row = pl.BlockSpec((tm, D_MODEL), lambda i: (i, 0))
    vec = pl.BlockSpec((1, D_MODEL), lambda i: (0, 0))
    lane_row = pl.BlockSpec((tm, LANES), lambda i: (i, 0))
    return pl.pallas_call(
        functools.partial(_mix_ln_router_body, prompt_tiles=prompt_tiles),
        grid=(n // tm,),
        in_specs=[hp_spec, hs_spec, pl.BlockSpec((D_MODEL, D_MODEL), lambda i: (0, 0)), row, vec, vec,
                  pl.BlockSpec((D_MODEL, LANES), lambda i: (0, 0)), pl.BlockSpec((1, LANES), lambda i: (0, 0))],
        out_specs=[row, row, pl.BlockSpec((tm * PACK_ROWS, LANES), lambda i: (i, 0)), lane_row, lane_row],
        out_shape=[jax.ShapeDtypeStruct((n, D_MODEL), F32), jax.ShapeDtypeStruct((n, D_MODEL), BF16),
                   jax.ShapeDtypeStruct((n * PACK_ROWS, LANES), jnp.int32),
                   jax.ShapeDtypeStruct((n, LANES), F32), jax.ShapeDtypeStruct((n, LANES), jnp.int32)],
        compiler_params=_params("parallel"),
        name="mix_ln_router",
    )(h_p, h_s, w, x, g[None, :], b[None, :], rw, rb[None, :])


def _add_ln_body(x_ref, y_ref, g_ref, b_ref, xo_ref, xbo_ref):
    y = _load_packed(y_ref, x_ref.shape[0])
    xn = _ln(ALPHA * x_ref[...] + y, g_ref[...], b_ref[...])
    xo_ref[...] = xn
    xbo_ref[...] = xn.astype(BF16)


def residual_ln(x, y_packed, g, b, tm=512):
    n = x.shape[0]
    row = pl.BlockSpec((tm, D_MODEL), lambda i: (i, 0))
    vec = pl.BlockSpec((1, D_MODEL), lambda i: (0, 0))
    return pl.pallas_call(
        _add_ln_body,
        grid=(n // tm,),
        in_specs=[row, pl.BlockSpec((tm * PACK_ROWS, LANES), lambda i: (i, 0)), vec, vec],
        out_specs=[row, row],
        out_shape=[jax.ShapeDtypeStruct((n, D_MODEL), F32), jax.ShapeDtypeStruct((n, D_MODEL), BF16)],
        compiler_params=_params("parallel"),
        name="residual_ln",
    )(x, y_packed, g[None, :], b[None, :])


ROUTER_E0 = N_GROUPS


def _route(x, w_ref, b_ref, wt_ref, cls_ref):
    xh, xl = _split2(x)
    wh, wl = _split2(w_ref[...])
    logits = _dot3(xh, xl, wh, wl, _NN) + b_ref[...]
    lane = lax.broadcasted_iota(jnp.int32, logits.shape, 1).astype(F32)
    big = float(LANES)
    rmax = lambda a: jnp.max(a, axis=-1, keepdims=True)
    rmin = lambda a: jnp.min(a, axis=-1, keepdims=True)
    is_g = lane < N_GROUPS
    gl = jnp.where(is_g, logits, -jnp.inf)
    gmax = rmax(gl)
    gsel = rmin(jnp.where(gl == gmax, lane, big))
    gsum = jnp.sum(jnp.where(is_g, jnp.exp(logits - gmax), 0.0), axis=-1, keepdims=True)
    g_w = 1.0 / gsum
    e_lo = ROUTER_E0 + EXP_PER_GROUP * gsel
    in_grp = (lane >= e_lo) & (lane < e_lo + EXP_PER_GROUP)
    el = jnp.where(in_grp, logits, -jnp.inf)
    e1 = rmax(el)
    i1 = rmin(jnp.where(el == e1, lane, big))
    el2 = jnp.where(lane == i1, -jnp.inf, el)
    e2 = rmax(el2)
    i2 = rmin(jnp.where(el2 == e2, lane, big))
    t = jnp.exp(e2 - e1)
    p1 = 1.0 / (1.0 + t)
    w1 = p1 * g_w
    w2 = (t * p1) * g_w
    a1 = i1 - e_lo
    a2 = i2 - e_lo
    first_low = a1 < a2
    lo = jnp.where(first_low, a1, a2)
    hi = jnp.where(first_low, a2, a1)
    w_lo = jnp.where(first_low, w1, w2)
    w_hi = jnp.where(first_low, w2, w1)
    off = jnp.where(lo == 0.0, 0.0, jnp.where(lo == 1.0, 3.0, 5.0))
    cls = gsel * float(len(PAIRS)) + off + (hi - lo - 1.0)
    wt_ref[...] = jnp.where(lane < LANES // 2, w_lo, w_hi)
    cls_ref[...] = jnp.broadcast_to(cls, logits.shape).astype(jnp.int32)


def _expert_body(ea_ref, eb_ref, valid_ref, x_ref, wt_ref, wga_ref, wua_ref, wda_ref,
                 wgb_ref, wub_ref, wdb_ref, y_ref):
    i = pl.program_id(0)

    @pl.when(valid_ref[i] == 0)
    def _():
        y_ref[...] = jnp.zeros_like(y_ref)

    @pl.when(valid_ref[i] != 0)
    def _():
        x = _load_packed(x_ref, MOE_TILE).astype(BF16)
        dot = functools.partial(jnp.dot, preferred_element_type=F32)

        def ffn(wg, wu, wd):
            h = jax.nn.silu(dot(x, wg[...])) * dot(x, wu[...])
            return dot(h.astype(BF16), wd[...])

        w_lo = wt_ref[:, 0:1]
        w_hi = wt_ref[:, LANES // 2:LANES // 2 + 1]
        _store_packed(y_ref, w_lo * ffn(wga_ref, wua_ref, wda_ref) + w_hi * ffn(wgb_ref, wub_ref, wdb_ref))


def experts(xs, wts, tile_ea, tile_eb, tile_valid, w_gate, w_up, w_down):
    wspec = lambda shape, which: pl.BlockSpec(
        (None,) + shape, lambda i, ea, eb, va: ((ea, eb)[which][i], 0, 0))
    up = (D_MODEL, D_EXPERT)
    down = (D_EXPERT, D_MODEL)
    packed = pl.BlockSpec((MOE_TILE * PACK_ROWS, LANES), lambda i, ea, eb, va: (i, 0))
    grid_spec = pltpu.PrefetchScalarGridSpec(
        num_scalar_prefetch=3,
        grid=(MOE_TILES,),
        in_specs=[packed,
                  pl.BlockSpec((MOE_TILE, LANES), lambda i, ea, eb, va: (i, 0)),
                  wspec(up, 0), wspec(up, 0), wspec(down, 0),
                  wspec(up, 1), wspec(up, 1), wspec(down, 1)],
        out_specs=packed,
    )
    return pl.pallas_call(
        _expert_body,
        grid_spec=grid_spec,
        out_shape=jax.ShapeDtypeStruct((MOE_ROWS * PACK_ROWS, LANES), jnp.int32),
        compiler_params=_params("arbitrary"),
        name="experts",
    )(tile_ea, tile_eb, tile_valid, xs, wts, w_gate, w_up, w_down, w_gate, w_up, w_down)


def sc_gather_rows(x, idx):
    n, d = x.shape
    n_out = idx.shape[0]
    mesh = plsc.VectorSubcoreMesh(core_axis_name="core", subcore_axis_name="subcore")

    @functools.partial(
        pl.kernel,
        out_type=jax.ShapeDtypeStruct((n_out, d), x.dtype),
        mesh=mesh,
    )
    def gather(x_hbm, i_hbm, o_hbm):
        def body(i_vmem, o_vmem):
            pltpu.sync_copy(x_hbm.at[i_vmem.at[0]], o_vmem)

        pltpu.emit_pipeline(
            body,
            grid=(n_out // SC_WINDOW,),
            in_specs=[pl.BlockSpec((1, SC_WINDOW), lambda i: (0, i))],
            out_specs=[pl.BlockSpec((SC_WINDOW, d), lambda i: (i, 0))],
            core_axis_name=("core", "subcore"),
            dimension_semantics=(pltpu.PARALLEL,),
        )(i_hbm, o_hbm)

    return gather(x, idx.reshape(1, n_out))


def sc_gather_tokens(x, idx):
    n = x.shape[0] // PACK_ROWS
    p = idx.shape[0]
    info = plsc.get_sparse_core_info()
    n_workers = info.num_cores * info.num_subcores
    per_worker = p // n_workers
    steps = per_worker // SC_TOKENS
    assert per_worker * n_workers == p and steps * SC_TOKENS == per_worker
    mesh = plsc.VectorSubcoreMesh(core_axis_name="core", subcore_axis_name="subcore")

    @functools.partial(
        pl.kernel,
        out_type=jax.ShapeDtypeStruct((p, PACK_ROWS, LANES), x.dtype),
        mesh=mesh,
        scratch_types=[pltpu.VMEM((per_worker,), jnp.int32),
                       pltpu.VMEM((SC_TOKENS, PACK_ROWS, LANES), x.dtype)],
    )
    def gather(x_hbm, i_hbm, o_hbm, idx_v, buf):
        worker = lax.axis_index("subcore") * info.num_cores + lax.axis_index("core")
        base = worker * per_worker
        pltpu.sync_copy(i_hbm.at[pl.ds(base, per_worker)], idx_v)

        @pl.loop(0, steps)
        def _(s):
            r0 = s * SC_TOKENS
            pltpu.sync_copy(x_hbm.at[idx_v.at[pl.ds(r0, SC_TOKENS)]], buf)
            pltpu.sync_copy(buf, o_hbm.at[pl.ds(base + r0, SC_TOKENS)])

    return gather(x.reshape(n, PACK_ROWS, LANES), idx).reshape(p * PACK_ROWS, LANES)


_CLASS_EA = [g * EXP_PER_GROUP + lo for g in range(N_GROUPS) for lo, hi in PAIRS]
_CLASS_EB = [g * EXP_PER_GROUP + hi for g in range(N_GROUPS) for lo, hi in PAIRS]


def dispatch_plan(cls):
    onehot = (cls[:, None] == jnp.arange(N_CLASSES, dtype=jnp.int32)[None, :]).astype(jnp.int32)
    csum = jnp.cumsum(onehot, axis=0)
    rank = jnp.sum(csum * onehot, axis=1) - 1
    counts = csum[-1]
    tiles_per = (counts + MOE_TILE - 1) // MOE_TILE
    tile_end = jnp.cumsum(tiles_per)
    tile_start = tile_end - tiles_per
    pos = jnp.sum(onehot * tile_start[None, :], axis=1) * MOE_TILE + rank
    n = cls.shape[0]
    src = (jnp.arange(MOE_ROWS, dtype=jnp.int32) % n).at[pos].set(jnp.arange(n, dtype=jnp.int32))
    t = jnp.arange(MOE_TILES, dtype=jnp.int32)
    tile_cls = jnp.sum((t[:, None] >= tile_end[None, :]).astype(jnp.int32), axis=1)
    valid = (tile_cls < N_CLASSES).astype(jnp.int32)
    last_cls = jnp.max(jnp.where(counts > 0, jnp.arange(N_CLASSES, dtype=jnp.int32), 0))
    tile_cls = jnp.where(valid == 1, tile_cls, last_cls)
    ea = jnp.asarray(_CLASS_EA, jnp.int32)[tile_cls]
    eb = jnp.asarray(_CLASS_EB, jnp.int32)[tile_cls]
    return pos, src, ea, eb, valid


def moe(xp, wts, cls, w_gate, w_up, w_down):
    pos, src, ea, eb, valid = dispatch_plan(cls[:, 0])
    xs = sc_gather_tokens(xp, src)
    ws = sc_gather_rows(wts, src)
    ys = experts(xs, ws, ea, eb, valid, w_gate, w_up, w_down)
    return sc_gather_tokens(ys, pos)


def _rows(a):
    return a.reshape(-1, a.shape[-1])


def kernel(x_prompt, x_sample, state_C, state_n, state_m, cache_k, cache_v, a_w_in, a_b_gate, a_norm, a_w_out, b_w_in, b_rel, b_w_out, ln1_g, ln1_b, ln2_g, ln2_b, r_w_group, r_b_group, r_w_expert, r_b_expert, e_w_gate, e_w_up, e_w_down):
    x = jnp.concatenate([_rows(x_prompt), _rows(x_sample)], axis=0)
    xb = x.astype(BF16)
    hk = A_HEADS * A_DK
    n_main = 2 * hk + 2 * D_MODEL
    outs = {k: [] for k in ("Cp", "np", "mp", "kp", "vp", "Cs", "ns", "ms", "ks", "vs")}
    for layer in range(DEPTH):
        j = layer // 2
        if layer % 2 == 0:
            w_in = a_w_in[j]
            proj = matmul(xb, w_in[:, :n_main].astype(BF16), BF16)
            g, gt = gate_preacts(x, w_in[:, n_main:])
            zc = jnp.zeros((BATCH, A_HEADS, A_DK, A_DV), F32)
            zn = jnp.zeros((BATCH, A_HEADS, A_DK), F32)
            zm = jnp.zeros((BATCH, A_HEADS), F32)
            gt = gt[:2 * A_HEADS]
            h_p, c_p, n_p, m_p = mlstm(proj, g, gt, a_b_gate[j], a_norm[j], zc, zn, zm,
                                       batch=BATCH, seq=SEQ, L=MLSTM_CHUNK, row0=0)
            gt_s = gt[:, N_PROMPT:].reshape(2 * A_HEADS, DEC_BATCH, DEC_SEQ).transpose(1, 0, 2)
            h_s, c_s, n_s, m_s = mlstm(proj, g, gt_s, a_b_gate[j], a_norm[j],
                                       state_C[j], state_n[j], state_m[j],
                                       batch=DEC_BATCH, seq=DEC_SEQ, L=DEC_SEQ, row0=N_PROMPT)
            outs["Cp"].append(c_p); outs["np"].append(n_p); outs["mp"].append(m_p[..., 0])
            outs["Cs"].append(c_s); outs["ns"].append(n_s); outs["ms"].append(m_s[..., 0])
            w_out = a_w_out[j]
        else:
            qkv = matmul(xb, b_w_in[j].astype(BF16), BF16)
            h_p = attn_prompt(qkv, b_rel[j])
            qkv_s = qkv[N_PROMPT:]
            k_new = qkv_s[:, D_MODEL:2 * D_MODEL].reshape(DEC_BATCH, DEC_SEQ, D_MODEL)
            v_new = qkv_s[:, 2 * D_MODEL:].reshape(DEC_BATCH, DEC_SEQ, D_MODEL)
            zpad = jnp.zeros((DEC_BATCH, SAMPLE_KEYS_PAD - SAMPLE_KEYS, D_MODEL), BF16)
            kk = jnp.concatenate([cache_k[j].reshape(DEC_BATCH, REACH, D_MODEL).astype(BF16), k_new, zpad], axis=1)
            vv = jnp.concatenate([cache_v[j].reshape(DEC_BATCH, REACH, D_MODEL).astype(BF16), v_new, zpad], axis=1)
            h_s = attn_sample(qkv, kk, vv, b_rel[j])
            kv_p = qkv[:N_PROMPT].reshape(BATCH, SEQ, 3 * D_MODEL)
            tail = lambda c: lax.slice(kv_p, (0, SEQ - REACH, c * D_MODEL), (BATCH, SEQ, (c + 1) * D_MODEL)).astype(
                F32).reshape(BATCH, REACH, B_HEADS, B_DH)
            outs["kp"].append(tail(1)); outs["vp"].append(tail(2))
            outs["ks"].append(k_new.reshape(DEC_BATCH, DEC_SEQ, B_HEADS, B_DH).astype(F32))
            outs["vs"].append(v_new.reshape(DEC_BATCH, DEC_SEQ, B_HEADS, B_DH).astype(F32))
            w_out = b_w_out[j]
        x, xb, xp, wts, cls = mix_ln_router(h_p, h_s, w_out.astype(BF16), x, ln1_g[layer], ln1_b[layer],
                                            r_w_group[layer], r_b_group[layer], r_w_expert[layer], r_b_expert[layer])
        y = moe(xp, wts, cls, e_w_gate[layer].astype(BF16), e_w_up[layer].astype(BF16),
                e_w_down[layer].astype(BF16))
        x, xb = residual_ln(x, y, ln2_g[layer], ln2_b[layer])
    st = lambda k: jnp.stack(outs[k])
    return (x[:N_PROMPT].reshape(BATCH, SEQ, D_MODEL), x[N_PROMPT:].reshape(DEC_BATCH, DEC_SEQ, D_MODEL),
            st("Cp"), st("np"), st("mp"), st("kp"), st("vp"),
            st("Cs"), st("ns"), st("ms"), st("ks"), st("vs"))
```

```python
import functools

import jax
import jax.numpy as jnp
from jax import lax
from jax.experimental import pallas as pl
from jax.experimental.pallas import tpu as pltpu
from jax.experimental.pallas import tpu_sc as plsc

F32 = jnp.float32
BF16 = jnp.bfloat16

D_MODEL = 2048
BATCH = 8
SEQ = 4096
DEPTH = 4
DEC_BATCH = 32
DEC_SEQ = 32
PAST_LEN = 1024
CHUNK = 64
A_HEADS = 8
A_DK = 128
A_DV = D_MODEL // A_HEADS
GATE_CAP = 15.0
B_HEADS = 16
B_DH = D_MODEL // B_HEADS
PREV_CHUNKS = 8
REACH = PREV_CHUNKS * CHUNK
REL_CLIP = 256
N_GROUPS = 4
EXP_PER_GROUP = 4
N_EXPERTS = N_GROUPS * EXP_PER_GROUP
D_EXPERT = D_MODEL // 4
ALPHA = (2 * DEPTH) ** 0.25
LN_EPS = 1e-5
RMS_EPS = 1e-6

N_PROMPT = BATCH * SEQ
N_SAMPLE = DEC_BATCH * DEC_SEQ
N_TOK = N_PROMPT + N_SAMPLE

VMEM_LIMIT = 56 * 1024 * 1024
LANES = 128

PAIRS = ((0, 1), (0, 2), (0, 3), (1, 2), (1, 3), (2, 3))
N_CLASSES = N_GROUPS * len(PAIRS)
MOE_TILE = 256
MOE_TILES = -(-(N_TOK // MOE_TILE + N_CLASSES) // 32) * 32
MOE_ROWS = MOE_TILES * MOE_TILE

MLSTM_CHUNK = 128
ATT_TQ = 256
ATT_TK = ATT_TQ + REACH
NEG = -1e30

SC_WINDOW = 128
SC_TOKENS = 32


def _params(*sem):
    return pltpu.CompilerParams(dimension_semantics=sem, vmem_limit_bytes=VMEM_LIMIT)


def _mm_body(x_ref, w_ref, o_ref, wb):
    @pl.when(pl.program_id(1) == 0)
    def _():
        wb[...] = w_ref[...].astype(BF16)

    o_ref[...] = jnp.dot(x_ref[...], wb[...], preferred_element_type=F32).astype(o_ref.dtype)


def matmul(x, w_all, layer, m, out_dtype, tm=512, tn=1024):
    n, k = x.shape
    return pl.pallas_call(
        _mm_body,
        grid=(m // tn, n // tm),
        in_specs=[pl.BlockSpec((tm, k), lambda j, i: (i, 0)),
                  pl.BlockSpec((None, k, tn), lambda j, i: (layer, 0, j))],
        out_specs=pl.BlockSpec((tm, tn), lambda j, i: (i, j)),
        out_shape=jax.ShapeDtypeStruct((n, m), out_dtype),
        scratch_shapes=[pltpu.VMEM((k, tn), BF16)],
        compiler_params=_params("parallel", "arbitrary"),
        name="matmul",
    )(x, w_all)


def _cast_body(w_ref, o_ref):
    o_ref[...] = w_ref[...].astype(BF16)


def cast_layer(w_all, layer):
    e, a, b = w_all.shape[1:]
    return pl.pallas_call(
        _cast_body,
        grid=(e,),
        in_specs=[pl.BlockSpec((None, None, a, b), lambda i: (layer, i, 0, 0))],
        out_specs=pl.BlockSpec((None, a, b), lambda i: (i, 0, 0)),
        out_shape=jax.ShapeDtypeStruct((e, a, b), BF16),
        compiler_params=_params("parallel"),
        name="cast_layer",
    )(w_all)


def _split2(x):
    hi = x.astype(BF16)
    lo = (x - hi.astype(F32)).astype(BF16)
    return hi, lo


def _dot3(xh, xl, wh, wl, dims):
    dg = functools.partial(lax.dot_general, dimension_numbers=dims, preferred_element_type=F32)
    return dg(xh, wh) + (dg(xh, wl) + dg(xl, wh))


_NN = (((1,), (0,)), ((), ()))
_NT = (((1,), (1,)), ((), ()))
_TN = (((0,), (0,)), ((), ()))


def _gate_body(x_ref, w_ref, wt_ref, g_ref, gt_ref):
    xh, xl = _split2(x_ref[...])
    wh, wl = _split2(w_ref[...])
    wth, wtl = _split2(wt_ref[...])
    g_ref[...] = _dot3(xh, xl, wh, wl, _NN)
    gt_ref[...] = _dot3(wth, wtl, xh, xl, _NT)


def gate_preacts(x, w_gate, tm=512):
    n = x.shape[0]
    w = jnp.pad(w_gate, ((0, 0), (0, LANES - w_gate.shape[1])))
    return pl.pallas_call(
        _gate_body,
        grid=(n // tm,),
        in_specs=[pl.BlockSpec((tm, D_MODEL), lambda i: (i, 0)),
                  pl.BlockSpec((D_MODEL, LANES), lambda i: (0, 0)),
                  pl.BlockSpec((LANES, D_MODEL), lambda i: (0, 0))],
        out_specs=[pl.BlockSpec((tm, LANES), lambda i: (i, 0)),
                   pl.BlockSpec((LANES, tm), lambda i: (0, i))],
        out_shape=[jax.ShapeDtypeStruct((n, LANES), F32),
                   jax.ShapeDtypeStruct((LANES, n), F32)],
        compiler_params=_params("parallel"),
        name="gate_preacts",
    )(x, w, w.T)


def _log_sigmoid(x):
    return jnp.minimum(x, 0.0) - jnp.log(1.0 + jnp.exp(-jnp.abs(x)))


def _split3(x):
    a = x.astype(BF16)
    r = x - a.astype(F32)
    b = r.astype(BF16)
    c = (r - b.astype(F32)).astype(BF16)
    return a, b, c


def _mlstm_body(q_ref, k_ref, v_ref, o_ref, g_ref, gt_ref, bc_ref, br_ref, gain_ref,
                c0_ref, n0_ref, m0_ref, h_ref, c_ref, n_ref, m_ref, *, L):
    step = pl.program_id(1)

    @pl.when(step == 0)
    def _():
        c_ref[...] = c0_ref[...]
        n_ref[...] = n0_ref[...]
        m_ref[...] = m0_ref[...]

    row = lax.broadcasted_iota(jnp.int32, (L, L), 0)
    col = lax.broadcasted_iota(jnp.int32, (L, L), 1)
    causal = col <= row
    tril = jnp.where(causal, 1.0, 0.0).astype(BF16)
    triu = jnp.where(row <= col, 1.0, 0.0).astype(BF16)

    pre_c = GATE_CAP * jnp.tanh((g_ref[...] + bc_ref[...]) * (1.0 / GATE_CAP))
    lf_c = _log_sigmoid(pre_c)
    a, b, c = _split3(lf_c)
    dot = functools.partial(jnp.dot, preferred_element_type=F32)
    bt_c = dot(tril, a) + (dot(tril, b) + dot(tril, c))
    pre_r = GATE_CAP * jnp.tanh((gt_ref[...] + br_ref[...]) * (1.0 / GATE_CAP))
    lf_r = _log_sigmoid(pre_r)
    a, b, c = _split3(lf_r)
    bt_r = dot(a, triu) + (dot(b, triu) + dot(c, triu))

    scale = A_DK ** -0.5
    for h in range(A_HEADS):
        btc = bt_c[:, A_HEADS + h:A_HEADS + h + 1]
        igc = pre_c[:, h:h + 1]
        btr = bt_r[A_HEADS + h:A_HEADS + h + 1, :]
        igr = pre_r[h:h + 1, :]
        m_prev = m_ref[0, h:h + 1, 0:1]
        log_d = jnp.where(causal, btc - btr + igr, -jnp.inf)
        log_inter = btc + m_prev
        m_t = jnp.maximum(log_inter, jnp.max(log_d, axis=-1, keepdims=True))
        d_mat = jnp.exp(log_d - m_t)
        inter = jnp.exp(log_inter - m_t)
        qh = q_ref[:, h * A_DK:(h + 1) * A_DK]
        kh = k_ref[:, h * A_DK:(h + 1) * A_DK]
        vh = v_ref[:, h * A_DV:(h + 1) * A_DV]
        s = lax.dot_general(qh, kh, _NT, preferred_element_type=F32) * (d_mat * scale)
        c_h = c_ref[0, h]
        n_h = n_ref[0, h:h + 1, :]
        num = dot(s.astype(BF16), vh) + dot(qh, c_h.astype(BF16)) * inter
        qn = jnp.sum(qh.astype(F32) * n_h, axis=-1, keepdims=True)
        den = jnp.sum(s, axis=-1, keepdims=True) + inter * qn
        den = jnp.maximum(jnp.abs(den), jnp.exp(-m_t))
        hh = num / den
        hh = hh * lax.rsqrt(jnp.mean(hh * hh, axis=-1, keepdims=True) + RMS_EPS)
        og = o_ref[:, h * A_DV:(h + 1) * A_DV].astype(F32)
        hh = hh * gain_ref[:, h * A_DV:(h + 1) * A_DV] * jax.nn.sigmoid(og)
        h_ref[:, h * A_DV:(h + 1) * A_DV] = hh.astype(h_ref.dtype)

        m_new = m_t[L - 1:L, :]
        bt_last = btc[L - 1:L, :]
        w_c = jnp.exp(bt_last - btc + igc - m_new)
        decay = jnp.exp(bt_last + m_prev - m_new)
        wk = kh.astype(F32) * (w_c * scale)
        c_ref[0, h] = decay * c_h + lax.dot_general(wk.astype(BF16), vh, _TN, preferred_element_type=F32)
        n_ref[0, h:h + 1, :] = decay * n_h + jnp.sum(wk, axis=0, keepdims=True)
        m_ref[0, h:h + 1, :] = jnp.broadcast_to(m_new, (1, LANES))


def mlstm(proj, g, gt, b_gate, gain, c0, n0, m0, *, batch, seq, L, row0):
    nc = seq // L
    bias = jnp.pad(b_gate.astype(F32), (0, LANES - 2 * A_HEADS))
    row_blk = lambda b, c: row0 // L + b * nc + c
    if L % LANES:
        gt_spec = pl.BlockSpec((None, 2 * A_HEADS, L), lambda b, c: (b, 0, 0))
    else:
        gt_spec = pl.BlockSpec((2 * A_HEADS, L), lambda b, c: (0, row_blk(b, c)))
    st = lambda *s: pl.BlockSpec((1,) + s, lambda b, c: (b,) + (0,) * len(s))
    in_specs = [pl.BlockSpec((L, A_HEADS * A_DK), lambda b, c: (row_blk(b, c), 0)),
                pl.BlockSpec((L, A_HEADS * A_DK), lambda b, c: (row_blk(b, c), 1)),
                pl.BlockSpec((L, D_MODEL), lambda b, c: (row_blk(b, c), 1)),
                pl.BlockSpec((L, D_MODEL), lambda b, c: (row_blk(b, c), 2)),
                pl.BlockSpec((L, LANES), lambda b, c: (row_blk(b, c), 0)),
                gt_spec,
                pl.BlockSpec((1, LANES), lambda b, c: (0, 0)),
                pl.BlockSpec((2 * A_HEADS, 1), lambda b, c: (0, 0)),
                pl.BlockSpec((1, D_MODEL), lambda b, c: (0, 0)),
                st(A_HEADS, A_DK, A_DV), st(A_HEADS, A_DK), st(A_HEADS, LANES)]
    args = [proj, proj, proj, proj, g, gt, bias[None, :], bias[:2 * A_HEADS, None], gain.astype(F32)[None, :],
            c0, n0, jnp.broadcast_to(m0[..., None], m0.shape + (LANES,))]
    return pl.pallas_call(
        functools.partial(_mlstm_body, L=L),
        grid=(batch, nc),
        in_specs=in_specs,
        out_specs=[pl.BlockSpec((L, D_MODEL), lambda b, c: (b * nc + c, 0)),
                   st(A_HEADS, A_DK, A_DV), st(A_HEADS, A_DK), st(A_HEADS, LANES)],
        out_shape=[jax.ShapeDtypeStruct((batch * seq, D_MODEL), BF16),
                   jax.ShapeDtypeStruct((batch, A_HEADS, A_DK, A_DV), F32),
                   jax.ShapeDtypeStruct((batch, A_HEADS, A_DK), F32),
                   jax.ShapeDtypeStruct((batch, A_HEADS, LANES), F32)],
        compiler_params=_params("parallel", "arbitrary"),
        name="mlstm",
    )(*args)


LOG2E = 1.4426950408889634
Q_SCALE = B_DH ** -0.5 * LOG2E


def _scale_q(q):
    return (q.astype(F32) * Q_SCALE).astype(BF16)


def _attn_prompt_body(q_ref, k_ref, v_ref, bias_ref, o_ref, qs, kpad, vpad):
    qs[...] = _scale_q(q_ref[...])
    kpad[0:REACH, :] = jnp.zeros((REACH, B_DH), BF16)
    kpad[REACH:, :] = k_ref[...]
    vpad[0:REACH, 0:B_DH] = jnp.zeros((REACH, B_DH), BF16)
    vpad[REACH:, 0:B_DH] = v_ref[...]
    vpad[:, B_DH:] = jnp.ones((SEQ + REACH, B_DH), BF16)

    def tile(t, mask_front):
        r0 = t * ATT_TQ if isinstance(t, int) else pl.multiple_of(t * ATT_TQ, ATT_TQ)
        q = qs[pl.ds(r0, ATT_TQ), :]
        kb = kpad[pl.ds(r0, ATT_TK), :]
        vb = vpad[pl.ds(r0, ATT_TK), :]
        s = lax.dot_general(q, kb, _NT, preferred_element_type=F32) + bias_ref[0]
        if mask_front:
            j = lax.broadcasted_iota(jnp.int32, (ATT_TQ, ATT_TK), 1)
            s = jnp.where(j + r0 >= REACH, s, NEG)
        p = jnp.exp2(s - jnp.max(s, axis=-1, keepdims=True)).astype(BF16)
        acc = jnp.dot(p, vb, preferred_element_type=F32)
        o_ref[pl.ds(r0, ATT_TQ), :] = (acc[:, :B_DH] / acc[:, B_DH:]).astype(o_ref.dtype)

    n_front = REACH // ATT_TQ
    for t in range(n_front):
        tile(t, True)

    def loop(t, carry):
        tile(t, False)
        return carry

    lax.fori_loop(n_front, SEQ // ATT_TQ, loop, 0, unroll=2)


def band_bias(rel_table, nq, nk):
    w = nq + nk
    rel = jnp.clip(REACH + nq - 1 - jnp.arange(w), -REL_CLIP, REL_CLIP) + REL_CLIP
    r = rel_table.astype(F32)[:, rel]
    heads = r.shape[0]
    skew = jnp.broadcast_to(r[:, None, :], (heads, nq, w)).reshape(heads, nq * w)
    skew = skew[:, :nq * (w - 1)].reshape(heads, nq, w - 1)
    bias = skew[:, :, nq - 1:nq - 1 + nk]
    i = jnp.arange(nq)[:, None]
    j = jnp.arange(nk)[None, :]
    dc = j // CHUNK - i // CHUNK
    allowed = (dc >= 0) & (dc <= PREV_CHUNKS)
    return jnp.where(allowed[None], bias * LOG2E, NEG)


def attn_prompt(qkv, rel_table):
    bias = band_bias(rel_table, ATT_TQ, ATT_TK)
    return pl.pallas_call(
        _attn_prompt_body,
        grid=(BATCH, B_HEADS),
        in_specs=[pl.BlockSpec((SEQ, B_DH), lambda b, h: (b, h)),
                  pl.BlockSpec((SEQ, B_DH), lambda b, h: (b, B_HEADS + h)),
                  pl.BlockSpec((SEQ, B_DH), lambda b, h: (b, 2 * B_HEADS + h)),
                  pl.BlockSpec((1, ATT_TQ, ATT_TK), lambda b, h: (h, 0, 0))],
        out_specs=pl.BlockSpec((SEQ, B_DH), lambda b, h: (b, h)),
        out_shape=jax.ShapeDtypeStruct((N_PROMPT, D_MODEL), BF16),
        scratch_shapes=[pltpu.VMEM((SEQ, B_DH), BF16),
                        pltpu.VMEM((SEQ + REACH, B_DH), BF16),
                        pltpu.VMEM((SEQ + REACH, 2 * B_DH), BF16)],
        compiler_params=_params("parallel", "parallel"),
        name="attn_prompt",
    )(qkv, qkv, qkv, bias)


SAMPLE_KEYS = REACH + DEC_SEQ
SAMPLE_KEYS_PAD = -(-SAMPLE_KEYS // LANES) * LANES


def _attn_sample_body(q_ref, k_ref, v_ref, bias_ref, o_ref):
    for h in range(B_HEADS):
        sl = slice(h * B_DH, (h + 1) * B_DH)
        s = lax.dot_general(_scale_q(q_ref[:, sl]), k_ref[0, :, sl], _NT, preferred_element_type=F32) + bias_ref[h]
        p = jnp.exp2(s - jnp.max(s, axis=-1, keepdims=True))
        l = jnp.sum(p, axis=-1, keepdims=True)
        o_ref[:, sl] = (jnp.dot(p.astype(BF16), v_ref[0, :, sl], preferred_element_type=F32) / l).astype(o_ref.dtype)


def attn_sample(qkv, kk, vv, rel_table):
    bias = band_bias(rel_table, DEC_SEQ, SAMPLE_KEYS_PAD)
    bias = jnp.where(jnp.arange(SAMPLE_KEYS_PAD)[None, None, :] < SAMPLE_KEYS, bias, NEG)
    return pl.pallas_call(
        _attn_sample_body,
        grid=(DEC_BATCH,),
        in_specs=[pl.BlockSpec((DEC_SEQ, D_MODEL), lambda b: (N_PROMPT // DEC_SEQ + b, 0)),
                  pl.BlockSpec((1, SAMPLE_KEYS_PAD, D_MODEL), lambda b: (b, 0, 0)),
                  pl.BlockSpec((1, SAMPLE_KEYS_PAD, D_MODEL), lambda b: (b, 0, 0)),
                  pl.BlockSpec((B_HEADS, DEC_SEQ, SAMPLE_KEYS_PAD), lambda b: (0, 0, 0))],
        out_specs=pl.BlockSpec((DEC_SEQ, D_MODEL), lambda b: (b, 0)),
        out_shape=jax.ShapeDtypeStruct((N_SAMPLE, D_MODEL), BF16),
        compiler_params=_params("parallel"),
        name="attn_sample",
    )(qkv, kk, vv, bias)


def _ln(z, g, b):
    mu = jnp.mean(z, axis=-1, keepdims=True)
    zc = z - mu
    var = jnp.mean(zc * zc, axis=-1, keepdims=True)
    return zc * lax.rsqrt(var + LN_EPS) * g + b


PACK_ROWS = D_MODEL // 2 // LANES
HI_MASK = -65536


def _store_packed(ref, v):
    t = v.shape[0]
    half = D_MODEL // 2
    hi = lax.bitcast_convert_type(v[:, :half].astype(BF16).astype(F32), jnp.int32)
    lo = lax.bitcast_convert_type(v[:, half:].astype(BF16).astype(F32), jnp.int32)
    words = (hi & HI_MASK) | lax.shift_right_logical(lo, 16)
    for a in range(PACK_ROWS):
        ref[pl.ds(a, t, stride=PACK_ROWS), :] = words[:, a * LANES:(a + 1) * LANES]


def _load_packed(ref, t):
    words = jnp.concatenate([ref[pl.ds(a, t, stride=PACK_ROWS), :] for a in range(PACK_ROWS)], axis=1)
    hi = lax.bitcast_convert_type(words & HI_MASK, F32)
    lo = lax.bitcast_convert_type(lax.shift_left(words, 16), F32)
    return jnp.concatenate([hi, lo], axis=1)


def _mix_ln_router_body(hp_ref, hs_ref, w_ref, x_ref, g_ref, b_ref, rw_ref, rb_ref,
                        xo_ref, xbo_ref, xp_ref, wt_ref, cls_ref, *, prompt_tiles):
    h = jnp.where(pl.program_id(0) < prompt_tiles, hp_ref[...], hs_ref[...])
    y = jnp.dot(h, w_ref[...], preferred_element_type=F32)
    xn = _ln(ALPHA * x_ref[...] + y, g_ref[...], b_ref[...])
    xo_ref[...] = xn
    xbo_ref[...] = xn.astype(BF16)
    _store_packed(xp_ref, xn)
    _route(xn, rw_ref, rb_ref, wt_ref, cls_ref)


def mix_ln_router(h_p, h_s, w, x, g, b, w_group, b_group, w_expert, b_expert, tm=256):
    n = x.shape[0]
    prompt_tiles = h_p.shape[0] // tm
    sample_tiles = h_s.shape[0] // tm
    assert prompt_tiles * tm == h_p.shape[0] and (prompt_tiles + sample_tiles) * tm == n
    hp_spec = pl.BlockSpec((tm, D_MODEL), lambda i: (jnp.minimum(i, prompt_tiles - 1), 0))
    hs_spec = pl.BlockSpec((tm, D_MODEL), lambda i: (jnp.maximum(i - prompt_tiles, 0), 0))
    rw = jnp.pad(jnp.concatenate([w_group, w_expert], axis=1), ((0, 0), (0, LANES - N_GROUPS - N_EXPERTS)))
    rb = jnp.pad(jnp.concatenate([b_group, b_expert]).astype(F32), (0, LANES - N_GROUPS - N_EXPERTS))
    row = pl.BlockSpec((tm, D_MODEL), lambda i: (i, 0))
    vec = pl.BlockSpec((1, D_MODEL), lambda i: (0, 0))
    lane_row = pl.BlockSpec((tm, LANES), lambda i: (i, 0))
    return pl.pallas_call(
        functools.partial(_mix_ln_router_body, prompt_tiles=prompt_tiles),
        grid=(n // tm,),
        in_specs=[hp_spec, hs_spec, pl.BlockSpec((D_MODEL, D_MODEL), lambda i: (0, 0)), row, vec, vec,
                  pl.BlockSpec((D_MODEL, LANES), lambda i: (0, 0)), pl.BlockSpec((1, LANES), lambda i: (0, 0))],
        out_specs=[row, row, pl.BlockSpec((tm * PACK_ROWS, LANES), lambda i: (i, 0)), lane_row, lane_row],
        out_shape=[jax.ShapeDtypeStruct((n, D_MODEL), F32), jax.ShapeDtypeStruct((n, D_MODEL), BF16),
                   jax.ShapeDtypeStruct((n * PACK_ROWS, LANES), jnp.int32),
                   jax.ShapeDtypeStruct((n, LANES), F32), jax.ShapeDtypeStruct((n, LANES), jnp.int32)],
        compiler_params=_params("parallel"),
        name="mix_ln_router",
    )(h_p, h_s, w, x, g[None, :], b[None, :], rw, rb[None, :])


def _add_ln_body(x_ref, y_ref, g_ref, b_ref, xo_ref, xbo_ref):
    y = _load_packed(y_ref, x_ref.shape[0])
    xn = _ln(ALPHA * x_ref[...] + y, g_ref[...], b_ref[...])
    xo_ref[...] = xn
    xbo_ref[...] = xn.astype(BF16)


def residual_ln(x, y_packed, g, b, tm=512):
    n = x.shape[0]
    row = pl.BlockSpec((tm, D_MODEL), lambda i: (i, 0))
    vec = pl.BlockSpec((1, D_MODEL), lambda i: (0, 0))
    return pl.pallas_call(
        _add_ln_body,
        grid=(n // tm,),
        in_specs=[row, pl.BlockSpec((tm * PACK_ROWS, LANES), lambda i: (i, 0)), vec, vec],
        out_specs=[row, row],
        out_shape=[jax.ShapeDtypeStruct((n, D_MODEL), F32), jax.ShapeDtypeStruct((n, D_MODEL), BF16)],
        compiler_params=_params("parallel"),
        name="residual_ln",
    )(x, y_packed, g[None, :], b[None, :])


def _add_ln_split_body(x_ref, y_ref, g_ref, b_ref, op_ref, os_ref, *, prompt_tiles):
    y = _load_packed(y_ref, x_ref.shape[0])
    xn = _ln(ALPHA * x_ref[...] + y, g_ref[...], b_ref[...])
    i = pl.program_id(0)

    @pl.when(i < prompt_tiles)
    def _():
        op_ref[...] = xn

    @pl.when(i >= prompt_tiles)
    def _():
        os_ref[...] = xn


def residual_ln_split(x, y_packed, g, b, n_prompt, tm=512):
    n = x.shape[0]
    prompt_tiles = n_prompt // tm
    assert prompt_tiles * tm == n_prompt and n % tm == 0
    row = pl.BlockSpec((tm, D_MODEL), lambda i: (i, 0))
    vec = pl.BlockSpec((1, D_MODEL), lambda i: (0, 0))
    return pl.pallas_call(
        functools.partial(_add_ln_split_body, prompt_tiles=prompt_tiles),
        grid=(n // tm,),
        in_specs=[row, pl.BlockSpec((tm * PACK_ROWS, LANES), lambda i: (i, 0)), vec, vec],
        out_specs=[pl.BlockSpec((tm, D_MODEL), lambda i: (jnp.minimum(i, prompt_tiles - 1), 0)),
                   pl.BlockSpec((tm, D_MODEL), lambda i: (jnp.maximum(i - prompt_tiles, 0), 0))],
        out_shape=[jax.ShapeDtypeStruct((n_prompt, D_MODEL), F32),
                   jax.ShapeDtypeStruct((n - n_prompt, D_MODEL), F32)],
        compiler_params=_params("arbitrary"),
        name="residual_ln_split",
    )(x, y_packed, g[None, :], b[None, :])


ROUTER_E0 = N_GROUPS


def _route(x, w_ref, b_ref, wt_ref, cls_ref):
    xh, xl = _split2(x)
    wh, wl = _split2(w_ref[...])
    logits = _dot3(xh, xl, wh, wl, _NN) + b_ref[...]
    lane = lax.broadcasted_iota(jnp.int32, logits.shape, 1).astype(F32)
    big = float(LANES)
    rmax = lambda a: jnp.max(a, axis=-1, keepdims=True)
    rmin = lambda a: jnp.min(a, axis=-1, keepdims=True)
    is_g = lane < N_GROUPS
    gl = jnp.where(is_g, logits, -jnp.inf)
    gmax = rmax(gl)
    gsel = rmin(jnp.where(gl == gmax, lane, big))
    gsum = jnp.sum(jnp.where(is_g, jnp.exp(logits - gmax), 0.0), axis=-1, keepdims=True)
    g_w = 1.0 / gsum
    e_lo = ROUTER_E0 + EXP_PER_GROUP * gsel
    in_grp = (lane >= e_lo) & (lane < e_lo + EXP_PER_GROUP)
    el = jnp.where(in_grp, logits, -jnp.inf)
    e1 = rmax(el)
    i1 = rmin(jnp.where(el == e1, lane, big))
    el2 = jnp.where(lane == i1, -jnp.inf, el)
    e2 = rmax(el2)
    i2 = rmin(jnp.where(el2 == e2, lane, big))
    t = jnp.exp(e2 - e1)
    p1 = 1.0 / (1.0 + t)
    w1 = p1 * g_w
    w2 = (t * p1) * g_w
    a1 = i1 - e_lo
    a2 = i2 - e_lo
    first_low = a1 < a2
    lo = jnp.where(first_low, a1, a2)
    hi = jnp.where(first_low, a2, a1)
    w_lo = jnp.where(first_low, w1, w2)
    w_hi = jnp.where(first_low, w2, w1)
    off = jnp.where(lo == 0.0, 0.0, jnp.where(lo == 1.0, 3.0, 5.0))
    cls = gsel * float(len(PAIRS)) + off + (hi - lo - 1.0)
    wt_ref[...] = jnp.where(lane < LANES // 2, w_lo, w_hi)
    cls_ref[...] = jnp.broadcast_to(cls, logits.shape).astype(jnp.int32)


def _expert_body(ea_ref, eb_ref, valid_ref, x_ref, wt_ref, wga_ref, wua_ref, wda_ref,
                 wgb_ref, wub_ref, wdb_ref, y_ref):
    i = pl.program_id(0)

    @pl.when(valid_ref[i] == 0)
    def _():
        y_ref[...] = jnp.zeros_like(y_ref)

    @pl.when(valid_ref[i] != 0)
    def _():
        x = _load_packed(x_ref, MOE_TILE).astype(BF16)
        dot = functools.partial(jnp.dot, preferred_element_type=F32)

        def ffn(wg, wu, wd):
            h = jax.nn.silu(dot(x, wg[...])) * dot(x, wu[...])
            return dot(h.astype(BF16), wd[...])

        w_lo = wt_ref[:, 0:1]
        w_hi = wt_ref[:, LANES // 2:LANES // 2 + 1]
        _store_packed(y_ref, w_lo * ffn(wga_ref, wua_ref, wda_ref) + w_hi * ffn(wgb_ref, wub_ref, wdb_ref))


def experts(xs, wts, tile_ea, tile_eb, tile_valid, w_gate, w_up, w_down):
    wspec = lambda shape, which: pl.BlockSpec(
        (None,) + shape, lambda i, ea, eb, va: ((ea, eb)[which][i], 0, 0))
    up = (D_MODEL, D_EXPERT)
    down = (D_EXPERT, D_MODEL)
    packed = pl.BlockSpec((MOE_TILE * PACK_ROWS, LANES), lambda i, ea, eb, va: (i, 0))
    grid_spec = pltpu.PrefetchScalarGridSpec(
        num_scalar_prefetch=3,
        grid=(MOE_TILES,),
        in_specs=[packed,
                  pl.BlockSpec((MOE_TILE, LANES), lambda i, ea, eb, va: (i, 0)),
                  wspec(up, 0), wspec(up, 0), wspec(down, 0),
                  wspec(up, 1), wspec(up, 1), wspec(down, 1)],
        out_specs=packed,
    )
    return pl.pallas_call(
        _expert_body,
        grid_spec=grid_spec,
        out_shape=jax.ShapeDtypeStruct((MOE_ROWS * PACK_ROWS, LANES), jnp.int32),
        compiler_params=_params("arbitrary"),
        name="experts",
    )(tile_ea, tile_eb, tile_valid, xs, wts, w_gate, w_up, w_down, w_gate, w_up, w_down)


def sc_gather_rows(x, idx):
    n, d = x.shape
    n_out = idx.shape[0]
    mesh = plsc.VectorSubcoreMesh(core_axis_name="core", subcore_axis_name="subcore")

    @functools.partial(
        pl.kernel,
        out_type=jax.ShapeDtypeStruct((n_out, d), x.dtype),
        mesh=mesh,
    )
    def gather(x_hbm, i_hbm, o_hbm):
        def body(i_vmem, o_vmem):
            pltpu.sync_copy(x_hbm.at[i_vmem.at[0]], o_vmem)

        pltpu.emit_pipeline(
            body,
            grid=(n_out // SC_WINDOW,),
            in_specs=[pl.BlockSpec((1, SC_WINDOW), lambda i: (0, i))],
            out_specs=[pl.BlockSpec((SC_WINDOW, d), lambda i: (i, 0))],
            core_axis_name=("core", "subcore"),
            dimension_semantics=(pltpu.PARALLEL,),
        )(i_hbm, o_hbm)

    return gather(x, idx.reshape(1, n_out))


def sc_gather_tokens(x, idx):
    n = x.shape[0] // PACK_ROWS
    p = idx.shape[0]
    info = plsc.get_sparse_core_info()
    n_workers = info.num_cores * info.num_subcores
    per_worker = p // n_workers
    steps = per_worker // SC_TOKENS
    assert per_worker * n_workers == p and steps * SC_TOKENS == per_worker
    mesh = plsc.VectorSubcoreMesh(core_axis_name="core", subcore_axis_name="subcore")

    @functools.partial(
        pl.kernel,
        out_type=jax.ShapeDtypeStruct((p, PACK_ROWS, LANES), x.dtype),
        mesh=mesh,
        scratch_types=[pltpu.VMEM((per_worker,), jnp.int32),
                       pltpu.VMEM((SC_TOKENS, PACK_ROWS, LANES), x.dtype)],
    )
    def gather(x_hbm, i_hbm, o_hbm, idx_v, buf):
        worker = lax.axis_index("subcore") * info.num_cores + lax.axis_index("core")
        base = worker * per_worker
        pltpu.sync_copy(i_hbm.at[pl.ds(base, per_worker)], idx_v)

        @pl.loop(0, steps)
        def _(s):
            r0 = s * SC_TOKENS
            pltpu.sync_copy(x_hbm.at[idx_v.at[pl.ds(r0, SC_TOKENS)]], buf)
            pltpu.sync_copy(buf, o_hbm.at[pl.ds(base + r0, SC_TOKENS)])

    return gather(x.reshape(n, PACK_ROWS, LANES), idx).reshape(p * PACK_ROWS, LANES)


_CLASS_EA = [g * EXP_PER_GROUP + lo for g in range(N_GROUPS) for lo, hi in PAIRS]
_CLASS_EB = [g * EXP_PER_GROUP + hi for g in range(N_GROUPS) for lo, hi in PAIRS]


def dispatch_plan(cls):
    onehot = (cls[:, None] == jnp.arange(N_CLASSES, dtype=jnp.int32)[None, :]).astype(jnp.int32)
    csum = jnp.cumsum(onehot, axis=0)
    rank = jnp.sum(csum * onehot, axis=1) - 1
    counts = csum[-1]
    tiles_per = (counts + MOE_TILE - 1) // MOE_TILE
    tile_end = jnp.cumsum(tiles_per)
    tile_start = tile_end - tiles_per
    pos = jnp.sum(onehot * tile_start[None, :], axis=1) * MOE_TILE + rank
    n = cls.shape[0]
    src = (jnp.arange(MOE_ROWS, dtype=jnp.int32) % n).at[pos].set(jnp.arange(n, dtype=jnp.int32))
    t = jnp.arange(MOE_TILES, dtype=jnp.int32)
    tile_cls = jnp.sum((t[:, None] >= tile_end[None, :]).astype(jnp.int32), axis=1)
    valid = (tile_cls < N_CLASSES).astype(jnp.int32)
    last_cls = jnp.max(jnp.where(counts > 0, jnp.arange(N_CLASSES, dtype=jnp.int32), 0))
    tile_cls = jnp.where(valid == 1, tile_cls, last_cls)
    ea = jnp.asarray(_CLASS_EA, jnp.int32)[tile_cls]
    eb = jnp.asarray(_CLASS_EB, jnp.int32)[tile_cls]
    return pos, src, ea, eb, valid


def moe(xp, wts, cls, w_gate, w_up, w_down):
    pos, src, ea, eb, valid = dispatch_plan(cls[:, 0])
    xs = sc_gather_tokens(xp, src)
    ws = sc_gather_rows(wts, src)
    ys = experts(xs, ws, ea, eb, valid, w_gate, w_up, w_down)
    return sc_gather_tokens(ys, pos)


def _rows(a):
    return a.reshape(-1, a.shape[-1])


def kernel(x_prompt, x_sample, state_C, state_n, state_m, cache_k, cache_v, a_w_in, a_b_gate, a_norm, a_w_out, b_w_in, b_rel, b_w_out, ln1_g, ln1_b, ln2_g, ln2_b, r_w_group, r_b_group, r_w_expert, r_b_expert, e_w_gate, e_w_up, e_w_down):
    x = jnp.concatenate([_rows(x_prompt), _rows(x_sample)], axis=0)
    xb = x.astype(BF16)
    hk = A_HEADS * A_DK
    n_main = 2 * hk + 2 * D_MODEL
    outs = {k: [] for k in ("Cp", "np", "mp", "kp", "vp", "Cs", "ns", "ms", "ks", "vs")}
    for layer in range(DEPTH):
        j = layer // 2
        if layer % 2 == 0:
            proj = matmul(xb, a_w_in, j, n_main, BF16)
            g, gt = gate_preacts(x, a_w_in[j, :, n_main:])
            zc = jnp.zeros((BATCH, A_HEADS, A_DK, A_DV), F32)
            zn = jnp.zeros((BATCH, A_HEADS, A_DK), F32)
            zm = jnp.zeros((BATCH, A_HEADS), F32)
            gt = gt[:2 * A_HEADS]
            h_p, c_p, n_p, m_p = mlstm(proj, g, gt, a_b_gate[j], a_norm[j], zc, zn, zm,
                                       batch=BATCH, seq=SEQ, L=MLSTM_CHUNK, row0=0)
            gt_s = gt[:, N_PROMPT:].reshape(2 * A_HEADS, DEC_BATCH, DEC_SEQ).transpose(1, 0, 2)
            h_s, c_s, n_s, m_s = mlstm(proj, g, gt_s, a_b_gate[j], a_norm[j],
                                       state_C[j], state_n[j], state_m[j],
                                       batch=DEC_BATCH, seq=DEC_SEQ, L=DEC_SEQ, row0=N_PROMPT)
            outs["Cp"].append(c_p); outs["np"].append(n_p); outs["mp"].append(m_p[..., 0])
            outs["Cs"].append(c_s); outs["ns"].append(n_s); outs["ms"].append(m_s[..., 0])
            w_out = a_w_out[j]
        else:
            qkv = matmul(xb, b_w_in, j, 3 * D_MODEL, BF16)
            h_p = attn_prompt(qkv, b_rel[j])
            qkv_s = qkv[N_PROMPT:]
            k_new = qkv_s[:, D_MODEL:2 * D_MODEL].reshape(DEC_BATCH, DEC_SEQ, D_MODEL)
            v_new = qkv_s[:, 2 * D_MODEL:].reshape(DEC_BATCH, DEC_SEQ, D_MODEL)
            zpad = jnp.zeros((DEC_BATCH, SAMPLE_KEYS_PAD - SAMPLE_KEYS, D_MODEL), BF16)
            kk = jnp.concatenate([cache_k[j].reshape(DEC_BATCH, REACH, D_MODEL).astype(BF16), k_new, zpad], axis=1)
            vv = jnp.concatenate([cache_v[j].reshape(DEC_BATCH, REACH, D_MODEL).astype(BF16), v_new, zpad], axis=1)
            h_s = attn_sample(qkv, kk, vv, b_rel[j])
            kv_p = qkv[:N_PROMPT].reshape(BATCH, SEQ, 3 * D_MODEL)
            tail = lambda c: lax.slice(kv_p, (0, SEQ - REACH, c * D_MODEL), (BATCH, SEQ, (c + 1) * D_MODEL)).astype(
                F32).reshape(BATCH, REACH, B_HEADS, B_DH)
            outs["kp"].append(tail(1)); outs["vp"].append(tail(2))
            outs["ks"].append(k_new.reshape(DEC_BATCH, DEC_SEQ, B_HEADS, B_DH).astype(F32))
            outs["vs"].append(v_new.reshape(DEC_BATCH, DEC_SEQ, B_HEADS, B_DH).astype(F32))
            w_out = b_w_out[j]
        x, xb, xp, wts, cls = mix_ln_router(h_p, h_s, w_out.astype(BF16), x, ln1_g[layer], ln1_b[layer],
                                            r_w_group[layer], r_b_group[layer], r_w_expert[layer], r_b_expert[layer])
        y = moe(xp, wts, cls, cast_layer(e_w_gate, layer), cast_layer(e_w_up, layer), cast_layer(e_w_down, layer))
        if layer < DEPTH - 1:
            x, xb = residual_ln(x, y, ln2_g[layer], ln2_b[layer])
        else:
            y_p, y_s = residual_ln_split(x, y, ln2_g[layer], ln2_b[layer], N_PROMPT)
    st = lambda k: jnp.stack(outs[k])
    return (y_p.reshape(BATCH, SEQ, D_MODEL), y_s.reshape(DEC_BATCH, DEC_SEQ, D_MODEL),
            st("Cp"), st("np"), st("mp"), st("kp"), st("vp"),
            st("Cs"), st("ns"), st("ms"), st("ks"), st("vs"))
```

```python
import functools

import jax
import jax.numpy as jnp
from jax import lax
from jax.experimental import pallas as pl
from jax.experimental.pallas import tpu as pltpu
from jax.experimental.pallas import tpu_sc as plsc

F32 = jnp.float32
BF16 = jnp.bfloat16

D_MODEL = 2048
BATCH = 8
SEQ = 4096
DEPTH = 4
DEC_BATCH = 32
DEC_SEQ = 32
PAST_LEN = 1024
CHUNK = 64
A_HEADS = 8
A_DK = 128
A_DV = D_MODEL // A_HEADS
GATE_CAP = 15.0
B_HEADS = 16
B_DH = D_MODEL // B_HEADS
PREV_CHUNKS = 8
REACH = PREV_CHUNKS * CHUNK
REL_CLIP = 256
N_GROUPS = 4
EXP_PER_GROUP = 4
N_EXPERTS = N_GROUPS * EXP_PER_GROUP
D_EXPERT = D_MODEL // 4
ALPHA = (2 * DEPTH) ** 0.25
LN_EPS = 1e-5
RMS_EPS = 1e-6

N_PROMPT = BATCH * SEQ
N_SAMPLE = DEC_BATCH * DEC_SEQ
N_TOK = N_PROMPT + N_SAMPLE

VMEM_LIMIT = 56 * 1024 * 1024
LANES = 128

PAIRS = ((0, 1), (0, 2), (0, 3), (1, 2), (1, 3), (2, 3))
N_CLASSES = N_GROUPS * len(PAIRS)
MOE_TILE = 256
MOE_TILES = -(-(N_TOK // MOE_TILE + N_CLASSES) // 32) * 32
MOE_ROWS = MOE_TILES * MOE_TILE

MLSTM_CHUNK = 128
ATT_TQ = 256
ATT_TK = ATT_TQ + REACH
NEG = -1e30

SC_WINDOW = 128
SC_TOKENS = 32


def _params(*sem):
    return pltpu.CompilerParams(dimension_semantics=sem, vmem_limit_bytes=VMEM_LIMIT)


def _mm_body(x_ref, w_ref, o_ref, wb):
    @pl.when(pl.program_id(1) == 0)
    def _():
        wb[...] = w_ref[...].astype(BF16)

    o_ref[...] = jnp.dot(x_ref[...], wb[...], preferred_element_type=F32).astype(o_ref.dtype)


def matmul(x, w_all, layer, m, out_dtype, tm=512, tn=1024):
    n, k = x.shape
    return pl.pallas_call(
        _mm_body,
        grid=(m // tn, n // tm),
        in_specs=[pl.BlockSpec((tm, k), lambda j, i: (i, 0)),
                  pl.BlockSpec((None, k, tn), lambda j, i: (layer, 0, j))],
        out_specs=pl.BlockSpec((tm, tn), lambda j, i: (i, j)),
        out_shape=jax.ShapeDtypeStruct((n, m), out_dtype),
        scratch_shapes=[pltpu.VMEM((k, tn), BF16)],
        compiler_params=_params("parallel", "arbitrary"),
        name="matmul",
    )(x, w_all)


def _cast_body(w_ref, o_ref):
    o_ref[...] = w_ref[...].astype(BF16)


def cast_layer(w_all, layer):
    e, a, b = w_all.shape[1:]
    return pl.pallas_call(
        _cast_body,
        grid=(e,),
        in_specs=[pl.BlockSpec((None, None, a, b), lambda i: (layer, i, 0, 0))],
        out_specs=pl.BlockSpec((None, a, b), lambda i: (i, 0, 0)),
        out_shape=jax.ShapeDtypeStruct((e, a, b), BF16),
        compiler_params=_params("parallel"),
        name="cast_layer",
    )(w_all)


def _split2(x):
    hi = x.astype(BF16)
    lo = (x - hi.astype(F32)).astype(BF16)
    return hi, lo


def _dot3(xh, xl, wh, wl, dims):
    dg = functools.partial(lax.dot_general, dimension_numbers=dims, preferred_element_type=F32)
    return dg(xh, wh) + (dg(xh, wl) + dg(xl, wh))


_NN = (((1,), (0,)), ((), ()))
_NT = (((1,), (1,)), ((), ()))
_TN = (((0,), (0,)), ((), ()))


def _gate_body(x_ref, w_ref, wt_ref, g_ref, gt_ref):
    xh, xl = _split2(x_ref[...])
    wh, wl = _split2(w_ref[...])
    wth, wtl = _split2(wt_ref[...])
    g_ref[...] = _dot3(xh, xl, wh, wl, _NN)
    gt_ref[...] = _dot3(wth, wtl, xh, xl, _NT)


def gate_preacts(x, w_gate, tm=512):
    n = x.shape[0]
    w = jnp.pad(w_gate, ((0, 0), (0, LANES - w_gate.shape[1])))
    return pl.pallas_call(
        _gate_body,
        grid=(n // tm,),
        in_specs=[pl.BlockSpec((tm, D_MODEL), lambda i: (i, 0)),
                  pl.BlockSpec((D_MODEL, LANES), lambda i: (0, 0)),
                  pl.BlockSpec((LANES, D_MODEL), lambda i: (0, 0))],
        out_specs=[pl.BlockSpec((tm, LANES), lambda i: (i, 0)),
                   pl.BlockSpec((LANES, tm), lambda i: (0, i))],
        out_shape=[jax.ShapeDtypeStruct((n, LANES), F32),
                   jax.ShapeDtypeStruct((LANES, n), F32)],
        compiler_params=_params("parallel"),
        name="gate_preacts",
    )(x, w, w.T)


def _log_sigmoid(x):
    return jnp.minimum(x, 0.0) - jnp.log(1.0 + jnp.exp(-jnp.abs(x)))


def _split3(x):
    a = x.astype(BF16)
    r = x - a.astype(F32)
    b = r.astype(BF16)
    c = (r - b.astype(F32)).astype(BF16)
    return a, b, c


def _mlstm_body(q_ref, k_ref, v_ref, o_ref, g_ref, gt_ref, bc_ref, br_ref, gain_ref,
                c0_ref, n0_ref, m0_ref, h_ref, c_ref, n_ref, m_ref, *, L):
    step = pl.program_id(1)

    @pl.when(step == 0)
    def _():
        c_ref[...] = c0_ref[...]
        n_ref[...] = n0_ref[...]
        m_ref[...] = m0_ref[...]

    row = lax.broadcasted_iota(jnp.int32, (L, L), 0)
    col = lax.broadcasted_iota(jnp.int32, (L, L), 1)
    causal = col <= row
    tril = jnp.where(causal, 1.0, 0.0).astype(BF16)
    triu = jnp.where(row <= col, 1.0, 0.0).astype(BF16)

    pre_c = GATE_CAP * jnp.tanh((g_ref[...] + bc_ref[...]) * (1.0 / GATE_CAP))
    lf_c = _log_sigmoid(pre_c)
    a, b, c = _split3(lf_c)
    dot = functools.partial(jnp.dot, preferred_element_type=F32)
    bt_c = dot(tril, a) + (dot(tril, b) + dot(tril, c))
    pre_r = GATE_CAP * jnp.tanh((gt_ref[...] + br_ref[...]) * (1.0 / GATE_CAP))
    lf_r = _log_sigmoid(pre_r)
    a, b, c = _split3(lf_r)
    bt_r = dot(a, triu) + (dot(b, triu) + dot(c, triu))

    scale = A_DK ** -0.5
    for h in range(A_HEADS):
        btc = bt_c[:, A_HEADS + h:A_HEADS + h + 1]
        igc = pre_c[:, h:h + 1]
        btr = bt_r[A_HEADS + h:A_HEADS + h + 1, :]
        igr = pre_r[h:h + 1, :]
        m_prev = m_ref[0, h:h + 1, 0:1]
        log_d = jnp.where(causal, btc - btr + igr, -jnp.inf)
        log_inter = btc + m_prev
        m_t = jnp.maximum(log_inter, jnp.max(log_d, axis=-1, keepdims=True))
        d_mat = jnp.exp(log_d - m_t)
        inter = jnp.exp(log_inter - m_t)
        qh = q_ref[:, h * A_DK:(h + 1) * A_DK]
        kh = k_ref[:, h * A_DK:(h + 1) * A_DK]
        vh = v_ref[:, h * A_DV:(h + 1) * A_DV]
        s = lax.dot_general(qh, kh, _NT, preferred_element_type=F32) * (d_mat * scale)
        c_h = c_ref[0, h]
        n_h = n_ref[0, h:h + 1, :]
        num = dot(s.astype(BF16), vh) + dot(qh, c_h.astype(BF16)) * inter
        qn = jnp.sum(qh.astype(F32) * n_h, axis=-1, keepdims=True)
        den = jnp.sum(s, axis=-1, keepdims=True) + inter * qn
        den = jnp.maximum(jnp.abs(den), jnp.exp(-m_t))
        hh = num / den
        hh = hh * lax.rsqrt(jnp.mean(hh * hh, axis=-1, keepdims=True) + RMS_EPS)
        og = o_ref[:, h * A_DV:(h + 1) * A_DV].astype(F32)
        hh = hh * gain_ref[:, h * A_DV:(h + 1) * A_DV] * jax.nn.sigmoid(og)
        h_ref[:, h * A_DV:(h + 1) * A_DV] = hh.astype(h_ref.dtype)

        m_new = m_t[L - 1:L, :]
        bt_last = btc[L - 1:L, :]
        w_c = jnp.exp(bt_last - btc + igc - m_new)
        decay = jnp.exp(bt_last + m_prev - m_new)
        wk = kh.astype(F32) * (w_c * scale)
        c_ref[0, h] = decay * c_h + lax.dot_general(wk.astype(BF16), vh, _TN, preferred_element_type=F32)
        n_ref[0, h:h + 1, :] = decay * n_h + jnp.sum(wk, axis=0, keepdims=True)
        m_ref[0, h:h + 1, :] = jnp.broadcast_to(m_new, (1, LANES))


def mlstm(proj, g, gt, b_gate, gain, c0, n0, m0, *, batch, seq, L, row0):
    nc = seq // L
    bias = jnp.pad(b_gate.astype(F32), (0, LANES - 2 * A_HEADS))
    row_blk = lambda b, c: row0 // L + b * nc + c
    if L % LANES:
        gt_spec = pl.BlockSpec((None, 2 * A_HEADS, L), lambda b, c: (b, 0, 0))
    else:
        gt_spec = pl.BlockSpec((2 * A_HEADS, L), lambda b, c: (0, row_blk(b, c)))
    st = lambda *s: pl.BlockSpec((1,) + s, lambda b, c: (b,) + (0,) * len(s))
    in_specs = [pl.BlockSpec((L, A_HEADS * A_DK), lambda b, c: (row_blk(b, c), 0)),
                pl.BlockSpec((L, A_HEADS * A_DK), lambda b, c: (row_blk(b, c), 1)),
                pl.BlockSpec((L, D_MODEL), lambda b, c: (row_blk(b, c), 1)),
                pl.BlockSpec((L, D_MODEL), lambda b, c: (row_blk(b, c), 2)),
                pl.BlockSpec((L, LANES), lambda b, c: (row_blk(b, c), 0)),
                gt_spec,
                pl.BlockSpec((1, LANES), lambda b, c: (0, 0)),
                pl.BlockSpec((2 * A_HEADS, 1), lambda b, c: (0, 0)),
                pl.BlockSpec((1, D_MODEL), lambda b, c: (0, 0)),
                st(A_HEADS, A_DK, A_DV), st(A_HEADS, A_DK), st(A_HEADS, LANES)]
    args = [proj, proj, proj, proj, g, gt, bias[None, :], bias[:2 * A_HEADS, None], gain.astype(F32)[None, :],
            c0, n0, jnp.broadcast_to(m0[..., None], m0.shape + (LANES,))]
    return pl.pallas_call(
        functools.partial(_mlstm_body, L=L),
        grid=(batch, nc),
        in_specs=in_specs,
        out_specs=[pl.BlockSpec((L, D_MODEL), lambda b, c: (b * nc + c, 0)),
                   st(A_HEADS, A_DK, A_DV), st(A_HEADS, A_DK), st(A_HEADS, LANES)],
        out_shape=[jax.ShapeDtypeStruct((batch * seq, D_MODEL), BF16),
                   jax.ShapeDtypeStruct((batch, A_HEADS, A_DK, A_DV), F32),
                   jax.ShapeDtypeStruct((batch, A_HEADS, A_DK), F32),
                   jax.ShapeDtypeStruct((batch, A_HEADS, LANES), F32)],
        compiler_params=_params("parallel", "arbitrary"),
        name="mlstm",
    )(*args)


LOG2E = 1.4426950408889634
Q_SCALE = B_DH ** -0.5 * LOG2E


def _scale_q(q):
    return (q.astype(F32) * Q_SCALE).astype(BF16)


def _attn_prompt_body(q_ref, k_ref, v_ref, bias_ref, o_ref, kt_ref, vt_ref, qs, kpad, vpad):
    kt_ref[0] = k_ref[SEQ - REACH:, :].astype(F32)
    vt_ref[0] = v_ref[SEQ - REACH:, :].astype(F32)
    qs[...] = _scale_q(q_ref[...])
    kpad[0:REACH, :] = jnp.zeros((REACH, B_DH), BF16)
    kpad[REACH:, :] = k_ref[...]
    vpad[0:REACH, 0:B_DH] = jnp.zeros((REACH, B_DH), BF16)
    vpad[REACH:, 0:B_DH] = v_ref[...]
    vpad[:, B_DH:] = jnp.ones((SEQ + REACH, B_DH), BF16)

    def tile(t, mask_front):
        r0 = t * ATT_TQ if isinstance(t, int) else pl.multiple_of(t * ATT_TQ, ATT_TQ)
        q = qs[pl.ds(r0, ATT_TQ), :]
        kb = kpad[pl.ds(r0, ATT_TK), :]
        vb = vpad[pl.ds(r0, ATT_TK), :]
        s = lax.dot_general(q, kb, _NT, preferred_element_type=F32) + bias_ref[0]
        if mask_front:
            j = lax.broadcasted_iota(jnp.int32, (ATT_TQ, ATT_TK), 1)
            s = jnp.where(j + r0 >= REACH, s, NEG)
        p = jnp.exp2(s - jnp.max(s, axis=-1, keepdims=True)).astype(BF16)
        acc = jnp.dot(p, vb, preferred_element_type=F32)
        o_ref[pl.ds(r0, ATT_TQ), :] = (acc[:, :B_DH] / acc[:, B_DH:]).astype(o_ref.dtype)

    n_front = REACH // ATT_TQ
    for t in range(n_front):
        tile(t, True)

    def loop(t, carry):
        tile(t, False)
        return carry

    lax.fori_loop(n_front, SEQ // ATT_TQ, loop, 0, unroll=14)


def band_bias(rel_table, nq, nk):
    w = nq + nk
    rel = jnp.clip(REACH + nq - 1 - jnp.arange(w), -REL_CLIP, REL_CLIP) + REL_CLIP
    r = rel_table.astype(F32)[:, rel]
    heads = r.shape[0]
    skew = jnp.broadcast_to(r[:, None, :], (heads, nq, w)).reshape(heads, nq * w)
    skew = skew[:, :nq * (w - 1)].reshape(heads, nq, w - 1)
    bias = skew[:, :, nq - 1:nq - 1 + nk]
    i = jnp.arange(nq)[:, None]
    j = jnp.arange(nk)[None, :]
    dc = j // CHUNK - i // CHUNK
    allowed = (dc >= 0) & (dc <= PREV_CHUNKS)
    return jnp.where(allowed[None], bias * LOG2E, NEG)


def attn_prompt(qkv, rel_table):
    bias = band_bias(rel_table, ATT_TQ, ATT_TK)
    tail = pl.BlockSpec((1, REACH, B_DH), lambda b, h: (b, 0, h))
    return pl.pallas_call(
        _attn_prompt_body,
        grid=(BATCH, B_HEADS),
        in_specs=[pl.BlockSpec((SEQ, B_DH), lambda b, h: (b, h)),
                  pl.BlockSpec((SEQ, B_DH), lambda b, h: (b, B_HEADS + h)),
                  pl.BlockSpec((SEQ, B_DH), lambda b, h: (b, 2 * B_HEADS + h)),
                  pl.BlockSpec((1, ATT_TQ, ATT_TK), lambda b, h: (h, 0, 0))],
        out_specs=[pl.BlockSpec((SEQ, B_DH), lambda b, h: (b, h)), tail, tail],
        out_shape=[jax.ShapeDtypeStruct((N_PROMPT, D_MODEL), BF16),
                   jax.ShapeDtypeStruct((BATCH, REACH, D_MODEL), F32),
                   jax.ShapeDtypeStruct((BATCH, REACH, D_MODEL), F32)],
        scratch_shapes=[pltpu.VMEM((SEQ, B_DH), BF16),
                        pltpu.VMEM((SEQ + REACH, B_DH), BF16),
                        pltpu.VMEM((SEQ + REACH, 2 * B_DH), BF16)],
        compiler_params=_params("parallel", "parallel"),
        name="attn_prompt",
    )(qkv, qkv, qkv, bias)


def _attn_sample_body(q_ref, kn_ref, vn_ref, ck_ref, cv_ref, bias_c_ref, bias_n_ref, o_ref, ks_ref, vs_ref):
    ks_ref[0] = kn_ref[...].astype(F32)
    vs_ref[0] = vn_ref[...].astype(F32)
    for h in range(B_HEADS):
        sl = slice(h * B_DH, (h + 1) * B_DH)
        q = _scale_q(q_ref[:, sl])
        kc = ck_ref[pl.ds(h, REACH, stride=B_HEADS), :].astype(BF16)
        vc = cv_ref[pl.ds(h, REACH, stride=B_HEADS), :].astype(BF16)
        s_c = lax.dot_general(q, kc, _NT, preferred_element_type=F32) + bias_c_ref[h]
        s_n = lax.dot_general(q, kn_ref[:, sl], _NT, preferred_element_type=F32) + bias_n_ref[h]
        m = jnp.maximum(jnp.max(s_c, axis=-1, keepdims=True), jnp.max(s_n, axis=-1, keepdims=True))
        p_c = jnp.exp2(s_c - m)
        p_n = jnp.exp2(s_n - m)
        l = jnp.sum(p_c, axis=-1, keepdims=True) + jnp.sum(p_n, axis=-1, keepdims=True)
        dot = functools.partial(jnp.dot, preferred_element_type=F32)
        o_ref[:, sl] = ((dot(p_c.astype(BF16), vc) + dot(p_n.astype(BF16), vn_ref[:, sl])) / l).astype(o_ref.dtype)


def attn_sample(qkv, cache_k, cache_v, layer, rel_table):
    bias = band_bias(rel_table, DEC_SEQ, REACH + DEC_SEQ)
    cache_rows = REACH * B_HEADS
    ck = cache_k.reshape(cache_k.shape[0], DEC_BATCH, cache_rows, B_DH)
    cv = cache_v.reshape(cache_v.shape[0], DEC_BATCH, cache_rows, B_DH)
    row = lambda c: pl.BlockSpec((DEC_SEQ, D_MODEL), lambda b: (N_PROMPT // DEC_SEQ + b, c))
    cache = pl.BlockSpec((None, None, cache_rows, B_DH), lambda b: (layer, b, 0, 0))
    new = pl.BlockSpec((1, DEC_SEQ, D_MODEL), lambda b: (b, 0, 0))
    return pl.pallas_call(
        _attn_sample_body,
        grid=(DEC_BATCH,),
        in_specs=[row(0), row(1), row(2), cache, cache,
                  pl.BlockSpec((B_HEADS, DEC_SEQ, REACH), lambda b: (0, 0, 0)),
                  pl.BlockSpec((B_HEADS, DEC_SEQ, DEC_SEQ), lambda b: (0, 0, 0))],
        out_specs=[pl.BlockSpec((DEC_SEQ, D_MODEL), lambda b: (b, 0)), new, new],
        out_shape=[jax.ShapeDtypeStruct((N_SAMPLE, D_MODEL), BF16),
                   jax.ShapeDtypeStruct((DEC_BATCH, DEC_SEQ, D_MODEL), F32),
                   jax.ShapeDtypeStruct((DEC_BATCH, DEC_SEQ, D_MODEL), F32)],
        compiler_params=_params("parallel"),
        name="attn_sample",
    )(qkv, qkv, qkv, ck, cv, bias[:, :, :REACH], bias[:, :, REACH:])


def _ln(z, g, b):
    mu = jnp.mean(z, axis=-1, keepdims=True)
    zc = z - mu
    var = jnp.mean(zc * zc, axis=-1, keepdims=True)
    return zc * lax.rsqrt(var + LN_EPS) * g + b


PACK_ROWS = D_MODEL // 2 // LANES
HI_MASK = -65536


def _store_packed(ref, v, token0=0):
    t = v.shape[0]
    half = D_MODEL // 2
    hi = lax.bitcast_convert_type(v[:, :half].astype(BF16).astype(F32), jnp.int32)
    lo = lax.bitcast_convert_type(v[:, half:].astype(BF16).astype(F32), jnp.int32)
    words = (hi & HI_MASK) | lax.shift_right_logical(lo, 16)
    for a in range(PACK_ROWS):
        ref[pl.ds(token0 * PACK_ROWS + a, t, stride=PACK_ROWS), :] = words[:, a * LANES:(a + 1) * LANES]


def _load_packed(ref, t):
    words = jnp.concatenate([ref[pl.ds(a, t, stride=PACK_ROWS), :] for a in range(PACK_ROWS)], axis=1)
    hi = lax.bitcast_convert_type(words & HI_MASK, F32)
    lo = lax.bitcast_convert_type(lax.shift_left(words, 16), F32)
    return jnp.concatenate([hi, lo], axis=1)


def _mix_ln_router_body(hp_ref, hs_ref, w_ref, x_ref, g_ref, b_ref, rw_ref, rb_ref,
                        xo_ref, xbo_ref, xp_ref, wt_ref, cls_ref, *, prompt_tiles):
    h = jnp.where(pl.program_id(0) < prompt_tiles, hp_ref[...], hs_ref[...])
    y = jnp.dot(h, w_ref[...], preferred_element_type=F32)
    xn = _ln(ALPHA * x_ref[...] + y, g_ref[...], b_ref[...])
    xo_ref[...] = xn
    xbo_ref[...] = xn.astype(BF16)
    _store_packed(xp_ref, xn)
    _route(xn, rw_ref, rb_ref, wt_ref, cls_ref)


def mix_ln_router(h_p, h_s, w, x, g, b, w_group, b_group, w_expert, b_expert, tm=256):
    n = x.shape[0]
    prompt_tiles = h_p.shape[0] // tm
    sample_tiles = h_s.shape[0] // tm
    assert prompt_tiles * tm == h_p.shape[0] and (prompt_tiles + sample_tiles) * tm == n
    hp_spec = pl.BlockSpec((tm, D_MODEL), lambda i: (jnp.minimum(i, prompt_tiles - 1), 0))
    hs_spec = pl.BlockSpec((tm, D_MODEL), lambda i: (jnp.maximum(i - prompt_tiles, 0), 0))
    rw = jnp.pad(jnp.concatenate([w_group, w_expert], axis=1), ((0, 0), (0, LANES - N_GROUPS - N_EXPERTS)))
    rb = jnp.pad(jnp.concatenate([b_group, b_expert]).astype(F32), (0, LANES - N_GROUPS - N_EXPERTS))
    row = pl.BlockSpec((tm, D_MODEL), lambda i: (i, 0))
    vec = pl.BlockSpec((1, D_MODEL), lambda i: (0, 0))
    lane_row = pl.BlockSpec((tm, LANES), lambda i: (i, 0))
    return pl.pallas_call(
        functools.partial(_mix_ln_router_body, prompt_tiles=prompt_tiles),
        grid=(n // tm,),
        in_specs=[hp_spec, hs_spec, pl.BlockSpec((D_MODEL, D_MODEL), lambda i: (0, 0)), row, vec, vec,
                  pl.BlockSpec((D_MODEL, LANES), lambda i: (0, 0)), pl.BlockSpec((1, LANES), lambda i: (0, 0))],
        out_specs=[row, row, pl.BlockSpec((tm * PACK_ROWS, LANES), lambda i: (i, 0)), lane_row, lane_row],
        out_shape=[jax.ShapeDtypeStruct((n, D_MODEL), F32), jax.ShapeDtypeStruct((n, D_MODEL), BF16),
                   jax.ShapeDtypeStruct((n * PACK_ROWS, LANES), jnp.int32),
                   jax.ShapeDtypeStruct((n, LANES), F32), jax.ShapeDtypeStruct((n, LANES), jnp.int32)],
        compiler_params=_params("parallel"),
        name="mix_ln_router",
    )(h_p, h_s, w, x, g[None, :], b[None, :], rw, rb[None, :])


def _add_ln_body(x_ref, y_ref, g_ref, b_ref, xo_ref, xbo_ref):
    y = _load_packed(y_ref, x_ref.shape[0])
    xn = _ln(ALPHA * x_ref[...] + y, g_ref[...], b_ref[...])
    xo_ref[...] = xn
    xbo_ref[...] = xn.astype(BF16)


def residual_ln(x, y_packed, g, b, tm=512):
    n = x.shape[0]
    row = pl.BlockSpec((tm, D_MODEL), lambda i: (i, 0))
    vec = pl.BlockSpec((1, D_MODEL), lambda i: (0, 0))
    return pl.pallas_call(
        _add_ln_body,
        grid=(n // tm,),
        in_specs=[row, pl.BlockSpec((tm * PACK_ROWS, LANES), lambda i: (i, 0)), vec, vec],
        out_specs=[row, row],
        out_shape=[jax.ShapeDtypeStruct((n, D_MODEL), F32), jax.ShapeDtypeStruct((n, D_MODEL), BF16)],
        compiler_params=_params("parallel"),
        name="residual_ln",
    )(x, y_packed, g[None, :], b[None, :])


def _add_ln_split_body(x_ref, y_ref, g_ref, b_ref, op_ref, os_ref, *, prompt_tiles):
    y = _load_packed(y_ref, x_ref.shape[0])
    xn = _ln(ALPHA * x_ref[...] + y, g_ref[...], b_ref[...])
    i = pl.program_id(0)

    @pl.when(i < prompt_tiles)
    def _():
        op_ref[...] = xn

    @pl.when(i >= prompt_tiles)
    def _():
        os_ref[...] = xn


def residual_ln_split(x, y_packed, g, b, n_prompt, tm=512):
    n = x.shape[0]
    prompt_tiles = n_prompt // tm
    assert prompt_tiles * tm == n_prompt and n % tm == 0
    row = pl.BlockSpec((tm, D_MODEL), lambda i: (i, 0))
    vec = pl.BlockSpec((1, D_MODEL), lambda i: (0, 0))
    return pl.pallas_call(
        functools.partial(_add_ln_split_body, prompt_tiles=prompt_tiles),
        grid=(n // tm,),
        in_specs=[row, pl.BlockSpec((tm * PACK_ROWS, LANES), lambda i: (i, 0)), vec, vec],
        out_specs=[pl.BlockSpec((tm, D_MODEL), lambda i: (jnp.minimum(i, prompt_tiles - 1), 0)),
                   pl.BlockSpec((tm, D_MODEL), lambda i: (jnp.maximum(i - prompt_tiles, 0), 0))],
        out_shape=[jax.ShapeDtypeStruct((n_prompt, D_MODEL), F32),
                   jax.ShapeDtypeStruct((n - n_prompt, D_MODEL), F32)],
        compiler_params=_params("arbitrary"),
        name="residual_ln_split",
    )(x, y_packed, g[None, :], b[None, :])


ROUTER_E0 = N_GROUPS


def _route(x, w_ref, b_ref, wt_ref, cls_ref):
    t = x.shape[0]
    xh, xl = _split2(x)
    wh, wl = _split2(w_ref[...])
    r = jnp.dot(jnp.concatenate([xh, xl], axis=0), jnp.concatenate([wh, wl], axis=1), preferred_element_type=F32)
    logits = (r[:t, :LANES] + (r[:t, LANES:] + r[t:, :LANES]) + r[t:, LANES:]) + b_ref[...]
    lane = lax.broadcasted_iota(jnp.int32, logits.shape, 1).astype(F32)
    big = float(LANES)
    rmax = lambda a: jnp.max(a, axis=-1, keepdims=True)
    rmin = lambda a: jnp.min(a, axis=-1, keepdims=True)
    is_g = lane < N_GROUPS
    gl = jnp.where(is_g, logits, -jnp.inf)
    gmax = rmax(gl)
    gsel = rmin(jnp.where(gl == gmax, lane, big))
    gsum = jnp.sum(jnp.where(is_g, jnp.exp(logits - gmax), 0.0), axis=-1, keepdims=True)
    g_w = 1.0 / gsum
    e_lo = ROUTER_E0 + EXP_PER_GROUP * gsel
    in_grp = (lane >= e_lo) & (lane < e_lo + EXP_PER_GROUP)
    el = jnp.where(in_grp, logits, -jnp.inf)
    e1 = rmax(el)
    i1 = rmin(jnp.where(el == e1, lane, big))
    el2 = jnp.where(lane == i1, -jnp.inf, el)
    e2 = rmax(el2)
    i2 = rmin(jnp.where(el2 == e2, lane, big))
    t = jnp.exp(e2 - e1)
    p1 = 1.0 / (1.0 + t)
    w1 = p1 * g_w
    w2 = (t * p1) * g_w
    a1 = i1 - e_lo
    a2 = i2 - e_lo
    first_low = a1 < a2
    lo = jnp.where(first_low, a1, a2)
    hi = jnp.where(first_low, a2, a1)
    w_lo = jnp.where(first_low, w1, w2)
    w_hi = jnp.where(first_low, w2, w1)
    off = jnp.where(lo == 0.0, 0.0, jnp.where(lo == 1.0, 3.0, 5.0))
    cls = gsel * float(len(PAIRS)) + off + (hi - lo - 1.0)
    wt_ref[...] = jnp.where(lane < LANES // 2, w_lo, w_hi)
    cls_ref[...] = jnp.broadcast_to(cls, logits.shape).astype(jnp.int32)


def _expert_body(ea_ref, eb_ref, valid_ref, x_ref, wt_ref, wga_ref, wua_ref, wda_ref,
                 wgb_ref, wub_ref, wdb_ref, y_ref):
    i = pl.program_id(0)

    @pl.when(valid_ref[i] == 0)
    def _():
        y_ref[...] = jnp.zeros_like(y_ref)

    @pl.when(valid_ref[i] != 0)
    def _():
        x = _load_packed(x_ref, MOE_TILE).astype(BF16)
        dot = functools.partial(jnp.dot, preferred_element_type=F32)

        def ffn(wg, wu, wd):
            h = jax.nn.silu(dot(x, wg[...])) * dot(x, wu[...])
            return dot(h.astype(BF16), wd[...])

        w_lo = wt_ref[:, 0:1]
        w_hi = wt_ref[:, LANES // 2:LANES // 2 + 1]
        _store_packed(y_ref, w_lo * ffn(wga_ref, wua_ref, wda_ref) + w_hi * ffn(wgb_ref, wub_ref, wdb_ref))


def experts(xs, wts, tile_ea, tile_eb, tile_valid, w_gate, w_up, w_down):
    wspec = lambda shape, which: pl.BlockSpec(
        (None,) + shape, lambda i, ea, eb, va: ((ea, eb)[which][i], 0, 0))
    up = (D_MODEL, D_EXPERT)
    down = (D_EXPERT, D_MODEL)
    packed = pl.BlockSpec((MOE_TILE * PACK_ROWS, LANES), lambda i, ea, eb, va: (i, 0))
    grid_spec = pltpu.PrefetchScalarGridSpec(
        num_scalar_prefetch=3,
        grid=(MOE_TILES,),
        in_specs=[packed,
                  pl.BlockSpec((MOE_TILE, LANES), lambda i, ea, eb, va: (i, 0)),
                  wspec(up, 0), wspec(up, 0), wspec(down, 0),
                  wspec(up, 1), wspec(up, 1), wspec(down, 1)],
        out_specs=packed,
    )
    return pl.pallas_call(
        _expert_body,
        grid_spec=grid_spec,
        out_shape=jax.ShapeDtypeStruct((MOE_ROWS * PACK_ROWS, LANES), jnp.int32),
        compiler_params=_params("arbitrary"),
        name="experts",
    )(tile_ea, tile_eb, tile_valid, xs, wts, w_gate, w_up, w_down, w_gate, w_up, w_down)


def sc_gather_rows(x, idx):
    n, d = x.shape
    n_out = idx.shape[0]
    mesh = plsc.VectorSubcoreMesh(core_axis_name="core", subcore_axis_name="subcore")

    @functools.partial(
        pl.kernel,
        out_type=jax.ShapeDtypeStruct((n_out, d), x.dtype),
        mesh=mesh,
    )
    def gather(x_hbm, i_hbm, o_hbm):
        def body(i_vmem, o_vmem):
            pltpu.sync_copy(x_hbm.at[i_vmem.at[0]], o_vmem)

        pltpu.emit_pipeline(
            body,
            grid=(n_out // SC_WINDOW,),
            in_specs=[pl.BlockSpec((1, SC_WINDOW), lambda i: (0, i))],
            out_specs=[pl.BlockSpec((SC_WINDOW, d), lambda i: (i, 0))],
            core_axis_name=("core", "subcore"),
            dimension_semantics=(pltpu.PARALLEL,),
        )(i_hbm, o_hbm)

    return gather(x, idx.reshape(1, n_out))


def sc_gather_tokens(x, idx):
    n = x.shape[0] // PACK_ROWS
    p = idx.shape[0]
    info = plsc.get_sparse_core_info()
    n_workers = info.num_cores * info.num_subcores
    per_worker = p // n_workers
    steps = per_worker // SC_TOKENS
    assert per_worker * n_workers == p and steps * SC_TOKENS == per_worker
    mesh = plsc.VectorSubcoreMesh(core_axis_name="core", subcore_axis_name="subcore")

    @functools.partial(
        pl.kernel,
        out_type=jax.ShapeDtypeStruct((p, PACK_ROWS, LANES), x.dtype),
        mesh=mesh,
        scratch_types=[pltpu.VMEM((per_worker,), jnp.int32),
                       pltpu.VMEM((SC_TOKENS, PACK_ROWS, LANES), x.dtype)],
    )
    def gather(x_hbm, i_hbm, o_hbm, idx_v, buf):
        worker = lax.axis_index("subcore") * info.num_cores + lax.axis_index("core")
        base = worker * per_worker
        pltpu.sync_copy(i_hbm.at[pl.ds(base, per_worker)], idx_v)

        @pl.loop(0, steps)
        def _(s):
            r0 = s * SC_TOKENS
            pltpu.sync_copy(x_hbm.at[idx_v.at[pl.ds(r0, SC_TOKENS)]], buf)
            pltpu.sync_copy(buf, o_hbm.at[pl.ds(base + r0, SC_TOKENS)])

    return gather(x.reshape(n, PACK_ROWS, LANES), idx).reshape(p * PACK_ROWS, LANES)


_CLASS_EA = [g * EXP_PER_GROUP + lo for g in range(N_GROUPS) for lo, hi in PAIRS]
_CLASS_EB = [g * EXP_PER_GROUP + hi for g in range(N_GROUPS) for lo, hi in PAIRS]


def dispatch_plan(cls):
    onehot = (cls[:, None] == jnp.arange(N_CLASSES, dtype=jnp.int32)[None, :]).astype(jnp.int32)
    csum = jnp.cumsum(onehot, axis=0)
    rank = jnp.sum(csum * onehot, axis=1) - 1
    counts = csum[-1]
    tiles_per = (counts + MOE_TILE - 1) // MOE_TILE
    tile_end = jnp.cumsum(tiles_per)
    tile_start = tile_end - tiles_per
    pos = jnp.sum(onehot * tile_start[None, :], axis=1) * MOE_TILE + rank
    n = cls.shape[0]
    src = (jnp.arange(MOE_ROWS, dtype=jnp.int32) % n).at[pos].set(jnp.arange(n, dtype=jnp.int32))
    t = jnp.arange(MOE_TILES, dtype=jnp.int32)
    tile_cls = jnp.sum((t[:, None] >= tile_end[None, :]).astype(jnp.int32), axis=1)
    valid = (tile_cls < N_CLASSES).astype(jnp.int32)
    last_cls = jnp.max(jnp.where(counts > 0, jnp.arange(N_CLASSES, dtype=jnp.int32), 0))
    tile_cls = jnp.where(valid == 1, tile_cls, last_cls)
    ea = jnp.asarray(_CLASS_EA, jnp.int32)[tile_cls]
    eb = jnp.asarray(_CLASS_EB, jnp.int32)[tile_cls]
    return pos, src, ea, eb, valid


def moe(xp, wts, cls, w_gate, w_up, w_down):
    pos, src, ea, eb, valid = dispatch_plan(cls[:, 0])
    xs = sc_gather_tokens(xp, src)
    ws = sc_gather_rows(wts, src)
    ys = experts(xs, ws, ea, eb, valid, w_gate, w_up, w_down)
    return sc_gather_tokens(ys, pos)


def _rows(a):
    return a.reshape(-1, a.shape[-1])


def kernel(x_prompt, x_sample, state_C, state_n, state_m, cache_k, cache_v, a_w_in, a_b_gate, a_norm, a_w_out, b_w_in, b_rel, b_w_out, ln1_g, ln1_b, ln2_g, ln2_b, r_w_group, r_b_group, r_w_expert, r_b_expert, e_w_gate, e_w_up, e_w_down):
    x = jnp.concatenate([_rows(x_prompt), _rows(x_sample)], axis=0)
    xb = x.astype(BF16)
    hk = A_HEADS * A_DK
    n_main = 2 * hk + 2 * D_MODEL
    outs = {k: [] for k in ("Cp", "np", "mp", "kp", "vp", "Cs", "ns", "ms", "ks", "vs")}
    for layer in range(DEPTH):
        j = layer // 2
        if layer % 2 == 0:
            proj = matmul(xb, a_w_in, j, n_main, BF16)
            g, gt = gate_preacts(x, a_w_in[j, :, n_main:])
            zc = jnp.zeros((BATCH, A_HEADS, A_DK, A_DV), F32)
            zn = jnp.zeros((BATCH, A_HEADS, A_DK), F32)
            zm = jnp.zeros((BATCH, A_HEADS), F32)
            gt = gt[:2 * A_HEADS]
            h_p, c_p, n_p, m_p = mlstm(proj, g, gt, a_b_gate[j], a_norm[j], zc, zn, zm,
                                       batch=BATCH, seq=SEQ, L=MLSTM_CHUNK, row0=0)
            gt_s = gt[:, N_PROMPT:].reshape(2 * A_HEADS, DEC_BATCH, DEC_SEQ).transpose(1, 0, 2)
            h_s, c_s, n_s, m_s = mlstm(proj, g, gt_s, a_b_gate[j], a_norm[j],
                                       state_C[j], state_n[j], state_m[j],
                                       batch=DEC_BATCH, seq=DEC_SEQ, L=DEC_SEQ, row0=N_PROMPT)
            outs["Cp"].append(c_p); outs["np"].append(n_p); outs["mp"].append(m_p[..., 0])
            outs["Cs"].append(c_s); outs["ns"].append(n_s); outs["ms"].append(m_s[..., 0])
            w_out = a_w_out[j]
        else:
            qkv = matmul(xb, b_w_in, j, 3 * D_MODEL, BF16)
            h_p, k_p, v_p = attn_prompt(qkv, b_rel[j])
            h_s, k_s, v_s = attn_sample(qkv, cache_k, cache_v, j, b_rel[j])
            heads = lambda a: a.reshape(a.shape[:2] + (B_HEADS, B_DH))
            outs["kp"].append(heads(k_p)); outs["vp"].append(heads(v_p))
            outs["ks"].append(heads(k_s)); outs["vs"].append(heads(v_s))
            w_out = b_w_out[j]
        x, xb, xp, wts, cls = mix_ln_router(h_p, h_s, w_out.astype(BF16), x, ln1_g[layer], ln1_b[layer],
                                            r_w_group[layer], r_b_group[layer], r_w_expert[layer], r_b_expert[layer])
        y = moe(xp, wts, cls, cast_layer(e_w_gate, layer), cast_layer(e_w_up, layer), cast_layer(e_w_down, layer))
        if layer < DEPTH - 1:
            x, xb = residual_ln(x, y, ln2_g[layer], ln2_b[layer])
        else:
            y_p, y_s = residual_ln_split(x, y, ln2_g[layer], ln2_b[layer], N_PROMPT)
    st = lambda k: jnp.stack(outs[k])
    return (y_p.reshape(BATCH, SEQ, D_MODEL), y_s.reshape(DEC_BATCH, DEC_SEQ, D_MODEL),
            st("Cp"), st("np"), st("mp"), st("kp"), st("vp"),
            st("Cs"), st("ns"), st("ms"), st("ks"), st("vs"))
```

```python
import functools

import jax
import jax.numpy as jnp
from jax import lax
from jax.experimental import pallas as pl
from jax.experimental.pallas import tpu as pltpu
from jax.experimental.pallas import tpu_sc as plsc

F32 = jnp.float32
BF16 = jnp.bfloat16

D_MODEL = 2048
BATCH = 8
SEQ = 4096
DEPTH = 4
DEC_BATCH = 32
DEC_SEQ = 32
PAST_LEN = 1024
CHUNK = 64
A_HEADS = 8
A_DK = 128
A_DV = D_MODEL // A_HEADS
GATE_CAP = 15.0
B_HEADS = 16
B_DH = D_MODEL // B_HEADS
PREV_CHUNKS = 8
REACH = PREV_CHUNKS * CHUNK
REL_CLIP = 256
N_GROUPS = 4
EXP_PER_GROUP = 4
N_EXPERTS = N_GROUPS * EXP_PER_GROUP
D_EXPERT = D_MODEL // 4
ALPHA = (2 * DEPTH) ** 0.25
LN_EPS = 1e-5
RMS_EPS = 1e-6

N_PROMPT = BATCH * SEQ
N_SAMPLE = DEC_BATCH * DEC_SEQ
N_TOK = N_PROMPT + N_SAMPLE

VMEM_LIMIT = 56 * 1024 * 1024
LANES = 128

PAIRS = ((0, 1), (0, 2), (0, 3), (1, 2), (1, 3), (2, 3))
N_CLASSES = N_GROUPS * len(PAIRS)
MOE_TILE = 256
MOE_TILES = -(-(N_TOK // MOE_TILE + N_CLASSES) // 32) * 32
MOE_ROWS = MOE_TILES * MOE_TILE

MLSTM_CHUNK = 128
ATT_TQ = 256
ATT_TK = ATT_TQ + REACH
NEG = -1e30

SC_WINDOW = 128
SC_TOKENS = 32


def _params(*sem):
    return pltpu.CompilerParams(dimension_semantics=sem, vmem_limit_bytes=VMEM_LIMIT)


def _mm_body(x_ref, w_ref, o_ref, wb):
    @pl.when(pl.program_id(1) == 0)
    def _():
        wb[...] = w_ref[...].astype(BF16)

    o_ref[...] = jnp.dot(x_ref[...], wb[...], preferred_element_type=F32).astype(o_ref.dtype)


def matmul(x, w_all, layer, m, out_dtype, tm=512, tn=1024):
    n, k = x.shape
    return pl.pallas_call(
        _mm_body,
        grid=(m // tn, n // tm),
        in_specs=[pl.BlockSpec((tm, k), lambda j, i: (i, 0)),
                  pl.BlockSpec((None, k, tn), lambda j, i: (layer, 0, j))],
        out_specs=pl.BlockSpec((tm, tn), lambda j, i: (i, j)),
        out_shape=jax.ShapeDtypeStruct((n, m), out_dtype),
        scratch_shapes=[pltpu.VMEM((k, tn), BF16)],
        compiler_params=_params("parallel", "arbitrary"),
        name="matmul",
    )(x, w_all)


def _cast_body(w_ref, o_ref):
    o_ref[...] = w_ref[...].astype(BF16)


def cast_layer(w_all, layer):
    e, a, b = w_all.shape[1:]
    return pl.pallas_call(
        _cast_body,
        grid=(e,),
        in_specs=[pl.BlockSpec((None, None, a, b), lambda i: (layer, i, 0, 0))],
        out_specs=pl.BlockSpec((None, a, b), lambda i: (i, 0, 0)),
        out_shape=jax.ShapeDtypeStruct((e, a, b), BF16),
        compiler_params=_params("parallel"),
        name="cast_layer",
    )(w_all)


def _split2(x):
    hi = x.astype(BF16)
    lo = (x - hi.astype(F32)).astype(BF16)
    return hi, lo


def _dot3(xh, xl, wh, wl, dims):
    dg = functools.partial(lax.dot_general, dimension_numbers=dims, preferred_element_type=F32)
    return dg(xh, wh) + (dg(xh, wl) + dg(xl, wh))


_NN = (((1,), (0,)), ((), ()))
_NT = (((1,), (1,)), ((), ()))
_TN = (((0,), (0,)), ((), ()))


def _dot_hilo(x, w):
    t = x.shape[0]
    xh, xl = _split2(x)
    wh, wl = _split2(w)
    r = jnp.dot(jnp.concatenate([xh, xl], axis=0), jnp.concatenate([wh, wl], axis=1), preferred_element_type=F32)
    return r[:t, :LANES] + (r[:t, LANES:] + r[t:, :LANES]) + r[t:, LANES:]


def _gate_body(x_ref, w_ref, g_ref):
    g_ref[...] = _dot_hilo(x_ref[...], w_ref[...])


def gate_preacts(x, w_gate, tm=512):
    n = x.shape[0]
    w = jnp.pad(w_gate, ((0, 0), (0, LANES - w_gate.shape[1])))
    return pl.pallas_call(
        _gate_body,
        grid=(n // tm,),
        in_specs=[pl.BlockSpec((tm, D_MODEL), lambda i: (i, 0)),
                  pl.BlockSpec((D_MODEL, LANES), lambda i: (0, 0))],
        out_specs=pl.BlockSpec((tm, LANES), lambda i: (i, 0)),
        out_shape=jax.ShapeDtypeStruct((n, LANES), F32),
        compiler_params=_params("parallel"),
        name="gate_preacts",
    )(x, w)


def _log_sigmoid(x):
    return jnp.minimum(x, 0.0) - jnp.log(1.0 + jnp.exp(-jnp.abs(x)))


def _split3(x):
    a = x.astype(BF16)
    r = x - a.astype(F32)
    b = r.astype(BF16)
    c = (r - b.astype(F32)).astype(BF16)
    return a, b, c


A_STATE = A_DV + LANES


def _cummax_rows(x):
    n = x.shape[0]
    row = lax.broadcasted_iota(jnp.int32, x.shape, 0)
    k = 1
    while k < n:
        x = jnp.maximum(x, jnp.where(row >= k, pltpu.roll(x, k, axis=0), -jnp.inf))
        k *= 2
    return x


def _mlstm_body(q_ref, k_ref, v_ref, o_ref, g_ref, bias_ref, gain_ref, c0_ref, m0_ref,
                h_ref, c_ref, m_ref, *, L):
    step = pl.program_id(1)

    @pl.when(step == 0)
    def _():
        c_ref[...] = c0_ref[...]
        m_ref[...] = m0_ref[...]

    dot = functools.partial(jnp.dot, preferred_element_type=F32)
    row = lax.broadcasted_iota(jnp.int32, (L, L), 0)
    col = lax.broadcasted_iota(jnp.int32, (L, L), 1)
    causal = col <= row
    tril = jnp.where(causal, 1.0, 0.0).astype(BF16)

    pre = GATE_CAP * jnp.tanh((g_ref[...] + bias_ref[...]) * (1.0 / GATE_CAP))
    ig = pre
    lf = _log_sigmoid(pltpu.roll(pre, LANES - A_HEADS, axis=1))
    a3, b3, c3 = _split3(lf)
    bt = dot(tril, a3) + (dot(tril, b3) + dot(tril, c3))
    a = ig - bt
    m_prev = m_ref[0]
    m_t = bt + jnp.maximum(m_prev, _cummax_rows(a))
    u = bt - m_t
    inter = jnp.exp(bt + m_prev - m_t)
    eminus = jnp.exp(-m_t)
    m_new = m_t[L - 1:L, :]
    bt_last = bt[L - 1:L, :]
    w_c = jnp.exp((bt_last - m_new) + a)
    decay = jnp.exp(bt_last + m_prev - m_new)
    m_ref[0] = m_new
    a_pad = a if L == LANES else jnp.concatenate([a, jnp.zeros((LANES - L, LANES), F32)], axis=0)
    a_t = jnp.transpose(a_pad)
    ie = jnp.concatenate([inter, eminus], axis=1).astype(BF16)
    sel_r = lax.broadcasted_iota(jnp.int32, (2 * LANES, 2 * LANES), 0)
    sel_c = lax.broadcasted_iota(jnp.int32, (2 * LANES, 2 * LANES), 1)
    same_half = (sel_r >= LANES) == (sel_c >= LANES)
    ones_l = jnp.ones((L, LANES), BF16)
    zeros_l = jnp.zeros((L, LANES), BF16)
    zeros_k = jnp.zeros((A_DK, LANES), BF16)
    mean_cols = jnp.full((A_DV, LANES), 1.0, BF16)

    scale = A_DK ** -0.5
    heads = range(A_HEADS)
    dk_sl = lambda h: slice(h * A_DK, (h + 1) * A_DK)
    dv_sl = lambda h: slice(h * A_DV, (h + 1) * A_DV)
    ie_b, qk = [], []
    for h in heads:
        sel = jnp.where(same_half & ((sel_r & (LANES - 1)) == h), 1.0, 0.0).astype(BF16)
        ie_b.append(dot(ie, sel))
        qk.append(lax.dot_general(q_ref[:, dk_sl(h)], k_ref[:, dk_sl(h)], _NT, preferred_element_type=F32))
    out = []
    for h in heads:
        d_mat = jnp.where(causal, jnp.exp(u[:, h:h + 1] + a_t[h:h + 1, :L]), 0.0)
        s = qk[h] * (d_mat * scale)
        q_i = (q_ref[:, dk_sl(h)].astype(F32) * ie_b[h][:, :LANES]).astype(BF16)
        c_b = c_ref[0, h].astype(BF16)
        rhs = jnp.concatenate([
            jnp.concatenate([v_ref[:, dv_sl(h)], ones_l, zeros_l], axis=1),
            jnp.concatenate([c_b[:, :A_DV], zeros_k, c_b[:, A_DV:]], axis=1)], axis=0)
        out.append(dot(jnp.concatenate([s.astype(BF16), q_i], axis=1), rhs))
    hh, ms = [], []
    for h in heads:
        den = jnp.maximum(jnp.abs(out[h][:, A_DV:A_DV + LANES] + out[h][:, A_DV + LANES:]), ie_b[h][:, LANES:])
        r = 1.0 / den
        hh.append(out[h][:, :A_DV] * jnp.concatenate([r, r], axis=1))
        ms.append(dot((hh[h] * hh[h]).astype(BF16), mean_cols))
    upd = []
    for h in heads:
        rs = lax.rsqrt(ms[h] * (1.0 / A_DV) + RMS_EPS)
        og = o_ref[:, dv_sl(h)].astype(F32)
        y = hh[h] * jnp.concatenate([rs, rs], axis=1) * gain_ref[:, dv_sl(h)] * jax.nn.sigmoid(og)
        h_ref[:, dv_sl(h)] = y.astype(h_ref.dtype)
        wk = (k_ref[:, dk_sl(h)].astype(F32) * (w_c[:, h:h + 1] * scale)).astype(BF16)
        upd.append(lax.dot_general(wk, jnp.concatenate([v_ref[:, dv_sl(h)], ones_l], axis=1), _TN,
                                   preferred_element_type=F32))
    for h in heads:
        c_ref[0, h] = decay[:, h:h + 1] * c_ref[0, h] + upd[h]


def mlstm(proj, g, b_gate, gain, c0, n0, m0, *, batch, seq, L, row0):
    nc = seq // L
    bias = jnp.pad(b_gate.astype(F32), (0, LANES - 2 * A_HEADS))
    row_blk = lambda b, c: row0 // L + b * nc + c
    st = lambda *s: pl.BlockSpec((1,) + s, lambda b, c: (b,) + (0,) * len(s))
    state0 = jnp.concatenate([c0, jnp.broadcast_to(n0[..., None], n0.shape + (LANES,))], axis=-1)
    m_lanes = jnp.pad(m0, ((0, 0), (0, LANES - A_HEADS)))[:, None, :]
    h, state, m = pl.pallas_call(
        functools.partial(_mlstm_body, L=L),
        grid=(batch, nc),
        in_specs=[pl.BlockSpec((L, A_HEADS * A_DK), lambda b, c: (row_blk(b, c), 0)),
                  pl.BlockSpec((L, A_HEADS * A_DK), lambda b, c: (row_blk(b, c), 1)),
                  pl.BlockSpec((L, D_MODEL), lambda b, c: (row_blk(b, c), 1)),
                  pl.BlockSpec((L, D_MODEL), lambda b, c: (row_blk(b, c), 2)),
                  pl.BlockSpec((L, LANES), lambda b, c: (row_blk(b, c), 0)),
                  pl.BlockSpec((1, LANES), lambda b, c: (0, 0)),
                  pl.BlockSpec((1, D_MODEL), lambda b, c: (0, 0)),
                  st(A_HEADS, A_DK, A_STATE), st(1, LANES)],
        out_specs=[pl.BlockSpec((L, D_MODEL), lambda b, c: (b * nc + c, 0)),
                   st(A_HEADS, A_DK, A_STATE), st(1, LANES)],
        out_shape=[jax.ShapeDtypeStruct((batch * seq, D_MODEL), BF16),
                   jax.ShapeDtypeStruct((batch, A_HEADS, A_DK, A_STATE), F32),
                   jax.ShapeDtypeStruct((batch, 1, LANES), F32)],
        compiler_params=_params("parallel", "arbitrary"),
        name="mlstm",
    )(proj, proj, proj, proj, g, bias[None, :], gain.astype(F32)[None, :], state0, m_lanes)
    return h, state[..., :A_DV], state[..., A_DV], m[:, 0, :A_HEADS]


LOG2E = 1.4426950408889634
Q_SCALE = B_DH ** -0.5 * LOG2E


def _scale_q(q):
    return (q.astype(F32) * Q_SCALE).astype(BF16)


def _attn_prompt_body(q_ref, k_ref, v_ref, bias_ref, o_ref, kt_ref, vt_ref, qs, kpad, vpad):
    kt_ref[0] = k_ref[SEQ - REACH:, :].astype(F32)
    vt_ref[0] = v_ref[SEQ - REACH:, :].astype(F32)
    qs[...] = _scale_q(q_ref[...])
    kpad[0:REACH, :] = jnp.zeros((REACH, B_DH), BF16)
    kpad[REACH:, :] = k_ref[...]
    vpad[0:REACH, 0:B_DH] = jnp.zeros((REACH, B_DH), BF16)
    vpad[REACH:, 0:B_DH] = v_ref[...]
    vpad[:, B_DH:] = jnp.ones((SEQ + REACH, B_DH), BF16)

    def tile(t, mask_front):
        r0 = t * ATT_TQ if isinstance(t, int) else pl.multiple_of(t * ATT_TQ, ATT_TQ)
        q = qs[pl.ds(r0, ATT_TQ), :]
        kb = kpad[pl.ds(r0, ATT_TK), :]
        vb = vpad[pl.ds(r0, ATT_TK), :]
        s = lax.dot_general(q, kb, _NT, preferred_element_type=F32) + bias_ref[0]
        if mask_front:
            j = lax.broadcasted_iota(jnp.int32, (ATT_TQ, ATT_TK), 1)
            s = jnp.where(j + r0 >= REACH, s, NEG)
        p = jnp.exp2(s - jnp.max(s, axis=-1, keepdims=True)).astype(BF16)
        acc = jnp.dot(p, vb, preferred_element_type=F32)
        o_ref[pl.ds(r0, ATT_TQ), :] = (acc[:, :B_DH] / acc[:, B_DH:]).astype(o_ref.dtype)

    n_front = REACH // ATT_TQ
    for t in range(n_front):
        tile(t, True)

    def loop(t, carry):
        tile(t, False)
        return carry

    lax.fori_loop(n_front, SEQ // ATT_TQ, loop, 0, unroll=14)


def band_bias(rel_table, nq, nk):
    w = nq + nk
    rel = jnp.clip(REACH + nq - 1 - jnp.arange(w), -REL_CLIP, REL_CLIP) + REL_CLIP
    r = rel_table.astype(F32)[:, rel]
    heads = r.shape[0]
    skew = jnp.broadcast_to(r[:, None, :], (heads, nq, w)).reshape(heads, nq * w)
    skew = skew[:, :nq * (w - 1)].reshape(heads, nq, w - 1)
    bias = skew[:, :, nq - 1:nq - 1 + nk]
    i = jnp.arange(nq)[:, None]
    j = jnp.arange(nk)[None, :]
    dc = j // CHUNK - i // CHUNK
    allowed = (dc >= 0) & (dc <= PREV_CHUNKS)
    return jnp.where(allowed[None], bias * LOG2E, NEG)


def attn_prompt(qkv, rel_table):
    bias = band_bias(rel_table, ATT_TQ, ATT_TK)
    tail = pl.BlockSpec((1, REACH, B_DH), lambda b, h: (b, 0, h))
    return pl.pallas_call(
        _attn_prompt_body,
        grid=(BATCH, B_HEADS),
        in_specs=[pl.BlockSpec((SEQ, B_DH), lambda b, h: (b, h)),
                  pl.BlockSpec((SEQ, B_DH), lambda b, h: (b, B_HEADS + h)),
                  pl.BlockSpec((SEQ, B_DH), lambda b, h: (b, 2 * B_HEADS + h)),
                  pl.BlockSpec((1, ATT_TQ, ATT_TK), lambda b, h: (h, 0, 0))],
        out_specs=[pl.BlockSpec((SEQ, B_DH), lambda b, h: (b, h)), tail, tail],
        out_shape=[jax.ShapeDtypeStruct((N_PROMPT, D_MODEL), BF16),
                   jax.ShapeDtypeStruct((BATCH, REACH, D_MODEL), F32),
                   jax.ShapeDtypeStruct((BATCH, REACH, D_MODEL), F32)],
        scratch_shapes=[pltpu.VMEM((SEQ, B_DH), BF16),
                        pltpu.VMEM((SEQ + REACH, B_DH), BF16),
                        pltpu.VMEM((SEQ + REACH, 2 * B_DH), BF16)],
        compiler_params=_params("parallel", "parallel"),
        name="attn_prompt",
    )(qkv, qkv, qkv, bias)


def _attn_sample_body(q_ref, kn_ref, vn_ref, ck_ref, cv_ref, bias_c_ref, bias_n_ref, o_ref, ks_ref, vs_ref):
    ks_ref[0] = kn_ref[...].astype(F32)
    vs_ref[0] = vn_ref[...].astype(F32)
    for h in range(B_HEADS):
        sl = slice(h * B_DH, (h + 1) * B_DH)
        q = _scale_q(q_ref[:, sl])
        kc = ck_ref[pl.ds(h, REACH, stride=B_HEADS), :].astype(BF16)
        vc = cv_ref[pl.ds(h, REACH, stride=B_HEADS), :].astype(BF16)
        s_c = lax.dot_general(q, kc, _NT, preferred_element_type=F32) + bias_c_ref[h]
        s_n = lax.dot_general(q, kn_ref[:, sl], _NT, preferred_element_type=F32) + bias_n_ref[h]
        m = jnp.maximum(jnp.max(s_c, axis=-1, keepdims=True), jnp.max(s_n, axis=-1, keepdims=True))
        p_c = jnp.exp2(s_c - m)
        p_n = jnp.exp2(s_n - m)
        l = jnp.sum(p_c, axis=-1, keepdims=True) + jnp.sum(p_n, axis=-1, keepdims=True)
        dot = functools.partial(jnp.dot, preferred_element_type=F32)
        o_ref[:, sl] = ((dot(p_c.astype(BF16), vc) + dot(p_n.astype(BF16), vn_ref[:, sl])) / l).astype(o_ref.dtype)


def attn_sample(qkv, cache_k, cache_v, layer, rel_table):
    bias = band_bias(rel_table, DEC_SEQ, REACH + DEC_SEQ)
    cache_rows = REACH * B_HEADS
    ck = cache_k.reshape(cache_k.shape[0], DEC_BATCH, cache_rows, B_DH)
    cv = cache_v.reshape(cache_v.shape[0], DEC_BATCH, cache_rows, B_DH)
    row = lambda c: pl.BlockSpec((DEC_SEQ, D_MODEL), lambda b: (N_PROMPT // DEC_SEQ + b, c))
    cache = pl.BlockSpec((None, None, cache_rows, B_DH), lambda b: (layer, b, 0, 0))
    new = pl.BlockSpec((1, DEC_SEQ, D_MODEL), lambda b: (b, 0, 0))
    return pl.pallas_call(
        _attn_sample_body,
        grid=(DEC_BATCH,),
        in_specs=[row(0), row(1), row(2), cache, cache,
                  pl.BlockSpec((B_HEADS, DEC_SEQ, REACH), lambda b: (0, 0, 0)),
                  pl.BlockSpec((B_HEADS, DEC_SEQ, DEC_SEQ), lambda b: (0, 0, 0))],
        out_specs=[pl.BlockSpec((DEC_SEQ, D_MODEL), lambda b: (b, 0)), new, new],
        out_shape=[jax.ShapeDtypeStruct((N_SAMPLE, D_MODEL), BF16),
                   jax.ShapeDtypeStruct((DEC_BATCH, DEC_SEQ, D_MODEL), F32),
                   jax.ShapeDtypeStruct((DEC_BATCH, DEC_SEQ, D_MODEL), F32)],
        compiler_params=_params("parallel"),
        name="attn_sample",
    )(qkv, qkv, qkv, ck, cv, bias[:, :, :REACH], bias[:, :, REACH:])


def _ln(z, g, b):
    mu = jnp.mean(z, axis=-1, keepdims=True)
    zc = z - mu
    var = jnp.mean(zc * zc, axis=-1, keepdims=True)
    return zc * lax.rsqrt(var + LN_EPS) * g + b


PACK_ROWS = D_MODEL // 2 // LANES
HI_MASK = -65536


def _store_packed(ref, v, token0=0):
    t = v.shape[0]
    half = D_MODEL // 2
    hi = lax.bitcast_convert_type(v[:, :half].astype(BF16).astype(F32), jnp.int32)
    lo = lax.bitcast_convert_type(v[:, half:].astype(BF16).astype(F32), jnp.int32)
    words = (hi & HI_MASK) | lax.shift_right_logical(lo, 16)
    for a in range(PACK_ROWS):
        ref[pl.ds(token0 * PACK_ROWS + a, t, stride=PACK_ROWS), :] = words[:, a * LANES:(a + 1) * LANES]


def _load_packed(ref, t):
    words = jnp.concatenate([ref[pl.ds(a, t, stride=PACK_ROWS), :] for a in range(PACK_ROWS)], axis=1)
    hi = lax.bitcast_convert_type(words & HI_MASK, F32)
    lo = lax.bitcast_convert_type(lax.shift_left(words, 16), F32)
    return jnp.concatenate([hi, lo], axis=1)


def _mix_ln_router_body(hp_ref, hs_ref, w_ref, x_ref, g_ref, b_ref, rw_ref, rb_ref,
                        xo_ref, xbo_ref, xp_ref, wt_ref, cls_ref, *, prompt_tiles):
    h = jnp.where(pl.program_id(0) < prompt_tiles, hp_ref[...], hs_ref[...])
    y = jnp.dot(h, w_ref[...], preferred_element_type=F32)
    xn = _ln(ALPHA * x_ref[...] + y, g_ref[...], b_ref[...])
    xo_ref[...] = xn
    xbo_ref[...] = xn.astype(BF16)
    _store_packed(xp_ref, xn)
    _route(xn, rw_ref, rb_ref, wt_ref, cls_ref)


def mix_ln_router(h_p, h_s, w, x, g, b, w_group, b_group, w_expert, b_expert, tm=256):
    n = x.shape[0]
    prompt_tiles = h_p.shape[0] // tm
    sample_tiles = h_s.shape[0] // tm
    assert prompt_tiles * tm == h_p.shape[0] and (prompt_tiles + sample_tiles) * tm == n
    hp_spec = pl.BlockSpec((tm, D_MODEL), lambda i: (jnp.minimum(i, prompt_tiles - 1), 0))
    hs_spec = pl.BlockSpec((tm, D_MODEL), lambda i: (jnp.maximum(i - prompt_tiles, 0), 0))
    rw = jnp.pad(jnp.concatenate([w_group, w_expert], axis=1), ((0, 0), (0, LANES - N_GROUPS - N_EXPERTS)))
    rb = jnp.pad(jnp.concatenate([b_group, b_expert]).astype(F32), (0, LANES - N_GROUPS - N_EXPERTS))
    row = pl.BlockSpec((tm, D_MODEL), lambda i: (i, 0))
    vec = pl.BlockSpec((1, D_MODEL), lambda i: (0, 0))
    lane_row = pl.BlockSpec((tm, LANES), lambda i: (i, 0))
    return pl.pallas_call(
        functools.partial(_mix_ln_router_body, prompt_tiles=prompt_tiles),
        grid=(n // tm,),
        in_specs=[hp_spec, hs_spec, pl.BlockSpec((D_MODEL, D_MODEL), lambda i: (0, 0)), row, vec, vec,
                  pl.BlockSpec((D_MODEL, LANES), lambda i: (0, 0)), pl.BlockSpec((1, LANES), lambda i: (0, 0))],
        out_specs=[row, row, pl.BlockSpec((tm * PACK_ROWS, LANES), lambda i: (i, 0)), lane_row, lane_row],
        out_shape=[jax.ShapeDtypeStruct((n, D_MODEL), F32), jax.ShapeDtypeStruct((n, D_MODEL), BF16),
                   jax.ShapeDtypeStruct((n * PACK_ROWS, LANES), jnp.int32),
                   jax.ShapeDtypeStruct((n, LANES), F32), jax.ShapeDtypeStruct((n, LANES), jnp.int32)],
        compiler_params=_params("parallel"),
        name="mix_ln_router",
    )(h_p, h_s, w, x, g[None, :], b[None, :], rw, rb[None, :])


def _add_ln_body(x_ref, y_ref, g_ref, b_ref, xo_ref, xbo_ref):
    y = _load_packed(y_ref, x_ref.shape[0])
    xn = _ln(ALPHA * x_ref[...] + y, g_ref[...], b_ref[...])
    xo_ref[...] = xn
    xbo_ref[...] = xn.astype(BF16)


def residual_ln(x, y_packed, g, b, tm=512):
    n = x.shape[0]
    row = pl.BlockSpec((tm, D_MODEL), lambda i: (i, 0))
    vec = pl.BlockSpec((1, D_MODEL), lambda i: (0, 0))
    return pl.pallas_call(
        _add_ln_body,
        grid=(n // tm,),
        in_specs=[row, pl.BlockSpec((tm * PACK_ROWS, LANES), lambda i: (i, 0)), vec, vec],
        out_specs=[row, row],
        out_shape=[jax.ShapeDtypeStruct((n, D_MODEL), F32), jax.ShapeDtypeStruct((n, D_MODEL), BF16)],
        compiler_params=_params("parallel"),
        name="residual_ln",
    )(x, y_packed, g[None, :], b[None, :])


def _add_ln_split_body(x_ref, y_ref, g_ref, b_ref, op_ref, os_ref, *, prompt_tiles):
    y = _load_packed(y_ref, x_ref.shape[0])
    xn = _ln(ALPHA * x_ref[...] + y, g_ref[...], b_ref[...])
    i = pl.program_id(0)

    @pl.when(i < prompt_tiles)
    def _():
        op_ref[...] = xn

    @pl.when(i >= prompt_tiles)
    def _():
        os_ref[...] = xn


def residual_ln_split(x, y_packed, g, b, n_prompt, tm=512):
    n = x.shape[0]
    prompt_tiles = n_prompt // tm
    assert prompt_tiles * tm == n_prompt and n % tm == 0
    row = pl.BlockSpec((tm, D_MODEL), lambda i: (i, 0))
    vec = pl.BlockSpec((1, D_MODEL), lambda i: (0, 0))
    return pl.pallas_call(
        functools.partial(_add_ln_split_body, prompt_tiles=prompt_tiles),
        grid=(n // tm,),
        in_specs=[row, pl.BlockSpec((tm * PACK_ROWS, LANES), lambda i: (i, 0)), vec, vec],
        out_specs=[pl.BlockSpec((tm, D_MODEL), lambda i: (jnp.minimum(i, prompt_tiles - 1), 0)),
                   pl.BlockSpec((tm, D_MODEL), lambda i: (jnp.maximum(i - prompt_tiles, 0), 0))],
        out_shape=[jax.ShapeDtypeStruct((n_prompt, D_MODEL), F32),
                   jax.ShapeDtypeStruct((n - n_prompt, D_MODEL), F32)],
        compiler_params=_params("arbitrary"),
        name="residual_ln_split",
    )(x, y_packed, g[None, :], b[None, :])


ROUTER_E0 = N_GROUPS


def _route(x, w_ref, b_ref, wt_ref, cls_ref):
    logits = _dot_hilo(x, w_ref[...]) + b_ref[...]
    lane = lax.broadcasted_iota(jnp.int32, logits.shape, 1).astype(F32)
    big = float(LANES)
    rmax = lambda a: jnp.max(a, axis=-1, keepdims=True)
    rmin = lambda a: jnp.min(a, axis=-1, keepdims=True)
    is_g = lane < N_GROUPS
    gl = jnp.where(is_g, logits, -jnp.inf)
    gmax = rmax(gl)
    gsel = rmin(jnp.where(gl == gmax, lane, big))
    gsum = jnp.sum(jnp.where(is_g, jnp.exp(logits - gmax), 0.0), axis=-1, keepdims=True)
    g_w = 1.0 / gsum
    e_lo = ROUTER_E0 + EXP_PER_GROUP * gsel
    in_grp = (lane >= e_lo) & (lane < e_lo + EXP_PER_GROUP)
    el = jnp.where(in_grp, logits, -jnp.inf)
    e1 = rmax(el)
    i1 = rmin(jnp.where(el == e1, lane, big))
    el2 = jnp.where(lane == i1, -jnp.inf, el)
    e2 = rmax(el2)
    i2 = rmin(jnp.where(el2 == e2, lane, big))
    t = jnp.exp(e2 - e1)
    p1 = 1.0 / (1.0 + t)
    w1 = p1 * g_w
    w2 = (t * p1) * g_w
    a1 = i1 - e_lo
    a2 = i2 - e_lo
    first_low = a1 < a2
    lo = jnp.where(first_low, a1, a2)
    hi = jnp.where(first_low, a2, a1)
    w_lo = jnp.where(first_low, w1, w2)
    w_hi = jnp.where(first_low, w2, w1)
    off = jnp.where(lo == 0.0, 0.0, jnp.where(lo == 1.0, 3.0, 5.0))
    cls = gsel * float(len(PAIRS)) + off + (hi - lo - 1.0)
    wt_ref[...] = jnp.where(lane < LANES // 2, w_lo, w_hi)
    cls_ref[...] = jnp.broadcast_to(cls, logits.shape).astype(jnp.int32)


def _expert_body(ea_ref, eb_ref, valid_ref, x_ref, wt_ref, wga_ref, wua_ref, wda_ref,
                 wgb_ref, wub_ref, wdb_ref, y_ref):
    i = pl.program_id(0)

    @pl.when(valid_ref[i] == 0)
    def _():
        y_ref[...] = jnp.zeros_like(y_ref)

    @pl.when(valid_ref[i] != 0)
    def _():
        x = _load_packed(x_ref, MOE_TILE).astype(BF16)
        dot = functools.partial(jnp.dot, preferred_element_type=F32)

        ga, ua = dot(x, wga_ref[...]), dot(x, wua_ref[...])
        gb, ub = dot(x, wgb_ref[...]), dot(x, wub_ref[...])
        ya = dot((jax.nn.silu(ga) * ua).astype(BF16), wda_ref[...])
        yb = dot((jax.nn.silu(gb) * ub).astype(BF16), wdb_ref[...])
        w_lo = wt_ref[:, 0:1]
        w_hi = wt_ref[:, LANES // 2:LANES // 2 + 1]
        _store_packed(y_ref, w_lo * ya + w_hi * yb)


def experts(xs, wts, tile_ea, tile_eb, tile_valid, w_gate, w_up, w_down):
    wspec = lambda shape, which: pl.BlockSpec(
        (None,) + shape, lambda i, ea, eb, va: ((ea, eb)[which][i], 0, 0))
    up = (D_MODEL, D_EXPERT)
    down = (D_EXPERT, D_MODEL)
    packed = pl.BlockSpec((MOE_TILE * PACK_ROWS, LANES), lambda i, ea, eb, va: (i, 0))
    grid_spec = pltpu.PrefetchScalarGridSpec(
        num_scalar_prefetch=3,
        grid=(MOE_TILES,),
        in_specs=[packed,
                  pl.BlockSpec((MOE_TILE, LANES), lambda i, ea, eb, va: (i, 0)),
                  wspec(up, 0), wspec(up, 0), wspec(down, 0),
                  wspec(up, 1), wspec(up, 1), wspec(down, 1)],
        out_specs=packed,
    )
    return pl.pallas_call(
        _expert_body,
        grid_spec=grid_spec,
        out_shape=jax.ShapeDtypeStruct((MOE_ROWS * PACK_ROWS, LANES), jnp.int32),
        compiler_params=_params("arbitrary"),
        name="experts",
    )(tile_ea, tile_eb, tile_valid, xs, wts, w_gate, w_up, w_down, w_gate, w_up, w_down)


def sc_gather_rows(x, idx):
    n, d = x.shape
    n_out = idx.shape[0]
    mesh = plsc.VectorSubcoreMesh(core_axis_name="core", subcore_axis_name="subcore")

    @functools.partial(
        pl.kernel,
        out_type=jax.ShapeDtypeStruct((n_out, d), x.dtype),
        mesh=mesh,
    )
    def gather(x_hbm, i_hbm, o_hbm):
        def body(i_vmem, o_vmem):
            pltpu.sync_copy(x_hbm.at[i_vmem.at[0]], o_vmem)

        pltpu.emit_pipeline(
            body,
            grid=(n_out // SC_WINDOW,),
            in_specs=[pl.BlockSpec((1, SC_WINDOW), lambda i: (0, i))],
            out_specs=[pl.BlockSpec((SC_WINDOW, d), lambda i: (i, 0))],
            core_axis_name=("core", "subcore"),
            dimension_semantics=(pltpu.PARALLEL,),
        )(i_hbm, o_hbm)

    return gather(x, idx.reshape(1, n_out))


def sc_gather_tokens(x, idx):
    n = x.shape[0] // PACK_ROWS
    p = idx.shape[0]
    info = plsc.get_sparse_core_info()
    n_workers = info.num_cores * info.num_subcores
    per_worker = p // n_workers
    steps = per_worker // SC_TOKENS
    assert per_worker * n_workers == p and steps * SC_TOKENS == per_worker
    mesh = plsc.VectorSubcoreMesh(core_axis_name="core", subcore_axis_name="subcore")

    @functools.partial(
        pl.kernel,
        out_type=jax.ShapeDtypeStruct((p, PACK_ROWS, LANES), x.dtype),
        mesh=mesh,
        scratch_types=[pltpu.VMEM((per_worker,), jnp.int32),
                       pltpu.VMEM((SC_TOKENS, PACK_ROWS, LANES), x.dtype)],
    )
    def gather(x_hbm, i_hbm, o_hbm, idx_v, buf):
        worker = lax.axis_index("subcore") * info.num_cores + lax.axis_index("core")
        base = worker * per_worker
        pltpu.sync_copy(i_hbm.at[pl.ds(base, per_worker)], idx_v)

        @pl.loop(0, steps)
        def _(s):
            r0 = s * SC_TOKENS
            pltpu.sync_copy(x_hbm.at[idx_v.at[pl.ds(r0, SC_TOKENS)]], buf)
            pltpu.sync_copy(buf, o_hbm.at[pl.ds(base + r0, SC_TOKENS)])

    return gather(x.reshape(n, PACK_ROWS, LANES), idx).reshape(p * PACK_ROWS, LANES)


_CLASS_EA = [g * EXP_PER_GROUP + lo for g in range(N_GROUPS) for lo, hi in PAIRS]
_CLASS_EB = [g * EXP_PER_GROUP + hi for g in range(N_GROUPS) for lo, hi in PAIRS]


def dispatch_plan(cls):
    onehot = (cls[:, None] == jnp.arange(N_CLASSES, dtype=jnp.int32)[None, :]).astype(jnp.int32)
    csum = jnp.cumsum(onehot, axis=0)
    rank = jnp.sum(csum * onehot, axis=1) - 1
    counts = csum[-1]
    tiles_per = (counts + MOE_TILE - 1) // MOE_TILE
    tile_end = jnp.cumsum(tiles_per)
    tile_start = tile_end - tiles_per
    pos = jnp.sum(onehot * tile_start[None, :], axis=1) * MOE_TILE + rank
    n = cls.shape[0]
    src = (jnp.arange(MOE_ROWS, dtype=jnp.int32) % n).at[pos].set(jnp.arange(n, dtype=jnp.int32))
    t = jnp.arange(MOE_TILES, dtype=jnp.int32)
    tile_cls = jnp.sum((t[:, None] >= tile_end[None, :]).astype(jnp.int32), axis=1)
    valid = (tile_cls < N_CLASSES).astype(jnp.int32)
    last_cls = jnp.max(jnp.where(counts > 0, jnp.arange(N_CLASSES, dtype=jnp.int32), 0))
    tile_cls = jnp.where(valid == 1, tile_cls, last_cls)
    ea = jnp.asarray(_CLASS_EA, jnp.int32)[tile_cls]
    eb = jnp.asarray(_CLASS_EB, jnp.int32)[tile_cls]
    return pos, src, ea, eb, valid


def moe(xp, wts, cls, w_gate, w_up, w_down):
    pos, src, ea, eb, valid = dispatch_plan(cls[:, 0])
    xs = sc_gather_tokens(xp, src)
    ws = sc_gather_rows(wts, src)
    ys = experts(xs, ws, ea, eb, valid, w_gate, w_up, w_down)
    return sc_gather_tokens(ys, pos)


def _rows(a):
    return a.reshape(-1, a.shape[-1])


def kernel(x_prompt, x_sample, state_C, state_n, state_m, cache_k, cache_v, a_w_in, a_b_gate, a_norm, a_w_out, b_w_in, b_rel, b_w_out, ln1_g, ln1_b, ln2_g, ln2_b, r_w_group, r_b_group, r_w_expert, r_b_expert, e_w_gate, e_w_up, e_w_down):
    x = jnp.concatenate([_rows(x_prompt), _rows(x_sample)], axis=0)
    xb = x.astype(BF16)
    hk = A_HEADS * A_DK
    n_main = 2 * hk + 2 * D_MODEL
    outs = {k: [] for k in ("Cp", "np", "mp", "kp", "vp", "Cs", "ns", "ms", "ks", "vs")}
    for layer in range(DEPTH):
        j = layer // 2
        if layer % 2 == 0:
            proj = matmul(xb, a_w_in, j, n_main, BF16)
            g = gate_preacts(x, a_w_in[j, :, n_main:])
            zc = jnp.zeros((BATCH, A_HEADS, A_DK, A_DV), F32)
            zn = jnp.zeros((BATCH, A_HEADS, A_DK), F32)
            zm = jnp.zeros((BATCH, A_HEADS), F32)
            h_p, c_p, n_p, m_p = mlstm(proj, g, a_b_gate[j], a_norm[j], zc, zn, zm,
                                       batch=BATCH, seq=SEQ, L=MLSTM_CHUNK, row0=0)
            h_s, c_s, n_s, m_s = mlstm(proj, g, a_b_gate[j], a_norm[j],
                                       state_C[j], state_n[j], state_m[j],
                                       batch=DEC_BATCH, seq=DEC_SEQ, L=DEC_SEQ, row0=N_PROMPT)
            outs["Cp"].append(c_p); outs["np"].append(n_p); outs["mp"].append(m_p)
            outs["Cs"].append(c_s); outs["ns"].append(n_s); outs["ms"].append(m_s)
            w_out = a_w_out[j]
        else:
            qkv = matmul(xb, b_w_in, j, 3 * D_MODEL, BF16)
            h_p, k_p, v_p = attn_prompt(qkv, b_rel[j])
            h_s, k_s, v_s = attn_sample(qkv, cache_k, cache_v, j, b_rel[j])
            heads = lambda a: a.reshape(a.shape[:2] + (B_HEADS, B_DH))
            outs["kp"].append(heads(k_p)); outs["vp"].append(heads(v_p))
            outs["ks"].append(heads(k_s)); outs["vs"].append(heads(v_s))
            w_out = b_w_out[j]
        x, xb, xp, wts, cls = mix_ln_router(h_p, h_s, w_out.astype(BF16), x, ln1_g[layer], ln1_b[layer],
                                            r_w_group[layer], r_b_group[layer], r_w_expert[layer], r_b_expert[layer])
        y = moe(xp, wts, cls, cast_layer(e_w_gate, layer), cast_layer(e_w_up, layer), cast_layer(e_w_down, layer))
        if layer < DEPTH - 1:
            x, xb = residual_ln(x, y, ln2_g[layer], ln2_b[layer])
        else:
            y_p, y_s = residual_ln_split(x, y, ln2_g[layer], ln2_b[layer], N_PROMPT)
    st = lambda k: jnp.stack(outs[k])
    return (y_p.reshape(BATCH, SEQ, D_MODEL), y_s.reshape(DEC_BATCH, DEC_SEQ, D_MODEL),
            st("Cp"), st("np"), st("mp"), st("kp"), st("vp"),
            st("Cs"), st("ns"), st("ms"), st("ks"), st("vs"))
```

```python
import functools

import jax
import jax.numpy as jnp
from jax import lax
from jax.experimental import pallas as pl
from jax.experimental.pallas import tpu as pltpu
from jax.experimental.pallas import tpu_sc as plsc

F32 = jnp.float32
BF16 = jnp.bfloat16

D_MODEL = 2048
BATCH = 8
SEQ = 4096
DEPTH = 4
DEC_BATCH = 32
DEC_SEQ = 32
PAST_LEN = 1024
CHUNK = 64
A_HEADS = 8
A_DK = 128
A_DV = D_MODEL // A_HEADS
GATE_CAP = 15.0
B_HEADS = 16
B_DH = D_MODEL // B_HEADS
PREV_CHUNKS = 8
REACH = PREV_CHUNKS * CHUNK
REL_CLIP = 256
N_GROUPS = 4
EXP_PER_GROUP = 4
N_EXPERTS = N_GROUPS * EXP_PER_GROUP
D_EXPERT = D_MODEL // 4
ALPHA = (2 * DEPTH) ** 0.25
LN_EPS = 1e-5
RMS_EPS = 1e-6

N_PROMPT = BATCH * SEQ
N_SAMPLE = DEC_BATCH * DEC_SEQ
N_TOK = N_PROMPT + N_SAMPLE

VMEM_LIMIT = 56 * 1024 * 1024
LANES = 128

PAIRS = ((0, 1), (0, 2), (0, 3), (1, 2), (1, 3), (2, 3))
N_CLASSES = N_GROUPS * len(PAIRS)
MOE_TILE = 256
MOE_TILES = -(-(N_TOK // MOE_TILE + N_CLASSES) // 32) * 32
MOE_ROWS = MOE_TILES * MOE_TILE

MLSTM_CHUNK = 128
ATT_TQ = 256
ATT_TK = ATT_TQ + REACH
NEG = -1e30

SC_WINDOW = 128
SC_TOKENS = 32


def _params(*sem):
    return pltpu.CompilerParams(dimension_semantics=sem, vmem_limit_bytes=VMEM_LIMIT)


def _mm_body(x_ref, w_ref, o_ref, wb):
    @pl.when(pl.program_id(1) == 0)
    def _():
        wb[...] = w_ref[...].astype(BF16)

    o_ref[...] = jnp.dot(x_ref[...], wb[...], preferred_element_type=F32).astype(o_ref.dtype)


def matmul(x, w_all, layer, m, out_dtype, tm=512, tn=1024):
    n, k = x.shape
    return pl.pallas_call(
        _mm_body,
        grid=(m // tn, n // tm),
        in_specs=[pl.BlockSpec((tm, k), lambda j, i: (i, 0)),
                  pl.BlockSpec((None, k, tn), lambda j, i: (layer, 0, j))],
        out_specs=pl.BlockSpec((tm, tn), lambda j, i: (i, j)),
        out_shape=jax.ShapeDtypeStruct((n, m), out_dtype),
        scratch_shapes=[pltpu.VMEM((k, tn), BF16)],
        compiler_params=_params("parallel", "arbitrary"),
        name="matmul",
    )(x, w_all)


def _cast_body(w_ref, o_ref):
    o_ref[...] = w_ref[...].astype(BF16)


def cast_layer(w_all, layer):
    e, a, b = w_all.shape[1:]
    per_step = 2
    return pl.pallas_call(
        _cast_body,
        grid=(e // per_step,),
        in_specs=[pl.BlockSpec((None, per_step, a, b), lambda i: (layer, i, 0, 0))],
        out_specs=pl.BlockSpec((per_step, a, b), lambda i: (i, 0, 0)),
        out_shape=jax.ShapeDtypeStruct((e, a, b), BF16),
        compiler_params=_params("parallel"),
        name="cast_layer",
    )(w_all)


def _split2(x):
    hi = x.astype(BF16)
    lo = (x - hi.astype(F32)).astype(BF16)
    return hi, lo


def _dot3(xh, xl, wh, wl, dims):
    dg = functools.partial(lax.dot_general, dimension_numbers=dims, preferred_element_type=F32)
    return dg(xh, wh) + (dg(xh, wl) + dg(xl, wh))


_NN = (((1,), (0,)), ((), ()))
_NT = (((1,), (1,)), ((), ()))
_TN = (((0,), (0,)), ((), ()))


def _dot_hilo(x, w):
    t = x.shape[0]
    xh, xl = _split2(x)
    wh, wl = _split2(w)
    r = jnp.dot(jnp.concatenate([xh, xl], axis=0), jnp.concatenate([wh, wl], axis=1), preferred_element_type=F32)
    return r[:t, :LANES] + (r[:t, LANES:] + r[t:, :LANES]) + r[t:, LANES:]


def _gate_body(x_ref, w_ref, g_ref):
    g_ref[...] = _dot_hilo(x_ref[...], w_ref[...])


def gate_preacts(x, w_gate, tm=512):
    n = x.shape[0]
    w = jnp.pad(w_gate, ((0, 0), (0, LANES - w_gate.shape[1])))
    return pl.pallas_call(
        _gate_body,
        grid=(n // tm,),
        in_specs=[pl.BlockSpec((tm, D_MODEL), lambda i: (i, 0)),
                  pl.BlockSpec((D_MODEL, LANES), lambda i: (0, 0))],
        out_specs=pl.BlockSpec((tm, LANES), lambda i: (i, 0)),
        out_shape=jax.ShapeDtypeStruct((n, LANES), F32),
        compiler_params=_params("parallel"),
        name="gate_preacts",
    )(x, w)


def _log_sigmoid(x):
    return jnp.minimum(x, 0.0) - jnp.log(1.0 + jnp.exp(-jnp.abs(x)))


def _split3(x):
    a = x.astype(BF16)
    r = x - a.astype(F32)
    b = r.astype(BF16)
    c = (r - b.astype(F32)).astype(BF16)
    return a, b, c


A_STATE = A_DV + LANES


def _cummax_rows(x):
    n = x.shape[0]
    row = lax.broadcasted_iota(jnp.int32, x.shape, 0)
    k = 1
    while k < n:
        x = jnp.maximum(x, jnp.where(row >= k, pltpu.roll(x, k, axis=0), -jnp.inf))
        k *= 2
    return x


def _mlstm_body(q_ref, k_ref, v_ref, o_ref, g_ref, bias_ref, gain_ref, c0_ref, m0_ref,
                h_ref, c_ref, m_ref, *, L):
    step = pl.program_id(1)

    @pl.when(step == 0)
    def _():
        c_ref[...] = c0_ref[...]
        m_ref[...] = m0_ref[...]

    dot = functools.partial(jnp.dot, preferred_element_type=F32)
    row = lax.broadcasted_iota(jnp.int32, (L, L), 0)
    col = lax.broadcasted_iota(jnp.int32, (L, L), 1)
    causal = col <= row
    tril = jnp.where(causal, 1.0, 0.0).astype(BF16)

    pre = GATE_CAP * jnp.tanh((g_ref[...] + bias_ref[...]) * (1.0 / GATE_CAP))
    ig = pre
    lf = _log_sigmoid(pltpu.roll(pre, LANES - A_HEADS, axis=1))
    a3, b3, c3 = _split3(lf)
    bt = dot(tril, a3) + (dot(tril, b3) + dot(tril, c3))
    a = ig - bt
    m_prev = m_ref[0]
    m_t = bt + jnp.maximum(m_prev, _cummax_rows(a))
    u = bt - m_t
    inter = jnp.exp(bt + m_prev - m_t)
    eminus = jnp.exp(-m_t)
    m_new = m_t[L - 1:L, :]
    bt_last = bt[L - 1:L, :]
    w_c = jnp.exp((bt_last - m_new) + a)
    decay = jnp.exp(bt_last + m_prev - m_new)
    m_ref[0] = m_new
    a_pad = a if L == LANES else jnp.concatenate([a, jnp.zeros((LANES - L, LANES), F32)], axis=0)
    a_t = jnp.transpose(a_pad)
    ie = jnp.concatenate([inter, eminus], axis=1).astype(BF16)
    sel_r = lax.broadcasted_iota(jnp.int32, (2 * LANES, 2 * LANES), 0)
    sel_c = lax.broadcasted_iota(jnp.int32, (2 * LANES, 2 * LANES), 1)
    same_half = (sel_r >= LANES) == (sel_c >= LANES)
    ones_l = jnp.ones((L, LANES), BF16)
    zeros_l = jnp.zeros((L, LANES), BF16)
    zeros_k = jnp.zeros((A_DK, LANES), BF16)
    mean_cols = jnp.full((A_DV, LANES), 1.0, BF16)

    scale = A_DK ** -0.5
    heads = range(A_HEADS)
    dk_sl = lambda h: slice(h * A_DK, (h + 1) * A_DK)
    dv_sl = lambda h: slice(h * A_DV, (h + 1) * A_DV)
    ie_b, qk = [], []
    for h in heads:
        sel = jnp.where(same_half & ((sel_r & (LANES - 1)) == h), 1.0, 0.0).astype(BF16)
        ie_b.append(dot(ie, sel))
        qk.append(lax.dot_general(q_ref[:, dk_sl(h)], k_ref[:, dk_sl(h)], _NT, preferred_element_type=F32))
    out = []
    for h in heads:
        d_mat = jnp.where(causal, jnp.exp(u[:, h:h + 1] + a_t[h:h + 1, :L]), 0.0)
        s = qk[h] * (d_mat * scale)
        q_i = (q_ref[:, dk_sl(h)].astype(F32) * ie_b[h][:, :LANES]).astype(BF16)
        c_b = c_ref[0, h].astype(BF16)
        rhs = jnp.concatenate([
            jnp.concatenate([v_ref[:, dv_sl(h)], ones_l, zeros_l], axis=1),
            jnp.concatenate([c_b[:, :A_DV], zeros_k, c_b[:, A_DV:]], axis=1)], axis=0)
        out.append(dot(jnp.concatenate([s.astype(BF16), q_i], axis=1), rhs))
    hh, ms = [], []
    for h in heads:
        den = jnp.maximum(jnp.abs(out[h][:, A_DV:A_DV + LANES] + out[h][:, A_DV + LANES:]), ie_b[h][:, LANES:])
        r = 1.0 / den
        hh.append(out[h][:, :A_DV] * jnp.concatenate([r, r], axis=1))
        ms.append(dot((hh[h] * hh[h]).astype(BF16), mean_cols))
    upd = []
    for h in heads:
        rs = lax.rsqrt(ms[h] * (1.0 / A_DV) + RMS_EPS)
        og = o_ref[:, dv_sl(h)].astype(F32)
        y = hh[h] * jnp.concatenate([rs, rs], axis=1) * gain_ref[:, dv_sl(h)] * jax.nn.sigmoid(og)
        h_ref[:, dv_sl(h)] = y.astype(h_ref.dtype)
        wk = (k_ref[:, dk_sl(h)].astype(F32) * (w_c[:, h:h + 1] * scale)).astype(BF16)
        upd.append(lax.dot_general(wk, jnp.concatenate([v_ref[:, dv_sl(h)], ones_l], axis=1), _TN,
                                   preferred_element_type=F32))
    for h in heads:
        c_ref[0, h] = decay[:, h:h + 1] * c_ref[0, h] + upd[h]


def mlstm(proj, g, b_gate, gain, c0, n0, m0, *, batch, seq, L, row0):
    nc = seq // L
    bias = jnp.pad(b_gate.astype(F32), (0, LANES - 2 * A_HEADS))
    row_blk = lambda b, c: row0 // L + b * nc + c
    st = lambda *s: pl.BlockSpec((1,) + s, lambda b, c: (b,) + (0,) * len(s))
    state0 = jnp.concatenate([c0, jnp.broadcast_to(n0[..., None], n0.shape + (LANES,))], axis=-1)
    m_lanes = jnp.pad(m0, ((0, 0), (0, LANES - A_HEADS)))[:, None, :]
    h, state, m = pl.pallas_call(
        functools.partial(_mlstm_body, L=L),
        grid=(batch, nc),
        in_specs=[pl.BlockSpec((L, A_HEADS * A_DK), lambda b, c: (row_blk(b, c), 0)),
                  pl.BlockSpec((L, A_HEADS * A_DK), lambda b, c: (row_blk(b, c), 1)),
                  pl.BlockSpec((L, D_MODEL), lambda b, c: (row_blk(b, c), 1)),
                  pl.BlockSpec((L, D_MODEL), lambda b, c: (row_blk(b, c), 2)),
                  pl.BlockSpec((L, LANES), lambda b, c: (row_blk(b, c), 0)),
                  pl.BlockSpec((1, LANES), lambda b, c: (0, 0)),
                  pl.BlockSpec((1, D_MODEL), lambda b, c: (0, 0)),
                  st(A_HEADS, A_DK, A_STATE), st(1, LANES)],
        out_specs=[pl.BlockSpec((L, D_MODEL), lambda b, c: (b * nc + c, 0)),
                   st(A_HEADS, A_DK, A_STATE), st(1, LANES)],
        out_shape=[jax.ShapeDtypeStruct((batch * seq, D_MODEL), BF16),
                   jax.ShapeDtypeStruct((batch, A_HEADS, A_DK, A_STATE), F32),
                   jax.ShapeDtypeStruct((batch, 1, LANES), F32)],
        compiler_params=_params("parallel", "arbitrary"),
        name="mlstm",
    )(proj, proj, proj, proj, g, bias[None, :], gain.astype(F32)[None, :], state0, m_lanes)
    return h, state[..., :A_DV], state[..., A_DV], m[:, 0, :A_HEADS]


LOG2E = 1.4426950408889634
Q_SCALE = B_DH ** -0.5 * LOG2E


def _scale_q(q):
    return (q.astype(F32) * Q_SCALE).astype(BF16)


def _attn_prompt_body(q_ref, k_ref, v_ref, bias_ref, o_ref, kt_ref, vt_ref, qs, kpad, vpad):
    kt_ref[0] = k_ref[SEQ - REACH:, :].astype(F32)
    vt_ref[0] = v_ref[SEQ - REACH:, :].astype(F32)
    qs[...] = _scale_q(q_ref[...])
    kpad[0:REACH, :] = jnp.zeros((REACH, B_DH), BF16)
    kpad[REACH:, :] = k_ref[...]
    vpad[0:REACH, 0:B_DH] = jnp.zeros((REACH, B_DH), BF16)
    vpad[REACH:, 0:B_DH] = v_ref[...]
    vpad[:, B_DH:] = jnp.ones((SEQ + REACH, B_DH), BF16)

    def scores(t):
        r0 = t * ATT_TQ
        kb = kpad[r0:r0 + ATT_TK, :]
        s = lax.dot_general(qs[r0:r0 + ATT_TQ, :], kb, _NT, preferred_element_type=F32) + bias_ref[0]
        if r0 < REACH:
            j = lax.broadcasted_iota(jnp.int32, (ATT_TQ, ATT_TK), 1)
            s = jnp.where(j + r0 >= REACH, s, NEG)
        return s

    def finish(t, s):
        r0 = t * ATT_TQ
        p = jnp.exp2(s - jnp.max(s, axis=-1, keepdims=True)).astype(BF16)
        acc = jnp.dot(p, vpad[r0:r0 + ATT_TK, :], preferred_element_type=F32)
        o_ref[r0:r0 + ATT_TQ, :] = (acc[:, :B_DH] / acc[:, B_DH:]).astype(o_ref.dtype)

    n_tiles = SEQ // ATT_TQ
    s_next = scores(0)
    for t in range(n_tiles):
        s_cur = s_next
        if t + 1 < n_tiles:
            s_next = scores(t + 1)
        finish(t, s_cur)


def band_bias(rel_table, nq, nk):
    w = nq + nk
    rel = jnp.clip(REACH + nq - 1 - jnp.arange(w), -REL_CLIP, REL_CLIP) + REL_CLIP
    r = rel_table.astype(F32)[:, rel]
    heads = r.shape[0]
    skew = jnp.broadcast_to(r[:, None, :], (heads, nq, w)).reshape(heads, nq * w)
    skew = skew[:, :nq * (w - 1)].reshape(heads, nq, w - 1)
    bias = skew[:, :, nq - 1:nq - 1 + nk]
    i = jnp.arange(nq)[:, None]
    j = jnp.arange(nk)[None, :]
    dc = j // CHUNK - i // CHUNK
    allowed = (dc >= 0) & (dc <= PREV_CHUNKS)
    return jnp.where(allowed[None], bias * LOG2E, NEG)


def attn_prompt(qkv, rel_table):
    bias = band_bias(rel_table, ATT_TQ, ATT_TK)
    tail = pl.BlockSpec((1, REACH, B_DH), lambda b, h: (b, 0, h))
    return pl.pallas_call(
        _attn_prompt_body,
        grid=(BATCH, B_HEADS),
        in_specs=[pl.BlockSpec((SEQ, B_DH), lambda b, h: (b, h)),
                  pl.BlockSpec((SEQ, B_DH), lambda b, h: (b, B_HEADS + h)),
                  pl.BlockSpec((SEQ, B_DH), lambda b, h: (b, 2 * B_HEADS + h)),
                  pl.BlockSpec((1, ATT_TQ, ATT_TK), lambda b, h: (h, 0, 0))],
        out_specs=[pl.BlockSpec((SEQ, B_DH), lambda b, h: (b, h)), tail, tail],
        out_shape=[jax.ShapeDtypeStruct((N_PROMPT, D_MODEL), BF16),
                   jax.ShapeDtypeStruct((BATCH, REACH, D_MODEL), F32),
                   jax.ShapeDtypeStruct((BATCH, REACH, D_MODEL), F32)],
        scratch_shapes=[pltpu.VMEM((SEQ, B_DH), BF16),
                        pltpu.VMEM((SEQ + REACH, B_DH), BF16),
                        pltpu.VMEM((SEQ + REACH, 2 * B_DH), BF16)],
        compiler_params=_params("parallel", "parallel"),
        name="attn_prompt",
    )(qkv, qkv, qkv, bias)


def _attn_sample_body(q_ref, kn_ref, vn_ref, ck_ref, cv_ref, bias_c_ref, bias_n_ref, o_ref, ks_ref, vs_ref):
    ks_ref[0] = kn_ref[...].astype(F32)
    vs_ref[0] = vn_ref[...].astype(F32)
    for h in range(B_HEADS):
        sl = slice(h * B_DH, (h + 1) * B_DH)
        q = _scale_q(q_ref[:, sl])
        kc = ck_ref[pl.ds(h, REACH, stride=B_HEADS), :].astype(BF16)
        vc = cv_ref[pl.ds(h, REACH, stride=B_HEADS), :].astype(BF16)
        s_c = lax.dot_general(q, kc, _NT, preferred_element_type=F32) + bias_c_ref[h]
        s_n = lax.dot_general(q, kn_ref[:, sl], _NT, preferred_element_type=F32) + bias_n_ref[h]
        m = jnp.maximum(jnp.max(s_c, axis=-1, keepdims=True), jnp.max(s_n, axis=-1, keepdims=True))
        p_c = jnp.exp2(s_c - m)
        p_n = jnp.exp2(s_n - m)
        l = jnp.sum(p_c, axis=-1, keepdims=True) + jnp.sum(p_n, axis=-1, keepdims=True)
        dot = functools.partial(jnp.dot, preferred_element_type=F32)
        o_ref[:, sl] = ((dot(p_c.astype(BF16), vc) + dot(p_n.astype(BF16), vn_ref[:, sl])) / l).astype(o_ref.dtype)


def attn_sample(qkv, cache_k, cache_v, layer, rel_table):
    bias = band_bias(rel_table, DEC_SEQ, REACH + DEC_SEQ)
    cache_rows = REACH * B_HEADS
    ck = cache_k.reshape(cache_k.shape[0], DEC_BATCH, cache_rows, B_DH)
    cv = cache_v.reshape(cache_v.shape[0], DEC_BATCH, cache_rows, B_DH)
    row = lambda c: pl.BlockSpec((DEC_SEQ, D_MODEL), lambda b: (N_PROMPT // DEC_SEQ + b, c))
    cache = pl.BlockSpec((None, None, cache_rows, B_DH), lambda b: (layer, b, 0, 0))
    new = pl.BlockSpec((1, DEC_SEQ, D_MODEL), lambda b: (b, 0, 0))
    return pl.pallas_call(
        _attn_sample_body,
        grid=(DEC_BATCH,),
        in_specs=[row(0), row(1), row(2), cache, cache,
                  pl.BlockSpec((B_HEADS, DEC_SEQ, REACH), lambda b: (0, 0, 0)),
                  pl.BlockSpec((B_HEADS, DEC_SEQ, DEC_SEQ), lambda b: (0, 0, 0))],
        out_specs=[pl.BlockSpec((DEC_SEQ, D_MODEL), lambda b: (b, 0)), new, new],
        out_shape=[jax.ShapeDtypeStruct((N_SAMPLE, D_MODEL), BF16),
                   jax.ShapeDtypeStruct((DEC_BATCH, DEC_SEQ, D_MODEL), F32),
                   jax.ShapeDtypeStruct((DEC_BATCH, DEC_SEQ, D_MODEL), F32)],
        compiler_params=_params("parallel"),
        name="attn_sample",
    )(qkv, qkv, qkv, ck, cv, bias[:, :, :REACH], bias[:, :, REACH:])


def _ln(z, g, b):
    mu = jnp.mean(z, axis=-1, keepdims=True)
    zc = z - mu
    var = jnp.mean(zc * zc, axis=-1, keepdims=True)
    return zc * lax.rsqrt(var + LN_EPS) * g + b


PACK_ROWS = D_MODEL // 2 // LANES
HI_MASK = -65536


def _store_packed(ref, v, token0=0):
    t = v.shape[0]
    half = D_MODEL // 2
    hi = lax.bitcast_convert_type(v[:, :half].astype(BF16).astype(F32), jnp.int32)
    lo = lax.bitcast_convert_type(v[:, half:].astype(BF16).astype(F32), jnp.int32)
    words = (hi & HI_MASK) | lax.shift_right_logical(lo, 16)
    for a in range(PACK_ROWS):
        ref[pl.ds(token0 * PACK_ROWS + a, t, stride=PACK_ROWS), :] = words[:, a * LANES:(a + 1) * LANES]


def _load_packed(ref, t):
    words = jnp.concatenate([ref[pl.ds(a, t, stride=PACK_ROWS), :] for a in range(PACK_ROWS)], axis=1)
    hi = lax.bitcast_convert_type(words & HI_MASK, F32)
    lo = lax.bitcast_convert_type(lax.shift_left(words, 16), F32)
    return jnp.concatenate([hi, lo], axis=1)


def _mix_ln_router_body(hp_ref, hs_ref, w_ref, x_ref, g_ref, b_ref, rw_ref, rb_ref,
                        xo_ref, xbo_ref, xp_ref, wt_ref, cls_ref, y_even, y_odd, *, prompt_tiles, n_tiles):
    i = pl.program_id(0)
    from_prompt = jnp.minimum(i, n_tiles - 1) < prompt_tiles

    @pl.when(i == 0)
    def _():
        y_odd[...] = jnp.zeros_like(y_odd)

    half = D_MODEL // 2

    def run(y_new, y_old):
        h = jnp.where(from_prompt, hp_ref[...], hs_ref[...])
        y_new[:, :half] = jnp.dot(h, w_ref[:, :half], preferred_element_type=F32)
        xn = _ln(ALPHA * x_ref[...] + y_old[...], g_ref[...], b_ref[...])
        xo_ref[...] = xn
        xbo_ref[...] = xn.astype(BF16)
        logits = _dot_hilo(xn, rw_ref[...]) + rb_ref[...]
        y_new[:, half:] = jnp.dot(h, w_ref[:, half:], preferred_element_type=F32)
        _store_packed(xp_ref, xn)
        _route(logits, wt_ref, cls_ref)

    @pl.when(i % 2 == 0)
    def _():
        run(y_even, y_odd)

    @pl.when(i % 2 == 1)
    def _():
        run(y_odd, y_even)


def mix_ln_router(h_p, h_s, w, x, g, b, w_group, b_group, w_expert, b_expert, tm=256):
    n = x.shape[0]
    n_tiles = n // tm
    prompt_tiles = h_p.shape[0] // tm
    sample_tiles = h_s.shape[0] // tm
    assert prompt_tiles * tm == h_p.shape[0] and (prompt_tiles + sample_tiles) * tm == n
    mm_tile = lambda i: jnp.minimum(i, n_tiles - 1)
    ep_tile = lambda i: jnp.maximum(i - 1, 0)
    hp_spec = pl.BlockSpec((tm, D_MODEL), lambda i: (jnp.minimum(mm_tile(i), prompt_tiles - 1), 0))
    hs_spec = pl.BlockSpec((tm, D_MODEL), lambda i: (jnp.maximum(mm_tile(i) - prompt_tiles, 0), 0))
    rw = jnp.pad(jnp.concatenate([w_group, w_expert], axis=1), ((0, 0), (0, LANES - N_GROUPS - N_EXPERTS)))
    rb = jnp.pad(jnp.concatenate([b_group, b_expert]).astype(F32), (0, LANES - N_GROUPS - N_EXPERTS))
    row = pl.BlockSpec((tm, D_MODEL), lambda i: (ep_tile(i), 0))
    vec = pl.BlockSpec((1, D_MODEL), lambda i: (0, 0))
    lane_row = pl.BlockSpec((tm, LANES), lambda i: (ep_tile(i), 0))
    return pl.pallas_call(
        functools.partial(_mix_ln_router_body, prompt_tiles=prompt_tiles, n_tiles=n_tiles),
        grid=(n_tiles + 1,),
        in_specs=[hp_spec, hs_spec, pl.BlockSpec((D_MODEL, D_MODEL), lambda i: (0, 0)), row, vec, vec,
                  pl.BlockSpec((D_MODEL, LANES), lambda i: (0, 0)), pl.BlockSpec((1, LANES), lambda i: (0, 0))],
        out_specs=[row, row, pl.BlockSpec((tm * PACK_ROWS, LANES), lambda i: (ep_tile(i), 0)), lane_row, lane_row],
        out_shape=[jax.ShapeDtypeStruct((n, D_MODEL), F32), jax.ShapeDtypeStruct((n, D_MODEL), BF16),
                   jax.ShapeDtypeStruct((n * PACK_ROWS, LANES), jnp.int32),
                   jax.ShapeDtypeStruct((n, LANES), F32), jax.ShapeDtypeStruct((n, LANES), jnp.int32)],
        scratch_shapes=[pltpu.VMEM((tm, D_MODEL), F32), pltpu.VMEM((tm, D_MODEL), F32)],
        compiler_params=_params("arbitrary"),
        name="mix_ln_router",
    )(h_p, h_s, w, x, g[None, :], b[None, :], rw, rb[None, :])


def _add_ln_body(x_ref, y_ref, g_ref, b_ref, xo_ref, xbo_ref):
    y = _load_packed(y_ref, x_ref.shape[0])
    xn = _ln(ALPHA * x_ref[...] + y, g_ref[...], b_ref[...])
    xo_ref[...] = xn
    xbo_ref[...] = xn.astype(BF16)


def residual_ln(x, y_packed, g, b, tm=512):
    n = x.shape[0]
    row = pl.BlockSpec((tm, D_MODEL), lambda i: (i, 0))
    vec = pl.BlockSpec((1, D_MODEL), lambda i: (0, 0))
    return pl.pallas_call(
        _add_ln_body,
        grid=(n // tm,),
        in_specs=[row, pl.BlockSpec((tm * PACK_ROWS, LANES), lambda i: (i, 0)), vec, vec],
        out_specs=[row, row],
        out_shape=[jax.ShapeDtypeStruct((n, D_MODEL), F32), jax.ShapeDtypeStruct((n, D_MODEL), BF16)],
        compiler_params=_params("parallel"),
        name="residual_ln",
    )(x, y_packed, g[None, :], b[None, :])


def _add_ln_split_body(x_ref, y_ref, g_ref, b_ref, op_ref, os_ref, *, prompt_tiles):
    y = _load_packed(y_ref, x_ref.shape[0])
    xn = _ln(ALPHA * x_ref[...] + y, g_ref[...], b_ref[...])
    i = pl.program_id(0)

    @pl.when(i < prompt_tiles)
    def _():
        op_ref[...] = xn

    @pl.when(i >= prompt_tiles)
    def _():
        os_ref[...] = xn


def residual_ln_split(x, y_packed, g, b, n_prompt, tm=512):
    n = x.shape[0]
    prompt_tiles = n_prompt // tm
    assert prompt_tiles * tm == n_prompt and n % tm == 0
    row = pl.BlockSpec((tm, D_MODEL), lambda i: (i, 0))
    vec = pl.BlockSpec((1, D_MODEL), lambda i: (0, 0))
    return pl.pallas_call(
        functools.partial(_add_ln_split_body, prompt_tiles=prompt_tiles),
        grid=(n // tm,),
        in_specs=[row, pl.BlockSpec((tm * PACK_ROWS, LANES), lambda i: (i, 0)), vec, vec],
        out_specs=[pl.BlockSpec((tm, D_MODEL), lambda i: (jnp.minimum(i, prompt_tiles - 1), 0)),
                   pl.BlockSpec((tm, D_MODEL), lambda i: (jnp.maximum(i - prompt_tiles, 0), 0))],
        out_shape=[jax.ShapeDtypeStruct((n_prompt, D_MODEL), F32),
                   jax.ShapeDtypeStruct((n - n_prompt, D_MODEL), F32)],
        compiler_params=_params("arbitrary"),
        name="residual_ln_split",
    )(x, y_packed, g[None, :], b[None, :])


ROUTER_E0 = N_GROUPS


def _route(logits, wt_ref, cls_ref):
    lane = lax.broadcasted_iota(jnp.int32, logits.shape, 1).astype(F32)
    big = float(LANES)
    rmax = lambda a: jnp.max(a, axis=-1, keepdims=True)
    rmin = lambda a: jnp.min(a, axis=-1, keepdims=True)
    is_g = lane < N_GROUPS
    gl = jnp.where(is_g, logits, -jnp.inf)
    gmax = rmax(gl)
    gsel = rmin(jnp.where(gl == gmax, lane, big))
    gsum = jnp.sum(jnp.where(is_g, jnp.exp(logits - gmax), 0.0), axis=-1, keepdims=True)
    g_w = 1.0 / gsum
    e_lo = ROUTER_E0 + EXP_PER_GROUP * gsel
    in_grp = (lane >= e_lo) & (lane < e_lo + EXP_PER_GROUP)
    el = jnp.where(in_grp, logits, -jnp.inf)
    e1 = rmax(el)
    i1 = rmin(jnp.where(el == e1, lane, big))
    el2 = jnp.where(lane == i1, -jnp.inf, el)
    e2 = rmax(el2)
    i2 = rmin(jnp.where(el2 == e2, lane, big))
    t = jnp.exp(e2 - e1)
    p1 = 1.0 / (1.0 + t)
    w1 = p1 * g_w
    w2 = (t * p1) * g_w
    a1 = i1 - e_lo
    a2 = i2 - e_lo
    first_low = a1 < a2
    lo = jnp.where(first_low, a1, a2)
    hi = jnp.where(first_low, a2, a1)
    w_lo = jnp.where(first_low, w1, w2)
    w_hi = jnp.where(first_low, w2, w1)
    off = jnp.where(lo == 0.0, 0.0, jnp.where(lo == 1.0, 3.0, 5.0))
    cls = gsel * float(len(PAIRS)) + off + (hi - lo - 1.0)
    wt_ref[...] = jnp.where(lane < LANES // 2, w_lo, w_hi)
    cls_ref[...] = jnp.broadcast_to(cls, logits.shape).astype(jnp.int32)


def _expert_body(ea_ref, eb_ref, valid_ref, x_ref, wt_ref, wga_ref, wua_ref, wda_ref,
                 wgb_ref, wub_ref, wdb_ref, y_ref):
    i = pl.program_id(0)

    @pl.when(valid_ref[i] == 0)
    def _():
        y_ref[...] = jnp.zeros_like(y_ref)

    @pl.when(valid_ref[i] != 0)
    def _():
        x = _load_packed(x_ref, MOE_TILE).astype(BF16)
        dot = functools.partial(jnp.dot, preferred_element_type=F32)

        ga, ua = dot(x, wga_ref[...]), dot(x, wua_ref[...])
        gb, ub = dot(x, wgb_ref[...]), dot(x, wub_ref[...])
        ya = dot((jax.nn.silu(ga) * ua).astype(BF16), wda_ref[...])
        yb = dot((jax.nn.silu(gb) * ub).astype(BF16), wdb_ref[...])
        w_lo = wt_ref[:, 0:1]
        w_hi = wt_ref[:, LANES // 2:LANES // 2 + 1]
        _store_packed(y_ref, w_lo * ya + w_hi * yb)


def experts(xs, wts, tile_ea, tile_eb, tile_valid, w_gate, w_up, w_down):
    wspec = lambda shape, which: pl.BlockSpec(
        (None,) + shape, lambda i, ea, eb, va: ((ea, eb)[which][i], 0, 0))
    up = (D_MODEL, D_EXPERT)
    down = (D_EXPERT, D_MODEL)
    packed = pl.BlockSpec((MOE_TILE * PACK_ROWS, LANES), lambda i, ea, eb, va: (i, 0))
    grid_spec = pltpu.PrefetchScalarGridSpec(
        num_scalar_prefetch=3,
        grid=(MOE_TILES,),
        in_specs=[packed,
                  pl.BlockSpec((MOE_TILE, LANES), lambda i, ea, eb, va: (i, 0)),
                  wspec(up, 0), wspec(up, 0), wspec(down, 0),
                  wspec(up, 1), wspec(up, 1), wspec(down, 1)],
        out_specs=packed,
    )
    return pl.pallas_call(
        _expert_body,
        grid_spec=grid_spec,
        out_shape=jax.ShapeDtypeStruct((MOE_ROWS * PACK_ROWS, LANES), jnp.int32),
        compiler_params=_params("arbitrary"),
        name="experts",
    )(tile_ea, tile_eb, tile_valid, xs, wts, w_gate, w_up, w_down, w_gate, w_up, w_down)


def sc_gather_rows(x, idx):
    n, d = x.shape
    n_out = idx.shape[0]
    mesh = plsc.VectorSubcoreMesh(core_axis_name="core", subcore_axis_name="subcore")

    @functools.partial(
        pl.kernel,
        out_type=jax.ShapeDtypeStruct((n_out, d), x.dtype),
        mesh=mesh,
    )
    def gather(x_hbm, i_hbm, o_hbm):
        def body(i_vmem, o_vmem):
            pltpu.sync_copy(x_hbm.at[i_vmem.at[0]], o_vmem)

        pltpu.emit_pipeline(
            body,
            grid=(n_out // SC_WINDOW,),
            in_specs=[pl.BlockSpec((1, SC_WINDOW), lambda i: (0, i))],
            out_specs=[pl.BlockSpec((SC_WINDOW, d), lambda i: (i, 0))],
            core_axis_name=("core", "subcore"),
            dimension_semantics=(pltpu.PARALLEL,),
        )(i_hbm, o_hbm)

    return gather(x, idx.reshape(1, n_out))


def sc_gather_tokens(x, idx):
    n = x.shape[0] // PACK_ROWS
    p = idx.shape[0]
    info = plsc.get_sparse_core_info()
    n_workers = info.num_cores * info.num_subcores
    per_worker = p // n_workers
    steps = per_worker // SC_TOKENS
    assert per_worker * n_workers == p and steps * SC_TOKENS == per_worker
    mesh = plsc.VectorSubcoreMesh(core_axis_name="core", subcore_axis_name="subcore")

    @functools.partial(
        pl.kernel,
        out_type=jax.ShapeDtypeStruct((p, PACK_ROWS, LANES), x.dtype),
        mesh=mesh,
        scratch_types=[pltpu.VMEM((per_worker,), jnp.int32),
                       pltpu.VMEM((SC_TOKENS, PACK_ROWS, LANES), x.dtype)],
    )
    def gather(x_hbm, i_hbm, o_hbm, idx_v, buf):
        worker = lax.axis_index("subcore") * info.num_cores + lax.axis_index("core")
        base = worker * per_worker
        pltpu.sync_copy(i_hbm.at[pl.ds(base, per_worker)], idx_v)

        @pl.loop(0, steps)
        def _(s):
            r0 = s * SC_TOKENS
            pltpu.sync_copy(x_hbm.at[idx_v.at[pl.ds(r0, SC_TOKENS)]], buf)
            pltpu.sync_copy(buf, o_hbm.at[pl.ds(base + r0, SC_TOKENS)])

    return gather(x.reshape(n, PACK_ROWS, LANES), idx).reshape(p * PACK_ROWS, LANES)


_CLASS_EA = [g * EXP_PER_GROUP + lo for g in range(N_GROUPS) for lo, hi in PAIRS]
_CLASS_EB = [g * EXP_PER_GROUP + hi for g in range(N_GROUPS) for lo, hi in PAIRS]


def dispatch_plan(cls):
    onehot = (cls[:, None] == jnp.arange(N_CLASSES, dtype=jnp.int32)[None, :]).astype(jnp.int32)
    csum = jnp.cumsum(onehot, axis=0)
    rank = jnp.sum(csum * onehot, axis=1) - 1
    counts = csum[-1]
    tiles_per = (counts + MOE_TILE - 1) // MOE_TILE
    tile_end = jnp.cumsum(tiles_per)
    tile_start = tile_end - tiles_per
    pos = jnp.sum(onehot * tile_start[None, :], axis=1) * MOE_TILE + rank
    n = cls.shape[0]
    src = (jnp.arange(MOE_ROWS, dtype=jnp.int32) % n).at[pos].set(jnp.arange(n, dtype=jnp.int32))
    t = jnp.arange(MOE_TILES, dtype=jnp.int32)
    tile_cls = jnp.sum((t[:, None] >= tile_end[None, :]).astype(jnp.int32), axis=1)
    valid = (tile_cls < N_CLASSES).astype(jnp.int32)
    last_cls = jnp.max(jnp.where(counts > 0, jnp.arange(N_CLASSES, dtype=jnp.int32), 0))
    tile_cls = jnp.where(valid == 1, tile_cls, last_cls)
    ea = jnp.asarray(_CLASS_EA, jnp.int32)[tile_cls]
    eb = jnp.asarray(_CLASS_EB, jnp.int32)[tile_cls]
    return pos, src, ea, eb, valid


def moe(xp, wts, cls, w_gate, w_up, w_down):
    pos, src, ea, eb, valid = dispatch_plan(cls[:, 0])
    xs = sc_gather_tokens(xp, src)
    ws = sc_gather_rows(wts, src)
    ys = experts(xs, ws, ea, eb, valid, w_gate, w_up, w_down)
    return sc_gather_tokens(ys, pos)


def _rows(a):
    return a.reshape(-1, a.shape[-1])


def kernel(x_prompt, x_sample, state_C, state_n, state_m, cache_k, cache_v, a_w_in, a_b_gate, a_norm, a_w_out, b_w_in, b_rel, b_w_out, ln1_g, ln1_b, ln2_g, ln2_b, r_w_group, r_b_group, r_w_expert, r_b_expert, e_w_gate, e_w_up, e_w_down):
    x = jnp.concatenate([_rows(x_prompt), _rows(x_sample)], axis=0)
    xb = x.astype(BF16)
    hk = A_HEADS * A_DK
    n_main = 2 * hk + 2 * D_MODEL
    outs = {k: [] for k in ("Cp", "np", "mp", "kp", "vp", "Cs", "ns", "ms", "ks", "vs")}
    for layer in range(DEPTH):
        j = layer // 2
        if layer % 2 == 0:
            proj = matmul(xb, a_w_in, j, n_main, BF16)
            g = gate_preacts(x, a_w_in[j, :, n_main:])
            zc = jnp.zeros((BATCH, A_HEADS, A_DK, A_DV), F32)
            zn = jnp.zeros((BATCH, A_HEADS, A_DK), F32)
            zm = jnp.zeros((BATCH, A_HEADS), F32)
            h_p, c_p, n_p, m_p = mlstm(proj, g, a_b_gate[j], a_norm[j], zc, zn, zm,
                                       batch=BATCH, seq=SEQ, L=MLSTM_CHUNK, row0=0)
            h_s, c_s, n_s, m_s = mlstm(proj, g, a_b_gate[j], a_norm[j],
                                       state_C[j], state_n[j], state_m[j],
                                       batch=DEC_BATCH, seq=DEC_SEQ, L=DEC_SEQ, row0=N_PROMPT)
            outs["Cp"].append(c_p); outs["np"].append(n_p); outs["mp"].append(m_p)
            outs["Cs"].append(c_s); outs["ns"].append(n_s); outs["ms"].append(m_s)
            w_out = a_w_out[j]
        else:
            qkv = matmul(xb, b_w_in, j, 3 * D_MODEL, BF16)
            h_p, k_p, v_p = attn_prompt(qkv, b_rel[j])
            h_s, k_s, v_s = attn_sample(qkv, cache_k, cache_v, j, b_rel[j])
            heads = lambda a: a.reshape(a.shape[:2] + (B_HEADS, B_DH))
            outs["kp"].append(heads(k_p)); outs["vp"].append(heads(v_p))
            outs["ks"].append(heads(k_s)); outs["vs"].append(heads(v_s))
            w_out = b_w_out[j]
        x, xb, xp, wts, cls = mix_ln_router(h_p, h_s, w_out.astype(BF16), x, ln1_g[layer], ln1_b[layer],
                                            r_w_group[layer], r_b_group[layer], r_w_expert[layer], r_b_expert[layer])
        y = moe(xp, wts, cls, cast_layer(e_w_gate, layer), cast_layer(e_w_up, layer), cast_layer(e_w_down, layer))
        if layer < DEPTH - 1:
            x, xb = residual_ln(x, y, ln2_g[layer], ln2_b[layer])
        else:
            y_p, y_s = residual_ln_split(x, y, ln2_g[layer], ln2_b[layer], N_PROMPT)
    st = lambda k: jnp.stack(outs[k])
    return (y_p.reshape(BATCH, SEQ, D_MODEL), y_s.reshape(DEC_BATCH, DEC_SEQ, D_MODEL),
            st("Cp"), st("np"), st("mp"), st("kp"), st("vp"),
            st("Cs"), st("ns"), st("ms"), st("ks"), st("vs"))
```

```python
import functools

import jax
import jax.numpy as jnp
from jax import lax
from jax.experimental import pallas as pl
from jax.experimental.pallas import tpu as pltpu
from jax.experimental.pallas import tpu_sc as plsc

F32 = jnp.float32
BF16 = jnp.bfloat16

D_MODEL = 2048
BATCH = 8
SEQ = 4096
DEPTH = 4
DEC_BATCH = 32
DEC_SEQ = 32
PAST_LEN = 1024
CHUNK = 64
A_HEADS = 8
A_DK = 128
A_DV = D_MODEL // A_HEADS
GATE_CAP = 15.0
B_HEADS = 16
B_DH = D_MODEL // B_HEADS
PREV_CHUNKS = 8
REACH = PREV_CHUNKS * CHUNK
REL_CLIP = 256
N_GROUPS = 4
EXP_PER_GROUP = 4
N_EXPERTS = N_GROUPS * EXP_PER_GROUP
D_EXPERT = D_MODEL // 4
ALPHA = (2 * DEPTH) ** 0.25
LN_EPS = 1e-5
RMS_EPS = 1e-6

N_PROMPT = BATCH * SEQ
N_SAMPLE = DEC_BATCH * DEC_SEQ
N_TOK = N_PROMPT + N_SAMPLE

VMEM_LIMIT = 56 * 1024 * 1024
LANES = 128

PAIRS = ((0, 1), (0, 2), (0, 3), (1, 2), (1, 3), (2, 3))
N_CLASSES = N_GROUPS * len(PAIRS)
MOE_TILE = 256
MOE_TILES = -(-(N_TOK // MOE_TILE + N_CLASSES) // 32) * 32
MOE_ROWS = MOE_TILES * MOE_TILE

MLSTM_CHUNK = 128
ATT_TQ = 256
ATT_TK = ATT_TQ + REACH
NEG = -1e30

SC_WINDOW = 128
SC_TOKENS = 32


def _params(*sem):
    return pltpu.CompilerParams(dimension_semantics=sem, vmem_limit_bytes=VMEM_LIMIT)


def _mm_body(x_ref, w_ref, o_ref, wb):
    @pl.when(pl.program_id(1) == 0)
    def _():
        wb[...] = w_ref[...].astype(BF16)

    o_ref[...] = jnp.dot(x_ref[...], wb[...], preferred_element_type=F32).astype(o_ref.dtype)


def matmul(x, w_all, layer, m, out_dtype, tm=512, tn=1024):
    n, k = x.shape
    return pl.pallas_call(
        _mm_body,
        grid=(m // tn, n // tm),
        in_specs=[pl.BlockSpec((tm, k), lambda j, i: (i, 0)),
                  pl.BlockSpec((None, k, tn), lambda j, i: (layer, 0, j))],
        out_specs=pl.BlockSpec((tm, tn), lambda j, i: (i, j)),
        out_shape=jax.ShapeDtypeStruct((n, m), out_dtype),
        scratch_shapes=[pltpu.VMEM((k, tn), BF16)],
        compiler_params=_params("parallel", "arbitrary"),
        name="matmul",
    )(x, w_all)


def _cast_body(w_ref, o_ref):
    o_ref[...] = w_ref[...].astype(BF16)


def cast_layer(w_all, layer):
    e, a, b = w_all.shape[1:]
    per_step = 2
    return pl.pallas_call(
        _cast_body,
        grid=(e // per_step,),
        in_specs=[pl.BlockSpec((None, per_step, a, b), lambda i: (layer, i, 0, 0))],
        out_specs=pl.BlockSpec((per_step, a, b), lambda i: (i, 0, 0)),
        out_shape=jax.ShapeDtypeStruct((e, a, b), BF16),
        compiler_params=_params("parallel"),
        name="cast_layer",
    )(w_all)


def _split2(x):
    hi = x.astype(BF16)
    lo = (x - hi.astype(F32)).astype(BF16)
    return hi, lo


def _dot3(xh, xl, wh, wl, dims):
    dg = functools.partial(lax.dot_general, dimension_numbers=dims, preferred_element_type=F32)
    return dg(xh, wh) + (dg(xh, wl) + dg(xl, wh))


_NN = (((1,), (0,)), ((), ()))
_NT = (((1,), (1,)), ((), ()))
_TN = (((0,), (0,)), ((), ()))


def _dot_hilo(x, w):
    t = x.shape[0]
    xh, xl = _split2(x)
    wh, wl = _split2(w)
    r = jnp.dot(jnp.concatenate([xh, xl], axis=0), jnp.concatenate([wh, wl], axis=1), preferred_element_type=F32)
    return r[:t, :LANES] + (r[:t, LANES:] + r[t:, :LANES]) + r[t:, LANES:]


def _log_sigmoid(x):
    return jnp.minimum(x, 0.0) - jnp.log(1.0 + jnp.exp(-jnp.abs(x)))


def _split3(x):
    a = x.astype(BF16)
    r = x - a.astype(F32)
    b = r.astype(BF16)
    c = (r - b.astype(F32)).astype(BF16)
    return a, b, c


A_STATE = A_DV + LANES


def _cummax_rows(x):
    n = x.shape[0]
    row = lax.broadcasted_iota(jnp.int32, x.shape, 0)
    k = 1
    while k < n:
        x = jnp.maximum(x, jnp.where(row >= k, pltpu.roll(x, k, axis=0), -jnp.inf))
        k *= 2
    return x


def _mlstm_body(q_ref, k_ref, v_ref, o_ref, g_ref, bias_ref, gain_ref, c0_ref, m0_ref,
                h_ref, c_ref, m_ref, *, L):
    step = pl.program_id(1)

    @pl.when(step == 0)
    def _():
        c_ref[...] = c0_ref[...]
        m_ref[...] = m0_ref[...]

    dot = functools.partial(jnp.dot, preferred_element_type=F32)
    row = lax.broadcasted_iota(jnp.int32, (L, L), 0)
    col = lax.broadcasted_iota(jnp.int32, (L, L), 1)
    causal = col <= row
    tril = jnp.where(causal, 1.0, 0.0).astype(BF16)

    pre = GATE_CAP * jnp.tanh((g_ref[...] + bias_ref[...]) * (1.0 / GATE_CAP))
    ig = pre
    lf = _log_sigmoid(pltpu.roll(pre, LANES - A_HEADS, axis=1))
    a3, b3, c3 = _split3(lf)
    bt = dot(tril, a3) + (dot(tril, b3) + dot(tril, c3))
    a = ig - bt
    m_prev = m_ref[0]
    m_t = bt + jnp.maximum(m_prev, _cummax_rows(a))
    u = bt - m_t
    inter = jnp.exp(bt + m_prev - m_t)
    eminus = jnp.exp(-m_t)
    m_new = m_t[L - 1:L, :]
    bt_last = bt[L - 1:L, :]
    w_c = jnp.exp((bt_last - m_new) + a)
    decay = jnp.exp(bt_last + m_prev - m_new)
    m_ref[0] = m_new
    a_pad = a if L == LANES else jnp.concatenate([a, jnp.zeros((LANES - L, LANES), F32)], axis=0)
    a_t = jnp.transpose(a_pad)
    ie = jnp.concatenate([inter, eminus], axis=1).astype(BF16)
    sel_r = lax.broadcasted_iota(jnp.int32, (2 * LANES, 2 * LANES), 0)
    sel_c = lax.broadcasted_iota(jnp.int32, (2 * LANES, 2 * LANES), 1)
    same_half = (sel_r >= LANES) == (sel_c >= LANES)
    ones_l = jnp.ones((L, LANES), BF16)
    zeros_l = jnp.zeros((L, LANES), BF16)
    zeros_k = jnp.zeros((A_DK, LANES), BF16)
    mean_cols = jnp.full((A_DV, LANES), 1.0, BF16)

    scale = A_DK ** -0.5
    heads = range(A_HEADS)
    dk_sl = lambda h: slice(h * A_DK, (h + 1) * A_DK)
    dv_sl = lambda h: slice(h * A_DV, (h + 1) * A_DV)
    ie_b, qk = [], []
    for h in heads:
        sel = jnp.where(same_half & ((sel_r & (LANES - 1)) == h), 1.0, 0.0).astype(BF16)
        ie_b.append(dot(ie, sel))
        qk.append(lax.dot_general(q_ref[:, dk_sl(h)], k_ref[:, dk_sl(h)], _NT, preferred_element_type=F32))
    out = []
    for h in heads:
        d_mat = jnp.where(causal, jnp.exp(u[:, h:h + 1] + a_t[h:h + 1, :L]), 0.0)
        s = qk[h] * (d_mat * scale)
        q_i = (q_ref[:, dk_sl(h)].astype(F32) * ie_b[h][:, :LANES]).astype(BF16)
        c_b = c_ref[0, h].astype(BF16)
        rhs = jnp.concatenate([
            jnp.concatenate([v_ref[:, dv_sl(h)], ones_l, zeros_l], axis=1),
            jnp.concatenate([c_b[:, :A_DV], zeros_k, c_b[:, A_DV:]], axis=1)], axis=0)
        out.append(dot(jnp.concatenate([s.astype(BF16), q_i], axis=1), rhs))
    hh, ms = [], []
    for h in heads:
        den = jnp.maximum(jnp.abs(out[h][:, A_DV:A_DV + LANES] + out[h][:, A_DV + LANES:]), ie_b[h][:, LANES:])
        r = 1.0 / den
        hh.append(out[h][:, :A_DV] * jnp.concatenate([r, r], axis=1))
        ms.append(dot((hh[h] * hh[h]).astype(BF16), mean_cols))
    upd = []
    for h in heads:
        rs = lax.rsqrt(ms[h] * (1.0 / A_DV) + RMS_EPS)
        og = o_ref[:, dv_sl(h)].astype(F32)
        y = hh[h] * jnp.concatenate([rs, rs], axis=1) * gain_ref[:, dv_sl(h)] * jax.nn.sigmoid(og)
        h_ref[:, dv_sl(h)] = y.astype(h_ref.dtype)
        wk = (k_ref[:, dk_sl(h)].astype(F32) * (w_c[:, h:h + 1] * scale)).astype(BF16)
        upd.append(lax.dot_general(wk, jnp.concatenate([v_ref[:, dv_sl(h)], ones_l], axis=1), _TN,
                                   preferred_element_type=F32))
    for h in heads:
        c_ref[0, h] = decay[:, h:h + 1] * c_ref[0, h] + upd[h]


def mlstm(proj, g, b_gate, gain, c0, n0, m0, *, batch, seq, L, row0):
    nc = seq // L
    bias = jnp.pad(b_gate.astype(F32), (0, LANES - 2 * A_HEADS))
    row_blk = lambda b, c: row0 // L + b * nc + c
    st = lambda *s: pl.BlockSpec((1,) + s, lambda b, c: (b,) + (0,) * len(s))
    state0 = jnp.concatenate([c0, jnp.broadcast_to(n0[..., None], n0.shape + (LANES,))], axis=-1)
    m_lanes = jnp.pad(m0, ((0, 0), (0, LANES - A_HEADS)))[:, None, :]
    h, state, m = pl.pallas_call(
        functools.partial(_mlstm_body, L=L),
        grid=(batch, nc),
        in_specs=[pl.BlockSpec((L, A_HEADS * A_DK), lambda b, c: (row_blk(b, c), 0)),
                  pl.BlockSpec((L, A_HEADS * A_DK), lambda b, c: (row_blk(b, c), 1)),
                  pl.BlockSpec((L, D_MODEL), lambda b, c: (row_blk(b, c), 1)),
                  pl.BlockSpec((L, D_MODEL), lambda b, c: (row_blk(b, c), 2)),
                  pl.BlockSpec((L, LANES), lambda b, c: (row_blk(b, c), 0)),
                  pl.BlockSpec((1, LANES), lambda b, c: (0, 0)),
                  pl.BlockSpec((1, D_MODEL), lambda b, c: (0, 0)),
                  st(A_HEADS, A_DK, A_STATE), st(1, LANES)],
        out_specs=[pl.BlockSpec((L, D_MODEL), lambda b, c: (b * nc + c, 0)),
                   st(A_HEADS, A_DK, A_STATE), st(1, LANES)],
        out_shape=[jax.ShapeDtypeStruct((batch * seq, D_MODEL), BF16),
                   jax.ShapeDtypeStruct((batch, A_HEADS, A_DK, A_STATE), F32),
                   jax.ShapeDtypeStruct((batch, 1, LANES), F32)],
        compiler_params=_params("parallel", "arbitrary"),
        name="mlstm",
    )(proj, proj, proj, proj, g, bias[None, :], gain.astype(F32)[None, :], state0, m_lanes)
    return h, state[..., :A_DV], state[..., A_DV], m[:, 0, :A_HEADS]


LOG2E = 1.4426950408889634
Q_SCALE = B_DH ** -0.5 * LOG2E


def _scale_q(q):
    return (q.astype(F32) * Q_SCALE).astype(BF16)


def _attn_prompt_body(q_ref, k_ref, v_ref, bias_ref, o_ref, kt_ref, vt_ref, qs, kpad, vpad):
    kt_ref[0] = k_ref[SEQ - REACH:, :].astype(F32)
    vt_ref[0] = v_ref[SEQ - REACH:, :].astype(F32)
    qs[...] = _scale_q(q_ref[...])
    kpad[0:REACH, :] = jnp.zeros((REACH, B_DH), BF16)
    kpad[REACH:, :] = k_ref[...]
    vpad[0:REACH, 0:B_DH] = jnp.zeros((REACH, B_DH), BF16)
    vpad[REACH:, 0:B_DH] = v_ref[...]
    vpad[:, B_DH:] = jnp.ones((SEQ + REACH, B_DH), BF16)

    def scores(t):
        r0 = t * ATT_TQ
        kb = kpad[r0:r0 + ATT_TK, :]
        s = lax.dot_general(qs[r0:r0 + ATT_TQ, :], kb, _NT, preferred_element_type=F32) + bias_ref[0]
        if r0 < REACH:
            j = lax.broadcasted_iota(jnp.int32, (ATT_TQ, ATT_TK), 1)
            s = jnp.where(j + r0 >= REACH, s, NEG)
        return s

    def finish(t, s):
        r0 = t * ATT_TQ
        p = jnp.exp2(s - jnp.max(s, axis=-1, keepdims=True)).astype(BF16)
        acc = jnp.dot(p, vpad[r0:r0 + ATT_TK, :], preferred_element_type=F32)
        o_ref[r0:r0 + ATT_TQ, :] = (acc[:, :B_DH] / acc[:, B_DH:]).astype(o_ref.dtype)

    n_tiles = SEQ // ATT_TQ
    s_next = scores(0)
    for t in range(n_tiles):
        s_cur = s_next
        if t + 1 < n_tiles:
            s_next = scores(t + 1)
        finish(t, s_cur)


def band_bias(rel_table, nq, nk):
    w = nq + nk
    rel = jnp.clip(REACH + nq - 1 - jnp.arange(w), -REL_CLIP, REL_CLIP) + REL_CLIP
    r = rel_table.astype(F32)[:, rel]
    heads = r.shape[0]
    skew = jnp.broadcast_to(r[:, None, :], (heads, nq, w)).reshape(heads, nq * w)
    skew = skew[:, :nq * (w - 1)].reshape(heads, nq, w - 1)
    bias = skew[:, :, nq - 1:nq - 1 + nk]
    i = jnp.arange(nq)[:, None]
    j = jnp.arange(nk)[None, :]
    dc = j // CHUNK - i // CHUNK
    allowed = (dc >= 0) & (dc <= PREV_CHUNKS)
    return jnp.where(allowed[None], bias * LOG2E, NEG)


def attn_prompt(qkv, rel_table):
    bias = band_bias(rel_table, ATT_TQ, ATT_TK)
    tail = pl.BlockSpec((1, REACH, B_DH), lambda b, h: (b, 0, h))
    return pl.pallas_call(
        _attn_prompt_body,
        grid=(BATCH, B_HEADS),
        in_specs=[pl.BlockSpec((SEQ, B_DH), lambda b, h: (b, h)),
                  pl.BlockSpec((SEQ, B_DH), lambda b, h: (b, B_HEADS + h)),
                  pl.BlockSpec((SEQ, B_DH), lambda b, h: (b, 2 * B_HEADS + h)),
                  pl.BlockSpec((1, ATT_TQ, ATT_TK), lambda b, h: (h, 0, 0))],
        out_specs=[pl.BlockSpec((SEQ, B_DH), lambda b, h: (b, h)), tail, tail],
        out_shape=[jax.ShapeDtypeStruct((N_PROMPT, D_MODEL), BF16),
                   jax.ShapeDtypeStruct((BATCH, REACH, D_MODEL), F32),
                   jax.ShapeDtypeStruct((BATCH, REACH, D_MODEL), F32)],
        scratch_shapes=[pltpu.VMEM((SEQ, B_DH), BF16),
                        pltpu.VMEM((SEQ + REACH, B_DH), BF16),
                        pltpu.VMEM((SEQ + REACH, 2 * B_DH), BF16)],
        compiler_params=_params("parallel", "parallel"),
        name="attn_prompt",
    )(qkv, qkv, qkv, bias)


def _attn_sample_body(q_ref, kn_ref, vn_ref, ck_ref, cv_ref, bias_c_ref, bias_n_ref, o_ref, ks_ref, vs_ref):
    ks_ref[0] = kn_ref[...].astype(F32)
    vs_ref[0] = vn_ref[...].astype(F32)
    for h in range(B_HEADS):
        sl = slice(h * B_DH, (h + 1) * B_DH)
        q = _scale_q(q_ref[:, sl])
        kc = ck_ref[pl.ds(h, REACH, stride=B_HEADS), :].astype(BF16)
        vc = cv_ref[pl.ds(h, REACH, stride=B_HEADS), :].astype(BF16)
        s_c = lax.dot_general(q, kc, _NT, preferred_element_type=F32) + bias_c_ref[h]
        s_n = lax.dot_general(q, kn_ref[:, sl], _NT, preferred_element_type=F32) + bias_n_ref[h]
        m = jnp.maximum(jnp.max(s_c, axis=-1, keepdims=True), jnp.max(s_n, axis=-1, keepdims=True))
        p_c = jnp.exp2(s_c - m)
        p_n = jnp.exp2(s_n - m)
        l = jnp.sum(p_c, axis=-1, keepdims=True) + jnp.sum(p_n, axis=-1, keepdims=True)
        dot = functools.partial(jnp.dot, preferred_element_type=F32)
        o_ref[:, sl] = ((dot(p_c.astype(BF16), vc) + dot(p_n.astype(BF16), vn_ref[:, sl])) / l).astype(o_ref.dtype)


def attn_sample(qkv, cache_k, cache_v, layer, rel_table):
    bias = band_bias(rel_table, DEC_SEQ, REACH + DEC_SEQ)
    cache_rows = REACH * B_HEADS
    ck = cache_k.reshape(cache_k.shape[0], DEC_BATCH, cache_rows, B_DH)
    cv = cache_v.reshape(cache_v.shape[0], DEC_BATCH, cache_rows, B_DH)
    row = lambda c: pl.BlockSpec((DEC_SEQ, D_MODEL), lambda b: (N_PROMPT // DEC_SEQ + b, c))
    cache = pl.BlockSpec((None, None, cache_rows, B_DH), lambda b: (layer, b, 0, 0))
    new = pl.BlockSpec((1, DEC_SEQ, D_MODEL), lambda b: (b, 0, 0))
    return pl.pallas_call(
        _attn_sample_body,
        grid=(DEC_BATCH,),
        in_specs=[row(0), row(1), row(2), cache, cache,
                  pl.BlockSpec((B_HEADS, DEC_SEQ, REACH), lambda b: (0, 0, 0)),
                  pl.BlockSpec((B_HEADS, DEC_SEQ, DEC_SEQ), lambda b: (0, 0, 0))],
        out_specs=[pl.BlockSpec((DEC_SEQ, D_MODEL), lambda b: (b, 0)), new, new],
        out_shape=[jax.ShapeDtypeStruct((N_SAMPLE, D_MODEL), BF16),
                   jax.ShapeDtypeStruct((DEC_BATCH, DEC_SEQ, D_MODEL), F32),
                   jax.ShapeDtypeStruct((DEC_BATCH, DEC_SEQ, D_MODEL), F32)],
        compiler_params=_params("parallel"),
        name="attn_sample",
    )(qkv, qkv, qkv, ck, cv, bias[:, :, :REACH], bias[:, :, REACH:])


def _ln(z, g, b):
    mu = jnp.mean(z, axis=-1, keepdims=True)
    zc = z - mu
    var = jnp.mean(zc * zc, axis=-1, keepdims=True)
    return zc * lax.rsqrt(var + LN_EPS) * g + b


PACK_ROWS = D_MODEL // 2 // LANES
HI_MASK = -65536


def _store_packed(ref, v, token0=0):
    t = v.shape[0]
    half = D_MODEL // 2
    hi = lax.bitcast_convert_type(v[:, :half].astype(BF16).astype(F32), jnp.int32)
    lo = lax.bitcast_convert_type(v[:, half:].astype(BF16).astype(F32), jnp.int32)
    words = (hi & HI_MASK) | lax.shift_right_logical(lo, 16)
    for a in range(PACK_ROWS):
        ref[pl.ds(token0 * PACK_ROWS + a, t, stride=PACK_ROWS), :] = words[:, a * LANES:(a + 1) * LANES]


def _load_packed(ref, t):
    words = jnp.concatenate([ref[pl.ds(a, t, stride=PACK_ROWS), :] for a in range(PACK_ROWS)], axis=1)
    hi = lax.bitcast_convert_type(words & HI_MASK, F32)
    lo = lax.bitcast_convert_type(lax.shift_left(words, 16), F32)
    return jnp.concatenate([hi, lo], axis=1)


def _mix_ln_router_body(hp_ref, hs_ref, w_ref, x_ref, g_ref, b_ref, rw_ref, rb_ref,
                        xo_ref, xbo_ref, xp_ref, wt_ref, cls_ref, y_even, y_odd, *, prompt_tiles, n_tiles):
    i = pl.program_id(0)
    from_prompt = jnp.minimum(i, n_tiles - 1) < prompt_tiles

    @pl.when(i == 0)
    def _():
        y_odd[...] = jnp.zeros_like(y_odd)

    half = D_MODEL // 2

    def run(y_new, y_old):
        h = jnp.where(from_prompt, hp_ref[...], hs_ref[...])
        y_new[:, :half] = jnp.dot(h, w_ref[:, :half], preferred_element_type=F32)
        xn = _ln(ALPHA * x_ref[...] + y_old[...], g_ref[...], b_ref[...])
        xo_ref[...] = xn
        xbo_ref[...] = xn.astype(BF16)
        logits = _dot_hilo(xn, rw_ref[...]) + rb_ref[...]
        y_new[:, half:] = jnp.dot(h, w_ref[:, half:], preferred_element_type=F32)
        _store_packed(xp_ref, xn)
        _route(logits, wt_ref, cls_ref)

    @pl.when(i % 2 == 0)
    def _():
        run(y_even, y_odd)

    @pl.when(i % 2 == 1)
    def _():
        run(y_odd, y_even)


def mix_ln_router(h_p, h_s, w, x, g, b, w_group, b_group, w_expert, b_expert, tm=256):
    n = x.shape[0]
    n_tiles = n // tm
    prompt_tiles = h_p.shape[0] // tm
    sample_tiles = h_s.shape[0] // tm
    assert prompt_tiles * tm == h_p.shape[0] and (prompt_tiles + sample_tiles) * tm == n
    mm_tile = lambda i: jnp.minimum(i, n_tiles - 1)
    ep_tile = lambda i: jnp.maximum(i - 1, 0)
    hp_spec = pl.BlockSpec((tm, D_MODEL), lambda i: (jnp.minimum(mm_tile(i), prompt_tiles - 1), 0))
    hs_spec = pl.BlockSpec((tm, D_MODEL), lambda i: (jnp.maximum(mm_tile(i) - prompt_tiles, 0), 0))
    rw = jnp.pad(jnp.concatenate([w_group, w_expert], axis=1), ((0, 0), (0, LANES - N_GROUPS - N_EXPERTS)))
    rb = jnp.pad(jnp.concatenate([b_group, b_expert]).astype(F32), (0, LANES - N_GROUPS - N_EXPERTS))
    row = pl.BlockSpec((tm, D_MODEL), lambda i: (ep_tile(i), 0))
    vec = pl.BlockSpec((1, D_MODEL), lambda i: (0, 0))
    lane_row = pl.BlockSpec((tm, LANES), lambda i: (ep_tile(i), 0))
    return pl.pallas_call(
        functools.partial(_mix_ln_router_body, prompt_tiles=prompt_tiles, n_tiles=n_tiles),
        grid=(n_tiles + 1,),
        in_specs=[hp_spec, hs_spec, pl.BlockSpec((D_MODEL, D_MODEL), lambda i: (0, 0)), row, vec, vec,
                  pl.BlockSpec((D_MODEL, LANES), lambda i: (0, 0)), pl.BlockSpec((1, LANES), lambda i: (0, 0))],
        out_specs=[row, row, pl.BlockSpec((tm * PACK_ROWS, LANES), lambda i: (ep_tile(i), 0)), lane_row, lane_row],
        out_shape=[jax.ShapeDtypeStruct((n, D_MODEL), F32), jax.ShapeDtypeStruct((n, D_MODEL), BF16),
                   jax.ShapeDtypeStruct((n * PACK_ROWS, LANES), jnp.int32),
                   jax.ShapeDtypeStruct((n, LANES), F32), jax.ShapeDtypeStruct((n, LANES), jnp.int32)],
        scratch_shapes=[pltpu.VMEM((tm, D_MODEL), F32), pltpu.VMEM((tm, D_MODEL), F32)],
        compiler_params=_params("arbitrary"),
        name="mix_ln_router",
    )(h_p, h_s, w, x, g[None, :], b[None, :], rw, rb[None, :])


def _add_ln_body(x_ref, y_ref, g_ref, b_ref, *rest):
    y = _load_packed(y_ref, x_ref.shape[0])
    xn = _ln(ALPHA * x_ref[...] + y, g_ref[...], b_ref[...])
    if len(rest) == 2:
        xo_ref, xbo_ref = rest
    else:
        wg_ref, xo_ref, xbo_ref, go_ref = rest
        go_ref[...] = _dot_hilo(xn, wg_ref[...])
    xo_ref[...] = xn
    xbo_ref[...] = xn.astype(BF16)


def residual_ln(x, y_packed, g, b, w_gate=None, tm=512):
    n = x.shape[0]
    row = pl.BlockSpec((tm, D_MODEL), lambda i: (i, 0))
    vec = pl.BlockSpec((1, D_MODEL), lambda i: (0, 0))
    in_specs = [row, pl.BlockSpec((tm * PACK_ROWS, LANES), lambda i: (i, 0)), vec, vec]
    out_specs = [row, row]
    out_shape = [jax.ShapeDtypeStruct((n, D_MODEL), F32), jax.ShapeDtypeStruct((n, D_MODEL), BF16)]
    args = [x, y_packed, g[None, :], b[None, :]]
    if w_gate is not None:
        in_specs.append(pl.BlockSpec((D_MODEL, LANES), lambda i: (0, 0)))
        out_specs.append(pl.BlockSpec((tm, LANES), lambda i: (i, 0)))
        out_shape.append(jax.ShapeDtypeStruct((n, LANES), F32))
        args.append(jnp.pad(w_gate, ((0, 0), (0, LANES - w_gate.shape[1]))))
    return pl.pallas_call(
        _add_ln_body,
        grid=(n // tm,),
        in_specs=in_specs,
        out_specs=out_specs,
        out_shape=out_shape,
        compiler_params=_params("parallel"),
        name="residual_ln",
    )(*args)


def _embed_body(xp_ref, xs_ref, wg_ref, xo_ref, xbo_ref, go_ref, *, prompt_tiles):
    x = jnp.where(pl.program_id(0) < prompt_tiles, xp_ref[...], xs_ref[...])
    xo_ref[...] = x
    xbo_ref[...] = x.astype(BF16)
    go_ref[...] = _dot_hilo(x, wg_ref[...])


def embed(x_p, x_s, w_gate, tm=512):
    n = x_p.shape[0] + x_s.shape[0]
    prompt_tiles = x_p.shape[0] // tm
    assert prompt_tiles * tm == x_p.shape[0] and n % tm == 0
    row = pl.BlockSpec((tm, D_MODEL), lambda i: (i, 0))
    wg = jnp.pad(w_gate, ((0, 0), (0, LANES - w_gate.shape[1])))
    return pl.pallas_call(
        functools.partial(_embed_body, prompt_tiles=prompt_tiles),
        grid=(n // tm,),
        in_specs=[pl.BlockSpec((tm, D_MODEL), lambda i: (jnp.minimum(i, prompt_tiles - 1), 0)),
                  pl.BlockSpec((tm, D_MODEL), lambda i: (jnp.maximum(i - prompt_tiles, 0), 0)),
                  pl.BlockSpec((D_MODEL, LANES), lambda i: (0, 0))],
        out_specs=[row, row, pl.BlockSpec((tm, LANES), lambda i: (i, 0))],
        out_shape=[jax.ShapeDtypeStruct((n, D_MODEL), F32), jax.ShapeDtypeStruct((n, D_MODEL), BF16),
                   jax.ShapeDtypeStruct((n, LANES), F32)],
        compiler_params=_params("parallel"),
        name="embed",
    )(x_p, x_s, wg)


def _add_ln_split_body(x_ref, y_ref, g_ref, b_ref, op_ref, os_ref, *, prompt_tiles):
    y = _load_packed(y_ref, x_ref.shape[0])
    xn = _ln(ALPHA * x_ref[...] + y, g_ref[...], b_ref[...])
    i = pl.program_id(0)

    @pl.when(i < prompt_tiles)
    def _():
        op_ref[...] = xn

    @pl.when(i >= prompt_tiles)
    def _():
        os_ref[...] = xn


def residual_ln_split(x, y_packed, g, b, n_prompt, tm=512):
    n = x.shape[0]
    prompt_tiles = n_prompt // tm
    assert prompt_tiles * tm == n_prompt and n % tm == 0
    row = pl.BlockSpec((tm, D_MODEL), lambda i: (i, 0))
    vec = pl.BlockSpec((1, D_MODEL), lambda i: (0, 0))
    return pl.pallas_call(
        functools.partial(_add_ln_split_body, prompt_tiles=prompt_tiles),
        grid=(n // tm,),
        in_specs=[row, pl.BlockSpec((tm * PACK_ROWS, LANES), lambda i: (i, 0)), vec, vec],
        out_specs=[pl.BlockSpec((tm, D_MODEL), lambda i: (jnp.minimum(i, prompt_tiles - 1), 0)),
                   pl.BlockSpec((tm, D_MODEL), lambda i: (jnp.maximum(i - prompt_tiles, 0), 0))],
        out_shape=[jax.ShapeDtypeStruct((n_prompt, D_MODEL), F32),
                   jax.ShapeDtypeStruct((n - n_prompt, D_MODEL), F32)],
        compiler_params=_params("arbitrary"),
        name="residual_ln_split",
    )(x, y_packed, g[None, :], b[None, :])


ROUTER_E0 = N_GROUPS


def _route(logits, wt_ref, cls_ref):
    lane = lax.broadcasted_iota(jnp.int32, logits.shape, 1).astype(F32)
    big = float(LANES)
    rmax = lambda a: jnp.max(a, axis=-1, keepdims=True)
    rmin = lambda a: jnp.min(a, axis=-1, keepdims=True)
    is_g = lane < N_GROUPS
    gl = jnp.where(is_g, logits, -jnp.inf)
    gmax = rmax(gl)
    gsel = rmin(jnp.where(gl == gmax, lane, big))
    gsum = jnp.sum(jnp.where(is_g, jnp.exp(logits - gmax), 0.0), axis=-1, keepdims=True)
    g_w = 1.0 / gsum
    e_lo = ROUTER_E0 + EXP_PER_GROUP * gsel
    in_grp = (lane >= e_lo) & (lane < e_lo + EXP_PER_GROUP)
    el = jnp.where(in_grp, logits, -jnp.inf)
    e1 = rmax(el)
    i1 = rmin(jnp.where(el == e1, lane, big))
    el2 = jnp.where(lane == i1, -jnp.inf, el)
    e2 = rmax(el2)
    i2 = rmin(jnp.where(el2 == e2, lane, big))
    t = jnp.exp(e2 - e1)
    p1 = 1.0 / (1.0 + t)
    w1 = p1 * g_w
    w2 = (t * p1) * g_w
    a1 = i1 - e_lo
    a2 = i2 - e_lo
    first_low = a1 < a2
    lo = jnp.where(first_low, a1, a2)
    hi = jnp.where(first_low, a2, a1)
    w_lo = jnp.where(first_low, w1, w2)
    w_hi = jnp.where(first_low, w2, w1)
    off = jnp.where(lo == 0.0, 0.0, jnp.where(lo == 1.0, 3.0, 5.0))
    cls = gsel * float(len(PAIRS)) + off + (hi - lo - 1.0)
    wt_ref[...] = jnp.where(lane < LANES // 2, w_lo, w_hi)
    cls_ref[...] = jnp.broadcast_to(cls, logits.shape).astype(jnp.int32)


def _expert_body(ea_ref, eb_ref, valid_ref, x_ref, wt_ref, wga_ref, wua_ref, wda_ref,
                 wgb_ref, wub_ref, wdb_ref, y_ref):
    i = pl.program_id(0)

    @pl.when(valid_ref[i] == 0)
    def _():
        y_ref[...] = jnp.zeros_like(y_ref)

    @pl.when(valid_ref[i] != 0)
    def _():
        x = _load_packed(x_ref, MOE_TILE).astype(BF16)
        dot = functools.partial(jnp.dot, preferred_element_type=F32)

        ga, ua = dot(x, wga_ref[...]), dot(x, wua_ref[...])
        gb, ub = dot(x, wgb_ref[...]), dot(x, wub_ref[...])
        ya = dot((jax.nn.silu(ga) * ua).astype(BF16), wda_ref[...])
        yb = dot((jax.nn.silu(gb) * ub).astype(BF16), wdb_ref[...])
        w_lo = wt_ref[:, 0:1]
        w_hi = wt_ref[:, LANES // 2:LANES // 2 + 1]
        _store_packed(y_ref, w_lo * ya + w_hi * yb)


def experts(xs, wts, tile_ea, tile_eb, tile_valid, w_gate, w_up, w_down):
    wspec = lambda shape, which: pl.BlockSpec(
        (None,) + shape, lambda i, ea, eb, va: ((ea, eb)[which][i], 0, 0))
    up = (D_MODEL, D_EXPERT)
    down = (D_EXPERT, D_MODEL)
    packed = pl.BlockSpec((MOE_TILE * PACK_ROWS, LANES), lambda i, ea, eb, va: (i, 0))
    grid_spec = pltpu.PrefetchScalarGridSpec(
        num_scalar_prefetch=3,
        grid=(MOE_TILES,),
        in_specs=[packed,
                  pl.BlockSpec((MOE_TILE, LANES), lambda i, ea, eb, va: (i, 0)),
                  wspec(up, 0), wspec(up, 0), wspec(down, 0),
                  wspec(up, 1), wspec(up, 1), wspec(down, 1)],
        out_specs=packed,
    )
    return pl.pallas_call(
        _expert_body,
        grid_spec=grid_spec,
        out_shape=jax.ShapeDtypeStruct((MOE_ROWS * PACK_ROWS, LANES), jnp.int32),
        compiler_params=_params("arbitrary"),
        name="experts",
    )(tile_ea, tile_eb, tile_valid, xs, wts, w_gate, w_up, w_down, w_gate, w_up, w_down)


def sc_dispatch(xp, wts, pos, fill_rows):
    n = pos.shape[0]
    n_fill = fill_rows.shape[0]
    info = plsc.get_sparse_core_info()
    n_workers = info.num_cores * info.num_subcores
    per_worker = n // n_workers
    steps = per_worker // SC_TOKENS
    fill_steps = n_fill // n_workers // SC_TOKENS
    assert steps * SC_TOKENS * n_workers == n and fill_steps * SC_TOKENS * n_workers == n_fill
    mesh = plsc.VectorSubcoreMesh(core_axis_name="core", subcore_axis_name="subcore")
    tok = (SC_TOKENS, PACK_ROWS, LANES)

    @functools.partial(
        pl.kernel,
        out_type=[jax.ShapeDtypeStruct((n + n_fill, PACK_ROWS, LANES), xp.dtype),
                  jax.ShapeDtypeStruct((n + n_fill, LANES), wts.dtype)],
        mesh=mesh,
        scratch_types=[pltpu.VMEM((steps, SC_TOKENS), jnp.int32),
                       pltpu.VMEM((fill_steps, SC_TOKENS), jnp.int32),
                       pltpu.VMEM(tok, xp.dtype),
                       pltpu.VMEM((SC_TOKENS, LANES), wts.dtype)],
    )
    def scatter(x_hbm, w_hbm, pos_hbm, fill_hbm, zx_hbm, zw_hbm, xs_hbm, ws_hbm, pos_v, fill_v, xbuf, wbuf):
        worker = lax.axis_index("subcore") * info.num_cores + lax.axis_index("core")
        pltpu.sync_copy(pos_hbm.at[worker], pos_v)
        pltpu.sync_copy(fill_hbm.at[worker], fill_v)

        @pl.loop(0, steps)
        def _(s):
            rows = pl.ds(worker * per_worker + s * SC_TOKENS, SC_TOKENS)
            pltpu.sync_copy(x_hbm.at[rows], xbuf)
            pltpu.sync_copy(w_hbm.at[rows], wbuf)
            pltpu.sync_copy(xbuf, xs_hbm.at[pos_v.at[s]])
            pltpu.sync_copy(wbuf, ws_hbm.at[pos_v.at[s]])

        pltpu.sync_copy(zx_hbm, xbuf)
        pltpu.sync_copy(zw_hbm, wbuf)

        @pl.loop(0, fill_steps)
        def _(s):
            pltpu.sync_copy(xbuf, xs_hbm.at[fill_v.at[s]])
            pltpu.sync_copy(wbuf, ws_hbm.at[fill_v.at[s]])

    xs, ws = scatter(xp.reshape(n, PACK_ROWS, LANES), wts,
                     pos.reshape(n_workers, steps, SC_TOKENS), fill_rows.reshape(n_workers, fill_steps, SC_TOKENS),
                     jnp.zeros(tok, xp.dtype), jnp.zeros((SC_TOKENS, LANES), wts.dtype))
    return xs.reshape((n + n_fill) * PACK_ROWS, LANES), ws


def sc_gather_tokens(x, idx):
    n = x.shape[0] // PACK_ROWS
    p = idx.shape[0]
    info = plsc.get_sparse_core_info()
    n_workers = info.num_cores * info.num_subcores
    per_worker = p // n_workers
    steps = per_worker // SC_TOKENS
    assert per_worker * n_workers == p and steps * SC_TOKENS == per_worker
    mesh = plsc.VectorSubcoreMesh(core_axis_name="core", subcore_axis_name="subcore")

    @functools.partial(
        pl.kernel,
        out_type=jax.ShapeDtypeStruct((p, PACK_ROWS, LANES), x.dtype),
        mesh=mesh,
        scratch_types=[pltpu.VMEM((per_worker,), jnp.int32),
                       pltpu.VMEM((SC_TOKENS, PACK_ROWS, LANES), x.dtype)],
    )
    def gather(x_hbm, i_hbm, o_hbm, idx_v, buf):
        worker = lax.axis_index("subcore") * info.num_cores + lax.axis_index("core")
        base = worker * per_worker
        pltpu.sync_copy(i_hbm.at[pl.ds(base, per_worker)], idx_v)

        @pl.loop(0, steps)
        def _(s):
            r0 = s * SC_TOKENS
            pltpu.sync_copy(x_hbm.at[idx_v.at[pl.ds(r0, SC_TOKENS)]], buf)
            pltpu.sync_copy(buf, o_hbm.at[pl.ds(base + r0, SC_TOKENS)])

    return gather(x.reshape(n, PACK_ROWS, LANES), idx).reshape(p * PACK_ROWS, LANES)


_CLASS_EA = [g * EXP_PER_GROUP + lo for g in range(N_GROUPS) for lo, hi in PAIRS]
_CLASS_EB = [g * EXP_PER_GROUP + hi for g in range(N_GROUPS) for lo, hi in PAIRS]


def dispatch_plan(cls):
    onehot = (cls[:, None] == jnp.arange(N_CLASSES, dtype=jnp.int32)[None, :]).astype(jnp.int32)
    csum = jnp.cumsum(onehot, axis=0)
    rank = jnp.sum(csum * onehot, axis=1) - 1
    counts = csum[-1]
    tiles_per = (counts + MOE_TILE - 1) // MOE_TILE
    tile_end = jnp.cumsum(tiles_per)
    tile_start = tile_end - tiles_per
    pos = jnp.sum(onehot * tile_start[None, :], axis=1) * MOE_TILE + rank
    n = cls.shape[0]
    pad = tiles_per * MOE_TILE - counts
    pad_end = jnp.cumsum(pad)
    k = jnp.arange(MOE_ROWS - n, dtype=jnp.int32)
    k_cls = jnp.sum((k[:, None] >= pad_end[None, :]).astype(jnp.int32), axis=1)
    k_hot = (k_cls[:, None] == jnp.arange(N_CLASSES, dtype=jnp.int32)[None, :]).astype(jnp.int32)
    pad_row0 = tile_start * MOE_TILE + counts - (pad_end - pad)
    fill_rows = k + jnp.where(k_cls < N_CLASSES, jnp.sum(k_hot * pad_row0[None, :], axis=1),
                              tile_end[-1] * MOE_TILE - pad_end[-1])
    t = jnp.arange(MOE_TILES, dtype=jnp.int32)
    tile_cls = jnp.sum((t[:, None] >= tile_end[None, :]).astype(jnp.int32), axis=1)
    valid = (tile_cls < N_CLASSES).astype(jnp.int32)
    last_cls = jnp.max(jnp.where(counts > 0, jnp.arange(N_CLASSES, dtype=jnp.int32), 0))
    tile_cls = jnp.where(valid == 1, tile_cls, last_cls)
    ea = jnp.asarray(_CLASS_EA, jnp.int32)[tile_cls]
    eb = jnp.asarray(_CLASS_EB, jnp.int32)[tile_cls]
    return pos, fill_rows, ea, eb, valid


def moe(xp, wts, cls, w_gate, w_up, w_down):
    pos, fill_rows, ea, eb, valid = dispatch_plan(cls[:, 0])
    xs, ws = sc_dispatch(xp, wts, pos, fill_rows)
    ys = experts(xs, ws, ea, eb, valid, w_gate, w_up, w_down)
    return sc_gather_tokens(ys, pos)


def _rows(a):
    return a.reshape(-1, a.shape[-1])


def kernel(x_prompt, x_sample, state_C, state_n, state_m, cache_k, cache_v, a_w_in, a_b_gate, a_norm, a_w_out, b_w_in, b_rel, b_w_out, ln1_g, ln1_b, ln2_g, ln2_b, r_w_group, r_b_group, r_w_expert, r_b_expert, e_w_gate, e_w_up, e_w_down):
    hk = A_HEADS * A_DK
    n_main = 2 * hk + 2 * D_MODEL
    gate_w = lambda layer: a_w_in[layer // 2, :, n_main:]
    x, xb, g = embed(_rows(x_prompt), _rows(x_sample), gate_w(0))
    outs = {k: [] for k in ("Cp", "np", "mp", "kp", "vp", "Cs", "ns", "ms", "ks", "vs")}
    for layer in range(DEPTH):
        j = layer // 2
        if layer % 2 == 0:
            proj = matmul(xb, a_w_in, j, n_main, BF16)
            zc = jnp.zeros((BATCH, A_HEADS, A_DK, A_DV), F32)
            zn = jnp.zeros((BATCH, A_HEADS, A_DK), F32)
            zm = jnp.zeros((BATCH, A_HEADS), F32)
            h_p, c_p, n_p, m_p = mlstm(proj, g, a_b_gate[j], a_norm[j], zc, zn, zm,
                                       batch=BATCH, seq=SEQ, L=MLSTM_CHUNK, row0=0)
            h_s, c_s, n_s, m_s = mlstm(proj, g, a_b_gate[j], a_norm[j],
                                       state_C[j], state_n[j], state_m[j],
                                       batch=DEC_BATCH, seq=DEC_SEQ, L=DEC_SEQ, row0=N_PROMPT)
            outs["Cp"].append(c_p); outs["np"].append(n_p); outs["mp"].append(m_p)
            outs["Cs"].append(c_s); outs["ns"].append(n_s); outs["ms"].append(m_s)
            w_out = a_w_out[j]
        else:
            qkv = matmul(xb, b_w_in, j, 3 * D_MODEL, BF16)
            h_p, k_p, v_p = attn_prompt(qkv, b_rel[j])
            h_s, k_s, v_s = attn_sample(qkv, cache_k, cache_v, j, b_rel[j])
            heads = lambda a: a.reshape(a.shape[:2] + (B_HEADS, B_DH))
            outs["kp"].append(heads(k_p)); outs["vp"].append(heads(v_p))
            outs["ks"].append(heads(k_s)); outs["vs"].append(heads(v_s))
            w_out = b_w_out[j]
        x, xb, xp, wts, cls = mix_ln_router(h_p, h_s, w_out.astype(BF16), x, ln1_g[layer], ln1_b[layer],
                                            r_w_group[layer], r_b_group[layer], r_w_expert[layer], r_b_expert[layer])
        y = moe(xp, wts, cls, cast_layer(e_w_gate, layer), cast_layer(e_w_up, layer), cast_layer(e_w_down, layer))
        if layer < DEPTH - 1:
            if (layer + 1) % 2 == 0:
                x, xb, g = residual_ln(x, y, ln2_g[layer], ln2_b[layer], gate_w(layer + 1))
            else:
                x, xb = residual_ln(x, y, ln2_g[layer], ln2_b[layer])
        else:
            y_p, y_s = residual_ln_split(x, y, ln2_g[layer], ln2_b[layer], N_PROMPT)
    st = lambda k: jnp.stack(outs[k])
    return (y_p.reshape(BATCH, SEQ, D_MODEL), y_s.reshape(DEC_BATCH, DEC_SEQ, D_MODEL),
            st("Cp"), st("np"), st("mp"), st("kp"), st("vp"),
            st("Cs"), st("ns"), st("ms"), st("ks"), st("vs"))
```

```python
import functools

import jax
import jax.numpy as jnp
from jax import lax
from jax.experimental import pallas as pl
from jax.experimental.pallas import tpu as pltpu
from jax.experimental.pallas import tpu_sc as plsc

F32 = jnp.float32
BF16 = jnp.bfloat16

D_MODEL = 2048
BATCH = 8
SEQ = 4096
DEPTH = 4
DEC_BATCH = 32
DEC_SEQ = 32
PAST_LEN = 1024
CHUNK = 64
A_HEADS = 8
A_DK = 128
A_DV = D_MODEL // A_HEADS
GATE_CAP = 15.0
B_HEADS = 16
B_DH = D_MODEL // B_HEADS
PREV_CHUNKS = 8
REACH = PREV_CHUNKS * CHUNK
REL_CLIP = 256
N_GROUPS = 4
EXP_PER_GROUP = 4
N_EXPERTS = N_GROUPS * EXP_PER_GROUP
D_EXPERT = D_MODEL // 4
ALPHA = (2 * DEPTH) ** 0.25
LN_EPS = 1e-5
RMS_EPS = 1e-6

N_PROMPT = BATCH * SEQ
N_SAMPLE = DEC_BATCH * DEC_SEQ
N_TOK = N_PROMPT + N_SAMPLE

VMEM_LIMIT = 56 * 1024 * 1024
LANES = 128

PAIRS = ((0, 1), (0, 2), (0, 3), (1, 2), (1, 3), (2, 3))
N_CLASSES = N_GROUPS * len(PAIRS)
MOE_TILE = 256
MOE_TILES = -(-(N_TOK // MOE_TILE + N_CLASSES) // 32) * 32
MOE_ROWS = MOE_TILES * MOE_TILE

MLSTM_CHUNK = 128
ATT_TQ = 256
ATT_TK = ATT_TQ + REACH
NEG = -1e30

SC_TOKENS = 48
SC_FILL = 32


def _params(*sem):
    return pltpu.CompilerParams(dimension_semantics=sem, vmem_limit_bytes=VMEM_LIMIT)


def _mm_body(x_ref, w_ref, o_ref, wb):
    @pl.when(pl.program_id(1) == 0)
    def _():
        wb[...] = w_ref[...].astype(BF16)

    o_ref[...] = jnp.dot(x_ref[...], wb[...], preferred_element_type=F32).astype(o_ref.dtype)


def matmul(x, w_all, layer, m, out_dtype, tm=512, tn=1024):
    n, k = x.shape
    return pl.pallas_call(
        _mm_body,
        grid=(m // tn, n // tm),
        in_specs=[pl.BlockSpec((tm, k), lambda j, i: (i, 0)),
                  pl.BlockSpec((None, k, tn), lambda j, i: (layer, 0, j))],
        out_specs=pl.BlockSpec((tm, tn), lambda j, i: (i, j)),
        out_shape=jax.ShapeDtypeStruct((n, m), out_dtype),
        scratch_shapes=[pltpu.VMEM((k, tn), BF16)],
        compiler_params=_params("parallel", "arbitrary"),
        name="matmul",
    )(x, w_all)


def _cast_body(w_ref, o_ref):
    o_ref[...] = w_ref[...].astype(BF16)


def cast_layer(w_all, layer):
    e, a, b = w_all.shape[1:]
    per_step = 2
    return pl.pallas_call(
        _cast_body,
        grid=(e // per_step,),
        in_specs=[pl.BlockSpec((None, per_step, a, b), lambda i: (layer, i, 0, 0))],
        out_specs=pl.BlockSpec((per_step, a, b), lambda i: (i, 0, 0)),
        out_shape=jax.ShapeDtypeStruct((e, a, b), BF16),
        compiler_params=_params("parallel"),
        name="cast_layer",
    )(w_all)


def _split2(x):
    hi = x.astype(BF16)
    lo = (x - hi.astype(F32)).astype(BF16)
    return hi, lo


def _dot3(xh, xl, wh, wl, dims):
    dg = functools.partial(lax.dot_general, dimension_numbers=dims, preferred_element_type=F32)
    return dg(xh, wh) + (dg(xh, wl) + dg(xl, wh))


_NN = (((1,), (0,)), ((), ()))
_NT = (((1,), (1,)), ((), ()))
_TN = (((0,), (0,)), ((), ()))


def _dot_hilo(x, w):
    t = x.shape[0]
    xh, xl = _split2(x)
    wh, wl = _split2(w)
    r = jnp.dot(jnp.concatenate([xh, xl], axis=0), jnp.concatenate([wh, wl], axis=1), preferred_element_type=F32)
    return r[:t, :LANES] + (r[:t, LANES:] + r[t:, :LANES]) + r[t:, LANES:]


def _log_sigmoid(x):
    return jnp.minimum(x, 0.0) - jnp.log(1.0 + jnp.exp(-jnp.abs(x)))


def _split3(x):
    a = x.astype(BF16)
    r = x - a.astype(F32)
    b = r.astype(BF16)
    c = (r - b.astype(F32)).astype(BF16)
    return a, b, c


A_STATE = A_DV + LANES


def _cummax_rows(x):
    n = x.shape[0]
    row = lax.broadcasted_iota(jnp.int32, x.shape, 0)
    k = 1
    while k < n:
        x = jnp.maximum(x, jnp.where(row >= k, pltpu.roll(x, k, axis=0), -jnp.inf))
        k *= 2
    return x


def _mlstm_body(q_ref, k_ref, v_ref, o_ref, g_ref, bias_ref, gain_ref, c0_ref, m0_ref,
                h_ref, c_ref, m_ref, *, L):
    step = pl.program_id(1)

    @pl.when(step == 0)
    def _():
        c_ref[...] = c0_ref[...]
        m_ref[...] = m0_ref[...]

    dot = functools.partial(jnp.dot, preferred_element_type=F32)
    row = lax.broadcasted_iota(jnp.int32, (L, L), 0)
    col = lax.broadcasted_iota(jnp.int32, (L, L), 1)
    causal = col <= row
    tril = jnp.where(causal, 1.0, 0.0).astype(BF16)

    pre = GATE_CAP * jnp.tanh((g_ref[...] + bias_ref[...]) * (1.0 / GATE_CAP))
    ig = pre
    lf = _log_sigmoid(pltpu.roll(pre, LANES - A_HEADS, axis=1))
    a3, b3, c3 = _split3(lf)
    bt = dot(tril, a3) + (dot(tril, b3) + dot(tril, c3))
    a = ig - bt
    m_prev = m_ref[0]
    m_t = bt + jnp.maximum(m_prev, _cummax_rows(a))
    u = bt - m_t
    inter = jnp.exp(bt + m_prev - m_t)
    eminus = jnp.exp(-m_t)
    m_new = m_t[L - 1:L, :]
    bt_last = bt[L - 1:L, :]
    w_c = jnp.exp((bt_last - m_new) + a)
    decay = jnp.exp(bt_last + m_prev - m_new)
    m_ref[0] = m_new
    a_pad = a if L == LANES else jnp.concatenate([a, jnp.zeros((LANES - L, LANES), F32)], axis=0)
    a_t = jnp.transpose(a_pad)
    ie = jnp.concatenate([inter, eminus], axis=1).astype(BF16)
    sel_r = lax.broadcasted_iota(jnp.int32, (2 * LANES, 2 * LANES), 0)
    sel_c = lax.broadcasted_iota(jnp.int32, (2 * LANES, 2 * LANES), 1)
    same_half = (sel_r >= LANES) == (sel_c >= LANES)
    ones_l = jnp.ones((L, LANES), BF16)
    zeros_l = jnp.zeros((L, LANES), BF16)
    zeros_k = jnp.zeros((A_DK, LANES), BF16)
    mean_cols = jnp.full((A_DV, LANES), 1.0, BF16)

    scale = A_DK ** -0.5
    heads = range(A_HEADS)
    dk_sl = lambda h: slice(h * A_DK, (h + 1) * A_DK)
    dv_sl = lambda h: slice(h * A_DV, (h + 1) * A_DV)
    ie_b, qk = [], []
    for h in heads:
        sel = jnp.where(same_half & ((sel_r & (LANES - 1)) == h), 1.0, 0.0).astype(BF16)
        ie_b.append(dot(ie, sel))
        qk.append(lax.dot_general(q_ref[:, dk_sl(h)], k_ref[:, dk_sl(h)], _NT, preferred_element_type=F32))
    out = []
    for h in heads:
        d_mat = jnp.where(causal, jnp.exp(u[:, h:h + 1] + a_t[h:h + 1, :L]), 0.0)
        s = qk[h] * (d_mat * scale)
        q_i = (q_ref[:, dk_sl(h)].astype(F32) * ie_b[h][:, :LANES]).astype(BF16)
        c_b = c_ref[0, h].astype(BF16)
        rhs = jnp.concatenate([
            jnp.concatenate([v_ref[:, dv_sl(h)], ones_l, zeros_l], axis=1),
            jnp.concatenate([c_b[:, :A_DV], zeros_k, c_b[:, A_DV:]], axis=1)], axis=0)
        out.append(dot(jnp.concatenate([s.astype(BF16), q_i], axis=1), rhs))
    hh, ms = [], []
    for h in heads:
        den = jnp.maximum(jnp.abs(out[h][:, A_DV:A_DV + LANES] + out[h][:, A_DV + LANES:]), ie_b[h][:, LANES:])
        r = 1.0 / den
        hh.append(out[h][:, :A_DV] * jnp.concatenate([r, r], axis=1))
        ms.append(dot((hh[h] * hh[h]).astype(BF16), mean_cols))
    upd = []
    for h in heads:
        rs = lax.rsqrt(ms[h] * (1.0 / A_DV) + RMS_EPS)
        og = o_ref[:, dv_sl(h)].astype(F32)
        y = hh[h] * jnp.concatenate([rs, rs], axis=1) * gain_ref[:, dv_sl(h)] * jax.nn.sigmoid(og)
        h_ref[:, dv_sl(h)] = y.astype(h_ref.dtype)
        wk = (k_ref[:, dk_sl(h)].astype(F32) * (w_c[:, h:h + 1] * scale)).astype(BF16)
        upd.append(lax.dot_general(wk, jnp.concatenate([v_ref[:, dv_sl(h)], ones_l], axis=1), _TN,
                                   preferred_element_type=F32))
    for h in heads:
        c_ref[0, h] = decay[:, h:h + 1] * c_ref[0, h] + upd[h]


def mlstm(proj, g, b_gate, gain, c0, n0, m0, *, batch, seq, L, row0):
    nc = seq // L
    bias = jnp.pad(b_gate.astype(F32), (0, LANES - 2 * A_HEADS))
    row_blk = lambda b, c: row0 // L + b * nc + c
    st = lambda *s: pl.BlockSpec((1,) + s, lambda b, c: (b,) + (0,) * len(s))
    state0 = jnp.concatenate([c0, jnp.broadcast_to(n0[..., None], n0.shape + (LANES,))], axis=-1)
    m_lanes = jnp.pad(m0, ((0, 0), (0, LANES - A_HEADS)))[:, None, :]
    h, state, m = pl.pallas_call(
        functools.partial(_mlstm_body, L=L),
        grid=(batch, nc),
        in_specs=[pl.BlockSpec((L, A_HEADS * A_DK), lambda b, c: (row_blk(b, c), 0)),
                  pl.BlockSpec((L, A_HEADS * A_DK), lambda b, c: (row_blk(b, c), 1)),
                  pl.BlockSpec((L, D_MODEL), lambda b, c: (row_blk(b, c), 1)),
                  pl.BlockSpec((L, D_MODEL), lambda b, c: (row_blk(b, c), 2)),
                  pl.BlockSpec((L, LANES), lambda b, c: (row_blk(b, c), 0)),
                  pl.BlockSpec((1, LANES), lambda b, c: (0, 0)),
                  pl.BlockSpec((1, D_MODEL), lambda b, c: (0, 0)),
                  st(A_HEADS, A_DK, A_STATE), st(1, LANES)],
        out_specs=[pl.BlockSpec((L, D_MODEL), lambda b, c: (b * nc + c, 0)),
                   st(A_HEADS, A_DK, A_STATE), st(1, LANES)],
        out_shape=[jax.ShapeDtypeStruct((batch * seq, D_MODEL), BF16),
                   jax.ShapeDtypeStruct((batch, A_HEADS, A_DK, A_STATE), F32),
                   jax.ShapeDtypeStruct((batch, 1, LANES), F32)],
        compiler_params=_params("parallel", "arbitrary"),
        name="mlstm",
    )(proj, proj, proj, proj, g, bias[None, :], gain.astype(F32)[None, :], state0, m_lanes)
    return h, state[..., :A_DV], state[..., A_DV], m[:, 0, :A_HEADS]


LOG2E = 1.4426950408889634
Q_SCALE = B_DH ** -0.5 * LOG2E


def _scale_q(q):
    return (q.astype(F32) * Q_SCALE).astype(BF16)


def _attn_prompt_body(q_ref, k_ref, v_ref, bias_ref, o_ref, kt_ref, vt_ref, qs, kpad, vpad):
    kt_ref[0] = k_ref[SEQ - REACH:, :].astype(F32)
    vt_ref[0] = v_ref[SEQ - REACH:, :].astype(F32)
    qs[...] = _scale_q(q_ref[...])
    kpad[0:REACH, :] = jnp.zeros((REACH, B_DH), BF16)
    kpad[REACH:, :] = k_ref[...]
    vpad[0:REACH, 0:B_DH] = jnp.zeros((REACH, B_DH), BF16)
    vpad[REACH:, 0:B_DH] = v_ref[...]
    vpad[:, B_DH:] = jnp.ones((SEQ + REACH, B_DH), BF16)

    def scores(t):
        r0 = t * ATT_TQ
        kb = kpad[r0:r0 + ATT_TK, :]
        s = lax.dot_general(qs[r0:r0 + ATT_TQ, :], kb, _NT, preferred_element_type=F32) + bias_ref[0]
        if r0 < REACH:
            j = lax.broadcasted_iota(jnp.int32, (ATT_TQ, ATT_TK), 1)
            s = jnp.where(j + r0 >= REACH, s, NEG)
        return s

    def finish(t, s):
        r0 = t * ATT_TQ
        p = jnp.exp2(s - jnp.max(s, axis=-1, keepdims=True)).astype(BF16)
        acc = jnp.dot(p, vpad[r0:r0 + ATT_TK, :], preferred_element_type=F32)
        o_ref[r0:r0 + ATT_TQ, :] = (acc[:, :B_DH] / acc[:, B_DH:]).astype(o_ref.dtype)

    n_tiles = SEQ // ATT_TQ
    s_next = scores(0)
    for t in range(n_tiles):
        s_cur = s_next
        if t + 1 < n_tiles:
            s_next = scores(t + 1)
        finish(t, s_cur)


def band_bias(rel_table, nq, nk):
    w = nq + nk
    rel = jnp.clip(REACH + nq - 1 - jnp.arange(w), -REL_CLIP, REL_CLIP) + REL_CLIP
    r = rel_table.astype(F32)[:, rel]
    heads = r.shape[0]
    skew = jnp.broadcast_to(r[:, None, :], (heads, nq, w)).reshape(heads, nq * w)
    skew = skew[:, :nq * (w - 1)].reshape(heads, nq, w - 1)
    bias = skew[:, :, nq - 1:nq - 1 + nk]
    i = jnp.arange(nq)[:, None]
    j = jnp.arange(nk)[None, :]
    dc = j // CHUNK - i // CHUNK
    allowed = (dc >= 0) & (dc <= PREV_CHUNKS)
    return jnp.where(allowed[None], bias * LOG2E, NEG)


def attn_prompt(qkv, rel_table):
    bias = band_bias(rel_table, ATT_TQ, ATT_TK)
    tail = pl.BlockSpec((1, REACH, B_DH), lambda b, h: (b, 0, h))
    return pl.pallas_call(
        _attn_prompt_body,
        grid=(BATCH, B_HEADS),
        in_specs=[pl.BlockSpec((SEQ, B_DH), lambda b, h: (b, h)),
                  pl.BlockSpec((SEQ, B_DH), lambda b, h: (b, B_HEADS + h)),
                  pl.BlockSpec((SEQ, B_DH), lambda b, h: (b, 2 * B_HEADS + h)),
                  pl.BlockSpec((1, ATT_TQ, ATT_TK), lambda b, h: (h, 0, 0))],
        out_specs=[pl.BlockSpec((SEQ, B_DH), lambda b, h: (b, h)), tail, tail],
        out_shape=[jax.ShapeDtypeStruct((N_PROMPT, D_MODEL), BF16),
                   jax.ShapeDtypeStruct((BATCH, REACH, D_MODEL), F32),
                   jax.ShapeDtypeStruct((BATCH, REACH, D_MODEL), F32)],
        scratch_shapes=[pltpu.VMEM((SEQ, B_DH), BF16),
                        pltpu.VMEM((SEQ + REACH, B_DH), BF16),
                        pltpu.VMEM((SEQ + REACH, 2 * B_DH), BF16)],
        compiler_params=_params("parallel", "parallel"),
        name="attn_prompt",
    )(qkv, qkv, qkv, bias)


def _attn_sample_body(q_ref, kn_ref, vn_ref, ck_ref, cv_ref, bias_c_ref, bias_n_ref, o_ref, ks_ref, vs_ref):
    ks_ref[0] = kn_ref[...].astype(F32)
    vs_ref[0] = vn_ref[...].astype(F32)
    for h in range(B_HEADS):
        sl = slice(h * B_DH, (h + 1) * B_DH)
        q = _scale_q(q_ref[:, sl])
        kc = ck_ref[pl.ds(h, REACH, stride=B_HEADS), :].astype(BF16)
        vc = cv_ref[pl.ds(h, REACH, stride=B_HEADS), :].astype(BF16)
        s_c = lax.dot_general(q, kc, _NT, preferred_element_type=F32) + bias_c_ref[h]
        s_n = lax.dot_general(q, kn_ref[:, sl], _NT, preferred_element_type=F32) + bias_n_ref[h]
        m = jnp.maximum(jnp.max(s_c, axis=-1, keepdims=True), jnp.max(s_n, axis=-1, keepdims=True))
        p_c = jnp.exp2(s_c - m)
        p_n = jnp.exp2(s_n - m)
        l = jnp.sum(p_c, axis=-1, keepdims=True) + jnp.sum(p_n, axis=-1, keepdims=True)
        dot = functools.partial(jnp.dot, preferred_element_type=F32)
        o_ref[:, sl] = ((dot(p_c.astype(BF16), vc) + dot(p_n.astype(BF16), vn_ref[:, sl])) / l).astype(o_ref.dtype)


def attn_sample(qkv, cache_k, cache_v, layer, rel_table):
    bias = band_bias(rel_table, DEC_SEQ, REACH + DEC_SEQ)
    cache_rows = REACH * B_HEADS
    ck = cache_k.reshape(cache_k.shape[0], DEC_BATCH, cache_rows, B_DH)
    cv = cache_v.reshape(cache_v.shape[0], DEC_BATCH, cache_rows, B_DH)
    row = lambda c: pl.BlockSpec((DEC_SEQ, D_MODEL), lambda b: (N_PROMPT // DEC_SEQ + b, c))
    cache = pl.BlockSpec((None, None, cache_rows, B_DH), lambda b: (layer, b, 0, 0))
    new = pl.BlockSpec((1, DEC_SEQ, D_MODEL), lambda b: (b, 0, 0))
    return pl.pallas_call(
        _attn_sample_body,
        grid=(DEC_BATCH,),
        in_specs=[row(0), row(1), row(2), cache, cache,
                  pl.BlockSpec((B_HEADS, DEC_SEQ, REACH), lambda b: (0, 0, 0)),
                  pl.BlockSpec((B_HEADS, DEC_SEQ, DEC_SEQ), lambda b: (0, 0, 0))],
        out_specs=[pl.BlockSpec((DEC_SEQ, D_MODEL), lambda b: (b, 0)), new, new],
        out_shape=[jax.ShapeDtypeStruct((N_SAMPLE, D_MODEL), BF16),
                   jax.ShapeDtypeStruct((DEC_BATCH, DEC_SEQ, D_MODEL), F32),
                   jax.ShapeDtypeStruct((DEC_BATCH, DEC_SEQ, D_MODEL), F32)],
        compiler_params=_params("parallel"),
        name="attn_sample",
    )(qkv, qkv, qkv, ck, cv, bias[:, :, :REACH], bias[:, :, REACH:])


def _ln(z, g, b):
    mu = jnp.mean(z, axis=-1, keepdims=True)
    zc = z - mu
    var = jnp.mean(zc * zc, axis=-1, keepdims=True)
    return zc * lax.rsqrt(var + LN_EPS) * g + b


PACK_ROWS = D_MODEL // 2 // LANES
HI_MASK = -65536


def _store_packed(ref, v, token0=0):
    t = v.shape[0]
    half = D_MODEL // 2
    hi = lax.bitcast_convert_type(v[:, :half].astype(BF16).astype(F32), jnp.int32)
    lo = lax.bitcast_convert_type(v[:, half:].astype(BF16).astype(F32), jnp.int32)
    words = (hi & HI_MASK) | lax.shift_right_logical(lo, 16)
    for a in range(PACK_ROWS):
        ref[pl.ds(token0 * PACK_ROWS + a, t, stride=PACK_ROWS), :] = words[:, a * LANES:(a + 1) * LANES]


def _load_packed(ref, t):
    words = jnp.concatenate([ref[pl.ds(a, t, stride=PACK_ROWS), :] for a in range(PACK_ROWS)], axis=1)
    hi = lax.bitcast_convert_type(words & HI_MASK, F32)
    lo = lax.bitcast_convert_type(lax.shift_left(words, 16), F32)
    return jnp.concatenate([hi, lo], axis=1)


def _mix_ln_router_body(hp_ref, hs_ref, w_ref, x_ref, g_ref, b_ref, rw_ref, rb_ref,
                        xo_ref, xbo_ref, xp_ref, wt_ref, cls_ref, y_even, y_odd, *, prompt_tiles, n_tiles):
    i = pl.program_id(0)
    from_prompt = jnp.minimum(i, n_tiles - 1) < prompt_tiles

    @pl.when(i == 0)
    def _():
        y_odd[...] = jnp.zeros_like(y_odd)

    half = D_MODEL // 2

    def run(y_new, y_old):
        h = jnp.where(from_prompt, hp_ref[...], hs_ref[...])
        y_new[:, :half] = jnp.dot(h, w_ref[:, :half], preferred_element_type=F32)
        xn = _ln(ALPHA * x_ref[...] + y_old[...], g_ref[...], b_ref[...])
        xo_ref[...] = xn
        xbo_ref[...] = xn.astype(BF16)
        logits = _dot_hilo(xn, rw_ref[...]) + rb_ref[...]
        y_new[:, half:] = jnp.dot(h, w_ref[:, half:], preferred_element_type=F32)
        _store_packed(xp_ref, xn)
        _route(logits, wt_ref, cls_ref)

    @pl.when(i % 2 == 0)
    def _():
        run(y_even, y_odd)

    @pl.when(i % 2 == 1)
    def _():
        run(y_odd, y_even)


def mix_ln_router(h_p, h_s, w, x, g, b, w_group, b_group, w_expert, b_expert, tm=256):
    n = x.shape[0]
    n_tiles = n // tm
    prompt_tiles = h_p.shape[0] // tm
    sample_tiles = h_s.shape[0] // tm
    assert prompt_tiles * tm == h_p.shape[0] and (prompt_tiles + sample_tiles) * tm == n
    mm_tile = lambda i: jnp.minimum(i, n_tiles - 1)
    ep_tile = lambda i: jnp.maximum(i - 1, 0)
    hp_spec = pl.BlockSpec((tm, D_MODEL), lambda i: (jnp.minimum(mm_tile(i), prompt_tiles - 1), 0))
    hs_spec = pl.BlockSpec((tm, D_MODEL), lambda i: (jnp.maximum(mm_tile(i) - prompt_tiles, 0), 0))
    rw = jnp.pad(jnp.concatenate([w_group, w_expert], axis=1), ((0, 0), (0, LANES - N_GROUPS - N_EXPERTS)))
    rb = jnp.pad(jnp.concatenate([b_group, b_expert]).astype(F32), (0, LANES - N_GROUPS - N_EXPERTS))
    row = pl.BlockSpec((tm, D_MODEL), lambda i: (ep_tile(i), 0))
    vec = pl.BlockSpec((1, D_MODEL), lambda i: (0, 0))
    lane_row = pl.BlockSpec((tm, LANES), lambda i: (ep_tile(i), 0))
    return pl.pallas_call(
        functools.partial(_mix_ln_router_body, prompt_tiles=prompt_tiles, n_tiles=n_tiles),
        grid=(n_tiles + 1,),
        in_specs=[hp_spec, hs_spec, pl.BlockSpec((D_MODEL, D_MODEL), lambda i: (0, 0)), row, vec, vec,
                  pl.BlockSpec((D_MODEL, LANES), lambda i: (0, 0)), pl.BlockSpec((1, LANES), lambda i: (0, 0))],
        out_specs=[row, row, pl.BlockSpec((tm * PACK_ROWS, LANES), lambda i: (ep_tile(i), 0)), lane_row, lane_row],
        out_shape=[jax.ShapeDtypeStruct((n, D_MODEL), F32), jax.ShapeDtypeStruct((n, D_MODEL), BF16),
                   jax.ShapeDtypeStruct((n * PACK_ROWS, LANES), jnp.int32),
                   jax.ShapeDtypeStruct((n, LANES), F32), jax.ShapeDtypeStruct((n, LANES), jnp.int32)],
        scratch_shapes=[pltpu.VMEM((tm, D_MODEL), F32), pltpu.VMEM((tm, D_MODEL), F32)],
        compiler_params=_params("arbitrary"),
        name="mix_ln_router",
    )(h_p, h_s, w, x, g[None, :], b[None, :], rw, rb[None, :])


def _add_ln_body(x_ref, y_ref, g_ref, b_ref, *rest):
    y = _load_packed(y_ref, x_ref.shape[0])
    xn = _ln(ALPHA * x_ref[...] + y, g_ref[...], b_ref[...])
    if len(rest) == 2:
        xo_ref, xbo_ref = rest
    else:
        wg_ref, xo_ref, xbo_ref, go_ref = rest
        go_ref[...] = _dot_hilo(xn, wg_ref[...])
    xo_ref[...] = xn
    xbo_ref[...] = xn.astype(BF16)


def residual_ln(x, y_packed, g, b, w_gate=None, tm=512):
    n = x.shape[0]
    row = pl.BlockSpec((tm, D_MODEL), lambda i: (i, 0))
    vec = pl.BlockSpec((1, D_MODEL), lambda i: (0, 0))
    in_specs = [row, pl.BlockSpec((tm * PACK_ROWS, LANES), lambda i: (i, 0)), vec, vec]
    out_specs = [row, row]
    out_shape = [jax.ShapeDtypeStruct((n, D_MODEL), F32), jax.ShapeDtypeStruct((n, D_MODEL), BF16)]
    args = [x, y_packed, g[None, :], b[None, :]]
    if w_gate is not None:
        in_specs.append(pl.BlockSpec((D_MODEL, LANES), lambda i: (0, 0)))
        out_specs.append(pl.BlockSpec((tm, LANES), lambda i: (i, 0)))
        out_shape.append(jax.ShapeDtypeStruct((n, LANES), F32))
        args.append(jnp.pad(w_gate, ((0, 0), (0, LANES - w_gate.shape[1]))))
    return pl.pallas_call(
        _add_ln_body,
        grid=(n // tm,),
        in_specs=in_specs,
        out_specs=out_specs,
        out_shape=out_shape,
        compiler_params=_params("parallel"),
        name="residual_ln",
    )(*args)


def _embed_body(xp_ref, xs_ref, wg_ref, xo_ref, xbo_ref, go_ref, *, prompt_tiles):
    x = jnp.where(pl.program_id(0) < prompt_tiles, xp_ref[...], xs_ref[...])
    xo_ref[...] = x
    xbo_ref[...] = x.astype(BF16)
    go_ref[...] = _dot_hilo(x, wg_ref[...])


def embed(x_p, x_s, w_gate, tm=512):
    n = x_p.shape[0] + x_s.shape[0]
    prompt_tiles = x_p.shape[0] // tm
    assert prompt_tiles * tm == x_p.shape[0] and n % tm == 0
    row = pl.BlockSpec((tm, D_MODEL), lambda i: (i, 0))
    wg = jnp.pad(w_gate, ((0, 0), (0, LANES - w_gate.shape[1])))
    return pl.pallas_call(
        functools.partial(_embed_body, prompt_tiles=prompt_tiles),
        grid=(n // tm,),
        in_specs=[pl.BlockSpec((tm, D_MODEL), lambda i: (jnp.minimum(i, prompt_tiles - 1), 0)),
                  pl.BlockSpec((tm, D_MODEL), lambda i: (jnp.maximum(i - prompt_tiles, 0), 0)),
                  pl.BlockSpec((D_MODEL, LANES), lambda i: (0, 0))],
        out_specs=[row, row, pl.BlockSpec((tm, LANES), lambda i: (i, 0))],
        out_shape=[jax.ShapeDtypeStruct((n, D_MODEL), F32), jax.ShapeDtypeStruct((n, D_MODEL), BF16),
                   jax.ShapeDtypeStruct((n, LANES), F32)],
        compiler_params=_params("parallel"),
        name="embed",
    )(x_p, x_s, wg)


def _add_ln_split_body(x_ref, y_ref, g_ref, b_ref, op_ref, os_ref, *, prompt_tiles):
    y = _load_packed(y_ref, x_ref.shape[0])
    xn = _ln(ALPHA * x_ref[...] + y, g_ref[...], b_ref[...])
    i = pl.program_id(0)

    @pl.when(i < prompt_tiles)
    def _():
        op_ref[...] = xn

    @pl.when(i >= prompt_tiles)
    def _():
        os_ref[...] = xn


def residual_ln_split(x, y_packed, g, b, n_prompt, tm=512):
    n = x.shape[0]
    prompt_tiles = n_prompt // tm
    assert prompt_tiles * tm == n_prompt and n % tm == 0
    row = pl.BlockSpec((tm, D_MODEL), lambda i: (i, 0))
    vec = pl.BlockSpec((1, D_MODEL), lambda i: (0, 0))
    return pl.pallas_call(
        functools.partial(_add_ln_split_body, prompt_tiles=prompt_tiles),
        grid=(n // tm,),
        in_specs=[row, pl.BlockSpec((tm * PACK_ROWS, LANES), lambda i: (i, 0)), vec, vec],
        out_specs=[pl.BlockSpec((tm, D_MODEL), lambda i: (jnp.minimum(i, prompt_tiles - 1), 0)),
                   pl.BlockSpec((tm, D_MODEL), lambda i: (jnp.maximum(i - prompt_tiles, 0), 0))],
        out_shape=[jax.ShapeDtypeStruct((n_prompt, D_MODEL), F32),
                   jax.ShapeDtypeStruct((n - n_prompt, D_MODEL), F32)],
        compiler_params=_params("arbitrary"),
        name="residual_ln_split",
    )(x, y_packed, g[None, :], b[None, :])


ROUTER_E0 = N_GROUPS


def _route(logits, wt_ref, cls_ref):
    lane = lax.broadcasted_iota(jnp.int32, logits.shape, 1).astype(F32)
    big = float(LANES)
    rmax = lambda a: jnp.max(a, axis=-1, keepdims=True)
    rmin = lambda a: jnp.min(a, axis=-1, keepdims=True)
    is_g = lane < N_GROUPS
    gl = jnp.where(is_g, logits, -jnp.inf)
    gmax = rmax(gl)
    gsel = rmin(jnp.where(gl == gmax, lane, big))
    gsum = jnp.sum(jnp.where(is_g, jnp.exp(logits - gmax), 0.0), axis=-1, keepdims=True)
    g_w = 1.0 / gsum
    e_lo = ROUTER_E0 + EXP_PER_GROUP * gsel
    in_grp = (lane >= e_lo) & (lane < e_lo + EXP_PER_GROUP)
    el = jnp.where(in_grp, logits, -jnp.inf)
    e1 = rmax(el)
    i1 = rmin(jnp.where(el == e1, lane, big))
    el2 = jnp.where(lane == i1, -jnp.inf, el)
    e2 = rmax(el2)
    i2 = rmin(jnp.where(el2 == e2, lane, big))
    t = jnp.exp(e2 - e1)
    p1 = 1.0 / (1.0 + t)
    w1 = p1 * g_w
    w2 = (t * p1) * g_w
    a1 = i1 - e_lo
    a2 = i2 - e_lo
    first_low = a1 < a2
    lo = jnp.where(first_low, a1, a2)
    hi = jnp.where(first_low, a2, a1)
    w_lo = jnp.where(first_low, w1, w2)
    w_hi = jnp.where(first_low, w2, w1)
    off = jnp.where(lo == 0.0, 0.0, jnp.where(lo == 1.0, 3.0, 5.0))
    cls = gsel * float(len(PAIRS)) + off + (hi - lo - 1.0)
    wt_ref[...] = jnp.where(lane < LANES // 2, w_lo, w_hi)
    cls_ref[...] = jnp.broadcast_to(cls, logits.shape).astype(jnp.int32)


def _expert_body(ea_ref, eb_ref, valid_ref, x_ref, wt_ref, wga_ref, wua_ref, wda_ref,
                 wgb_ref, wub_ref, wdb_ref, y_ref):
    i = pl.program_id(0)

    @pl.when(valid_ref[i] == 0)
    def _():
        y_ref[...] = jnp.zeros_like(y_ref)

    @pl.when(valid_ref[i] != 0)
    def _():
        x = _load_packed(x_ref, MOE_TILE).astype(BF16)
        dot = functools.partial(jnp.dot, preferred_element_type=F32)

        ga, ua = dot(x, wga_ref[...]), dot(x, wua_ref[...])
        gb, ub = dot(x, wgb_ref[...]), dot(x, wub_ref[...])
        ya = dot((jax.nn.silu(ga) * ua).astype(BF16), wda_ref[...])
        yb = dot((jax.nn.silu(gb) * ub).astype(BF16), wdb_ref[...])
        w_lo = wt_ref[:, 0:1]
        w_hi = wt_ref[:, LANES // 2:LANES // 2 + 1]
        _store_packed(y_ref, w_lo * ya + w_hi * yb)


def experts(xs, wts, tile_ea, tile_eb, tile_valid, w_gate, w_up, w_down):
    wspec = lambda shape, which: pl.BlockSpec(
        (None,) + shape, lambda i, ea, eb, va: ((ea, eb)[which][i], 0, 0))
    up = (D_MODEL, D_EXPERT)
    down = (D_EXPERT, D_MODEL)
    packed = pl.BlockSpec((MOE_TILE * PACK_ROWS, LANES), lambda i, ea, eb, va: (i, 0))
    grid_spec = pltpu.PrefetchScalarGridSpec(
        num_scalar_prefetch=3,
        grid=(MOE_TILES,),
        in_specs=[packed,
                  pl.BlockSpec((MOE_TILE, LANES), lambda i, ea, eb, va: (i, 0)),
                  wspec(up, 0), wspec(up, 0), wspec(down, 0),
                  wspec(up, 1), wspec(up, 1), wspec(down, 1)],
        out_specs=packed,
    )
    return pl.pallas_call(
        _expert_body,
        grid_spec=grid_spec,
        out_shape=jax.ShapeDtypeStruct((MOE_ROWS * PACK_ROWS, LANES), jnp.int32),
        compiler_params=_params("arbitrary"),
        name="experts",
    )(tile_ea, tile_eb, tile_valid, xs, wts, w_gate, w_up, w_down, w_gate, w_up, w_down)


def sc_dispatch(xp, wts, pos, fill_rows):
    n = pos.shape[0]
    n_fill = fill_rows.shape[0]
    info = plsc.get_sparse_core_info()
    n_workers = info.num_cores * info.num_subcores
    per_worker = n // n_workers
    steps = per_worker // SC_TOKENS
    fill_steps = n_fill // n_workers // SC_FILL
    assert steps * SC_TOKENS * n_workers == n and steps % 2 == 0
    assert fill_steps * SC_FILL * n_workers == n_fill
    mesh = plsc.VectorSubcoreMesh(core_axis_name="core", subcore_axis_name="subcore")
    dma = pltpu.SemaphoreType.DMA

    @functools.partial(
        pl.kernel,
        out_type=[jax.ShapeDtypeStruct((n + n_fill, PACK_ROWS, LANES), xp.dtype),
                  jax.ShapeDtypeStruct((n + n_fill, LANES), wts.dtype)],
        mesh=mesh,
        scratch_types=[pltpu.VMEM((steps, SC_TOKENS), jnp.int32),
                       pltpu.VMEM((fill_steps, SC_FILL), jnp.int32),
                       pltpu.VMEM((2, SC_TOKENS, PACK_ROWS, LANES), xp.dtype),
                       pltpu.VMEM((2, SC_TOKENS, LANES), wts.dtype),
                       dma((2,)), dma((2,)), dma((2,)), dma((2,))],
    )
    def scatter(x_hbm, w_hbm, pos_hbm, fill_hbm, zx_hbm, zw_hbm, xs_hbm, ws_hbm,
                pos_v, fill_v, xbuf, wbuf, read_x, read_w, write_x, write_w):
        worker = lax.axis_index("subcore") * info.num_cores + lax.axis_index("core")
        base = worker * per_worker
        pltpu.sync_copy(pos_hbm.at[worker], pos_v)
        pltpu.sync_copy(fill_hbm.at[worker], fill_v)

        def reads(s, b):
            rows = pl.ds(base + s * SC_TOKENS, SC_TOKENS)
            return (pltpu.make_async_copy(x_hbm.at[rows], xbuf.at[b], read_x.at[b]),
                    pltpu.make_async_copy(w_hbm.at[rows], wbuf.at[b], read_w.at[b]))

        def writes(s, b):
            return (pltpu.make_async_copy(xbuf.at[b], xs_hbm.at[pos_v.at[s]], write_x.at[b]),
                    pltpu.make_async_copy(wbuf.at[b], ws_hbm.at[pos_v.at[s]], write_w.at[b]))

        def start(copies):
            for c in copies:
                c.start()

        def wait(copies):
            for c in copies:
                c.wait()

        start(reads(0, 0))

        @pl.loop(0, steps, step=2)
        def _(s):
            wait(reads(s, 0))
            start(reads(s + 1, 1))
            start(writes(s, 0))
            wait(reads(s + 1, 1))
            wait(writes(s, 0))

            @pl.when(s + 2 < steps)
            def _():
                start(reads(s + 2, 0))

            start(writes(s + 1, 1))
            wait(writes(s + 1, 1))

        zx = xbuf.at[0, pl.ds(0, SC_FILL)]
        zw = wbuf.at[0, pl.ds(0, SC_FILL)]
        pltpu.sync_copy(zx_hbm, zx)
        pltpu.sync_copy(zw_hbm, zw)
        fills = [(pltpu.make_async_copy(zx, xs_hbm.at[fill_v.at[s]], write_x.at[0]),
                  pltpu.make_async_copy(zw, ws_hbm.at[fill_v.at[s]], write_w.at[0])) for s in range(fill_steps)]
        for f in fills:
            start(f)
        for f in fills:
            wait(f)

    xs, ws = scatter(xp.reshape(n, PACK_ROWS, LANES), wts,
                     pos.reshape(n_workers, steps, SC_TOKENS), fill_rows.reshape(n_workers, fill_steps, SC_FILL),
                     jnp.zeros((SC_FILL, PACK_ROWS, LANES), xp.dtype), jnp.zeros((SC_FILL, LANES), wts.dtype))
    return xs.reshape((n + n_fill) * PACK_ROWS, LANES), ws


def sc_gather_tokens(x, idx):
    n = x.shape[0] // PACK_ROWS
    p = idx.shape[0]
    info = plsc.get_sparse_core_info()
    n_workers = info.num_cores * info.num_subcores
    per_worker = p // n_workers
    steps = per_worker // SC_TOKENS
    assert steps * SC_TOKENS * n_workers == p and steps % 2 == 0
    mesh = plsc.VectorSubcoreMesh(core_axis_name="core", subcore_axis_name="subcore")
    dma = pltpu.SemaphoreType.DMA

    @functools.partial(
        pl.kernel,
        out_type=jax.ShapeDtypeStruct((p, PACK_ROWS, LANES), x.dtype),
        mesh=mesh,
        scratch_types=[pltpu.VMEM((steps, SC_TOKENS), jnp.int32),
                       pltpu.VMEM((2, SC_TOKENS, PACK_ROWS, LANES), x.dtype),
                       dma((2,)), dma((2,))],
    )
    def gather(x_hbm, i_hbm, o_hbm, idx_v, buf, read_sem, write_sem):
        worker = lax.axis_index("subcore") * info.num_cores + lax.axis_index("core")
        base = worker * per_worker
        pltpu.sync_copy(i_hbm.at[worker], idx_v)

        def read(s, b):
            return pltpu.make_async_copy(x_hbm.at[idx_v.at[s]], buf.at[b], read_sem.at[b])

        def write(s, b):
            return pltpu.make_async_copy(buf.at[b], o_hbm.at[pl.ds(base + s * SC_TOKENS, SC_TOKENS)], write_sem.at[b])

        read(0, 0).start()

        @pl.loop(0, steps, step=2)
        def _(s):
            read(s, 0).wait()
            read(s + 1, 1).start()
            write(s, 0).start()
            read(s + 1, 1).wait()
            write(s, 0).wait()

            @pl.when(s + 2 < steps)
            def _():
                read(s + 2, 0).start()

            write(s + 1, 1).start()
            write(s + 1, 1).wait()

    out = gather(x.reshape(n, PACK_ROWS, LANES), idx.reshape(n_workers, steps, SC_TOKENS))
    return out.reshape(p * PACK_ROWS, LANES)


_CLASS_EA = [g * EXP_PER_GROUP + lo for g in range(N_GROUPS) for lo, hi in PAIRS]
_CLASS_EB = [g * EXP_PER_GROUP + hi for g in range(N_GROUPS) for lo, hi in PAIRS]


def dispatch_plan(cls):
    onehot = (cls[:, None] == jnp.arange(N_CLASSES, dtype=jnp.int32)[None, :]).astype(jnp.int32)
    csum = jnp.cumsum(onehot, axis=0)
    rank = jnp.sum(csum * onehot, axis=1) - 1
    counts = csum[-1]
    tiles_per = (counts + MOE_TILE - 1) // MOE_TILE
    tile_end = jnp.cumsum(tiles_per)
    tile_start = tile_end - tiles_per
    pos = jnp.sum(onehot * tile_start[None, :], axis=1) * MOE_TILE + rank
    n = cls.shape[0]
    pad = tiles_per * MOE_TILE - counts
    pad_end = jnp.cumsum(pad)
    k = jnp.arange(MOE_ROWS - n, dtype=jnp.int32)
    k_cls = jnp.sum((k[:, None] >= pad_end[None, :]).astype(jnp.int32), axis=1)
    k_hot = (k_cls[:, None] == jnp.arange(N_CLASSES, dtype=jnp.int32)[None, :]).astype(jnp.int32)
    pad_row0 = tile_start * MOE_TILE + counts - (pad_end - pad)
    fill_rows = k + jnp.where(k_cls < N_CLASSES, jnp.sum(k_hot * pad_row0[None, :], axis=1),
                              tile_end[-1] * MOE_TILE - pad_end[-1])
    t = jnp.arange(MOE_TILES, dtype=jnp.int32)
    tile_cls = jnp.sum((t[:, None] >= tile_end[None, :]).astype(jnp.int32), axis=1)
    valid = (tile_cls < N_CLASSES).astype(jnp.int32)
    last_cls = jnp.max(jnp.where(counts > 0, jnp.arange(N_CLASSES, dtype=jnp.int32), 0))
    tile_cls = jnp.where(valid == 1, tile_cls, last_cls)
    ea = jnp.asarray(_CLASS_EA, jnp.int32)[tile_cls]
    eb = jnp.asarray(_CLASS_EB, jnp.int32)[tile_cls]
    return pos, fill_rows, ea, eb, valid


def moe(xp, wts, cls, w_gate, w_up, w_down):
    pos, fill_rows, ea, eb, valid = dispatch_plan(cls[:, 0])
    xs, ws = sc_dispatch(xp, wts, pos, fill_rows)
    ys = experts(xs, ws, ea, eb, valid, w_gate, w_up, w_down)
    return sc_gather_tokens(ys, pos)


def _rows(a):
    return a.reshape(-1, a.shape[-1])


def kernel(x_prompt, x_sample, state_C, state_n, state_m, cache_k, cache_v, a_w_in, a_b_gate, a_norm, a_w_out, b_w_in, b_rel, b_w_out, ln1_g, ln1_b, ln2_g, ln2_b, r_w_group, r_b_group, r_w_expert, r_b_expert, e_w_gate, e_w_up, e_w_down):
    hk = A_HEADS * A_DK
    n_main = 2 * hk + 2 * D_MODEL
    gate_w = lambda layer: a_w_in[layer // 2, :, n_main:]
    x, xb, g = embed(_rows(x_prompt), _rows(x_sample), gate_w(0))
    outs = {k: [] for k in ("Cp", "np", "mp", "kp", "vp", "Cs", "ns", "ms", "ks", "vs")}
    for layer in range(DEPTH):
        j = layer // 2
        if layer % 2 == 0:
            proj = matmul(xb, a_w_in, j, n_main, BF16)
            zc = jnp.zeros((BATCH, A_HEADS, A_DK, A_DV), F32)
            zn = jnp.zeros((BATCH, A_HEADS, A_DK), F32)
            zm = jnp.zeros((BATCH, A_HEADS), F32)
            h_p, c_p, n_p, m_p = mlstm(proj, g, a_b_gate[j], a_norm[j], zc, zn, zm,
                                       batch=BATCH, seq=SEQ, L=MLSTM_CHUNK, row0=0)
            h_s, c_s, n_s, m_s = mlstm(proj, g, a_b_gate[j], a_norm[j],
                                       state_C[j], state_n[j], state_m[j],
                                       batch=DEC_BATCH, seq=DEC_SEQ, L=DEC_SEQ, row0=N_PROMPT)
            outs["Cp"].append(c_p); outs["np"].append(n_p); outs["mp"].append(m_p)
            outs["Cs"].append(c_s); outs["ns"].append(n_s); outs["ms"].append(m_s)
            w_out = a_w_out[j]
        else:
            qkv = matmul(xb, b_w_in, j, 3 * D_MODEL, BF16)
            h_p, k_p, v_p = attn_prompt(qkv, b_rel[j])
            h_s, k_s, v_s = attn_sample(qkv, cache_k, cache_v, j, b_rel[j])
            heads = lambda a: a.reshape(a.shape[:2] + (B_HEADS, B_DH))
            outs["kp"].append(heads(k_p)); outs["vp"].append(heads(v_p))
            outs["ks"].append(heads(k_s)); outs["vs"].append(heads(v_s))
            w_out = b_w_out[j]
        x, xb, xp, wts, cls = mix_ln_router(h_p, h_s, w_out.astype(BF16), x, ln1_g[layer], ln1_b[layer],
                                            r_w_group[layer], r_b_group[layer], r_w_expert[layer], r_b_expert[layer])
        y = moe(xp, wts, cls, cast_layer(e_w_gate, layer), cast_layer(e_w_up, layer), cast_layer(e_w_down, layer))
        if layer < DEPTH - 1:
            if (layer + 1) % 2 == 0:
                x, xb, g = residual_ln(x, y, ln2_g[layer], ln2_b[layer], gate_w(layer + 1))
            else:
                x, xb = residual_ln(x, y, ln2_g[layer], ln2_b[layer])
        else:
            y_p, y_s = residual_ln_split(x, y, ln2_g[layer], ln2_b[layer], N_PROMPT)
    st = lambda k: jnp.stack(outs[k])
    return (y_p.reshape(BATCH, SEQ, D_MODEL), y_s.reshape(DEC_BATCH, DEC_SEQ, D_MODEL),
            st("Cp"), st("np"), st("mp"), st("kp"), st("vp"),
            st("Cs"), st("ns"), st("ms"), st("ks"), st("vs"))
```

```python
import functools

import jax
import jax.numpy as jnp
from jax import lax
from jax.experimental import pallas as pl
from jax.experimental.pallas import tpu as pltpu
from jax.experimental.pallas import tpu_sc as plsc

F32 = jnp.float32
BF16 = jnp.bfloat16

D_MODEL = 2048
BATCH = 8
SEQ = 4096
DEPTH = 4
DEC_BATCH = 32
DEC_SEQ = 32
PAST_LEN = 1024
CHUNK = 64
A_HEADS = 8
A_DK = 128
A_DV = D_MODEL // A_HEADS
GATE_CAP = 15.0
B_HEADS = 16
B_DH = D_MODEL // B_HEADS
PREV_CHUNKS = 8
REACH = PREV_CHUNKS * CHUNK
REL_CLIP = 256
N_GROUPS = 4
EXP_PER_GROUP = 4
N_EXPERTS = N_GROUPS * EXP_PER_GROUP
D_EXPERT = D_MODEL // 4
ALPHA = (2 * DEPTH) ** 0.25
LN_EPS = 1e-5
RMS_EPS = 1e-6

N_PROMPT = BATCH * SEQ
N_SAMPLE = DEC_BATCH * DEC_SEQ
N_TOK = N_PROMPT + N_SAMPLE

VMEM_LIMIT = 56 * 1024 * 1024
LANES = 128

PAIRS = ((0, 1), (0, 2), (0, 3), (1, 2), (1, 3), (2, 3))
N_CLASSES = N_GROUPS * len(PAIRS)
MOE_TILE = 256
MOE_TILES = -(-(N_TOK // MOE_TILE + N_CLASSES) // 32) * 32
MOE_ROWS = MOE_TILES * MOE_TILE

MLSTM_CHUNK = 128
ATT_TQ = 256
ATT_TK = ATT_TQ + REACH
NEG = -1e30

SC_TOKENS = 48
SC_FILL = 32


def _params(*sem):
    return pltpu.CompilerParams(dimension_semantics=sem, vmem_limit_bytes=VMEM_LIMIT)


def _mm_body(x_ref, w_ref, o_ref, wb):
    @pl.when(pl.program_id(1) == 0)
    def _():
        wb[...] = w_ref[...].astype(BF16)

    o_ref[...] = jnp.dot(x_ref[...], wb[...], preferred_element_type=F32).astype(o_ref.dtype)


def matmul(x, w_all, layer, m, out_dtype, tm=512, tn=1024):
    n, k = x.shape
    return pl.pallas_call(
        _mm_body,
        grid=(m // tn, n // tm),
        in_specs=[pl.BlockSpec((tm, k), lambda j, i: (i, 0)),
                  pl.BlockSpec((None, k, tn), lambda j, i: (layer, 0, j))],
        out_specs=pl.BlockSpec((tm, tn), lambda j, i: (i, j)),
        out_shape=jax.ShapeDtypeStruct((n, m), out_dtype),
        scratch_shapes=[pltpu.VMEM((k, tn), BF16)],
        compiler_params=_params("parallel", "arbitrary"),
        name="matmul",
    )(x, w_all)


def _cast_body(w_ref, o_ref):
    o_ref[...] = w_ref[...].astype(BF16)


def cast_layer(w_all, layer):
    e, a, b = w_all.shape[1:]
    per_step = 2
    return pl.pallas_call(
        _cast_body,
        grid=(e // per_step,),
        in_specs=[pl.BlockSpec((None, per_step, a, b), lambda i: (layer, i, 0, 0))],
        out_specs=pl.BlockSpec((per_step, a, b), lambda i: (i, 0, 0)),
        out_shape=jax.ShapeDtypeStruct((e, a, b), BF16),
        compiler_params=_params("parallel"),
        name="cast_layer",
    )(w_all)


def _split2(x):
    hi = x.astype(BF16)
    lo = (x - hi.astype(F32)).astype(BF16)
    return hi, lo


def _dot3(xh, xl, wh, wl, dims):
    dg = functools.partial(lax.dot_general, dimension_numbers=dims, preferred_element_type=F32)
    return dg(xh, wh) + (dg(xh, wl) + dg(xl, wh))


_NN = (((1,), (0,)), ((), ()))
_NT = (((1,), (1,)), ((), ()))
_TN = (((0,), (0,)), ((), ()))


def _dot_hilo(x, w):
    t = x.shape[0]
    xh, xl = _split2(x)
    wh, wl = _split2(w)
    r = jnp.dot(jnp.concatenate([xh, xl], axis=0), jnp.concatenate([wh, wl], axis=1), preferred_element_type=F32)
    return r[:t, :LANES] + (r[:t, LANES:] + r[t:, :LANES]) + r[t:, LANES:]


def _log_sigmoid(x):
    return jnp.minimum(x, 0.0) - jnp.log(1.0 + jnp.exp(-jnp.abs(x)))


def _split3(x):
    a = x.astype(BF16)
    r = x - a.astype(F32)
    b = r.astype(BF16)
    c = (r - b.astype(F32)).astype(BF16)
    return a, b, c


A_STATE = A_DV + LANES


def _cummax_rows(x):
    n = x.shape[0]
    row = lax.broadcasted_iota(jnp.int32, x.shape, 0)
    k = 1
    while k < n:
        x = jnp.maximum(x, jnp.where(row >= k, pltpu.roll(x, k, axis=0), -jnp.inf))
        k *= 2
    return x


def _mlstm_body(q_ref, k_ref, v_ref, o_ref, g_ref, bias_ref, gain_ref, c0_ref, m0_ref,
                h_ref, c_ref, m_ref, *, L):
    step = pl.program_id(1)

    @pl.when(step == 0)
    def _():
        c_ref[...] = c0_ref[...]
        m_ref[...] = m0_ref[...]

    dot = functools.partial(jnp.dot, preferred_element_type=F32)
    row = lax.broadcasted_iota(jnp.int32, (L, L), 0)
    col = lax.broadcasted_iota(jnp.int32, (L, L), 1)
    causal = col <= row
    tril = jnp.where(causal, 1.0, 0.0).astype(BF16)

    pre = GATE_CAP * jnp.tanh((g_ref[...] + bias_ref[...]) * (1.0 / GATE_CAP))
    ig = pre
    lf = _log_sigmoid(pltpu.roll(pre, LANES - A_HEADS, axis=1))
    a3, b3, c3 = _split3(lf)
    bt = dot(tril, a3) + (dot(tril, b3) + dot(tril, c3))
    a = ig - bt
    m_prev = m_ref[0]
    m_t = bt + jnp.maximum(m_prev, _cummax_rows(a))
    u = bt - m_t
    inter = jnp.exp(bt + m_prev - m_t)
    eminus = jnp.exp(-m_t)
    m_new = m_t[L - 1:L, :]
    bt_last = bt[L - 1:L, :]
    w_c = jnp.exp((bt_last - m_new) + a)
    decay = jnp.exp(bt_last + m_prev - m_new)
    m_ref[0] = m_new
    a_pad = a if L == LANES else jnp.concatenate([a, jnp.zeros((LANES - L, LANES), F32)], axis=0)
    a_t = jnp.transpose(a_pad)
    ie = jnp.concatenate([inter, eminus], axis=1).astype(BF16)
    sel_r = lax.broadcasted_iota(jnp.int32, (2 * LANES, 2 * LANES), 0)
    sel_c = lax.broadcasted_iota(jnp.int32, (2 * LANES, 2 * LANES), 1)
    same_half = (sel_r >= LANES) == (sel_c >= LANES)
    ones_l = jnp.ones((L, LANES), BF16)
    zeros_l = jnp.zeros((L, LANES), BF16)
    zeros_k = jnp.zeros((A_DK, LANES), BF16)
    mean_cols = jnp.full((A_DV, LANES), 1.0, BF16)

    scale = A_DK ** -0.5
    heads = range(A_HEADS)
    dk_sl = lambda h: slice(h * A_DK, (h + 1) * A_DK)
    dv_sl = lambda h: slice(h * A_DV, (h + 1) * A_DV)
    ie_b, qk = [], []
    for h in heads:
        sel = jnp.where(same_half & ((sel_r & (LANES - 1)) == h), 1.0, 0.0).astype(BF16)
        ie_b.append(dot(ie, sel))
        qk.append(lax.dot_general(q_ref[:, dk_sl(h)], k_ref[:, dk_sl(h)], _NT, preferred_element_type=F32))
    out = []
    for h in heads:
        d_mat = jnp.where(causal, jnp.exp(u[:, h:h + 1] + a_t[h:h + 1, :L]), 0.0)
        s = qk[h] * (d_mat * scale)
        q_i = (q_ref[:, dk_sl(h)].astype(F32) * ie_b[h][:, :LANES]).astype(BF16)
        c_b = c_ref[0, h].astype(BF16)
        rhs = jnp.concatenate([
            jnp.concatenate([v_ref[:, dv_sl(h)], ones_l, zeros_l], axis=1),
            jnp.concatenate([c_b[:, :A_DV], zeros_k, c_b[:, A_DV:]], axis=1)], axis=0)
        out.append(dot(jnp.concatenate([s.astype(BF16), q_i], axis=1), rhs))
    hh, ms = [], []
    for h in heads:
        den = jnp.maximum(jnp.abs(out[h][:, A_DV:A_DV + LANES] + out[h][:, A_DV + LANES:]), ie_b[h][:, LANES:])
        r = 1.0 / den
        hh.append(out[h][:, :A_DV] * jnp.concatenate([r, r], axis=1))
        ms.append(dot((hh[h] * hh[h]).astype(BF16), mean_cols))
    upd = []
    for h in heads:
        rs = lax.rsqrt(ms[h] * (1.0 / A_DV) + RMS_EPS)
        og = o_ref[:, dv_sl(h)].astype(F32)
        y = hh[h] * jnp.concatenate([rs, rs], axis=1) * gain_ref[:, dv_sl(h)] * jax.nn.sigmoid(og)
        h_ref[:, dv_sl(h)] = y.astype(h_ref.dtype)
        wk = (k_ref[:, dk_sl(h)].astype(F32) * (w_c[:, h:h + 1] * scale)).astype(BF16)
        upd.append(lax.dot_general(wk, jnp.concatenate([v_ref[:, dv_sl(h)], ones_l], axis=1), _TN,
                                   preferred_element_type=F32))
    for h in heads:
        c_ref[0, h] = decay[:, h:h + 1] * c_ref[0, h] + upd[h]


def mlstm(proj, g, b_gate, gain, c0, n0, m0, *, batch, seq, L, row0):
    nc = seq // L
    bias = jnp.pad(b_gate.astype(F32), (0, LANES - 2 * A_HEADS))
    row_blk = lambda b, c: row0 // L + b * nc + c
    st = lambda *s: pl.BlockSpec((1,) + s, lambda b, c: (b,) + (0,) * len(s))
    state0 = jnp.concatenate([c0, jnp.broadcast_to(n0[..., None], n0.shape + (LANES,))], axis=-1)
    m_lanes = jnp.pad(m0, ((0, 0), (0, LANES - A_HEADS)))[:, None, :]
    h, state, m = pl.pallas_call(
        functools.partial(_mlstm_body, L=L),
        grid=(batch, nc),
        in_specs=[pl.BlockSpec((L, A_HEADS * A_DK), lambda b, c: (row_blk(b, c), 0)),
                  pl.BlockSpec((L, A_HEADS * A_DK), lambda b, c: (row_blk(b, c), 1)),
                  pl.BlockSpec((L, D_MODEL), lambda b, c: (row_blk(b, c), 1)),
                  pl.BlockSpec((L, D_MODEL), lambda b, c: (row_blk(b, c), 2)),
                  pl.BlockSpec((L, LANES), lambda b, c: (row_blk(b, c), 0)),
                  pl.BlockSpec((1, LANES), lambda b, c: (0, 0)),
                  pl.BlockSpec((1, D_MODEL), lambda b, c: (0, 0)),
                  st(A_HEADS, A_DK, A_STATE), st(1, LANES)],
        out_specs=[pl.BlockSpec((L, D_MODEL), lambda b, c: (b * nc + c, 0)),
                   st(A_HEADS, A_DK, A_STATE), st(1, LANES)],
        out_shape=[jax.ShapeDtypeStruct((batch * seq, D_MODEL), BF16),
                   jax.ShapeDtypeStruct((batch, A_HEADS, A_DK, A_STATE), F32),
                   jax.ShapeDtypeStruct((batch, 1, LANES), F32)],
        compiler_params=_params("parallel", "arbitrary"),
        name="mlstm",
    )(proj, proj, proj, proj, g, bias[None, :], gain.astype(F32)[None, :], state0, m_lanes)
    return h, state[..., :A_DV], state[..., A_DV], m[:, 0, :A_HEADS]


LOG2E = 1.4426950408889634
Q_SCALE = B_DH ** -0.5 * LOG2E


def _scale_q(q):
    return (q.astype(F32) * Q_SCALE).astype(BF16)


def _attn_prompt_body(q_ref, k_ref, v_ref, bias_ref, o_ref, kt_ref, vt_ref, qs, kpad, vpad):
    kt_ref[0] = k_ref[SEQ - REACH:, :].astype(F32)
    vt_ref[0] = v_ref[SEQ - REACH:, :].astype(F32)
    qs[...] = _scale_q(q_ref[...])
    kpad[0:REACH, :] = jnp.zeros((REACH, B_DH), BF16)
    kpad[REACH:, :] = k_ref[...]
    vpad[0:REACH, 0:B_DH] = jnp.zeros((REACH, B_DH), BF16)
    vpad[REACH:, 0:B_DH] = v_ref[...]
    vpad[:, B_DH:] = jnp.ones((SEQ + REACH, B_DH), BF16)

    def scores(t):
        r0 = t * ATT_TQ
        kb = kpad[r0:r0 + ATT_TK, :]
        s = lax.dot_general(qs[r0:r0 + ATT_TQ, :], kb, _NT, preferred_element_type=F32) + bias_ref[0]
        if r0 < REACH:
            j = lax.broadcasted_iota(jnp.int32, (ATT_TQ, ATT_TK), 1)
            s = jnp.where(j + r0 >= REACH, s, NEG)
        return s

    def finish(t, s):
        r0 = t * ATT_TQ
        p = jnp.exp2(s - jnp.max(s, axis=-1, keepdims=True)).astype(BF16)
        acc = jnp.dot(p, vpad[r0:r0 + ATT_TK, :], preferred_element_type=F32)
        o_ref[r0:r0 + ATT_TQ, :] = (acc[:, :B_DH] / acc[:, B_DH:]).astype(o_ref.dtype)

    n_tiles = SEQ // ATT_TQ
    s_next = scores(0)
    for t in range(n_tiles):
        s_cur = s_next
        if t + 1 < n_tiles:
            s_next = scores(t + 1)
        finish(t, s_cur)


def band_bias(rel_table, nq, nk):
    w = nq + nk
    rel = jnp.clip(REACH + nq - 1 - jnp.arange(w), -REL_CLIP, REL_CLIP) + REL_CLIP
    r = rel_table.astype(F32)[:, rel]
    heads = r.shape[0]
    skew = jnp.broadcast_to(r[:, None, :], (heads, nq, w)).reshape(heads, nq * w)
    skew = skew[:, :nq * (w - 1)].reshape(heads, nq, w - 1)
    bias = skew[:, :, nq - 1:nq - 1 + nk]
    i = jnp.arange(nq)[:, None]
    j = jnp.arange(nk)[None, :]
    dc = j // CHUNK - i // CHUNK
    allowed = (dc >= 0) & (dc <= PREV_CHUNKS)
    return jnp.where(allowed[None], bias * LOG2E, NEG)


def attn_prompt(qkv, rel_table):
    bias = band_bias(rel_table, ATT_TQ, ATT_TK)
    tail = pl.BlockSpec((1, REACH, B_DH), lambda b, h: (b, 0, h))
    return pl.pallas_call(
        _attn_prompt_body,
        grid=(BATCH, B_HEADS),
        in_specs=[pl.BlockSpec((SEQ, B_DH), lambda b, h: (b, h)),
                  pl.BlockSpec((SEQ, B_DH), lambda b, h: (b, B_HEADS + h)),
                  pl.BlockSpec((SEQ, B_DH), lambda b, h: (b, 2 * B_HEADS + h)),
                  pl.BlockSpec((1, ATT_TQ, ATT_TK), lambda b, h: (h, 0, 0))],
        out_specs=[pl.BlockSpec((SEQ, B_DH), lambda b, h: (b, h)), tail, tail],
        out_shape=[jax.ShapeDtypeStruct((N_PROMPT, D_MODEL), BF16),
                   jax.ShapeDtypeStruct((BATCH, REACH, D_MODEL), F32),
                   jax.ShapeDtypeStruct((BATCH, REACH, D_MODEL), F32)],
        scratch_shapes=[pltpu.VMEM((SEQ, B_DH), BF16),
                        pltpu.VMEM((SEQ + REACH, B_DH), BF16),
                        pltpu.VMEM((SEQ + REACH, 2 * B_DH), BF16)],
        compiler_params=_params("parallel", "parallel"),
        name="attn_prompt",
    )(qkv, qkv, qkv, bias)


def _attn_sample_body(q_ref, kn_ref, vn_ref, ck_ref, cv_ref, bias_c_ref, bias_n_ref, o_ref, ks_ref, vs_ref):
    ks_ref[0] = kn_ref[...].astype(F32)
    vs_ref[0] = vn_ref[...].astype(F32)
    for h in range(B_HEADS):
        sl = slice(h * B_DH, (h + 1) * B_DH)
        q = _scale_q(q_ref[:, sl])
        kc = ck_ref[pl.ds(h, REACH, stride=B_HEADS), :].astype(BF16)
        vc = cv_ref[pl.ds(h, REACH, stride=B_HEADS), :].astype(BF16)
        s_c = lax.dot_general(q, kc, _NT, preferred_element_type=F32) + bias_c_ref[h]
        s_n = lax.dot_general(q, kn_ref[:, sl], _NT, preferred_element_type=F32) + bias_n_ref[h]
        m = jnp.maximum(jnp.max(s_c, axis=-1, keepdims=True), jnp.max(s_n, axis=-1, keepdims=True))
        p_c = jnp.exp2(s_c - m)
        p_n = jnp.exp2(s_n - m)
        l = jnp.sum(p_c, axis=-1, keepdims=True) + jnp.sum(p_n, axis=-1, keepdims=True)
        dot = functools.partial(jnp.dot, preferred_element_type=F32)
        o_ref[:, sl] = ((dot(p_c.astype(BF16), vc) + dot(p_n.astype(BF16), vn_ref[:, sl])) / l).astype(o_ref.dtype)


def attn_sample(qkv, cache_k, cache_v, layer, rel_table):
    bias = band_bias(rel_table, DEC_SEQ, REACH + DEC_SEQ)
    cache_rows = REACH * B_HEADS
    ck = cache_k.reshape(cache_k.shape[0], DEC_BATCH, cache_rows, B_DH)
    cv = cache_v.reshape(cache_v.shape[0], DEC_BATCH, cache_rows, B_DH)
    row = lambda c: pl.BlockSpec((DEC_SEQ, D_MODEL), lambda b: (N_PROMPT // DEC_SEQ + b, c))
    cache = pl.BlockSpec((None, None, cache_rows, B_DH), lambda b: (layer, b, 0, 0))
    new = pl.BlockSpec((1, DEC_SEQ, D_MODEL), lambda b: (b, 0, 0))
    return pl.pallas_call(
        _attn_sample_body,
        grid=(DEC_BATCH,),
        in_specs=[row(0), row(1), row(2), cache, cache,
                  pl.BlockSpec((B_HEADS, DEC_SEQ, REACH), lambda b: (0, 0, 0)),
                  pl.BlockSpec((B_HEADS, DEC_SEQ, DEC_SEQ), lambda b: (0, 0, 0))],
        out_specs=[pl.BlockSpec((DEC_SEQ, D_MODEL), lambda b: (b, 0)), new, new],
        out_shape=[jax.ShapeDtypeStruct((N_SAMPLE, D_MODEL), BF16),
                   jax.ShapeDtypeStruct((DEC_BATCH, DEC_SEQ, D_MODEL), F32),
                   jax.ShapeDtypeStruct((DEC_BATCH, DEC_SEQ, D_MODEL), F32)],
        compiler_params=_params("parallel"),
        name="attn_sample",
    )(qkv, qkv, qkv, ck, cv, bias[:, :, :REACH], bias[:, :, REACH:])


def _ln(z, g, b):
    mu = jnp.mean(z, axis=-1, keepdims=True)
    zc = z - mu
    var = jnp.mean(zc * zc, axis=-1, keepdims=True)
    return zc * lax.rsqrt(var + LN_EPS) * g + b


PACK_ROWS = D_MODEL // 2 // LANES
HI_MASK = -65536


def _store_packed(ref, v, token0=0):
    t = v.shape[0]
    half = D_MODEL // 2
    hi = lax.bitcast_convert_type(v[:, :half].astype(BF16).astype(F32), jnp.int32)
    lo = lax.bitcast_convert_type(v[:, half:].astype(BF16).astype(F32), jnp.int32)
    words = (hi & HI_MASK) | lax.shift_right_logical(lo, 16)
    for a in range(PACK_ROWS):
        ref[pl.ds(token0 * PACK_ROWS + a, t, stride=PACK_ROWS), :] = words[:, a * LANES:(a + 1) * LANES]


def _load_packed(ref, t):
    words = jnp.concatenate([ref[pl.ds(a, t, stride=PACK_ROWS), :] for a in range(PACK_ROWS)], axis=1)
    hi = lax.bitcast_convert_type(words & HI_MASK, F32)
    lo = lax.bitcast_convert_type(lax.shift_left(words, 16), F32)
    return jnp.concatenate([hi, lo], axis=1)


def _mix_ln_router_body(hp_ref, hs_ref, w_ref, x_ref, g_ref, b_ref, rw_ref, rb_ref,
                        xp_ref, wt_ref, cls_ref, y_even, y_odd, *, prompt_tiles, n_tiles):
    i = pl.program_id(0)
    from_prompt = jnp.minimum(i, n_tiles - 1) < prompt_tiles

    @pl.when(i == 0)
    def _():
        y_odd[...] = jnp.zeros_like(y_odd)

    half = D_MODEL // 2

    def run(y_new, y_old):
        h = jnp.where(from_prompt, hp_ref[...], hs_ref[...])
        y_new[:, :half] = jnp.dot(h, w_ref[:, :half], preferred_element_type=F32)
        xn = _ln(ALPHA * x_ref[...] + y_old[...], g_ref[...], b_ref[...])
        logits = _dot_hilo(xn, rw_ref[...]) + rb_ref[...]
        y_new[:, half:] = jnp.dot(h, w_ref[:, half:], preferred_element_type=F32)
        _store_packed(xp_ref, xn)
        _route(logits, wt_ref, cls_ref)

    @pl.when(i % 2 == 0)
    def _():
        run(y_even, y_odd)

    @pl.when(i % 2 == 1)
    def _():
        run(y_odd, y_even)


def mix_ln_router(h_p, h_s, w, x, g, b, w_group, b_group, w_expert, b_expert, tm=256):
    n = x.shape[0]
    n_tiles = n // tm
    prompt_tiles = h_p.shape[0] // tm
    sample_tiles = h_s.shape[0] // tm
    assert prompt_tiles * tm == h_p.shape[0] and (prompt_tiles + sample_tiles) * tm == n
    mm_tile = lambda i: jnp.minimum(i, n_tiles - 1)
    ep_tile = lambda i: jnp.maximum(i - 1, 0)
    hp_spec = pl.BlockSpec((tm, D_MODEL), lambda i: (jnp.minimum(mm_tile(i), prompt_tiles - 1), 0))
    hs_spec = pl.BlockSpec((tm, D_MODEL), lambda i: (jnp.maximum(mm_tile(i) - prompt_tiles, 0), 0))
    rw = jnp.pad(jnp.concatenate([w_group, w_expert], axis=1), ((0, 0), (0, LANES - N_GROUPS - N_EXPERTS)))
    rb = jnp.pad(jnp.concatenate([b_group, b_expert]).astype(F32), (0, LANES - N_GROUPS - N_EXPERTS))
    row = pl.BlockSpec((tm, D_MODEL), lambda i: (ep_tile(i), 0))
    vec = pl.BlockSpec((1, D_MODEL), lambda i: (0, 0))
    lane_row = pl.BlockSpec((tm, LANES), lambda i: (ep_tile(i), 0))
    return pl.pallas_call(
        functools.partial(_mix_ln_router_body, prompt_tiles=prompt_tiles, n_tiles=n_tiles),
        grid=(n_tiles + 1,),
        in_specs=[hp_spec, hs_spec, pl.BlockSpec((D_MODEL, D_MODEL), lambda i: (0, 0)), row, vec, vec,
                  pl.BlockSpec((D_MODEL, LANES), lambda i: (0, 0)), pl.BlockSpec((1, LANES), lambda i: (0, 0))],
        out_specs=[pl.BlockSpec((tm * PACK_ROWS, LANES), lambda i: (ep_tile(i), 0)), lane_row, lane_row],
        out_shape=[jax.ShapeDtypeStruct((n * PACK_ROWS, LANES), jnp.int32),
                   jax.ShapeDtypeStruct((n, LANES), F32), jax.ShapeDtypeStruct((n, LANES), jnp.int32)],
        scratch_shapes=[pltpu.VMEM((tm, D_MODEL), F32), pltpu.VMEM((tm, D_MODEL), F32)],
        compiler_params=_params("arbitrary"),
        name="mix_ln_router",
    )(h_p, h_s, w, x, g[None, :], b[None, :], rw, rb[None, :])


def _add_ln_body(x_ref, y_ref, g_ref, b_ref, *rest):
    t = rest[-1].shape[0]
    xn = _ln(ALPHA * _load_packed(x_ref, t) + _load_packed(y_ref, t), g_ref[...], b_ref[...])
    if len(rest) == 2:
        xo_ref, xbo_ref = rest
    else:
        wg_ref, xo_ref, xbo_ref, go_ref = rest
        go_ref[...] = _dot_hilo(xn, wg_ref[...])
    xo_ref[...] = xn
    xbo_ref[...] = xn.astype(BF16)


def residual_ln(x_packed, y_packed, g, b, w_gate=None, tm=512):
    n = x_packed.shape[0] // PACK_ROWS
    row = pl.BlockSpec((tm, D_MODEL), lambda i: (i, 0))
    vec = pl.BlockSpec((1, D_MODEL), lambda i: (0, 0))
    packed = pl.BlockSpec((tm * PACK_ROWS, LANES), lambda i: (i, 0))
    in_specs = [packed, packed, vec, vec]
    out_specs = [row, row]
    out_shape = [jax.ShapeDtypeStruct((n, D_MODEL), F32), jax.ShapeDtypeStruct((n, D_MODEL), BF16)]
    args = [x_packed, y_packed, g[None, :], b[None, :]]
    if w_gate is not None:
        in_specs.append(pl.BlockSpec((D_MODEL, LANES), lambda i: (0, 0)))
        out_specs.append(pl.BlockSpec((tm, LANES), lambda i: (i, 0)))
        out_shape.append(jax.ShapeDtypeStruct((n, LANES), F32))
        args.append(jnp.pad(w_gate, ((0, 0), (0, LANES - w_gate.shape[1]))))
    return pl.pallas_call(
        _add_ln_body,
        grid=(n // tm,),
        in_specs=in_specs,
        out_specs=out_specs,
        out_shape=out_shape,
        compiler_params=_params("parallel"),
        name="residual_ln",
    )(*args)


def _embed_body(xp_ref, xs_ref, wg_ref, xo_ref, xbo_ref, go_ref, *, prompt_tiles):
    x = jnp.where(pl.program_id(0) < prompt_tiles, xp_ref[...], xs_ref[...])
    xo_ref[...] = x
    xbo_ref[...] = x.astype(BF16)
    go_ref[...] = _dot_hilo(x, wg_ref[...])


def embed(x_p, x_s, w_gate, tm=512):
    n = x_p.shape[0] + x_s.shape[0]
    prompt_tiles = x_p.shape[0] // tm
    assert prompt_tiles * tm == x_p.shape[0] and n % tm == 0
    row = pl.BlockSpec((tm, D_MODEL), lambda i: (i, 0))
    wg = jnp.pad(w_gate, ((0, 0), (0, LANES - w_gate.shape[1])))
    return pl.pallas_call(
        functools.partial(_embed_body, prompt_tiles=prompt_tiles),
        grid=(n // tm,),
        in_specs=[pl.BlockSpec((tm, D_MODEL), lambda i: (jnp.minimum(i, prompt_tiles - 1), 0)),
                  pl.BlockSpec((tm, D_MODEL), lambda i: (jnp.maximum(i - prompt_tiles, 0), 0)),
                  pl.BlockSpec((D_MODEL, LANES), lambda i: (0, 0))],
        out_specs=[row, row, pl.BlockSpec((tm, LANES), lambda i: (i, 0))],
        out_shape=[jax.ShapeDtypeStruct((n, D_MODEL), F32), jax.ShapeDtypeStruct((n, D_MODEL), BF16),
                   jax.ShapeDtypeStruct((n, LANES), F32)],
        compiler_params=_params("parallel"),
        name="embed",
    )(x_p, x_s, wg)


def _add_ln_split_body(x_ref, y_ref, g_ref, b_ref, op_ref, os_ref, *, prompt_tiles):
    t = op_ref.shape[0]
    xn = _ln(ALPHA * _load_packed(x_ref, t) + _load_packed(y_ref, t), g_ref[...], b_ref[...])
    i = pl.program_id(0)

    @pl.when(i < prompt_tiles)
    def _():
        op_ref[...] = xn

    @pl.when(i >= prompt_tiles)
    def _():
        os_ref[...] = xn


def residual_ln_split(x_packed, y_packed, g, b, n_prompt, tm=512):
    n = x_packed.shape[0] // PACK_ROWS
    prompt_tiles = n_prompt // tm
    assert prompt_tiles * tm == n_prompt and n % tm == 0
    vec = pl.BlockSpec((1, D_MODEL), lambda i: (0, 0))
    packed = pl.BlockSpec((tm * PACK_ROWS, LANES), lambda i: (i, 0))
    return pl.pallas_call(
        functools.partial(_add_ln_split_body, prompt_tiles=prompt_tiles),
        grid=(n // tm,),
        in_specs=[packed, packed, vec, vec],
        out_specs=[pl.BlockSpec((tm, D_MODEL), lambda i: (jnp.minimum(i, prompt_tiles - 1), 0)),
                   pl.BlockSpec((tm, D_MODEL), lambda i: (jnp.maximum(i - prompt_tiles, 0), 0))],
        out_shape=[jax.ShapeDtypeStruct((n_prompt, D_MODEL), F32),
                   jax.ShapeDtypeStruct((n - n_prompt, D_MODEL), F32)],
        compiler_params=_params("arbitrary"),
        name="residual_ln_split",
    )(x_packed, y_packed, g[None, :], b[None, :])


ROUTER_E0 = N_GROUPS


def _route(logits, wt_ref, cls_ref):
    lane = lax.broadcasted_iota(jnp.int32, logits.shape, 1).astype(F32)
    big = float(LANES)
    rmax = lambda a: jnp.max(a, axis=-1, keepdims=True)
    rmin = lambda a: jnp.min(a, axis=-1, keepdims=True)
    is_g = lane < N_GROUPS
    gl = jnp.where(is_g, logits, -jnp.inf)
    gmax = rmax(gl)
    gsel = rmin(jnp.where(gl == gmax, lane, big))
    gsum = jnp.sum(jnp.where(is_g, jnp.exp(logits - gmax), 0.0), axis=-1, keepdims=True)
    g_w = 1.0 / gsum
    e_lo = ROUTER_E0 + EXP_PER_GROUP * gsel
    in_grp = (lane >= e_lo) & (lane < e_lo + EXP_PER_GROUP)
    el = jnp.where(in_grp, logits, -jnp.inf)
    e1 = rmax(el)
    i1 = rmin(jnp.where(el == e1, lane, big))
    el2 = jnp.where(lane == i1, -jnp.inf, el)
    e2 = rmax(el2)
    i2 = rmin(jnp.where(el2 == e2, lane, big))
    t = jnp.exp(e2 - e1)
    p1 = 1.0 / (1.0 + t)
    w1 = p1 * g_w
    w2 = (t * p1) * g_w
    a1 = i1 - e_lo
    a2 = i2 - e_lo
    first_low = a1 < a2
    lo = jnp.where(first_low, a1, a2)
    hi = jnp.where(first_low, a2, a1)
    w_lo = jnp.where(first_low, w1, w2)
    w_hi = jnp.where(first_low, w2, w1)
    off = jnp.where(lo == 0.0, 0.0, jnp.where(lo == 1.0, 3.0, 5.0))
    cls = gsel * float(len(PAIRS)) + off + (hi - lo - 1.0)
    wt_ref[...] = jnp.where(lane < LANES // 2, w_lo, w_hi)
    cls_ref[...] = jnp.broadcast_to(cls, logits.shape).astype(jnp.int32)


def _expert_body(ea_ref, eb_ref, valid_ref, x_ref, wt_ref, wga_ref, wua_ref, wda_ref,
                 wgb_ref, wub_ref, wdb_ref, y_ref):
    i = pl.program_id(0)

    @pl.when(valid_ref[i] == 0)
    def _():
        y_ref[...] = jnp.zeros_like(y_ref)

    @pl.when(valid_ref[i] != 0)
    def _():
        x = _load_packed(x_ref, MOE_TILE).astype(BF16)
        dot = functools.partial(jnp.dot, preferred_element_type=F32)

        ga, ua = dot(x, wga_ref[...]), dot(x, wua_ref[...])
        gb, ub = dot(x, wgb_ref[...]), dot(x, wub_ref[...])
        ya = dot((jax.nn.silu(ga) * ua).astype(BF16), wda_ref[...])
        yb = dot((jax.nn.silu(gb) * ub).astype(BF16), wdb_ref[...])
        w_lo = wt_ref[:, 0:1]
        w_hi = wt_ref[:, LANES // 2:LANES // 2 + 1]
        _store_packed(y_ref, w_lo * ya + w_hi * yb)


def experts(xs, wts, tile_ea, tile_eb, tile_valid, w_gate, w_up, w_down):
    wspec = lambda shape, which: pl.BlockSpec(
        (None,) + shape, lambda i, ea, eb, va: ((ea, eb)[which][i], 0, 0))
    up = (D_MODEL, D_EXPERT)
    down = (D_EXPERT, D_MODEL)
    packed = pl.BlockSpec((MOE_TILE * PACK_ROWS, LANES), lambda i, ea, eb, va: (i, 0))
    grid_spec = pltpu.PrefetchScalarGridSpec(
        num_scalar_prefetch=3,
        grid=(MOE_TILES,),
        in_specs=[packed,
                  pl.BlockSpec((MOE_TILE, LANES), lambda i, ea, eb, va: (i, 0)),
                  wspec(up, 0), wspec(up, 0), wspec(down, 0),
                  wspec(up, 1), wspec(up, 1), wspec(down, 1)],
        out_specs=packed,
    )
    return pl.pallas_call(
        _expert_body,
        grid_spec=grid_spec,
        out_shape=jax.ShapeDtypeStruct((MOE_ROWS * PACK_ROWS, LANES), jnp.int32),
        compiler_params=_params("arbitrary"),
        name="experts",
    )(tile_ea, tile_eb, tile_valid, xs, wts, w_gate, w_up, w_down, w_gate, w_up, w_down)


def sc_dispatch(xp, wts, pos, fill_rows):
    n = pos.shape[0]
    n_fill = fill_rows.shape[0]
    info = plsc.get_sparse_core_info()
    n_workers = info.num_cores * info.num_subcores
    per_worker = n // n_workers
    steps = per_worker // SC_TOKENS
    fill_steps = n_fill // n_workers // SC_FILL
    assert steps * SC_TOKENS * n_workers == n and steps % 2 == 0
    assert fill_steps * SC_FILL * n_workers == n_fill
    mesh = plsc.VectorSubcoreMesh(core_axis_name="core", subcore_axis_name="subcore")
    dma = pltpu.SemaphoreType.DMA

    @functools.partial(
        pl.kernel,
        out_type=[jax.ShapeDtypeStruct((n + n_fill, PACK_ROWS, LANES), xp.dtype),
                  jax.ShapeDtypeStruct((n + n_fill, LANES), wts.dtype)],
        mesh=mesh,
        scratch_types=[pltpu.VMEM((steps, SC_TOKENS), jnp.int32),
                       pltpu.VMEM((fill_steps, SC_FILL), jnp.int32),
                       pltpu.VMEM((2, SC_TOKENS, PACK_ROWS, LANES), xp.dtype),
                       pltpu.VMEM((2, SC_TOKENS, LANES), wts.dtype),
                       dma((2,)), dma((2,)), dma((2,)), dma((2,))],
    )
    def scatter(x_hbm, w_hbm, pos_hbm, fill_hbm, zx_hbm, zw_hbm, xs_hbm, ws_hbm,
                pos_v, fill_v, xbuf, wbuf, read_x, read_w, write_x, write_w):
        worker = lax.axis_index("subcore") * info.num_cores + lax.axis_index("core")
        base = worker * per_worker
        pltpu.sync_copy(pos_hbm.at[worker], pos_v)
        pltpu.sync_copy(fill_hbm.at[worker], fill_v)

        def reads(s, b):
            rows = pl.ds(base + s * SC_TOKENS, SC_TOKENS)
            return (pltpu.make_async_copy(x_hbm.at[rows], xbuf.at[b], read_x.at[b]),
                    pltpu.make_async_copy(w_hbm.at[rows], wbuf.at[b], read_w.at[b]))

        def writes(s, b):
            return (pltpu.make_async_copy(xbuf.at[b], xs_hbm.at[pos_v.at[s]], write_x.at[b]),
                    pltpu.make_async_copy(wbuf.at[b], ws_hbm.at[pos_v.at[s]], write_w.at[b]))

        def start(copies):
            for c in copies:
                c.start()

        def wait(copies):
            for c in copies:
                c.wait()

        start(reads(0, 0))

        @pl.loop(0, steps, step=2)
        def _(s):
            wait(reads(s, 0))
            start(reads(s + 1, 1))
            start(writes(s, 0))
            wait(reads(s + 1, 1))
            wait(writes(s, 0))

            @pl.when(s + 2 < steps)
            def _():
                start(reads(s + 2, 0))

            start(writes(s + 1, 1))
            wait(writes(s + 1, 1))

        zx = xbuf.at[0, pl.ds(0, SC_FILL)]
        zw = wbuf.at[0, pl.ds(0, SC_FILL)]
        pltpu.sync_copy(zx_hbm, zx)
        pltpu.sync_copy(zw_hbm, zw)
        fills = [(pltpu.make_async_copy(zx, xs_hbm.at[fill_v.at[s]], write_x.at[0]),
                  pltpu.make_async_copy(zw, ws_hbm.at[fill_v.at[s]], write_w.at[0])) for s in range(fill_steps)]
        for f in fills:
            start(f)
        for f in fills:
            wait(f)

    xs, ws = scatter(xp.reshape(n, PACK_ROWS, LANES), wts,
                     pos.reshape(n_workers, steps, SC_TOKENS), fill_rows.reshape(n_workers, fill_steps, SC_FILL),
                     jnp.zeros((SC_FILL, PACK_ROWS, LANES), xp.dtype), jnp.zeros((SC_FILL, LANES), wts.dtype))
    return xs.reshape((n + n_fill) * PACK_ROWS, LANES), ws


def sc_gather_tokens(x, idx):
    n = x.shape[0] // PACK_ROWS
    p = idx.shape[0]
    info = plsc.get_sparse_core_info()
    n_workers = info.num_cores * info.num_subcores
    per_worker = p // n_workers
    steps = per_worker // SC_TOKENS
    assert steps * SC_TOKENS * n_workers == p and steps % 2 == 0
    mesh = plsc.VectorSubcoreMesh(core_axis_name="core", subcore_axis_name="subcore")
    dma = pltpu.SemaphoreType.DMA

    @functools.partial(
        pl.kernel,
        out_type=jax.ShapeDtypeStruct((p, PACK_ROWS, LANES), x.dtype),
        mesh=mesh,
        scratch_types=[pltpu.VMEM((steps, SC_TOKENS), jnp.int32),
                       pltpu.VMEM((2, SC_TOKENS, PACK_ROWS, LANES), x.dtype),
                       dma((2,)), dma((2,))],
    )
    def gather(x_hbm, i_hbm, o_hbm, idx_v, buf, read_sem, write_sem):
        worker = lax.axis_index("subcore") * info.num_cores + lax.axis_index("core")
        base = worker * per_worker
        pltpu.sync_copy(i_hbm.at[worker], idx_v)

        def read(s, b):
            return pltpu.make_async_copy(x_hbm.at[idx_v.at[s]], buf.at[b], read_sem.at[b])

        def write(s, b):
            return pltpu.make_async_copy(buf.at[b], o_hbm.at[pl.ds(base + s * SC_TOKENS, SC_TOKENS)], write_sem.at[b])

        read(0, 0).start()

        @pl.loop(0, steps, step=2)
        def _(s):
            read(s, 0).wait()
            read(s + 1, 1).start()
            write(s, 0).start()
            read(s + 1, 1).wait()
            write(s, 0).wait()

            @pl.when(s + 2 < steps)
            def _():
                read(s + 2, 0).start()

            write(s + 1, 1).start()
            write(s + 1, 1).wait()

    out = gather(x.reshape(n, PACK_ROWS, LANES), idx.reshape(n_workers, steps, SC_TOKENS))
    return out.reshape(p * PACK_ROWS, LANES)


_CLASS_EA = [g * EXP_PER_GROUP + lo for g in range(N_GROUPS) for lo, hi in PAIRS]
_CLASS_EB = [g * EXP_PER_GROUP + hi for g in range(N_GROUPS) for lo, hi in PAIRS]


def dispatch_plan(cls):
    onehot = (cls[:, None] == jnp.arange(N_CLASSES, dtype=jnp.int32)[None, :]).astype(jnp.int32)
    csum = jnp.cumsum(onehot, axis=0)
    rank = jnp.sum(csum * onehot, axis=1) - 1
    counts = csum[-1]
    tiles_per = (counts + MOE_TILE - 1) // MOE_TILE
    tile_end = jnp.cumsum(tiles_per)
    tile_start = tile_end - tiles_per
    pos = jnp.sum(onehot * tile_start[None, :], axis=1) * MOE_TILE + rank
    n = cls.shape[0]
    pad = tiles_per * MOE_TILE - counts
    pad_end = jnp.cumsum(pad)
    k = jnp.arange(MOE_ROWS - n, dtype=jnp.int32)
    k_cls = jnp.sum((k[:, None] >= pad_end[None, :]).astype(jnp.int32), axis=1)
    k_hot = (k_cls[:, None] == jnp.arange(N_CLASSES, dtype=jnp.int32)[None, :]).astype(jnp.int32)
    pad_row0 = tile_start * MOE_TILE + counts - (pad_end - pad)
    fill_rows = k + jnp.where(k_cls < N_CLASSES, jnp.sum(k_hot * pad_row0[None, :], axis=1),
                              tile_end[-1] * MOE_TILE - pad_end[-1])
    t = jnp.arange(MOE_TILES, dtype=jnp.int32)
    tile_cls = jnp.sum((t[:, None] >= tile_end[None, :]).astype(jnp.int32), axis=1)
    valid = (tile_cls < N_CLASSES).astype(jnp.int32)
    last_cls = jnp.max(jnp.where(counts > 0, jnp.arange(N_CLASSES, dtype=jnp.int32), 0))
    tile_cls = jnp.where(valid == 1, tile_cls, last_cls)
    ea = jnp.asarray(_CLASS_EA, jnp.int32)[tile_cls]
    eb = jnp.asarray(_CLASS_EB, jnp.int32)[tile_cls]
    return pos, fill_rows, ea, eb, valid


def moe(xp, wts, cls, e_w_gate, e_w_up, w_down, e_w_down, layer):
    pos, fill_rows, ea, eb, valid = dispatch_plan(cls[:, 0])
    xs, ws = sc_dispatch(xp, wts, pos, fill_rows)
    w_gate, w_up = cast_layer(e_w_gate, layer), cast_layer(e_w_up, layer)
    ys = experts(xs, ws, ea, eb, valid, w_gate, w_up, w_down)
    w_down_next = cast_layer(e_w_down, layer + 1) if layer + 1 < DEPTH else None
    return sc_gather_tokens(ys, pos), w_down_next


def _rows(a):
    return a.reshape(-1, a.shape[-1])


def kernel(x_prompt, x_sample, state_C, state_n, state_m, cache_k, cache_v, a_w_in, a_b_gate, a_norm, a_w_out, b_w_in, b_rel, b_w_out, ln1_g, ln1_b, ln2_g, ln2_b, r_w_group, r_b_group, r_w_expert, r_b_expert, e_w_gate, e_w_up, e_w_down):
    hk = A_HEADS * A_DK
    n_main = 2 * hk + 2 * D_MODEL
    gate_w = lambda layer: a_w_in[layer // 2, :, n_main:]
    x, xb, g = embed(_rows(x_prompt), _rows(x_sample), gate_w(0))
    w_down = cast_layer(e_w_down, 0)
    outs = {k: [] for k in ("Cp", "np", "mp", "kp", "vp", "Cs", "ns", "ms", "ks", "vs")}
    for layer in range(DEPTH):
        j = layer // 2
        if layer % 2 == 0:
            proj = matmul(xb, a_w_in, j, n_main, BF16)
            zc = jnp.zeros((BATCH, A_HEADS, A_DK, A_DV), F32)
            zn = jnp.zeros((BATCH, A_HEADS, A_DK), F32)
            zm = jnp.zeros((BATCH, A_HEADS), F32)
            h_p, c_p, n_p, m_p = mlstm(proj, g, a_b_gate[j], a_norm[j], zc, zn, zm,
                                       batch=BATCH, seq=SEQ, L=MLSTM_CHUNK, row0=0)
            h_s, c_s, n_s, m_s = mlstm(proj, g, a_b_gate[j], a_norm[j],
                                       state_C[j], state_n[j], state_m[j],
                                       batch=DEC_BATCH, seq=DEC_SEQ, L=DEC_SEQ, row0=N_PROMPT)
            outs["Cp"].append(c_p); outs["np"].append(n_p); outs["mp"].append(m_p)
            outs["Cs"].append(c_s); outs["ns"].append(n_s); outs["ms"].append(m_s)
            w_out = a_w_out[j]
        else:
            qkv = matmul(xb, b_w_in, j, 3 * D_MODEL, BF16)
            h_p, k_p, v_p = attn_prompt(qkv, b_rel[j])
            h_s, k_s, v_s = attn_sample(qkv, cache_k, cache_v, j, b_rel[j])
            heads = lambda a: a.reshape(a.shape[:2] + (B_HEADS, B_DH))
            outs["kp"].append(heads(k_p)); outs["vp"].append(heads(v_p))
            outs["ks"].append(heads(k_s)); outs["vs"].append(heads(v_s))
            w_out = b_w_out[j]
        xp, wts, cls = mix_ln_router(h_p, h_s, w_out.astype(BF16), x, ln1_g[layer], ln1_b[layer],
                                     r_w_group[layer], r_b_group[layer], r_w_expert[layer], r_b_expert[layer])
        y, w_down = moe(xp, wts, cls, e_w_gate, e_w_up, w_down, e_w_down, layer)
        if layer < DEPTH - 1:
            if (layer + 1) % 2 == 0:
                x, xb, g = residual_ln(xp, y, ln2_g[layer], ln2_b[layer], gate_w(layer + 1))
            else:
                x, xb = residual_ln(xp, y, ln2_g[layer], ln2_b[layer])
        else:
            y_p, y_s = residual_ln_split(xp, y, ln2_g[layer], ln2_b[layer], N_PROMPT)
    st = lambda k: jnp.stack(outs[k])
    return (y_p.reshape(BATCH, SEQ, D_MODEL), y_s.reshape(DEC_BATCH, DEC_SEQ, D_MODEL),
            st("Cp"), st("np"), st("mp"), st("kp"), st("vp"),
            st("Cs"), st("ns"), st("ms"), st("ks"), st("vs"))
```

```python
import functools

import jax
import jax.numpy as jnp
from jax import lax
from jax.experimental import pallas as pl
from jax.experimental.pallas import tpu as pltpu
from jax.experimental.pallas import tpu_sc as plsc

F32 = jnp.float32
BF16 = jnp.bfloat16

D_MODEL = 2048
BATCH = 8
SEQ = 4096
DEPTH = 4
DEC_BATCH = 32
DEC_SEQ = 32
PAST_LEN = 1024
CHUNK = 64
A_HEADS = 8
A_DK = 128
A_DV = D_MODEL // A_HEADS
GATE_CAP = 15.0
B_HEADS = 16
B_DH = D_MODEL // B_HEADS
PREV_CHUNKS = 8
REACH = PREV_CHUNKS * CHUNK
REL_CLIP = 256
N_GROUPS = 4
EXP_PER_GROUP = 4
N_EXPERTS = N_GROUPS * EXP_PER_GROUP
D_EXPERT = D_MODEL // 4
ALPHA = (2 * DEPTH) ** 0.25
LN_EPS = 1e-5
RMS_EPS = 1e-6

N_PROMPT = BATCH * SEQ
N_SAMPLE = DEC_BATCH * DEC_SEQ
N_TOK = N_PROMPT + N_SAMPLE

VMEM_LIMIT = 56 * 1024 * 1024
LANES = 128

PAIRS = ((0, 1), (0, 2), (0, 3), (1, 2), (1, 3), (2, 3))
N_CLASSES = N_GROUPS * len(PAIRS)
MOE_TILE = 256
MOE_TILES = -(-(N_TOK // MOE_TILE + N_CLASSES) // 32) * 32
MOE_ROWS = MOE_TILES * MOE_TILE

MLSTM_CHUNK = 128
ATT_TQ = 256
ATT_TK = ATT_TQ + REACH
NEG = -1e30

SC_TOKENS = 48
SC_FILL = 32


def _params(*sem):
    return pltpu.CompilerParams(dimension_semantics=sem, vmem_limit_bytes=VMEM_LIMIT)


def _mm_body(x_ref, w_ref, o_ref, wb):
    @pl.when(pl.program_id(1) == 0)
    def _():
        wb[...] = w_ref[...].astype(BF16)

    o_ref[...] = jnp.dot(x_ref[...], wb[...], preferred_element_type=F32).astype(o_ref.dtype)


def matmul(x, w_all, layer, m, out_dtype, tm=512, tn=1024):
    n, k = x.shape
    return pl.pallas_call(
        _mm_body,
        grid=(m // tn, n // tm),
        in_specs=[pl.BlockSpec((tm, k), lambda j, i: (i, 0)),
                  pl.BlockSpec((None, k, tn), lambda j, i: (layer, 0, j))],
        out_specs=pl.BlockSpec((tm, tn), lambda j, i: (i, j)),
        out_shape=jax.ShapeDtypeStruct((n, m), out_dtype),
        scratch_shapes=[pltpu.VMEM((k, tn), BF16)],
        compiler_params=_params("parallel", "arbitrary"),
        name="matmul",
    )(x, w_all)


def _cast_body(w_ref, *rest):
    o_ref = rest[-1]
    o_ref[...] = w_ref[...].astype(BF16)


def cast_layer(w_all, layer, after=None):
    e, a, b = w_all.shape[1:]
    per_step = 2
    in_specs = [pl.BlockSpec((None, per_step, a, b), lambda i: (layer, i, 0, 0))]
    args = [w_all]
    if after is not None:
        in_specs.append(pl.BlockSpec(memory_space=pl.ANY))
        args.append(after)
    return pl.pallas_call(
        _cast_body,
        grid=(e // per_step,),
        in_specs=in_specs,
        out_specs=pl.BlockSpec((per_step, a, b), lambda i: (i, 0, 0)),
        out_shape=jax.ShapeDtypeStruct((e, a, b), BF16),
        compiler_params=_params("parallel"),
        name="cast_layer",
    )(*args)


def _split2(x):
    hi = x.astype(BF16)
    lo = (x - hi.astype(F32)).astype(BF16)
    return hi, lo


def _dot3(xh, xl, wh, wl, dims):
    dg = functools.partial(lax.dot_general, dimension_numbers=dims, preferred_element_type=F32)
    return dg(xh, wh) + (dg(xh, wl) + dg(xl, wh))


_NN = (((1,), (0,)), ((), ()))
_NT = (((1,), (1,)), ((), ()))
_TN = (((0,), (0,)), ((), ()))


def _dot_hilo(x, w):
    t = x.shape[0]
    xh, xl = _split2(x)
    wh, wl = _split2(w)
    r = jnp.dot(jnp.concatenate([xh, xl], axis=0), jnp.concatenate([wh, wl], axis=1), preferred_element_type=F32)
    return r[:t, :LANES] + (r[:t, LANES:] + r[t:, :LANES]) + r[t:, LANES:]


def _log_sigmoid(x):
    return jnp.minimum(x, 0.0) - jnp.log(1.0 + jnp.exp(-jnp.abs(x)))


def _split3(x):
    a = x.astype(BF16)
    r = x - a.astype(F32)
    b = r.astype(BF16)
    c = (r - b.astype(F32)).astype(BF16)
    return a, b, c


A_STATE = A_DV + LANES


def _cummax_rows(x):
    n = x.shape[0]
    row = lax.broadcasted_iota(jnp.int32, x.shape, 0)
    k = 1
    while k < n:
        x = jnp.maximum(x, jnp.where(row >= k, pltpu.roll(x, k, axis=0), -jnp.inf))
        k *= 2
    return x


def _mlstm_body(q_ref, k_ref, v_ref, o_ref, g_ref, bias_ref, gain_ref, c0_ref, n0_ref, m0_ref,
                h_ref, c_out_ref, n_out_ref, m_ref, state, *, L):
    step = pl.program_id(1)

    @pl.when(step == 0)
    def _():
        state[:, :, :A_DV] = c0_ref[0]
        state[:, :, A_DV:] = n0_ref[0]
        m_ref[...] = m0_ref[...]

    dot = functools.partial(jnp.dot, preferred_element_type=F32)
    row = lax.broadcasted_iota(jnp.int32, (L, L), 0)
    col = lax.broadcasted_iota(jnp.int32, (L, L), 1)
    causal = col <= row
    tril = jnp.where(causal, 1.0, 0.0).astype(BF16)

    pre = GATE_CAP * jnp.tanh((g_ref[...] + bias_ref[...]) * (1.0 / GATE_CAP))
    ig = pre
    lf = _log_sigmoid(pltpu.roll(pre, LANES - A_HEADS, axis=1))
    a3, b3, c3 = _split3(lf)
    bt = dot(tril, a3) + (dot(tril, b3) + dot(tril, c3))
    a = ig - bt
    m_prev = m_ref[0]
    m_t = bt + jnp.maximum(m_prev, _cummax_rows(a))
    u = bt - m_t
    inter = jnp.exp(bt + m_prev - m_t)
    eminus = jnp.exp(-m_t)
    m_new = m_t[L - 1:L, :]
    bt_last = bt[L - 1:L, :]
    w_c = jnp.exp((bt_last - m_new) + a)
    decay = jnp.exp(bt_last + m_prev - m_new)
    m_ref[0] = m_new
    a_pad = a if L == LANES else jnp.concatenate([a, jnp.zeros((LANES - L, LANES), F32)], axis=0)
    a_t = jnp.transpose(a_pad)
    ie = jnp.concatenate([inter, eminus], axis=1).astype(BF16)
    sel_r = lax.broadcasted_iota(jnp.int32, (2 * LANES, 2 * LANES), 0)
    sel_c = lax.broadcasted_iota(jnp.int32, (2 * LANES, 2 * LANES), 1)
    same_half = (sel_r >= LANES) == (sel_c >= LANES)
    ones_l = jnp.ones((L, LANES), BF16)
    zeros_l = jnp.zeros((L, LANES), BF16)
    zeros_k = jnp.zeros((A_DK, LANES), BF16)
    mean_cols = jnp.full((A_DV, LANES), 1.0, BF16)

    scale = A_DK ** -0.5
    heads = range(A_HEADS)
    dk_sl = lambda h: slice(h * A_DK, (h + 1) * A_DK)
    dv_sl = lambda h: slice(h * A_DV, (h + 1) * A_DV)
    ie_b, qk = [], []
    for h in heads:
        sel = jnp.where(same_half & ((sel_r & (LANES - 1)) == h), 1.0, 0.0).astype(BF16)
        ie_b.append(dot(ie, sel))
        qk.append(lax.dot_general(q_ref[:, dk_sl(h)], k_ref[:, dk_sl(h)], _NT, preferred_element_type=F32))
    out = []
    for h in heads:
        d_mat = jnp.where(causal, jnp.exp(u[:, h:h + 1] + a_t[h:h + 1, :L]), 0.0)
        s = qk[h] * (d_mat * scale)
        q_i = (q_ref[:, dk_sl(h)].astype(F32) * ie_b[h][:, :LANES]).astype(BF16)
        c_b = state[h].astype(BF16)
        rhs = jnp.concatenate([
            jnp.concatenate([v_ref[:, dv_sl(h)], ones_l, zeros_l], axis=1),
            jnp.concatenate([c_b[:, :A_DV], zeros_k, c_b[:, A_DV:]], axis=1)], axis=0)
        out.append(dot(jnp.concatenate([s.astype(BF16), q_i], axis=1), rhs))
    hh, ms = [], []
    for h in heads:
        den = jnp.maximum(jnp.abs(out[h][:, A_DV:A_DV + LANES] + out[h][:, A_DV + LANES:]), ie_b[h][:, LANES:])
        r = 1.0 / den
        hh.append(out[h][:, :A_DV] * jnp.concatenate([r, r], axis=1))
        ms.append(dot((hh[h] * hh[h]).astype(BF16), mean_cols))
    upd = []
    for h in heads:
        rs = lax.rsqrt(ms[h] * (1.0 / A_DV) + RMS_EPS)
        og = o_ref[:, dv_sl(h)].astype(F32)
        y = hh[h] * jnp.concatenate([rs, rs], axis=1) * gain_ref[:, dv_sl(h)] * jax.nn.sigmoid(og)
        h_ref[:, dv_sl(h)] = y.astype(h_ref.dtype)
        wk = (k_ref[:, dk_sl(h)].astype(F32) * (w_c[:, h:h + 1] * scale)).astype(BF16)
        upd.append(lax.dot_general(wk, jnp.concatenate([v_ref[:, dv_sl(h)], ones_l], axis=1), _TN,
                                   preferred_element_type=F32))
    for h in heads:
        state[h] = decay[:, h:h + 1] * state[h] + upd[h]

    @pl.when(step == pl.num_programs(1) - 1)
    def _():
        c_out_ref[0] = state[:, :, :A_DV]
        n_out_ref[0] = state[:, :, A_DV:]


def mlstm(proj, g, b_gate, gain, c0, n0, m0, *, batch, seq, L, row0):
    nc = seq // L
    bias = jnp.pad(b_gate.astype(F32), (0, LANES - 2 * A_HEADS))
    row_blk = lambda b, c: row0 // L + b * nc + c
    st = lambda *s: pl.BlockSpec((1,) + s, lambda b, c: (b,) + (0,) * len(s))
    n0_lanes = jnp.broadcast_to(n0[..., None], n0.shape + (LANES,))
    m_lanes = jnp.pad(m0, ((0, 0), (0, LANES - A_HEADS)))[:, None, :]
    h, c, n, m = pl.pallas_call(
        functools.partial(_mlstm_body, L=L),
        grid=(batch, nc),
        in_specs=[pl.BlockSpec((L, A_HEADS * A_DK), lambda b, c: (row_blk(b, c), 0)),
                  pl.BlockSpec((L, A_HEADS * A_DK), lambda b, c: (row_blk(b, c), 1)),
                  pl.BlockSpec((L, D_MODEL), lambda b, c: (row_blk(b, c), 1)),
                  pl.BlockSpec((L, D_MODEL), lambda b, c: (row_blk(b, c), 2)),
                  pl.BlockSpec((L, LANES), lambda b, c: (row_blk(b, c), 0)),
                  pl.BlockSpec((1, LANES), lambda b, c: (0, 0)),
                  pl.BlockSpec((1, D_MODEL), lambda b, c: (0, 0)),
                  st(A_HEADS, A_DK, A_DV), st(A_HEADS, A_DK, LANES), st(1, LANES)],
        out_specs=[pl.BlockSpec((L, D_MODEL), lambda b, c: (b * nc + c, 0)),
                   st(A_HEADS, A_DK, A_DV), st(A_HEADS, A_DK, LANES), st(1, LANES)],
        out_shape=[jax.ShapeDtypeStruct((batch * seq, D_MODEL), BF16),
                   jax.ShapeDtypeStruct((batch, A_HEADS, A_DK, A_DV), F32),
                   jax.ShapeDtypeStruct((batch, A_HEADS, A_DK, LANES), F32),
                   jax.ShapeDtypeStruct((batch, 1, LANES), F32)],
        scratch_shapes=[pltpu.VMEM((A_HEADS, A_DK, A_STATE), F32)],
        compiler_params=_params("parallel", "arbitrary"),
        name="mlstm",
    )(proj, proj, proj, proj, g, bias[None, :], gain.astype(F32)[None, :], c0, n0_lanes, m_lanes)
    return h, c, n[..., 0], m[:, 0, :A_HEADS]


LOG2E = 1.4426950408889634
Q_SCALE = B_DH ** -0.5 * LOG2E


def _scale_q(q):
    return (q.astype(F32) * Q_SCALE).astype(BF16)


def _attn_prompt_body(q_ref, k_ref, v_ref, bias_ref, o_ref, kt_ref, vt_ref, qs, kpad, vpad):
    kt_ref[0] = k_ref[SEQ - REACH:, :].astype(F32)
    vt_ref[0] = v_ref[SEQ - REACH:, :].astype(F32)
    qs[...] = _scale_q(q_ref[...])
    kpad[0:REACH, :] = jnp.zeros((REACH, B_DH), BF16)
    kpad[REACH:, :] = k_ref[...]
    vpad[0:REACH, 0:B_DH] = jnp.zeros((REACH, B_DH), BF16)
    vpad[REACH:, 0:B_DH] = v_ref[...]
    vpad[:, B_DH:] = jnp.ones((SEQ + REACH, B_DH), BF16)

    def scores(t):
        r0 = t * ATT_TQ
        kb = kpad[r0:r0 + ATT_TK, :]
        s = lax.dot_general(qs[r0:r0 + ATT_TQ, :], kb, _NT, preferred_element_type=F32) + bias_ref[0]
        if r0 < REACH:
            j = lax.broadcasted_iota(jnp.int32, (ATT_TQ, ATT_TK), 1)
            s = jnp.where(j + r0 >= REACH, s, NEG)
        return s

    def finish(t, s):
        r0 = t * ATT_TQ
        p = jnp.exp2(s - jnp.max(s, axis=-1, keepdims=True)).astype(BF16)
        acc = jnp.dot(p, vpad[r0:r0 + ATT_TK, :], preferred_element_type=F32)
        o_ref[r0:r0 + ATT_TQ, :] = (acc[:, :B_DH] / acc[:, B_DH:]).astype(o_ref.dtype)

    n_tiles = SEQ // ATT_TQ
    s_next = scores(0)
    for t in range(n_tiles):
        s_cur = s_next
        if t + 1 < n_tiles:
            s_next = scores(t + 1)
        finish(t, s_cur)


def band_bias(rel_table, nq, nk):
    w = nq + nk
    rel = jnp.clip(REACH + nq - 1 - jnp.arange(w), -REL_CLIP, REL_CLIP) + REL_CLIP
    r = rel_table.astype(F32)[:, rel]
    heads = r.shape[0]
    skew = jnp.broadcast_to(r[:, None, :], (heads, nq, w)).reshape(heads, nq * w)
    skew = skew[:, :nq * (w - 1)].reshape(heads, nq, w - 1)
    bias = skew[:, :, nq - 1:nq - 1 + nk]
    i = jnp.arange(nq)[:, None]
    j = jnp.arange(nk)[None, :]
    dc = j // CHUNK - i // CHUNK
    allowed = (dc >= 0) & (dc <= PREV_CHUNKS)
    return jnp.where(allowed[None], bias * LOG2E, NEG)


def attn_prompt(qkv, rel_table):
    bias = band_bias(rel_table, ATT_TQ, ATT_TK)
    tail = pl.BlockSpec((1, REACH, B_DH), lambda b, h: (b, 0, h))
    return pl.pallas_call(
        _attn_prompt_body,
        grid=(BATCH, B_HEADS),
        in_specs=[pl.BlockSpec((SEQ, B_DH), lambda b, h: (b, h)),
                  pl.BlockSpec((SEQ, B_DH), lambda b, h: (b, B_HEADS + h)),
                  pl.BlockSpec((SEQ, B_DH), lambda b, h: (b, 2 * B_HEADS + h)),
                  pl.BlockSpec((1, ATT_TQ, ATT_TK), lambda b, h: (h, 0, 0))],
        out_specs=[pl.BlockSpec((SEQ, B_DH), lambda b, h: (b, h)), tail, tail],
        out_shape=[jax.ShapeDtypeStruct((N_PROMPT, D_MODEL), BF16),
                   jax.ShapeDtypeStruct((BATCH, REACH, D_MODEL), F32),
                   jax.ShapeDtypeStruct((BATCH, REACH, D_MODEL), F32)],
        scratch_shapes=[pltpu.VMEM((SEQ, B_DH), BF16),
                        pltpu.VMEM((SEQ + REACH, B_DH), BF16),
                        pltpu.VMEM((SEQ + REACH, 2 * B_DH), BF16)],
        compiler_params=_params("parallel", "parallel"),
        name="attn_prompt",
    )(qkv, qkv, qkv, bias)


def _attn_sample_body(q_ref, kn_ref, vn_ref, ck_ref, cv_ref, bias_c_ref, bias_n_ref, o_ref, ks_ref, vs_ref):
    ks_ref[0] = kn_ref[...].astype(F32)
    vs_ref[0] = vn_ref[...].astype(F32)
    for h in range(B_HEADS):
        sl = slice(h * B_DH, (h + 1) * B_DH)
        q = _scale_q(q_ref[:, sl])
        kc = ck_ref[pl.ds(h, REACH, stride=B_HEADS), :].astype(BF16)
        vc = cv_ref[pl.ds(h, REACH, stride=B_HEADS), :].astype(BF16)
        s_c = lax.dot_general(q, kc, _NT, preferred_element_type=F32) + bias_c_ref[h]
        s_n = lax.dot_general(q, kn_ref[:, sl], _NT, preferred_element_type=F32) + bias_n_ref[h]
        m = jnp.maximum(jnp.max(s_c, axis=-1, keepdims=True), jnp.max(s_n, axis=-1, keepdims=True))
        p_c = jnp.exp2(s_c - m)
        p_n = jnp.exp2(s_n - m)
        l = jnp.sum(p_c, axis=-1, keepdims=True) + jnp.sum(p_n, axis=-1, keepdims=True)
        dot = functools.partial(jnp.dot, preferred_element_type=F32)
        o_ref[:, sl] = ((dot(p_c.astype(BF16), vc) + dot(p_n.astype(BF16), vn_ref[:, sl])) / l).astype(o_ref.dtype)


def attn_sample(qkv, cache_k, cache_v, layer, rel_table):
    bias = band_bias(rel_table, DEC_SEQ, REACH + DEC_SEQ)
    cache_rows = REACH * B_HEADS
    ck = cache_k.reshape(cache_k.shape[0], DEC_BATCH, cache_rows, B_DH)
    cv = cache_v.reshape(cache_v.shape[0], DEC_BATCH, cache_rows, B_DH)
    row = lambda c: pl.BlockSpec((DEC_SEQ, D_MODEL), lambda b: (N_PROMPT // DEC_SEQ + b, c))
    cache = pl.BlockSpec((None, None, cache_rows, B_DH), lambda b: (layer, b, 0, 0))
    new = pl.BlockSpec((1, DEC_SEQ, D_MODEL), lambda b: (b, 0, 0))
    return pl.pallas_call(
        _attn_sample_body,
        grid=(DEC_BATCH,),
        in_specs=[row(0), row(1), row(2), cache, cache,
                  pl.BlockSpec((B_HEADS, DEC_SEQ, REACH), lambda b: (0, 0, 0)),
                  pl.BlockSpec((B_HEADS, DEC_SEQ, DEC_SEQ), lambda b: (0, 0, 0))],
        out_specs=[pl.BlockSpec((DEC_SEQ, D_MODEL), lambda b: (b, 0)), new, new],
        out_shape=[jax.ShapeDtypeStruct((N_SAMPLE, D_MODEL), BF16),
                   jax.ShapeDtypeStruct((DEC_BATCH, DEC_SEQ, D_MODEL), F32),
                   jax.ShapeDtypeStruct((DEC_BATCH, DEC_SEQ, D_MODEL), F32)],
        compiler_params=_params("parallel"),
        name="attn_sample",
    )(qkv, qkv, qkv, ck, cv, bias[:, :, :REACH], bias[:, :, REACH:])


def _ln(z, g, b):
    mu = jnp.mean(z, axis=-1, keepdims=True)
    zc = z - mu
    var = jnp.mean(zc * zc, axis=-1, keepdims=True)
    return zc * lax.rsqrt(var + LN_EPS) * g + b


PACK_ROWS = D_MODEL // 2 // LANES
HI_MASK = -65536


def _store_packed(ref, v, token0=0):
    t = v.shape[0]
    half = D_MODEL // 2
    hi = lax.bitcast_convert_type(v[:, :half].astype(BF16).astype(F32), jnp.int32)
    lo = lax.bitcast_convert_type(v[:, half:].astype(BF16).astype(F32), jnp.int32)
    words = (hi & HI_MASK) | lax.shift_right_logical(lo, 16)
    for a in range(PACK_ROWS):
        ref[pl.ds(token0 * PACK_ROWS + a, t, stride=PACK_ROWS), :] = words[:, a * LANES:(a + 1) * LANES]


def _load_packed(ref, t):
    words = jnp.concatenate([ref[pl.ds(a, t, stride=PACK_ROWS), :] for a in range(PACK_ROWS)], axis=1)
    hi = lax.bitcast_convert_type(words & HI_MASK, F32)
    lo = lax.bitcast_convert_type(lax.shift_left(words, 16), F32)
    return jnp.concatenate([hi, lo], axis=1)


def _mix_ln_router_body(hp_ref, hs_ref, w_ref, x_ref, g_ref, b_ref, rw_ref, rb_ref,
                        xp_ref, wt_ref, cls_ref, y_even, y_odd, *, prompt_tiles, n_tiles):
    i = pl.program_id(0)
    from_prompt = jnp.minimum(i, n_tiles - 1) < prompt_tiles

    @pl.when(i == 0)
    def _():
        y_odd[...] = jnp.zeros_like(y_odd)

    half = D_MODEL // 2

    def run(y_new, y_old):
        h = jnp.where(from_prompt, hp_ref[...], hs_ref[...])
        y_new[:, :half] = jnp.dot(h, w_ref[:, :half], preferred_element_type=F32)
        xn = _ln(ALPHA * x_ref[...] + y_old[...], g_ref[...], b_ref[...])
        logits = _dot_hilo(xn, rw_ref[...]) + rb_ref[...]
        y_new[:, half:] = jnp.dot(h, w_ref[:, half:], preferred_element_type=F32)
        _store_packed(xp_ref, xn)
        _route(logits, wt_ref, cls_ref)

    @pl.when(i % 2 == 0)
    def _():
        run(y_even, y_odd)

    @pl.when(i % 2 == 1)
    def _():
        run(y_odd, y_even)


def mix_ln_router(h_p, h_s, w, x, g, b, w_group, b_group, w_expert, b_expert, tm=256):
    n = x.shape[0]
    n_tiles = n // tm
    prompt_tiles = h_p.shape[0] // tm
    sample_tiles = h_s.shape[0] // tm
    assert prompt_tiles * tm == h_p.shape[0] and (prompt_tiles + sample_tiles) * tm == n
    mm_tile = lambda i: jnp.minimum(i, n_tiles - 1)
    ep_tile = lambda i: jnp.maximum(i - 1, 0)
    hp_spec = pl.BlockSpec((tm, D_MODEL), lambda i: (jnp.minimum(mm_tile(i), prompt_tiles - 1), 0))
    hs_spec = pl.BlockSpec((tm, D_MODEL), lambda i: (jnp.maximum(mm_tile(i) - prompt_tiles, 0), 0))
    rw = jnp.pad(jnp.concatenate([w_group, w_expert], axis=1), ((0, 0), (0, LANES - N_GROUPS - N_EXPERTS)))
    rb = jnp.pad(jnp.concatenate([b_group, b_expert]).astype(F32), (0, LANES - N_GROUPS - N_EXPERTS))
    row = pl.BlockSpec((tm, D_MODEL), lambda i: (ep_tile(i), 0))
    vec = pl.BlockSpec((1, D_MODEL), lambda i: (0, 0))
    lane_row = pl.BlockSpec((tm, LANES), lambda i: (ep_tile(i), 0))
    return pl.pallas_call(
        functools.partial(_mix_ln_router_body, prompt_tiles=prompt_tiles, n_tiles=n_tiles),
        grid=(n_tiles + 1,),
        in_specs=[hp_spec, hs_spec, pl.BlockSpec((D_MODEL, D_MODEL), lambda i: (0, 0)), row, vec, vec,
                  pl.BlockSpec((D_MODEL, LANES), lambda i: (0, 0)), pl.BlockSpec((1, LANES), lambda i: (0, 0))],
        out_specs=[pl.BlockSpec((tm * PACK_ROWS, LANES), lambda i: (ep_tile(i), 0)), lane_row, lane_row],
        out_shape=[jax.ShapeDtypeStruct((n * PACK_ROWS, LANES), jnp.int32),
                   jax.ShapeDtypeStruct((n, LANES), F32), jax.ShapeDtypeStruct((n, LANES), jnp.int32)],
        scratch_shapes=[pltpu.VMEM((tm, D_MODEL), F32), pltpu.VMEM((tm, D_MODEL), F32)],
        compiler_params=_params("arbitrary"),
        name="mix_ln_router",
    )(h_p, h_s, w, x, g[None, :], b[None, :], rw, rb[None, :])


def _add_ln_body(x_ref, y_ref, g_ref, b_ref, *rest):
    t = rest[-1].shape[0]
    xn = _ln(ALPHA * _load_packed(x_ref, t) + _load_packed(y_ref, t), g_ref[...], b_ref[...])
    if len(rest) == 2:
        xo_ref, xbo_ref = rest
    else:
        wg_ref, xo_ref, xbo_ref, go_ref = rest
        go_ref[...] = _dot_hilo(xn, wg_ref[...])
    xo_ref[...] = xn
    xbo_ref[...] = xn.astype(BF16)


def residual_ln(x_packed, y_packed, g, b, w_gate=None, tm=512):
    n = x_packed.shape[0] // PACK_ROWS
    row = pl.BlockSpec((tm, D_MODEL), lambda i: (i, 0))
    vec = pl.BlockSpec((1, D_MODEL), lambda i: (0, 0))
    packed = pl.BlockSpec((tm * PACK_ROWS, LANES), lambda i: (i, 0))
    in_specs = [packed, packed, vec, vec]
    out_specs = [row, row]
    out_shape = [jax.ShapeDtypeStruct((n, D_MODEL), F32), jax.ShapeDtypeStruct((n, D_MODEL), BF16)]
    args = [x_packed, y_packed, g[None, :], b[None, :]]
    if w_gate is not None:
        in_specs.append(pl.BlockSpec((D_MODEL, LANES), lambda i: (0, 0)))
        out_specs.append(pl.BlockSpec((tm, LANES), lambda i: (i, 0)))
        out_shape.append(jax.ShapeDtypeStruct((n, LANES), F32))
        args.append(jnp.pad(w_gate, ((0, 0), (0, LANES - w_gate.shape[1]))))
    return pl.pallas_call(
        _add_ln_body,
        grid=(n // tm,),
        in_specs=in_specs,
        out_specs=out_specs,
        out_shape=out_shape,
        compiler_params=_params("parallel"),
        name="residual_ln",
    )(*args)


def _embed_body(xp_ref, xs_ref, wg_ref, xo_ref, xbo_ref, go_ref, *, prompt_tiles):
    x = jnp.where(pl.program_id(0) < prompt_tiles, xp_ref[...], xs_ref[...])
    xo_ref[...] = x
    xbo_ref[...] = x.astype(BF16)
    go_ref[...] = _dot_hilo(x, wg_ref[...])


def embed(x_p, x_s, w_gate, tm=512):
    n = x_p.shape[0] + x_s.shape[0]
    prompt_tiles = x_p.shape[0] // tm
    assert prompt_tiles * tm == x_p.shape[0] and n % tm == 0
    row = pl.BlockSpec((tm, D_MODEL), lambda i: (i, 0))
    wg = jnp.pad(w_gate, ((0, 0), (0, LANES - w_gate.shape[1])))
    return pl.pallas_call(
        functools.partial(_embed_body, prompt_tiles=prompt_tiles),
        grid=(n // tm,),
        in_specs=[pl.BlockSpec((tm, D_MODEL), lambda i: (jnp.minimum(i, prompt_tiles - 1), 0)),
                  pl.BlockSpec((tm, D_MODEL), lambda i: (jnp.maximum(i - prompt_tiles, 0), 0)),
                  pl.BlockSpec((D_MODEL, LANES), lambda i: (0, 0))],
        out_specs=[row, row, pl.BlockSpec((tm, LANES), lambda i: (i, 0))],
        out_shape=[jax.ShapeDtypeStruct((n, D_MODEL), F32), jax.ShapeDtypeStruct((n, D_MODEL), BF16),
                   jax.ShapeDtypeStruct((n, LANES), F32)],
        compiler_params=_params("parallel"),
        name="embed",
    )(x_p, x_s, wg)


def _add_ln_split_body(x_ref, y_ref, g_ref, b_ref, op_ref, os_ref, *, prompt_tiles):
    t = op_ref.shape[0]
    xn = _ln(ALPHA * _load_packed(x_ref, t) + _load_packed(y_ref, t), g_ref[...], b_ref[...])
    i = pl.program_id(0)

    @pl.when(i < prompt_tiles)
    def _():
        op_ref[...] = xn

    @pl.when(i >= prompt_tiles)
    def _():
        os_ref[...] = xn


def residual_ln_split(x_packed, y_packed, g, b, n_prompt, tm=512):
    n = x_packed.shape[0] // PACK_ROWS
    prompt_tiles = n_prompt // tm
    assert prompt_tiles * tm == n_prompt and n % tm == 0
    vec = pl.BlockSpec((1, D_MODEL), lambda i: (0, 0))
    packed = pl.BlockSpec((tm * PACK_ROWS, LANES), lambda i: (i, 0))
    return pl.pallas_call(
        functools.partial(_add_ln_split_body, prompt_tiles=prompt_tiles),
        grid=(n // tm,),
        in_specs=[packed, packed, vec, vec],
        out_specs=[pl.BlockSpec((tm, D_MODEL), lambda i: (jnp.minimum(i, prompt_tiles - 1), 0)),
                   pl.BlockSpec((tm, D_MODEL), lambda i: (jnp.maximum(i - prompt_tiles, 0), 0))],
        out_shape=[jax.ShapeDtypeStruct((n_prompt, D_MODEL), F32),
                   jax.ShapeDtypeStruct((n - n_prompt, D_MODEL), F32)],
        compiler_params=_params("arbitrary"),
        name="residual_ln_split",
    )(x_packed, y_packed, g[None, :], b[None, :])


ROUTER_E0 = N_GROUPS


def _route(logits, wt_ref, cls_ref):
    lane = lax.broadcasted_iota(jnp.int32, logits.shape, 1).astype(F32)
    big = float(LANES)
    rmax = lambda a: jnp.max(a, axis=-1, keepdims=True)
    rmin = lambda a: jnp.min(a, axis=-1, keepdims=True)
    is_g = lane < N_GROUPS
    gl = jnp.where(is_g, logits, -jnp.inf)
    gmax = rmax(gl)
    gsel = rmin(jnp.where(gl == gmax, lane, big))
    gsum = jnp.sum(jnp.where(is_g, jnp.exp(logits - gmax), 0.0), axis=-1, keepdims=True)
    g_w = 1.0 / gsum
    e_lo = ROUTER_E0 + EXP_PER_GROUP * gsel
    in_grp = (lane >= e_lo) & (lane < e_lo + EXP_PER_GROUP)
    el = jnp.where(in_grp, logits, -jnp.inf)
    e1 = rmax(el)
    i1 = rmin(jnp.where(el == e1, lane, big))
    el2 = jnp.where(lane == i1, -jnp.inf, el)
    e2 = rmax(el2)
    i2 = rmin(jnp.where(el2 == e2, lane, big))
    t = jnp.exp(e2 - e1)
    p1 = 1.0 / (1.0 + t)
    w1 = p1 * g_w
    w2 = (t * p1) * g_w
    a1 = i1 - e_lo
    a2 = i2 - e_lo
    first_low = a1 < a2
    lo = jnp.where(first_low, a1, a2)
    hi = jnp.where(first_low, a2, a1)
    w_lo = jnp.where(first_low, w1, w2)
    w_hi = jnp.where(first_low, w2, w1)
    off = jnp.where(lo == 0.0, 0.0, jnp.where(lo == 1.0, 3.0, 5.0))
    cls = gsel * float(len(PAIRS)) + off + (hi - lo - 1.0)
    wt_ref[...] = jnp.where(lane < LANES // 2, w_lo, w_hi)
    cls_ref[...] = jnp.broadcast_to(cls, logits.shape).astype(jnp.int32)


def _expert_body(ea_ref, eb_ref, valid_ref, x_ref, wt_ref, wga_ref, wua_ref, wda_ref,
                 wgb_ref, wub_ref, wdb_ref, y_ref):
    i = pl.program_id(0)

    @pl.when(valid_ref[i] == 0)
    def _():
        y_ref[...] = jnp.zeros_like(y_ref)

    @pl.when(valid_ref[i] != 0)
    def _():
        x = _load_packed(x_ref, MOE_TILE).astype(BF16)
        dot = functools.partial(jnp.dot, preferred_element_type=F32)

        ga, ua = dot(x, wga_ref[...]), dot(x, wua_ref[...])
        gb, ub = dot(x, wgb_ref[...]), dot(x, wub_ref[...])
        ya = dot((jax.nn.silu(ga) * ua).astype(BF16), wda_ref[...])
        yb = dot((jax.nn.silu(gb) * ub).astype(BF16), wdb_ref[...])
        w_lo = wt_ref[:, 0:1]
        w_hi = wt_ref[:, LANES // 2:LANES // 2 + 1]
        _store_packed(y_ref, w_lo * ya + w_hi * yb)


def experts(xs, wts, tile_ea, tile_eb, tile_valid, w_gate, w_up, w_down):
    wspec = lambda shape, which: pl.BlockSpec(
        (None,) + shape, lambda i, ea, eb, va: ((ea, eb)[which][i], 0, 0))
    up = (D_MODEL, D_EXPERT)
    down = (D_EXPERT, D_MODEL)
    packed = pl.BlockSpec((MOE_TILE * PACK_ROWS, LANES), lambda i, ea, eb, va: (i, 0))
    grid_spec = pltpu.PrefetchScalarGridSpec(
        num_scalar_prefetch=3,
        grid=(MOE_TILES,),
        in_specs=[packed,
                  pl.BlockSpec((MOE_TILE, LANES), lambda i, ea, eb, va: (i, 0)),
                  wspec(up, 0), wspec(up, 0), wspec(down, 0),
                  wspec(up, 1), wspec(up, 1), wspec(down, 1)],
        out_specs=packed,
    )
    return pl.pallas_call(
        _expert_body,
        grid_spec=grid_spec,
        out_shape=jax.ShapeDtypeStruct((MOE_ROWS * PACK_ROWS, LANES), jnp.int32),
        compiler_params=_params("arbitrary"),
        name="experts",
    )(tile_ea, tile_eb, tile_valid, xs, wts, w_gate, w_up, w_down, w_gate, w_up, w_down)


def sc_dispatch(xp, wts, pos, fill_rows):
    n = pos.shape[0]
    n_fill = fill_rows.shape[0]
    info = plsc.get_sparse_core_info()
    n_workers = info.num_cores * info.num_subcores
    per_worker = n // n_workers
    steps = per_worker // SC_TOKENS
    fill_steps = n_fill // n_workers // SC_FILL
    assert steps * SC_TOKENS * n_workers == n and steps % 2 == 0
    assert fill_steps * SC_FILL * n_workers == n_fill
    mesh = plsc.VectorSubcoreMesh(core_axis_name="core", subcore_axis_name="subcore")
    dma = pltpu.SemaphoreType.DMA

    @functools.partial(
        pl.kernel,
        out_type=[jax.ShapeDtypeStruct((n + n_fill, PACK_ROWS, LANES), xp.dtype),
                  jax.ShapeDtypeStruct((n + n_fill, LANES), wts.dtype)],
        mesh=mesh,
        scratch_types=[pltpu.VMEM((steps, SC_TOKENS), jnp.int32),
                       pltpu.VMEM((fill_steps, SC_FILL), jnp.int32),
                       pltpu.VMEM((2, SC_TOKENS, PACK_ROWS, LANES), xp.dtype),
                       pltpu.VMEM((2, SC_TOKENS, LANES), wts.dtype),
                       dma((2,)), dma((2,)), dma((2,)), dma((2,))],
    )
    def scatter(x_hbm, w_hbm, pos_hbm, fill_hbm, zx_hbm, zw_hbm, xs_hbm, ws_hbm,
                pos_v, fill_v, xbuf, wbuf, read_x, read_w, write_x, write_w):
        worker = lax.axis_index("subcore") * info.num_cores + lax.axis_index("core")
        base = worker * per_worker
        pltpu.sync_copy(pos_hbm.at[worker], pos_v)
        pltpu.sync_copy(fill_hbm.at[worker], fill_v)

        def reads(s, b):
            rows = pl.ds(base + s * SC_TOKENS, SC_TOKENS)
            return (pltpu.make_async_copy(x_hbm.at[rows], xbuf.at[b], read_x.at[b]),
                    pltpu.make_async_copy(w_hbm.at[rows], wbuf.at[b], read_w.at[b]))

        def writes(s, b):
            return (pltpu.make_async_copy(xbuf.at[b], xs_hbm.at[pos_v.at[s]], write_x.at[b]),
                    pltpu.make_async_copy(wbuf.at[b], ws_hbm.at[pos_v.at[s]], write_w.at[b]))

        def start(copies):
            for c in copies:
                c.start()

        def wait(copies):
            for c in copies:
                c.wait()

        start(reads(0, 0))

        @pl.loop(0, steps, step=2)
        def _(s):
            wait(reads(s, 0))
            start(reads(s + 1, 1))
            start(writes(s, 0))
            wait(reads(s + 1, 1))
            wait(writes(s, 0))

            @pl.when(s + 2 < steps)
            def _():
                start(reads(s + 2, 0))

            start(writes(s + 1, 1))
            wait(writes(s + 1, 1))

        zx = xbuf.at[0, pl.ds(0, SC_FILL)]
        zw = wbuf.at[0, pl.ds(0, SC_FILL)]
        pltpu.sync_copy(zx_hbm, zx)
        pltpu.sync_copy(zw_hbm, zw)
        fills = [(pltpu.make_async_copy(zx, xs_hbm.at[fill_v.at[s]], write_x.at[0]),
                  pltpu.make_async_copy(zw, ws_hbm.at[fill_v.at[s]], write_w.at[0])) for s in range(fill_steps)]
        for f in fills:
            start(f)
        for f in fills:
            wait(f)

    xs, ws = scatter(xp.reshape(n, PACK_ROWS, LANES), wts,
                     pos.reshape(n_workers, steps, SC_TOKENS), fill_rows.reshape(n_workers, fill_steps, SC_FILL),
                     jnp.zeros((SC_FILL, PACK_ROWS, LANES), xp.dtype), jnp.zeros((SC_FILL, LANES), wts.dtype))
    return xs.reshape((n + n_fill) * PACK_ROWS, LANES), ws


def sc_gather_tokens(x, idx):
    n = x.shape[0] // PACK_ROWS
    p = idx.shape[0]
    info = plsc.get_sparse_core_info()
    n_workers = info.num_cores * info.num_subcores
    per_worker = p // n_workers
    steps = per_worker // SC_TOKENS
    assert steps * SC_TOKENS * n_workers == p and steps % 2 == 0
    mesh = plsc.VectorSubcoreMesh(core_axis_name="core", subcore_axis_name="subcore")
    dma = pltpu.SemaphoreType.DMA

    @functools.partial(
        pl.kernel,
        out_type=jax.ShapeDtypeStruct((p, PACK_ROWS, LANES), x.dtype),
        mesh=mesh,
        scratch_types=[pltpu.VMEM((steps, SC_TOKENS), jnp.int32),
                       pltpu.VMEM((2, SC_TOKENS, PACK_ROWS, LANES), x.dtype),
                       dma((2,)), dma((2,))],
    )
    def gather(x_hbm, i_hbm, o_hbm, idx_v, buf, read_sem, write_sem):
        worker = lax.axis_index("subcore") * info.num_cores + lax.axis_index("core")
        base = worker * per_worker
        pltpu.sync_copy(i_hbm.at[worker], idx_v)

        def read(s, b):
            return pltpu.make_async_copy(x_hbm.at[idx_v.at[s]], buf.at[b], read_sem.at[b])

        def write(s, b):
            return pltpu.make_async_copy(buf.at[b], o_hbm.at[pl.ds(base + s * SC_TOKENS, SC_TOKENS)], write_sem.at[b])

        read(0, 0).start()

        @pl.loop(0, steps, step=2)
        def _(s):
            read(s, 0).wait()
            read(s + 1, 1).start()
            write(s, 0).start()
            read(s + 1, 1).wait()
            write(s, 0).wait()

            @pl.when(s + 2 < steps)
            def _():
                read(s + 2, 0).start()

            write(s + 1, 1).start()
            write(s + 1, 1).wait()

    out = gather(x.reshape(n, PACK_ROWS, LANES), idx.reshape(n_workers, steps, SC_TOKENS))
    return out.reshape(p * PACK_ROWS, LANES)


_CLASS_EA = [g * EXP_PER_GROUP + lo for g in range(N_GROUPS) for lo, hi in PAIRS]
_CLASS_EB = [g * EXP_PER_GROUP + hi for g in range(N_GROUPS) for lo, hi in PAIRS]


def dispatch_plan(cls):
    onehot = (cls[:, None] == jnp.arange(N_CLASSES, dtype=jnp.int32)[None, :]).astype(jnp.int32)
    csum = jnp.cumsum(onehot, axis=0)
    rank = jnp.sum(csum * onehot, axis=1) - 1
    counts = csum[-1]
    tiles_per = (counts + MOE_TILE - 1) // MOE_TILE
    tile_end = jnp.cumsum(tiles_per)
    tile_start = tile_end - tiles_per
    pos = jnp.sum(onehot * tile_start[None, :], axis=1) * MOE_TILE + rank
    n = cls.shape[0]
    pad = tiles_per * MOE_TILE - counts
    pad_end = jnp.cumsum(pad)
    k = jnp.arange(MOE_ROWS - n, dtype=jnp.int32)
    k_cls = jnp.sum((k[:, None] >= pad_end[None, :]).astype(jnp.int32), axis=1)
    k_hot = (k_cls[:, None] == jnp.arange(N_CLASSES, dtype=jnp.int32)[None, :]).astype(jnp.int32)
    pad_row0 = tile_start * MOE_TILE + counts - (pad_end - pad)
    fill_rows = k + jnp.where(k_cls < N_CLASSES, jnp.sum(k_hot * pad_row0[None, :], axis=1),
                              tile_end[-1] * MOE_TILE - pad_end[-1])
    t = jnp.arange(MOE_TILES, dtype=jnp.int32)
    tile_cls = jnp.sum((t[:, None] >= tile_end[None, :]).astype(jnp.int32), axis=1)
    valid = (tile_cls < N_CLASSES).astype(jnp.int32)
    last_cls = jnp.max(jnp.where(counts > 0, jnp.arange(N_CLASSES, dtype=jnp.int32), 0))
    tile_cls = jnp.where(valid == 1, tile_cls, last_cls)
    ea = jnp.asarray(_CLASS_EA, jnp.int32)[tile_cls]
    eb = jnp.asarray(_CLASS_EB, jnp.int32)[tile_cls]
    return pos, fill_rows, ea, eb, valid


def moe(xp, wts, cls, e_w_gate, e_w_up, w_down, e_w_down, layer):
    pos, fill_rows, ea, eb, valid = dispatch_plan(cls[:, 0])
    xs, ws = sc_dispatch(xp, wts, pos, fill_rows)
    w_gate, w_up = cast_layer(e_w_gate, layer, after=pos), cast_layer(e_w_up, layer, after=pos)
    ys = experts(xs, ws, ea, eb, valid, w_gate, w_up, w_down)
    w_down_next = cast_layer(e_w_down, layer + 1, after=ys) if layer + 1 < DEPTH else None
    return sc_gather_tokens(ys, pos), w_down_next


def _rows(a):
    return a.reshape(-1, a.shape[-1])


def kernel(x_prompt, x_sample, state_C, state_n, state_m, cache_k, cache_v, a_w_in, a_b_gate, a_norm, a_w_out, b_w_in, b_rel, b_w_out, ln1_g, ln1_b, ln2_g, ln2_b, r_w_group, r_b_group, r_w_expert, r_b_expert, e_w_gate, e_w_up, e_w_down):
    hk = A_HEADS * A_DK
    n_main = 2 * hk + 2 * D_MODEL
    gate_w = lambda layer: a_w_in[layer // 2, :, n_main:]
    x, xb, g = embed(_rows(x_prompt), _rows(x_sample), gate_w(0))
    w_down = cast_layer(e_w_down, 0)
    outs = {k: [] for k in ("Cp", "np", "mp", "kp", "vp", "Cs", "ns", "ms", "ks", "vs")}
    for layer in range(DEPTH):
        j = layer // 2
        if layer % 2 == 0:
            proj = matmul(xb, a_w_in, j, n_main, BF16)
            zc = jnp.zeros((BATCH, A_HEADS, A_DK, A_DV), F32)
            zn = jnp.zeros((BATCH, A_HEADS, A_DK), F32)
            zm = jnp.zeros((BATCH, A_HEADS), F32)
            h_p, c_p, n_p, m_p = mlstm(proj, g, a_b_gate[j], a_norm[j], zc, zn, zm,
                                       batch=BATCH, seq=SEQ, L=MLSTM_CHUNK, row0=0)
            h_s, c_s, n_s, m_s = mlstm(proj, g, a_b_gate[j], a_norm[j],
                                       state_C[j], state_n[j], state_m[j],
                                       batch=DEC_BATCH, seq=DEC_SEQ, L=DEC_SEQ, row0=N_PROMPT)
            outs["Cp"].append(c_p); outs["np"].append(n_p); outs["mp"].append(m_p)
            outs["Cs"].append(c_s); outs["ns"].append(n_s); outs["ms"].append(m_s)
            w_out = a_w_out[j]
        else:
            qkv = matmul(xb, b_w_in, j, 3 * D_MODEL, BF16)
            h_p, k_p, v_p = attn_prompt(qkv, b_rel[j])
            h_s, k_s, v_s = attn_sample(qkv, cache_k, cache_v, j, b_rel[j])
            heads = lambda a: a.reshape(a.shape[:2] + (B_HEADS, B_DH))
            outs["kp"].append(heads(k_p)); outs["vp"].append(heads(v_p))
            outs["ks"].append(heads(k_s)); outs["vs"].append(heads(v_s))
            w_out = b_w_out[j]
        xp, wts, cls = mix_ln_router(h_p, h_s, w_out.astype(BF16), x, ln1_g[layer], ln1_b[layer],
                                     r_w_group[layer], r_b_group[layer], r_w_expert[layer], r_b_expert[layer])
        y, w_down = moe(xp, wts, cls, e_w_gate, e_w_up, w_down, e_w_down, layer)
        if layer < DEPTH - 1:
            if (layer + 1) % 2 == 0:
                x, xb, g = residual_ln(xp, y, ln2_g[layer], ln2_b[layer], gate_w(layer + 1))
            else:
                x, xb = residual_ln(xp, y, ln2_g[layer], ln2_b[layer])
        else:
            y_p, y_s = residual_ln_split(xp, y, ln2_g[layer], ln2_b[layer], N_PROMPT)
    st = lambda k: jnp.stack(outs[k])
    return (y_p.reshape(BATCH, SEQ, D_MODEL), y_s.reshape(DEC_BATCH, DEC_SEQ, D_MODEL),
            st("Cp"), st("np"), st("mp"), st("kp"), st("vp"),
            st("Cs"), st("ns"), st("ms"), st("ks"), st("vs"))
```

```python
import functools

import jax
import jax.numpy as jnp
from jax import lax
from jax.experimental import pallas as pl
from jax.experimental.pallas import tpu as pltpu
from jax.experimental.pallas import tpu_sc as plsc

F32 = jnp.float32
BF16 = jnp.bfloat16

D_MODEL = 2048
BATCH = 8
SEQ = 4096
DEPTH = 4
DEC_BATCH = 32
DEC_SEQ = 32
PAST_LEN = 1024
CHUNK = 64
A_HEADS = 8
A_DK = 128
A_DV = D_MODEL // A_HEADS
GATE_CAP = 15.0
B_HEADS = 16
B_DH = D_MODEL // B_HEADS
PREV_CHUNKS = 8
REACH = PREV_CHUNKS * CHUNK
REL_CLIP = 256
N_GROUPS = 4
EXP_PER_GROUP = 4
N_EXPERTS = N_GROUPS * EXP_PER_GROUP
D_EXPERT = D_MODEL // 4
ALPHA = (2 * DEPTH) ** 0.25
LN_EPS = 1e-5
RMS_EPS = 1e-6

N_PROMPT = BATCH * SEQ
N_SAMPLE = DEC_BATCH * DEC_SEQ
N_TOK = N_PROMPT + N_SAMPLE

VMEM_LIMIT = 56 * 1024 * 1024
LANES = 128

PAIRS = ((0, 1), (0, 2), (0, 3), (1, 2), (1, 3), (2, 3))
N_CLASSES = N_GROUPS * len(PAIRS)
MOE_TILE = 256
MOE_TILES = -(-(N_TOK // MOE_TILE + N_CLASSES) // 32) * 32
MOE_ROWS = MOE_TILES * MOE_TILE

MLSTM_CHUNK = 128
ATT_TQ = 256
ATT_TK = ATT_TQ + REACH
NEG = -1e30

SC_TOKENS = 48
SC_FILL = 32


def _params(*sem):
    return pltpu.CompilerParams(dimension_semantics=sem, vmem_limit_bytes=VMEM_LIMIT)


def _mm_body(x_ref, w_ref, o_ref, wb):
    @pl.when(pl.program_id(1) == 0)
    def _():
        wb[...] = w_ref[...].astype(BF16)

    o_ref[...] = jnp.dot(x_ref[...], wb[...], preferred_element_type=F32).astype(o_ref.dtype)


def matmul(x, w_all, layer, m, out_dtype, tm=1024, tn=1024):
    n, k = x.shape
    return pl.pallas_call(
        _mm_body,
        grid=(m // tn, n // tm),
        in_specs=[pl.BlockSpec((tm, k), lambda j, i: (i, 0)),
                  pl.BlockSpec((None, k, tn), lambda j, i: (layer, 0, j))],
        out_specs=pl.BlockSpec((tm, tn), lambda j, i: (i, j)),
        out_shape=jax.ShapeDtypeStruct((n, m), out_dtype),
        scratch_shapes=[pltpu.VMEM((k, tn), BF16)],
        compiler_params=_params("parallel", "arbitrary"),
        name="matmul",
    )(x, w_all)


def _cast_body(w_ref, *rest):
    o_ref = rest[-1]
    o_ref[...] = w_ref[...].astype(BF16)


def cast_layer(w_all, layer, after=None):
    e, a, b = w_all.shape[1:]
    per_step = 2
    in_specs = [pl.BlockSpec((None, per_step, a, b), lambda i: (layer, i, 0, 0))]
    args = [w_all]
    if after is not None:
        in_specs.append(pl.BlockSpec(memory_space=pl.ANY))
        args.append(after)
    return pl.pallas_call(
        _cast_body,
        grid=(e // per_step,),
        in_specs=in_specs,
        out_specs=pl.BlockSpec((per_step, a, b), lambda i: (i, 0, 0)),
        out_shape=jax.ShapeDtypeStruct((e, a, b), BF16),
        compiler_params=_params("parallel"),
        name="cast_layer",
    )(*args)


def _split2(x):
    hi = x.astype(BF16)
    lo = (x - hi.astype(F32)).astype(BF16)
    return hi, lo


def _dot3(xh, xl, wh, wl, dims):
    dg = functools.partial(lax.dot_general, dimension_numbers=dims, preferred_element_type=F32)
    return dg(xh, wh) + (dg(xh, wl) + dg(xl, wh))


_NN = (((1,), (0,)), ((), ()))
_NT = (((1,), (1,)), ((), ()))
_TN = (((0,), (0,)), ((), ()))


def _dot_hilo(x, w):
    t = x.shape[0]
    xh, xl = _split2(x)
    wh, wl = _split2(w)
    r = jnp.dot(jnp.concatenate([xh, xl], axis=0), jnp.concatenate([wh, wl], axis=1), preferred_element_type=F32)
    return r[:t, :LANES] + (r[:t, LANES:] + r[t:, :LANES]) + r[t:, LANES:]


def _log_sigmoid(x):
    return jnp.minimum(x, 0.0) - jnp.log(1.0 + jnp.exp(-jnp.abs(x)))


def _split3(x):
    a = x.astype(BF16)
    r = x - a.astype(F32)
    b = r.astype(BF16)
    c = (r - b.astype(F32)).astype(BF16)
    return a, b, c


A_STATE = A_DV + LANES


def _cummax_rows(x):
    n = x.shape[0]
    row = lax.broadcasted_iota(jnp.int32, x.shape, 0)
    k = 1
    while k < n:
        x = jnp.maximum(x, jnp.where(row >= k, pltpu.roll(x, k, axis=0), -jnp.inf))
        k *= 2
    return x


def _mlstm_body(q_ref, k_ref, v_ref, o_ref, g_ref, bias_ref, gain_ref, c0_ref, n0_ref, m0_ref,
                h_ref, c_out_ref, n_out_ref, m_ref, state, *, L):
    step = pl.program_id(1)

    @pl.when(step == 0)
    def _():
        state[:, :, :A_DV] = c0_ref[0]
        state[:, :, A_DV:] = n0_ref[0]
        m_ref[...] = m0_ref[...]

    dot = functools.partial(jnp.dot, preferred_element_type=F32)
    row = lax.broadcasted_iota(jnp.int32, (L, L), 0)
    col = lax.broadcasted_iota(jnp.int32, (L, L), 1)
    causal = col <= row
    tril = jnp.where(causal, 1.0, 0.0).astype(BF16)

    pre = GATE_CAP * jnp.tanh((g_ref[...] + bias_ref[...]) * (1.0 / GATE_CAP))
    ig = pre
    lf = _log_sigmoid(pltpu.roll(pre, LANES - A_HEADS, axis=1))
    a3, b3, c3 = _split3(lf)
    bt = dot(tril, a3) + (dot(tril, b3) + dot(tril, c3))
    a = ig - bt
    m_prev = m_ref[0]
    m_t = bt + jnp.maximum(m_prev, _cummax_rows(a))
    u = bt - m_t
    inter = jnp.exp(bt + m_prev - m_t)
    eminus = jnp.exp(-m_t)
    m_new = m_t[L - 1:L, :]
    bt_last = bt[L - 1:L, :]
    w_c = jnp.exp((bt_last - m_new) + a)
    decay = jnp.exp(bt_last + m_prev - m_new)
    m_ref[0] = m_new
    a_pad = a if L == LANES else jnp.concatenate([a, jnp.zeros((LANES - L, LANES), F32)], axis=0)
    a_t = jnp.transpose(a_pad)
    ie = jnp.concatenate([inter, eminus], axis=1).astype(BF16)
    sel_r = lax.broadcasted_iota(jnp.int32, (2 * LANES, 2 * LANES), 0)
    sel_c = lax.broadcasted_iota(jnp.int32, (2 * LANES, 2 * LANES), 1)
    same_half = (sel_r >= LANES) == (sel_c >= LANES)
    ones_l = jnp.ones((L, LANES), BF16)
    zeros_l = jnp.zeros((L, LANES), BF16)
    zeros_k = jnp.zeros((A_DK, LANES), BF16)
    mean_cols = jnp.full((A_DV, LANES), 1.0, BF16)

    scale = A_DK ** -0.5
    heads = range(A_HEADS)
    dk_sl = lambda h: slice(h * A_DK, (h + 1) * A_DK)
    dv_sl = lambda h: slice(h * A_DV, (h + 1) * A_DV)
    ie_b, qk = [], []
    for h in heads:
        sel = jnp.where(same_half & ((sel_r & (LANES - 1)) == h), 1.0, 0.0).astype(BF16)
        ie_b.append(dot(ie, sel))
        qk.append(lax.dot_general(q_ref[:, dk_sl(h)], k_ref[:, dk_sl(h)], _NT, preferred_element_type=F32))
    out = []
    for h in heads:
        d_mat = jnp.where(causal, jnp.exp(u[:, h:h + 1] + a_t[h:h + 1, :L]), 0.0)
        s = qk[h] * (d_mat * scale)
        q_i = (q_ref[:, dk_sl(h)].astype(F32) * ie_b[h][:, :LANES]).astype(BF16)
        c_b = state[h].astype(BF16)
        rhs = jnp.concatenate([
            jnp.concatenate([v_ref[:, dv_sl(h)], ones_l, zeros_l], axis=1),
            jnp.concatenate([c_b[:, :A_DV], zeros_k, c_b[:, A_DV:]], axis=1)], axis=0)
        out.append(dot(jnp.concatenate([s.astype(BF16), q_i], axis=1), rhs))
    hh, ms = [], []
    for h in heads:
        den = jnp.maximum(jnp.abs(out[h][:, A_DV:A_DV + LANES] + out[h][:, A_DV + LANES:]), ie_b[h][:, LANES:])
        r = 1.0 / den
        hh.append(out[h][:, :A_DV] * jnp.concatenate([r, r], axis=1))
        ms.append(dot((hh[h] * hh[h]).astype(BF16), mean_cols))
    upd = []
    for h in heads:
        rs = lax.rsqrt(ms[h] * (1.0 / A_DV) + RMS_EPS)
        og = o_ref[:, dv_sl(h)].astype(F32)
        y = hh[h] * jnp.concatenate([rs, rs], axis=1) * gain_ref[:, dv_sl(h)] * jax.nn.sigmoid(og)
        h_ref[:, dv_sl(h)] = y.astype(h_ref.dtype)
        wk = (k_ref[:, dk_sl(h)].astype(F32) * (w_c[:, h:h + 1] * scale)).astype(BF16)
        upd.append(lax.dot_general(wk, jnp.concatenate([v_ref[:, dv_sl(h)], ones_l], axis=1), _TN,
                                   preferred_element_type=F32))
    for h in heads:
        state[h] = decay[:, h:h + 1] * state[h] + upd[h]

    @pl.when(step == pl.num_programs(1) - 1)
    def _():
        c_out_ref[0] = state[:, :, :A_DV]
        n_out_ref[0] = state[:, :, A_DV:]


def mlstm(proj, g, b_gate, gain, c0, n0, m0, *, batch, seq, L, row0):
    nc = seq // L
    bias = jnp.pad(b_gate.astype(F32), (0, LANES - 2 * A_HEADS))
    row_blk = lambda b, c: row0 // L + b * nc + c
    st = lambda *s: pl.BlockSpec((1,) + s, lambda b, c: (b,) + (0,) * len(s))
    n0_lanes = jnp.broadcast_to(n0[..., None], n0.shape + (LANES,))
    m_lanes = jnp.pad(m0, ((0, 0), (0, LANES - A_HEADS)))[:, None, :]
    h, c, n, m = pl.pallas_call(
        functools.partial(_mlstm_body, L=L),
        grid=(batch, nc),
        in_specs=[pl.BlockSpec((L, A_HEADS * A_DK), lambda b, c: (row_blk(b, c), 0)),
                  pl.BlockSpec((L, A_HEADS * A_DK), lambda b, c: (row_blk(b, c), 1)),
                  pl.BlockSpec((L, D_MODEL), lambda b, c: (row_blk(b, c), 1)),
                  pl.BlockSpec((L, D_MODEL), lambda b, c: (row_blk(b, c), 2)),
                  pl.BlockSpec((L, LANES), lambda b, c: (row_blk(b, c), 0)),
                  pl.BlockSpec((1, LANES), lambda b, c: (0, 0)),
                  pl.BlockSpec((1, D_MODEL), lambda b, c: (0, 0)),
                  st(A_HEADS, A_DK, A_DV), st(A_HEADS, A_DK, LANES), st(1, LANES)],
        out_specs=[pl.BlockSpec((L, D_MODEL), lambda b, c: (b * nc + c, 0)),
                   st(A_HEADS, A_DK, A_DV), st(A_HEADS, A_DK, LANES), st(1, LANES)],
        out_shape=[jax.ShapeDtypeStruct((batch * seq, D_MODEL), BF16),
                   jax.ShapeDtypeStruct((batch, A_HEADS, A_DK, A_DV), F32),
                   jax.ShapeDtypeStruct((batch, A_HEADS, A_DK, LANES), F32),
                   jax.ShapeDtypeStruct((batch, 1, LANES), F32)],
        scratch_shapes=[pltpu.VMEM((A_HEADS, A_DK, A_STATE), F32)],
        compiler_params=_params("parallel", "arbitrary"),
        name="mlstm",
    )(proj, proj, proj, proj, g, bias[None, :], gain.astype(F32)[None, :], c0, n0_lanes, m_lanes)
    return h, c, n[..., 0], m[:, 0, :A_HEADS]


LOG2E = 1.4426950408889634
Q_SCALE = B_DH ** -0.5 * LOG2E


def _scale_q(q):
    return (q.astype(F32) * Q_SCALE).astype(BF16)


def _attn_prompt_body(q_ref, k_ref, v_ref, bias_ref, o_ref, kt_ref, vt_ref, qs, kpad, vpad):
    kt_ref[0] = k_ref[SEQ - REACH:, :].astype(F32)
    vt_ref[0] = v_ref[SEQ - REACH:, :].astype(F32)
    qs[...] = _scale_q(q_ref[...])
    kpad[0:REACH, :] = jnp.zeros((REACH, B_DH), BF16)
    kpad[REACH:, :] = k_ref[...]
    vpad[0:REACH, 0:B_DH] = jnp.zeros((REACH, B_DH), BF16)
    vpad[REACH:, 0:B_DH] = v_ref[...]
    vpad[:, B_DH:] = jnp.ones((SEQ + REACH, B_DH), BF16)

    def scores(t):
        r0 = t * ATT_TQ
        kb = kpad[r0:r0 + ATT_TK, :]
        s = lax.dot_general(qs[r0:r0 + ATT_TQ, :], kb, _NT, preferred_element_type=F32) + bias_ref[0]
        if r0 < REACH:
            j = lax.broadcasted_iota(jnp.int32, (ATT_TQ, ATT_TK), 1)
            s = jnp.where(j + r0 >= REACH, s, NEG)
        return s

    def finish(t, s):
        r0 = t * ATT_TQ
        p = jnp.exp2(s - jnp.max(s, axis=-1, keepdims=True)).astype(BF16)
        acc = jnp.dot(p, vpad[r0:r0 + ATT_TK, :], preferred_element_type=F32)
        o_ref[r0:r0 + ATT_TQ, :] = (acc[:, :B_DH] / acc[:, B_DH:]).astype(o_ref.dtype)

    n_tiles = SEQ // ATT_TQ
    s_next = scores(0)
    for t in range(n_tiles):
        s_cur = s_next
        if t + 1 < n_tiles:
            s_next = scores(t + 1)
        finish(t, s_cur)


def band_bias(rel_table, nq, nk):
    w = nq + nk
    rel = jnp.clip(REACH + nq - 1 - jnp.arange(w), -REL_CLIP, REL_CLIP) + REL_CLIP
    r = rel_table.astype(F32)[:, rel]
    heads = r.shape[0]
    skew = jnp.broadcast_to(r[:, None, :], (heads, nq, w)).reshape(heads, nq * w)
    skew = skew[:, :nq * (w - 1)].reshape(heads, nq, w - 1)
    bias = skew[:, :, nq - 1:nq - 1 + nk]
    i = jnp.arange(nq)[:, None]
    j = jnp.arange(nk)[None, :]
    dc = j // CHUNK - i // CHUNK
    allowed = (dc >= 0) & (dc <= PREV_CHUNKS)
    return jnp.where(allowed[None], bias * LOG2E, NEG)


def attn_prompt(qkv, rel_table):
    bias = band_bias(rel_table, ATT_TQ, ATT_TK)
    tail = pl.BlockSpec((1, REACH, B_DH), lambda b, h: (b, 0, h))
    return pl.pallas_call(
        _attn_prompt_body,
        grid=(BATCH, B_HEADS),
        in_specs=[pl.BlockSpec((SEQ, B_DH), lambda b, h: (b, h)),
                  pl.BlockSpec((SEQ, B_DH), lambda b, h: (b, B_HEADS + h)),
                  pl.BlockSpec((SEQ, B_DH), lambda b, h: (b, 2 * B_HEADS + h)),
                  pl.BlockSpec((1, ATT_TQ, ATT_TK), lambda b, h: (h, 0, 0))],
        out_specs=[pl.BlockSpec((SEQ, B_DH), lambda b, h: (b, h)), tail, tail],
        out_shape=[jax.ShapeDtypeStruct((N_PROMPT, D_MODEL), BF16),
                   jax.ShapeDtypeStruct((BATCH, REACH, D_MODEL), F32),
                   jax.ShapeDtypeStruct((BATCH, REACH, D_MODEL), F32)],
        scratch_shapes=[pltpu.VMEM((SEQ, B_DH), BF16),
                        pltpu.VMEM((SEQ + REACH, B_DH), BF16),
                        pltpu.VMEM((SEQ + REACH, 2 * B_DH), BF16)],
        compiler_params=_params("parallel", "parallel"),
        name="attn_prompt",
    )(qkv, qkv, qkv, bias)


def _attn_sample_body(q_ref, kn_ref, vn_ref, ck_ref, cv_ref, bias_c_ref, bias_n_ref, o_ref, ks_ref, vs_ref):
    ks_ref[0] = kn_ref[...].astype(F32)
    vs_ref[0] = vn_ref[...].astype(F32)
    for h in range(B_HEADS):
        sl = slice(h * B_DH, (h + 1) * B_DH)
        q = _scale_q(q_ref[:, sl])
        kc = ck_ref[pl.ds(h, REACH, stride=B_HEADS), :].astype(BF16)
        vc = cv_ref[pl.ds(h, REACH, stride=B_HEADS), :].astype(BF16)
        s_c = lax.dot_general(q, kc, _NT, preferred_element_type=F32) + bias_c_ref[h]
        s_n = lax.dot_general(q, kn_ref[:, sl], _NT, preferred_element_type=F32) + bias_n_ref[h]
        m = jnp.maximum(jnp.max(s_c, axis=-1, keepdims=True), jnp.max(s_n, axis=-1, keepdims=True))
        p_c = jnp.exp2(s_c - m)
        p_n = jnp.exp2(s_n - m)
        l = jnp.sum(p_c, axis=-1, keepdims=True) + jnp.sum(p_n, axis=-1, keepdims=True)
        dot = functools.partial(jnp.dot, preferred_element_type=F32)
        o_ref[:, sl] = ((dot(p_c.astype(BF16), vc) + dot(p_n.astype(BF16), vn_ref[:, sl])) / l).astype(o_ref.dtype)


def attn_sample(qkv, cache_k, cache_v, layer, rel_table):
    bias = band_bias(rel_table, DEC_SEQ, REACH + DEC_SEQ)
    cache_rows = REACH * B_HEADS
    ck = cache_k.reshape(cache_k.shape[0], DEC_BATCH, cache_rows, B_DH)
    cv = cache_v.reshape(cache_v.shape[0], DEC_BATCH, cache_rows, B_DH)
    row = lambda c: pl.BlockSpec((DEC_SEQ, D_MODEL), lambda b: (N_PROMPT // DEC_SEQ + b, c))
    cache = pl.BlockSpec((None, None, cache_rows, B_DH), lambda b: (layer, b, 0, 0))
    new = pl.BlockSpec((1, DEC_SEQ, D_MODEL), lambda b: (b, 0, 0))
    return pl.pallas_call(
        _attn_sample_body,
        grid=(DEC_BATCH,),
        in_specs=[row(0), row(1), row(2), cache, cache,
                  pl.BlockSpec((B_HEADS, DEC_SEQ, REACH), lambda b: (0, 0, 0)),
                  pl.BlockSpec((B_HEADS, DEC_SEQ, DEC_SEQ), lambda b: (0, 0, 0))],
        out_specs=[pl.BlockSpec((DEC_SEQ, D_MODEL), lambda b: (b, 0)), new, new],
        out_shape=[jax.ShapeDtypeStruct((N_SAMPLE, D_MODEL), BF16),
                   jax.ShapeDtypeStruct((DEC_BATCH, DEC_SEQ, D_MODEL), F32),
                   jax.ShapeDtypeStruct((DEC_BATCH, DEC_SEQ, D_MODEL), F32)],
        compiler_params=_params("parallel"),
        name="attn_sample",
    )(qkv, qkv, qkv, ck, cv, bias[:, :, :REACH], bias[:, :, REACH:])


def _ln(z, g, b):
    mu = jnp.mean(z, axis=-1, keepdims=True)
    zc = z - mu
    var = jnp.mean(zc * zc, axis=-1, keepdims=True)
    return zc * lax.rsqrt(var + LN_EPS) * g + b


PACK_ROWS = D_MODEL // 2 // LANES
HI_MASK = -65536


def _store_packed(ref, v, token0=0):
    t = v.shape[0]
    half = D_MODEL // 2
    hi = lax.bitcast_convert_type(v[:, :half].astype(BF16).astype(F32), jnp.int32)
    lo = lax.bitcast_convert_type(v[:, half:].astype(BF16).astype(F32), jnp.int32)
    words = (hi & HI_MASK) | lax.shift_right_logical(lo, 16)
    for a in range(PACK_ROWS):
        ref[pl.ds(token0 * PACK_ROWS + a, t, stride=PACK_ROWS), :] = words[:, a * LANES:(a + 1) * LANES]


def _load_packed(ref, t):
    words = jnp.concatenate([ref[pl.ds(a, t, stride=PACK_ROWS), :] for a in range(PACK_ROWS)], axis=1)
    hi = lax.bitcast_convert_type(words & HI_MASK, F32)
    lo = lax.bitcast_convert_type(lax.shift_left(words, 16), F32)
    return jnp.concatenate([hi, lo], axis=1)


def _mix_ln_router_body(hp_ref, hs_ref, w_ref, x_ref, g_ref, b_ref, rw_ref, rb_ref,
                        xp_ref, wt_ref, cls_ref, y_even, y_odd, *, prompt_tiles, n_tiles):
    i = pl.program_id(0)
    from_prompt = jnp.minimum(i, n_tiles - 1) < prompt_tiles

    @pl.when(i == 0)
    def _():
        y_odd[...] = jnp.zeros_like(y_odd)

    half = D_MODEL // 2

    def run(y_new, y_old):
        h = jnp.where(from_prompt, hp_ref[...], hs_ref[...])
        y_new[:, :half] = jnp.dot(h, w_ref[:, :half], preferred_element_type=F32)
        xn = _ln(ALPHA * x_ref[...] + y_old[...], g_ref[...], b_ref[...])
        logits = _dot_hilo(xn, rw_ref[...]) + rb_ref[...]
        y_new[:, half:] = jnp.dot(h, w_ref[:, half:], preferred_element_type=F32)
        _store_packed(xp_ref, xn)
        _route(logits, wt_ref, cls_ref)

    @pl.when(i % 2 == 0)
    def _():
        run(y_even, y_odd)

    @pl.when(i % 2 == 1)
    def _():
        run(y_odd, y_even)


def mix_ln_router(h_p, h_s, w, x, g, b, w_group, b_group, w_expert, b_expert, tm=256):
    n = x.shape[0]
    n_tiles = n // tm
    prompt_tiles = h_p.shape[0] // tm
    sample_tiles = h_s.shape[0] // tm
    assert prompt_tiles * tm == h_p.shape[0] and (prompt_tiles + sample_tiles) * tm == n
    mm_tile = lambda i: jnp.minimum(i, n_tiles - 1)
    ep_tile = lambda i: jnp.maximum(i - 1, 0)
    hp_spec = pl.BlockSpec((tm, D_MODEL), lambda i: (jnp.minimum(mm_tile(i), prompt_tiles - 1), 0))
    hs_spec = pl.BlockSpec((tm, D_MODEL), lambda i: (jnp.maximum(mm_tile(i) - prompt_tiles, 0), 0))
    rw = jnp.pad(jnp.concatenate([w_group, w_expert], axis=1), ((0, 0), (0, LANES - N_GROUPS - N_EXPERTS)))
    rb = jnp.pad(jnp.concatenate([b_group, b_expert]).astype(F32), (0, LANES - N_GROUPS - N_EXPERTS))
    row = pl.BlockSpec((tm, D_MODEL), lambda i: (ep_tile(i), 0))
    vec = pl.BlockSpec((1, D_MODEL), lambda i: (0, 0))
    lane_row = pl.BlockSpec((tm, LANES), lambda i: (ep_tile(i), 0))
    return pl.pallas_call(
        functools.partial(_mix_ln_router_body, prompt_tiles=prompt_tiles, n_tiles=n_tiles),
        grid=(n_tiles + 1,),
        in_specs=[hp_spec, hs_spec, pl.BlockSpec((D_MODEL, D_MODEL), lambda i: (0, 0)), row, vec, vec,
                  pl.BlockSpec((D_MODEL, LANES), lambda i: (0, 0)), pl.BlockSpec((1, LANES), lambda i: (0, 0))],
        out_specs=[pl.BlockSpec((tm * PACK_ROWS, LANES), lambda i: (ep_tile(i), 0)), lane_row, lane_row],
        out_shape=[jax.ShapeDtypeStruct((n * PACK_ROWS, LANES), jnp.int32),
                   jax.ShapeDtypeStruct((n, LANES), F32), jax.ShapeDtypeStruct((n, LANES), jnp.int32)],
        scratch_shapes=[pltpu.VMEM((tm, D_MODEL), F32), pltpu.VMEM((tm, D_MODEL), F32)],
        compiler_params=_params("arbitrary"),
        name="mix_ln_router",
    )(h_p, h_s, w, x, g[None, :], b[None, :], rw, rb[None, :])


def _add_ln_body(x_ref, y_ref, g_ref, b_ref, *rest):
    t = rest[-1].shape[0]
    xn = _ln(ALPHA * _load_packed(x_ref, t) + _load_packed(y_ref, t), g_ref[...], b_ref[...])
    if len(rest) == 2:
        xo_ref, xbo_ref = rest
    else:
        wg_ref, xo_ref, xbo_ref, go_ref = rest
        go_ref[...] = _dot_hilo(xn, wg_ref[...])
    xo_ref[...] = xn
    xbo_ref[...] = xn.astype(BF16)


def residual_ln(x_packed, y_packed, g, b, w_gate=None, tm=512):
    n = x_packed.shape[0] // PACK_ROWS
    row = pl.BlockSpec((tm, D_MODEL), lambda i: (i, 0))
    vec = pl.BlockSpec((1, D_MODEL), lambda i: (0, 0))
    packed = pl.BlockSpec((tm * PACK_ROWS, LANES), lambda i: (i, 0))
    in_specs = [packed, packed, vec, vec]
    out_specs = [row, row]
    out_shape = [jax.ShapeDtypeStruct((n, D_MODEL), F32), jax.ShapeDtypeStruct((n, D_MODEL), BF16)]
    args = [x_packed, y_packed, g[None, :], b[None, :]]
    if w_gate is not None:
        in_specs.append(pl.BlockSpec((D_MODEL, LANES), lambda i: (0, 0)))
        out_specs.append(pl.BlockSpec((tm, LANES), lambda i: (i, 0)))
        out_shape.append(jax.ShapeDtypeStruct((n, LANES), F32))
        args.append(jnp.pad(w_gate, ((0, 0), (0, LANES - w_gate.shape[1]))))
    return pl.pallas_call(
        _add_ln_body,
        grid=(n // tm,),
        in_specs=in_specs,
        out_specs=out_specs,
        out_shape=out_shape,
        compiler_params=_params("parallel"),
        name="residual_ln",
    )(*args)


def _embed_body(xp_ref, xs_ref, wg_ref, xo_ref, xbo_ref, go_ref, *, prompt_tiles):
    x = jnp.where(pl.program_id(0) < prompt_tiles, xp_ref[...], xs_ref[...])
    xo_ref[...] = x
    xbo_ref[...] = x.astype(BF16)
    go_ref[...] = _dot_hilo(x, wg_ref[...])


def embed(x_p, x_s, w_gate, tm=512):
    n = x_p.shape[0] + x_s.shape[0]
    prompt_tiles = x_p.shape[0] // tm
    assert prompt_tiles * tm == x_p.shape[0] and n % tm == 0
    row = pl.BlockSpec((tm, D_MODEL), lambda i: (i, 0))
    wg = jnp.pad(w_gate, ((0, 0), (0, LANES - w_gate.shape[1])))
    return pl.pallas_call(
        functools.partial(_embed_body, prompt_tiles=prompt_tiles),
        grid=(n // tm,),
        in_specs=[pl.BlockSpec((tm, D_MODEL), lambda i: (jnp.minimum(i, prompt_tiles - 1), 0)),
                  pl.BlockSpec((tm, D_MODEL), lambda i: (jnp.maximum(i - prompt_tiles, 0), 0)),
                  pl.BlockSpec((D_MODEL, LANES), lambda i: (0, 0))],
        out_specs=[row, row, pl.BlockSpec((tm, LANES), lambda i: (i, 0))],
        out_shape=[jax.ShapeDtypeStruct((n, D_MODEL), F32), jax.ShapeDtypeStruct((n, D_MODEL), BF16),
                   jax.ShapeDtypeStruct((n, LANES), F32)],
        compiler_params=_params("parallel"),
        name="embed",
    )(x_p, x_s, wg)


def _add_ln_split_body(x_ref, y_ref, g_ref, b_ref, op_ref, os_ref, *, prompt_tiles):
    t = op_ref.shape[0]
    xn = _ln(ALPHA * _load_packed(x_ref, t) + _load_packed(y_ref, t), g_ref[...], b_ref[...])
    i = pl.program_id(0)

    @pl.when(i < prompt_tiles)
    def _():
        op_ref[...] = xn

    @pl.when(i >= prompt_tiles)
    def _():
        os_ref[...] = xn


def residual_ln_split(x_packed, y_packed, g, b, n_prompt, tm=512):
    n = x_packed.shape[0] // PACK_ROWS
    prompt_tiles = n_prompt // tm
    assert prompt_tiles * tm == n_prompt and n % tm == 0
    vec = pl.BlockSpec((1, D_MODEL), lambda i: (0, 0))
    packed = pl.BlockSpec((tm * PACK_ROWS, LANES), lambda i: (i, 0))
    return pl.pallas_call(
        functools.partial(_add_ln_split_body, prompt_tiles=prompt_tiles),
        grid=(n // tm,),
        in_specs=[packed, packed, vec, vec],
        out_specs=[pl.BlockSpec((tm, D_MODEL), lambda i: (jnp.minimum(i, prompt_tiles - 1), 0)),
                   pl.BlockSpec((tm, D_MODEL), lambda i: (jnp.maximum(i - prompt_tiles, 0), 0))],
        out_shape=[jax.ShapeDtypeStruct((n_prompt, D_MODEL), F32),
                   jax.ShapeDtypeStruct((n - n_prompt, D_MODEL), F32)],
        compiler_params=_params("arbitrary"),
        name="residual_ln_split",
    )(x_packed, y_packed, g[None, :], b[None, :])


ROUTER_E0 = N_GROUPS


def _route(logits, wt_ref, cls_ref):
    lane = lax.broadcasted_iota(jnp.int32, logits.shape, 1).astype(F32)
    big = float(LANES)
    rmax = lambda a: jnp.max(a, axis=-1, keepdims=True)
    rmin = lambda a: jnp.min(a, axis=-1, keepdims=True)
    is_g = lane < N_GROUPS
    gl = jnp.where(is_g, logits, -jnp.inf)
    gmax = rmax(gl)
    gsel = rmin(jnp.where(gl == gmax, lane, big))
    gsum = jnp.sum(jnp.where(is_g, jnp.exp(logits - gmax), 0.0), axis=-1, keepdims=True)
    g_w = 1.0 / gsum
    e_lo = ROUTER_E0 + EXP_PER_GROUP * gsel
    in_grp = (lane >= e_lo) & (lane < e_lo + EXP_PER_GROUP)
    el = jnp.where(in_grp, logits, -jnp.inf)
    e1 = rmax(el)
    i1 = rmin(jnp.where(el == e1, lane, big))
    el2 = jnp.where(lane == i1, -jnp.inf, el)
    e2 = rmax(el2)
    i2 = rmin(jnp.where(el2 == e2, lane, big))
    t = jnp.exp(e2 - e1)
    p1 = 1.0 / (1.0 + t)
    w1 = p1 * g_w
    w2 = (t * p1) * g_w
    a1 = i1 - e_lo
    a2 = i2 - e_lo
    first_low = a1 < a2
    lo = jnp.where(first_low, a1, a2)
    hi = jnp.where(first_low, a2, a1)
    w_lo = jnp.where(first_low, w1, w2)
    w_hi = jnp.where(first_low, w2, w1)
    off = jnp.where(lo == 0.0, 0.0, jnp.where(lo == 1.0, 3.0, 5.0))
    cls = gsel * float(len(PAIRS)) + off + (hi - lo - 1.0)
    wt_ref[...] = jnp.where(lane < LANES // 2, w_lo, w_hi)
    cls_ref[...] = jnp.broadcast_to(cls, logits.shape).astype(jnp.int32)


def _expert_body(ea_ref, eb_ref, valid_ref, x_ref, wt_ref, wga_ref, wua_ref, wda_ref,
                 wgb_ref, wub_ref, wdb_ref, y_ref):
    i = pl.program_id(0)

    @pl.when(valid_ref[i] == 0)
    def _():
        y_ref[...] = jnp.zeros_like(y_ref)

    @pl.when(valid_ref[i] != 0)
    def _():
        x = _load_packed(x_ref, MOE_TILE).astype(BF16)
        dot = functools.partial(jnp.dot, preferred_element_type=F32)

        ga, ua = dot(x, wga_ref[...]), dot(x, wua_ref[...])
        gb, ub = dot(x, wgb_ref[...]), dot(x, wub_ref[...])
        ya = dot((jax.nn.silu(ga) * ua).astype(BF16), wda_ref[...])
        yb = dot((jax.nn.silu(gb) * ub).astype(BF16), wdb_ref[...])
        w_lo = wt_ref[:, 0:1]
        w_hi = wt_ref[:, LANES // 2:LANES // 2 + 1]
        _store_packed(y_ref, w_lo * ya + w_hi * yb)


def experts(xs, wts, tile_ea, tile_eb, tile_valid, w_gate, w_up, w_down):
    wspec = lambda shape, which: pl.BlockSpec(
        (None,) + shape, lambda i, ea, eb, va: ((ea, eb)[which][i], 0, 0))
    up = (D_MODEL, D_EXPERT)
    down = (D_EXPERT, D_MODEL)
    packed = pl.BlockSpec((MOE_TILE * PACK_ROWS, LANES), lambda i, ea, eb, va: (i, 0))
    grid_spec = pltpu.PrefetchScalarGridSpec(
        num_scalar_prefetch=3,
        grid=(MOE_TILES,),
        in_specs=[packed,
                  pl.BlockSpec((MOE_TILE, LANES), lambda i, ea, eb, va: (i, 0)),
                  wspec(up, 0), wspec(up, 0), wspec(down, 0),
                  wspec(up, 1), wspec(up, 1), wspec(down, 1)],
        out_specs=packed,
    )
    return pl.pallas_call(
        _expert_body,
        grid_spec=grid_spec,
        out_shape=jax.ShapeDtypeStruct((MOE_ROWS * PACK_ROWS, LANES), jnp.int32),
        compiler_params=_params("arbitrary"),
        name="experts",
    )(tile_ea, tile_eb, tile_valid, xs, wts, w_gate, w_up, w_down, w_gate, w_up, w_down)


def sc_dispatch(xp, wts, pos, fill_rows):
    n = pos.shape[0]
    n_fill = fill_rows.shape[0]
    info = plsc.get_sparse_core_info()
    n_workers = info.num_cores * info.num_subcores
    per_worker = n // n_workers
    steps = per_worker // SC_TOKENS
    fill_steps = n_fill // n_workers // SC_FILL
    assert steps * SC_TOKENS * n_workers == n and steps % 2 == 0
    assert fill_steps * SC_FILL * n_workers == n_fill
    mesh = plsc.VectorSubcoreMesh(core_axis_name="core", subcore_axis_name="subcore")
    dma = pltpu.SemaphoreType.DMA

    @functools.partial(
        pl.kernel,
        out_type=[jax.ShapeDtypeStruct((n + n_fill, PACK_ROWS, LANES), xp.dtype),
                  jax.ShapeDtypeStruct((n + n_fill, LANES), wts.dtype)],
        mesh=mesh,
        scratch_types=[pltpu.VMEM((steps, SC_TOKENS), jnp.int32),
                       pltpu.VMEM((fill_steps, SC_FILL), jnp.int32),
                       pltpu.VMEM((2, SC_TOKENS, PACK_ROWS, LANES), xp.dtype),
                       pltpu.VMEM((2, SC_TOKENS, LANES), wts.dtype),
                       dma((2,)), dma((2,)), dma((2,)), dma((2,))],
    )
    def scatter(x_hbm, w_hbm, pos_hbm, fill_hbm, zx_hbm, zw_hbm, xs_hbm, ws_hbm,
                pos_v, fill_v, xbuf, wbuf, read_x, read_w, write_x, write_w):
        worker = lax.axis_index("subcore") * info.num_cores + lax.axis_index("core")
        base = worker * per_worker
        pltpu.sync_copy(pos_hbm.at[worker], pos_v)
        pltpu.sync_copy(fill_hbm.at[worker], fill_v)

        def reads(s, b):
            rows = pl.ds(base + s * SC_TOKENS, SC_TOKENS)
            return (pltpu.make_async_copy(x_hbm.at[rows], xbuf.at[b], read_x.at[b]),
                    pltpu.make_async_copy(w_hbm.at[rows], wbuf.at[b], read_w.at[b]))

        def writes(s, b):
            return (pltpu.make_async_copy(xbuf.at[b], xs_hbm.at[pos_v.at[s]], write_x.at[b]),
                    pltpu.make_async_copy(wbuf.at[b], ws_hbm.at[pos_v.at[s]], write_w.at[b]))

        def start(copies):
            for c in copies:
                c.start()

        def wait(copies):
            for c in copies:
                c.wait()

        start(reads(0, 0))

        @pl.loop(0, steps, step=2)
        def _(s):
            wait(reads(s, 0))
            start(reads(s + 1, 1))
            start(writes(s, 0))
            wait(reads(s + 1, 1))
            wait(writes(s, 0))

            @pl.when(s + 2 < steps)
            def _():
                start(reads(s + 2, 0))

            start(writes(s + 1, 1))
            wait(writes(s + 1, 1))

        zx = xbuf.at[0, pl.ds(0, SC_FILL)]
        zw = wbuf.at[0, pl.ds(0, SC_FILL)]
        pltpu.sync_copy(zx_hbm, zx)
        pltpu.sync_copy(zw_hbm, zw)
        fills = [(pltpu.make_async_copy(zx, xs_hbm.at[fill_v.at[s]], write_x.at[0]),
                  pltpu.make_async_copy(zw, ws_hbm.at[fill_v.at[s]], write_w.at[0])) for s in range(fill_steps)]
        for f in fills:
            start(f)
        for f in fills:
            wait(f)

    xs, ws = scatter(xp.reshape(n, PACK_ROWS, LANES), wts,
                     pos.reshape(n_workers, steps, SC_TOKENS), fill_rows.reshape(n_workers, fill_steps, SC_FILL),
                     jnp.zeros((SC_FILL, PACK_ROWS, LANES), xp.dtype), jnp.zeros((SC_FILL, LANES), wts.dtype))
    return xs.reshape((n + n_fill) * PACK_ROWS, LANES), ws


def sc_gather_tokens(x, idx):
    n = x.shape[0] // PACK_ROWS
    p = idx.shape[0]
    info = plsc.get_sparse_core_info()
    n_workers = info.num_cores * info.num_subcores
    per_worker = p // n_workers
    steps = per_worker // SC_TOKENS
    assert steps * SC_TOKENS * n_workers == p and steps % 2 == 0
    mesh = plsc.VectorSubcoreMesh(core_axis_name="core", subcore_axis_name="subcore")
    dma = pltpu.SemaphoreType.DMA

    @functools.partial(
        pl.kernel,
        out_type=jax.ShapeDtypeStruct((p, PACK_ROWS, LANES), x.dtype),
        mesh=mesh,
        scratch_types=[pltpu.VMEM((steps, SC_TOKENS), jnp.int32),
                       pltpu.VMEM((2, SC_TOKENS, PACK_ROWS, LANES), x.dtype),
                       dma((2,)), dma((2,))],
    )
    def gather(x_hbm, i_hbm, o_hbm, idx_v, buf, read_sem, write_sem):
        worker = lax.axis_index("subcore") * info.num_cores + lax.axis_index("core")
        base = worker * per_worker
        pltpu.sync_copy(i_hbm.at[worker], idx_v)

        def read(s, b):
            return pltpu.make_async_copy(x_hbm.at[idx_v.at[s]], buf.at[b], read_sem.at[b])

        def write(s, b):
            return pltpu.make_async_copy(buf.at[b], o_hbm.at[pl.ds(base + s * SC_TOKENS, SC_TOKENS)], write_sem.at[b])

        read(0, 0).start()

        @pl.loop(0, steps, step=2)
        def _(s):
            read(s, 0).wait()
            read(s + 1, 1).start()
            write(s, 0).start()
            read(s + 1, 1).wait()
            write(s, 0).wait()

            @pl.when(s + 2 < steps)
            def _():
                read(s + 2, 0).start()

            write(s + 1, 1).start()
            write(s + 1, 1).wait()

    out = gather(x.reshape(n, PACK_ROWS, LANES), idx.reshape(n_workers, steps, SC_TOKENS))
    return out.reshape(p * PACK_ROWS, LANES)


_CLASS_EA = [g * EXP_PER_GROUP + lo for g in range(N_GROUPS) for lo, hi in PAIRS]
_CLASS_EB = [g * EXP_PER_GROUP + hi for g in range(N_GROUPS) for lo, hi in PAIRS]


def dispatch_plan(cls):
    onehot = (cls[:, None] == jnp.arange(N_CLASSES, dtype=jnp.int32)[None, :]).astype(jnp.int32)
    csum = jnp.cumsum(onehot, axis=0)
    rank = jnp.sum(csum * onehot, axis=1) - 1
    counts = csum[-1]
    tiles_per = (counts + MOE_TILE - 1) // MOE_TILE
    tile_end = jnp.cumsum(tiles_per)
    tile_start = tile_end - tiles_per
    pos = jnp.sum(onehot * tile_start[None, :], axis=1) * MOE_TILE + rank
    n = cls.shape[0]
    pad = tiles_per * MOE_TILE - counts
    pad_end = jnp.cumsum(pad)
    k = jnp.arange(MOE_ROWS - n, dtype=jnp.int32)
    k_cls = jnp.sum((k[:, None] >= pad_end[None, :]).astype(jnp.int32), axis=1)
    k_hot = (k_cls[:, None] == jnp.arange(N_CLASSES, dtype=jnp.int32)[None, :]).astype(jnp.int32)
    pad_row0 = tile_start * MOE_TILE + counts - (pad_end - pad)
    fill_rows = k + jnp.where(k_cls < N_CLASSES, jnp.sum(k_hot * pad_row0[None, :], axis=1),
                              tile_end[-1] * MOE_TILE - pad_end[-1])
    t = jnp.arange(MOE_TILES, dtype=jnp.int32)
    tile_cls = jnp.sum((t[:, None] >= tile_end[None, :]).astype(jnp.int32), axis=1)
    valid = (tile_cls < N_CLASSES).astype(jnp.int32)
    last_cls = jnp.max(jnp.where(counts > 0, jnp.arange(N_CLASSES, dtype=jnp.int32), 0))
    tile_cls = jnp.where(valid == 1, tile_cls, last_cls)
    ea = jnp.asarray(_CLASS_EA, jnp.int32)[tile_cls]
    eb = jnp.asarray(_CLASS_EB, jnp.int32)[tile_cls]
    return pos, fill_rows, ea, eb, valid


def moe(xp, wts, cls, e_w_gate, e_w_up, w_down, e_w_down, layer):
    pos, fill_rows, ea, eb, valid = dispatch_plan(cls[:, 0])
    xs, ws = sc_dispatch(xp, wts, pos, fill_rows)
    w_gate, w_up = cast_layer(e_w_gate, layer, after=pos), cast_layer(e_w_up, layer, after=pos)
    ys = experts(xs, ws, ea, eb, valid, w_gate, w_up, w_down)
    y = sc_gather_tokens(ys, pos)
    if layer + 1 == DEPTH:
        return y, None
    return lax.optimization_barrier((y, cast_layer(e_w_down, layer + 1, after=ys)))


def _rows(a):
    return a.reshape(-1, a.shape[-1])


def kernel(x_prompt, x_sample, state_C, state_n, state_m, cache_k, cache_v, a_w_in, a_b_gate, a_norm, a_w_out, b_w_in, b_rel, b_w_out, ln1_g, ln1_b, ln2_g, ln2_b, r_w_group, r_b_group, r_w_expert, r_b_expert, e_w_gate, e_w_up, e_w_down):
    hk = A_HEADS * A_DK
    n_main = 2 * hk + 2 * D_MODEL
    gate_w = lambda layer: a_w_in[layer // 2, :, n_main:]
    x, xb, g = embed(_rows(x_prompt), _rows(x_sample), gate_w(0))
    w_down = cast_layer(e_w_down, 0)
    outs = {k: [] for k in ("Cp", "np", "mp", "kp", "vp", "Cs", "ns", "ms", "ks", "vs")}
    for layer in range(DEPTH):
        j = layer // 2
        if layer % 2 == 0:
            proj = matmul(xb, a_w_in, j, n_main, BF16)
            zc = jnp.zeros((BATCH, A_HEADS, A_DK, A_DV), F32)
            zn = jnp.zeros((BATCH, A_HEADS, A_DK), F32)
            zm = jnp.zeros((BATCH, A_HEADS), F32)
            h_p, c_p, n_p, m_p = mlstm(proj, g, a_b_gate[j], a_norm[j], zc, zn, zm,
                                       batch=BATCH, seq=SEQ, L=MLSTM_CHUNK, row0=0)
            h_s, c_s, n_s, m_s = mlstm(proj, g, a_b_gate[j], a_norm[j],
                                       state_C[j], state_n[j], state_m[j],
                                       batch=DEC_BATCH, seq=DEC_SEQ, L=DEC_SEQ, row0=N_PROMPT)
            outs["Cp"].append(c_p); outs["np"].append(n_p); outs["mp"].append(m_p)
            outs["Cs"].append(c_s); outs["ns"].append(n_s); outs["ms"].append(m_s)
            w_out = a_w_out[j]
        else:
            qkv = matmul(xb, b_w_in, j, 3 * D_MODEL, BF16)
            h_p, k_p, v_p = attn_prompt(qkv, b_rel[j])
            h_s, k_s, v_s = attn_sample(qkv, cache_k, cache_v, j, b_rel[j])
            heads = lambda a: a.reshape(a.shape[:2] + (B_HEADS, B_DH))
            outs["kp"].append(heads(k_p)); outs["vp"].append(heads(v_p))
            outs["ks"].append(heads(k_s)); outs["vs"].append(heads(v_s))
            w_out = b_w_out[j]
        xp, wts, cls = mix_ln_router(h_p, h_s, w_out.astype(BF16), x, ln1_g[layer], ln1_b[layer],
                                     r_w_group[layer], r_b_group[layer], r_w_expert[layer], r_b_expert[layer])
        y, w_down = moe(xp, wts, cls, e_w_gate, e_w_up, w_down, e_w_down, layer)
        if layer < DEPTH - 1:
            if (layer + 1) % 2 == 0:
                x, xb, g = residual_ln(xp, y, ln2_g[layer], ln2_b[layer], gate_w(layer + 1))
            else:
                x, xb = residual_ln(xp, y, ln2_g[layer], ln2_b[layer])
        else:
            y_p, y_s = residual_ln_split(xp, y, ln2_g[layer], ln2_b[layer], N_PROMPT)
    st = lambda k: jnp.stack(outs[k])
    return (y_p.reshape(BATCH, SEQ, D_MODEL), y_s.reshape(DEC_BATCH, DEC_SEQ, D_MODEL),
            st("Cp"), st("np"), st("mp"), st("kp"), st("vp"),
            st("Cs"), st("ns"), st("ms"), st("ks"), st("vs"))
```

```python
import functools

import jax
import jax.numpy as jnp
from jax import lax
from jax.experimental import pallas as pl
from jax.experimental.pallas import tpu as pltpu
from jax.experimental.pallas import tpu_sc as plsc

F32 = jnp.float32
BF16 = jnp.bfloat16

D_MODEL = 2048
BATCH = 8
SEQ = 4096
DEPTH = 4
DEC_BATCH = 32
DEC_SEQ = 32
PAST_LEN = 1024
CHUNK = 64
A_HEADS = 8
A_DK = 128
A_DV = D_MODEL // A_HEADS
GATE_CAP = 15.0
B_HEADS = 16
B_DH = D_MODEL // B_HEADS
PREV_CHUNKS = 8
REACH = PREV_CHUNKS * CHUNK
REL_CLIP = 256
N_GROUPS = 4
EXP_PER_GROUP = 4
N_EXPERTS = N_GROUPS * EXP_PER_GROUP
D_EXPERT = D_MODEL // 4
ALPHA = (2 * DEPTH) ** 0.25
LN_EPS = 1e-5
RMS_EPS = 1e-6

N_PROMPT = BATCH * SEQ
N_SAMPLE = DEC_BATCH * DEC_SEQ
N_TOK = N_PROMPT + N_SAMPLE

VMEM_LIMIT = 56 * 1024 * 1024
LANES = 128

PAIRS = ((0, 1), (0, 2), (0, 3), (1, 2), (1, 3), (2, 3))
N_CLASSES = N_GROUPS * len(PAIRS)
MOE_TILE = 256
SC_WORKERS = 32
MOE_TILES = -(-(N_TOK // MOE_TILE + N_CLASSES) // SC_WORKERS) * SC_WORKERS
MOE_ROWS = MOE_TILES * MOE_TILE

MLSTM_CHUNK = 128
ATT_TQ = 256
ATT_TK = ATT_TQ + REACH
NEG = -1e30

SC_TOKENS = 48
SC_FILL = 32


def _params(*sem):
    return pltpu.CompilerParams(dimension_semantics=sem, vmem_limit_bytes=VMEM_LIMIT)


def _mm_body(x_ref, w_ref, o_ref, wb):
    @pl.when(pl.program_id(1) == 0)
    def _():
        wb[...] = w_ref[...].astype(BF16)

    o_ref[...] = jnp.dot(x_ref[...], wb[...], preferred_element_type=F32).astype(o_ref.dtype)


def matmul(x, w_all, layer, m, out_dtype, tm=1536, tn=1024):
    n, k = x.shape
    return pl.pallas_call(
        _mm_body,
        grid=(m // tn, n // tm),
        in_specs=[pl.BlockSpec((tm, k), lambda j, i: (i, 0)),
                  pl.BlockSpec((None, k, tn), lambda j, i: (layer, 0, j))],
        out_specs=pl.BlockSpec((tm, tn), lambda j, i: (i, j)),
        out_shape=jax.ShapeDtypeStruct((n, m), out_dtype),
        scratch_shapes=[pltpu.VMEM((k, tn), BF16)],
        compiler_params=_params("parallel", "arbitrary"),
        name="matmul",
    )(x, w_all)


def _cast_body(w_ref, *rest):
    o_ref = rest[-1]
    o_ref[...] = w_ref[...].astype(BF16)


def cast_layer(w_all, layer, after=None):
    e, a, b = w_all.shape[1:]
    per_step = 2
    in_specs = [pl.BlockSpec((None, per_step, a, b), lambda i: (layer, i, 0, 0))]
    args = [w_all]
    if after is not None:
        in_specs.append(pl.BlockSpec(memory_space=pl.ANY))
        args.append(after)
    return pl.pallas_call(
        _cast_body,
        grid=(e // per_step,),
        in_specs=in_specs,
        out_specs=pl.BlockSpec((per_step, a, b), lambda i: (i, 0, 0)),
        out_shape=jax.ShapeDtypeStruct((e, a, b), BF16),
        compiler_params=_params("parallel"),
        name="cast_layer",
    )(*args)


def _split2(x):
    hi = x.astype(BF16)
    lo = (x - hi.astype(F32)).astype(BF16)
    return hi, lo


_NT = (((1,), (1,)), ((), ()))
_TN = (((0,), (0,)), ((), ()))


def _dot_hilo(x, w):
    t = x.shape[0]
    xh, xl = _split2(x)
    wh, wl = _split2(w)
    r = jnp.dot(jnp.concatenate([xh, xl], axis=0), jnp.concatenate([wh, wl], axis=1), preferred_element_type=F32)
    return r[:t, :LANES] + (r[:t, LANES:] + r[t:, :LANES]) + r[t:, LANES:]


def _log_sigmoid(x):
    return jnp.minimum(x, 0.0) - jnp.log(1.0 + jnp.exp(-jnp.abs(x)))


def _split3(x):
    a = x.astype(BF16)
    r = x - a.astype(F32)
    b = r.astype(BF16)
    c = (r - b.astype(F32)).astype(BF16)
    return a, b, c


A_STATE = A_DV + LANES


def _cummax_rows(x):
    n = x.shape[0]
    row = lax.broadcasted_iota(jnp.int32, x.shape, 0)
    k = 1
    while k < n:
        x = jnp.maximum(x, jnp.where(row >= k, pltpu.roll(x, k, axis=0), -jnp.inf))
        k *= 2
    return x


def _mlstm_body(q_ref, k_ref, v_ref, o_ref, g_ref, bias_ref, gain_ref, c0_ref, n0_ref, m0_ref,
                h_ref, c_out_ref, n_out_ref, m_ref, state, *, L, chunks):
    step = pl.program_id(1)

    @pl.when(step == 0)
    def _():
        state[:, :, :A_DV] = c0_ref[0]
        state[:, :, A_DV:] = n0_ref[0]
        m_ref[...] = m0_ref[...]

    for c in range(chunks):
        rows = pl.ds(c * L, L)
        _mlstm_chunk(q_ref.at[rows], k_ref.at[rows], v_ref.at[rows], o_ref.at[rows], g_ref.at[rows],
                     bias_ref, gain_ref, h_ref.at[rows], m_ref, state, L=L)

    @pl.when(step == pl.num_programs(1) - 1)
    def _():
        c_out_ref[0] = state[:, :, :A_DV]
        n_out_ref[0] = state[:, :, A_DV:]


def _mlstm_chunk(q_ref, k_ref, v_ref, o_ref, g_ref, bias_ref, gain_ref, h_ref, m_ref, state, *, L):
    dot = functools.partial(jnp.dot, preferred_element_type=F32)
    row = lax.broadcasted_iota(jnp.int32, (L, L), 0)
    col = lax.broadcasted_iota(jnp.int32, (L, L), 1)
    causal = col <= row
    tril = jnp.where(causal, 1.0, 0.0).astype(BF16)

    pre = GATE_CAP * jnp.tanh((g_ref[...] + bias_ref[...]) * (1.0 / GATE_CAP))
    ig = pre
    lf = _log_sigmoid(pltpu.roll(pre, LANES - A_HEADS, axis=1))
    a3, b3, c3 = _split3(lf)
    bt = dot(tril, a3) + (dot(tril, b3) + dot(tril, c3))
    a = ig - bt
    m_prev = m_ref[0]
    m_t = bt + jnp.maximum(m_prev, _cummax_rows(a))
    u = bt - m_t
    inter = jnp.exp(bt + m_prev - m_t)
    eminus = jnp.exp(-m_t)
    m_new = m_t[L - 1:L, :]
    bt_last = bt[L - 1:L, :]
    w_c = jnp.exp((bt_last - m_new) + a)
    decay = jnp.exp(bt_last + m_prev - m_new)
    m_ref[0] = m_new
    a_pad = a if L == LANES else jnp.concatenate([a, jnp.zeros((LANES - L, LANES), F32)], axis=0)
    a_t = jnp.transpose(a_pad)
    ie = jnp.concatenate([inter, eminus], axis=1).astype(BF16)
    sel_r = lax.broadcasted_iota(jnp.int32, (2 * LANES, 2 * LANES), 0)
    sel_c = lax.broadcasted_iota(jnp.int32, (2 * LANES, 2 * LANES), 1)
    same_half = (sel_r >= LANES) == (sel_c >= LANES)
    ones_l = jnp.ones((L, LANES), BF16)
    zeros_l = jnp.zeros((L, LANES), BF16)
    zeros_k = jnp.zeros((A_DK, LANES), BF16)
    mean_cols = jnp.full((A_DV, LANES), 1.0, BF16)

    scale = A_DK ** -0.5
    heads = range(A_HEADS)
    dk_sl = lambda h: slice(h * A_DK, (h + 1) * A_DK)
    dv_sl = lambda h: slice(h * A_DV, (h + 1) * A_DV)
    ie_b, qk = [], []
    for h in heads:
        sel = jnp.where(same_half & ((sel_r & (LANES - 1)) == h), 1.0, 0.0).astype(BF16)
        ie_b.append(dot(ie, sel))
        qk.append(lax.dot_general(q_ref[:, dk_sl(h)], k_ref[:, dk_sl(h)], _NT, preferred_element_type=F32))
    out = []
    for h in heads:
        d_mat = jnp.where(causal, jnp.exp(u[:, h:h + 1] + a_t[h:h + 1, :L]), 0.0)
        s = qk[h] * (d_mat * scale)
        q_i = (q_ref[:, dk_sl(h)].astype(F32) * ie_b[h][:, :LANES]).astype(BF16)
        c_b = state[h].astype(BF16)
        rhs = jnp.concatenate([
            jnp.concatenate([v_ref[:, dv_sl(h)], ones_l, zeros_l], axis=1),
            jnp.concatenate([c_b[:, :A_DV], zeros_k, c_b[:, A_DV:]], axis=1)], axis=0)
        out.append(dot(jnp.concatenate([s.astype(BF16), q_i], axis=1), rhs))
    hh, ms = [], []
    for h in heads:
        den = jnp.maximum(jnp.abs(out[h][:, A_DV:A_DV + LANES] + out[h][:, A_DV + LANES:]), ie_b[h][:, LANES:])
        r = 1.0 / den
        hh.append(out[h][:, :A_DV] * jnp.concatenate([r, r], axis=1))
        ms.append(dot((hh[h] * hh[h]).astype(BF16), mean_cols))
    upd = []
    for h in heads:
        rs = lax.rsqrt(ms[h] * (1.0 / A_DV) + RMS_EPS)
        og = o_ref[:, dv_sl(h)].astype(F32)
        y = hh[h] * jnp.concatenate([rs, rs], axis=1) * gain_ref[:, dv_sl(h)] * jax.nn.sigmoid(og)
        h_ref[:, dv_sl(h)] = y.astype(h_ref.dtype)
        wk = (k_ref[:, dk_sl(h)].astype(F32) * (w_c[:, h:h + 1] * scale)).astype(BF16)
        upd.append(lax.dot_general(wk, jnp.concatenate([v_ref[:, dv_sl(h)], ones_l], axis=1), _TN,
                                   preferred_element_type=F32))
    for h in heads:
        state[h] = decay[:, h:h + 1] * state[h] + upd[h]


def mlstm(proj, g, b_gate, gain, c0, n0, m0, *, batch, seq, L, row0):
    chunks = 2 if seq % (2 * L) == 0 else 1
    rows = chunks * L
    nc = seq // rows
    bias = jnp.pad(b_gate.astype(F32), (0, LANES - 2 * A_HEADS))
    row_blk = lambda b, c: row0 // rows + b * nc + c
    st = lambda *s: pl.BlockSpec((1,) + s, lambda b, c: (b,) + (0,) * len(s))
    n0_lanes = jnp.broadcast_to(n0[..., None], n0.shape + (LANES,))
    m_lanes = jnp.pad(m0, ((0, 0), (0, LANES - A_HEADS)))[:, None, :]
    h, c, n, m = pl.pallas_call(
        functools.partial(_mlstm_body, L=L, chunks=chunks),
        grid=(batch, nc),
        in_specs=[pl.BlockSpec((rows, A_HEADS * A_DK), lambda b, c: (row_blk(b, c), 0)),
                  pl.BlockSpec((rows, A_HEADS * A_DK), lambda b, c: (row_blk(b, c), 1)),
                  pl.BlockSpec((rows, D_MODEL), lambda b, c: (row_blk(b, c), 1)),
                  pl.BlockSpec((rows, D_MODEL), lambda b, c: (row_blk(b, c), 2)),
                  pl.BlockSpec((rows, LANES), lambda b, c: (row_blk(b, c), 0)),
                  pl.BlockSpec((1, LANES), lambda b, c: (0, 0)),
                  pl.BlockSpec((1, D_MODEL), lambda b, c: (0, 0)),
                  st(A_HEADS, A_DK, A_DV), st(A_HEADS, A_DK, LANES), st(1, LANES)],
        out_specs=[pl.BlockSpec((rows, D_MODEL), lambda b, c: (b * nc + c, 0)),
                   st(A_HEADS, A_DK, A_DV), st(A_HEADS, A_DK, LANES), st(1, LANES)],
        out_shape=[jax.ShapeDtypeStruct((batch * seq, D_MODEL), BF16),
                   jax.ShapeDtypeStruct((batch, A_HEADS, A_DK, A_DV), F32),
                   jax.ShapeDtypeStruct((batch, A_HEADS, A_DK, LANES), F32),
                   jax.ShapeDtypeStruct((batch, 1, LANES), F32)],
        scratch_shapes=[pltpu.VMEM((A_HEADS, A_DK, A_STATE), F32)],
        compiler_params=_params("parallel", "arbitrary"),
        name="mlstm",
    )(proj, proj, proj, proj, g, bias[None, :], gain.astype(F32)[None, :], c0, n0_lanes, m_lanes)
    return h, c, n[..., 0], m[:, 0, :A_HEADS]


LOG2E = 1.4426950408889634
Q_SCALE = B_DH ** -0.5 * LOG2E


def _scale_q(q):
    return (q.astype(F32) * Q_SCALE).astype(BF16)


def _attn_prompt_body(q_ref, k_ref, v_ref, bias_ref, o_ref, kt_ref, vt_ref, qs, kpad, vpad):
    kt_ref[0] = k_ref[SEQ - REACH:, :].astype(F32)
    vt_ref[0] = v_ref[SEQ - REACH:, :].astype(F32)
    qs[...] = _scale_q(q_ref[...])
    kpad[0:REACH, :] = jnp.zeros((REACH, B_DH), BF16)
    kpad[REACH:, :] = k_ref[...]
    vpad[0:REACH, 0:B_DH] = jnp.zeros((REACH, B_DH), BF16)
    vpad[REACH:, 0:B_DH] = v_ref[...]
    vpad[:, B_DH:] = jnp.ones((SEQ + REACH, B_DH), BF16)

    def scores(t):
        r0 = t * ATT_TQ
        kb = kpad[r0:r0 + ATT_TK, :]
        s = lax.dot_general(qs[r0:r0 + ATT_TQ, :], kb, _NT, preferred_element_type=F32) + bias_ref[0]
        if r0 < REACH:
            j = lax.broadcasted_iota(jnp.int32, (ATT_TQ, ATT_TK), 1)
            s = jnp.where(j + r0 >= REACH, s, NEG)
        return s

    def finish(t, s):
        r0 = t * ATT_TQ
        p = jnp.exp2(s - jnp.max(s, axis=-1, keepdims=True)).astype(BF16)
        acc = jnp.dot(p, vpad[r0:r0 + ATT_TK, :], preferred_element_type=F32)
        o_ref[r0:r0 + ATT_TQ, :] = (acc[:, :B_DH] / acc[:, B_DH:]).astype(o_ref.dtype)

    n_tiles = SEQ // ATT_TQ
    s_next = scores(0)
    for t in range(n_tiles):
        s_cur = s_next
        if t + 1 < n_tiles:
            s_next = scores(t + 1)
        finish(t, s_cur)


def band_bias(rel_table, nq, nk):
    w = nq + nk
    rel = jnp.clip(REACH + nq - 1 - jnp.arange(w), -REL_CLIP, REL_CLIP) + REL_CLIP
    r = rel_table.astype(F32)[:, rel]
    heads = r.shape[0]
    skew = jnp.broadcast_to(r[:, None, :], (heads, nq, w)).reshape(heads, nq * w)
    skew = skew[:, :nq * (w - 1)].reshape(heads, nq, w - 1)
    bias = skew[:, :, nq - 1:nq - 1 + nk]
    i = jnp.arange(nq)[:, None]
    j = jnp.arange(nk)[None, :]
    dc = j // CHUNK - i // CHUNK
    allowed = (dc >= 0) & (dc <= PREV_CHUNKS)
    return jnp.where(allowed[None], bias * LOG2E, NEG)


def attn_prompt(qkv, rel_table):
    bias = band_bias(rel_table, ATT_TQ, ATT_TK)
    tail = pl.BlockSpec((1, REACH, B_DH), lambda b, h: (b, 0, h))
    return pl.pallas_call(
        _attn_prompt_body,
        grid=(BATCH, B_HEADS),
        in_specs=[pl.BlockSpec((SEQ, B_DH), lambda b, h: (b, h)),
                  pl.BlockSpec((SEQ, B_DH), lambda b, h: (b, B_HEADS + h)),
                  pl.BlockSpec((SEQ, B_DH), lambda b, h: (b, 2 * B_HEADS + h)),
                  pl.BlockSpec((1, ATT_TQ, ATT_TK), lambda b, h: (h, 0, 0))],
        out_specs=[pl.BlockSpec((SEQ, B_DH), lambda b, h: (b, h)), tail, tail],
        out_shape=[jax.ShapeDtypeStruct((N_PROMPT, D_MODEL), BF16),
                   jax.ShapeDtypeStruct((BATCH, REACH, D_MODEL), F32),
                   jax.ShapeDtypeStruct((BATCH, REACH, D_MODEL), F32)],
        scratch_shapes=[pltpu.VMEM((SEQ, B_DH), BF16),
                        pltpu.VMEM((SEQ + REACH, B_DH), BF16),
                        pltpu.VMEM((SEQ + REACH, 2 * B_DH), BF16)],
        compiler_params=_params("parallel", "parallel"),
        name="attn_prompt",
    )(qkv, qkv, qkv, bias)


def _attn_sample_body(q_ref, kn_ref, vn_ref, ck_ref, cv_ref, bias_c_ref, bias_n_ref, o_ref, ks_ref, vs_ref):
    ks_ref[0] = kn_ref[...].astype(F32)
    vs_ref[0] = vn_ref[...].astype(F32)
    for h in range(B_HEADS):
        sl = slice(h * B_DH, (h + 1) * B_DH)
        q = _scale_q(q_ref[:, sl])
        kc = ck_ref[pl.ds(h, REACH, stride=B_HEADS), :].astype(BF16)
        vc = cv_ref[pl.ds(h, REACH, stride=B_HEADS), :].astype(BF16)
        s_c = lax.dot_general(q, kc, _NT, preferred_element_type=F32) + bias_c_ref[h]
        s_n = lax.dot_general(q, kn_ref[:, sl], _NT, preferred_element_type=F32) + bias_n_ref[h]
        m = jnp.maximum(jnp.max(s_c, axis=-1, keepdims=True), jnp.max(s_n, axis=-1, keepdims=True))
        p_c = jnp.exp2(s_c - m)
        p_n = jnp.exp2(s_n - m)
        l = jnp.sum(p_c, axis=-1, keepdims=True) + jnp.sum(p_n, axis=-1, keepdims=True)
        dot = functools.partial(jnp.dot, preferred_element_type=F32)
        o_ref[:, sl] = ((dot(p_c.astype(BF16), vc) + dot(p_n.astype(BF16), vn_ref[:, sl])) / l).astype(o_ref.dtype)


def attn_sample(qkv, cache_k, cache_v, layer, rel_table):
    bias = band_bias(rel_table, DEC_SEQ, REACH + DEC_SEQ)
    cache_rows = REACH * B_HEADS
    ck = cache_k.reshape(cache_k.shape[0], DEC_BATCH, cache_rows, B_DH)
    cv = cache_v.reshape(cache_v.shape[0], DEC_BATCH, cache_rows, B_DH)
    row = lambda c: pl.BlockSpec((DEC_SEQ, D_MODEL), lambda b: (N_PROMPT // DEC_SEQ + b, c))
    cache = pl.BlockSpec((None, None, cache_rows, B_DH), lambda b: (layer, b, 0, 0))
    new = pl.BlockSpec((1, DEC_SEQ, D_MODEL), lambda b: (b, 0, 0))
    return pl.pallas_call(
        _attn_sample_body,
        grid=(DEC_BATCH,),
        in_specs=[row(0), row(1), row(2), cache, cache,
                  pl.BlockSpec((B_HEADS, DEC_SEQ, REACH), lambda b: (0, 0, 0)),
                  pl.BlockSpec((B_HEADS, DEC_SEQ, DEC_SEQ), lambda b: (0, 0, 0))],
        out_specs=[pl.BlockSpec((DEC_SEQ, D_MODEL), lambda b: (b, 0)), new, new],
        out_shape=[jax.ShapeDtypeStruct((N_SAMPLE, D_MODEL), BF16),
                   jax.ShapeDtypeStruct((DEC_BATCH, DEC_SEQ, D_MODEL), F32),
                   jax.ShapeDtypeStruct((DEC_BATCH, DEC_SEQ, D_MODEL), F32)],
        compiler_params=_params("parallel"),
        name="attn_sample",
    )(qkv, qkv, qkv, ck, cv, bias[:, :, :REACH], bias[:, :, REACH:])


def _ln(z, g, b):
    mu = jnp.mean(z, axis=-1, keepdims=True)
    zc = z - mu
    var = jnp.mean(zc * zc, axis=-1, keepdims=True)
    return zc * lax.rsqrt(var + LN_EPS) * g + b


PACK_ROWS = D_MODEL // 2 // LANES
HI_MASK = -65536


def _store_packed(ref, v, token0=0):
    t = v.shape[0]
    half = D_MODEL // 2
    hi = lax.bitcast_convert_type(v[:, :half].astype(BF16).astype(F32), jnp.int32)
    lo = lax.bitcast_convert_type(v[:, half:].astype(BF16).astype(F32), jnp.int32)
    words = (hi & HI_MASK) | lax.shift_right_logical(lo, 16)
    for a in range(PACK_ROWS):
        ref[pl.ds(token0 * PACK_ROWS + a, t, stride=PACK_ROWS), :] = words[:, a * LANES:(a + 1) * LANES]


def _load_packed(ref, t):
    words = jnp.concatenate([ref[pl.ds(a, t, stride=PACK_ROWS), :] for a in range(PACK_ROWS)], axis=1)
    hi = lax.bitcast_convert_type(words & HI_MASK, F32)
    lo = lax.bitcast_convert_type(lax.shift_left(words, 16), F32)
    return jnp.concatenate([hi, lo], axis=1)


def _mix_ln_router_body(hp_ref, hs_ref, w_ref, x_ref, g_ref, b_ref, rw_ref, rb_ref,
                        xp_ref, wt_ref, cls_ref, y_even, y_odd, *, prompt_tiles, n_tiles):
    i = pl.program_id(0)
    from_prompt = jnp.minimum(i, n_tiles - 1) < prompt_tiles

    @pl.when(i == 0)
    def _():
        y_odd[...] = jnp.zeros_like(y_odd)

    half = D_MODEL // 2

    def run(y_new, y_old):
        h = jnp.where(from_prompt, hp_ref[...], hs_ref[...])
        y_new[:, :half] = jnp.dot(h, w_ref[:, :half], preferred_element_type=F32)
        xn = _ln(ALPHA * x_ref[...] + y_old[...], g_ref[...], b_ref[...])
        logits = _dot_hilo(xn, rw_ref[...]) + rb_ref[...]
        y_new[:, half:] = jnp.dot(h, w_ref[:, half:], preferred_element_type=F32)
        _store_packed(xp_ref, xn)
        _route(logits, wt_ref, cls_ref)

    @pl.when(i % 2 == 0)
    def _():
        run(y_even, y_odd)

    @pl.when(i % 2 == 1)
    def _():
        run(y_odd, y_even)


def mix_ln_router(h_p, h_s, w, x, g, b, w_group, b_group, w_expert, b_expert, tm=256):
    n = x.shape[0]
    n_tiles = n // tm
    prompt_tiles = h_p.shape[0] // tm
    sample_tiles = h_s.shape[0] // tm
    assert prompt_tiles * tm == h_p.shape[0] and (prompt_tiles + sample_tiles) * tm == n
    mm_tile = lambda i: jnp.minimum(i, n_tiles - 1)
    ep_tile = lambda i: jnp.maximum(i - 1, 0)
    hp_spec = pl.BlockSpec((tm, D_MODEL), lambda i: (jnp.minimum(mm_tile(i), prompt_tiles - 1), 0))
    hs_spec = pl.BlockSpec((tm, D_MODEL), lambda i: (jnp.maximum(mm_tile(i) - prompt_tiles, 0), 0))
    rw = jnp.pad(jnp.concatenate([w_group, w_expert], axis=1), ((0, 0), (0, LANES - N_GROUPS - N_EXPERTS)))
    rb = jnp.pad(jnp.concatenate([b_group, b_expert]).astype(F32), (0, LANES - N_GROUPS - N_EXPERTS))
    row = pl.BlockSpec((tm, D_MODEL), lambda i: (ep_tile(i), 0))
    vec = pl.BlockSpec((1, D_MODEL), lambda i: (0, 0))
    lane_row = pl.BlockSpec((tm, LANES), lambda i: (ep_tile(i), 0))
    return pl.pallas_call(
        functools.partial(_mix_ln_router_body, prompt_tiles=prompt_tiles, n_tiles=n_tiles),
        grid=(n_tiles + 1,),
        in_specs=[hp_spec, hs_spec, pl.BlockSpec((D_MODEL, D_MODEL), lambda i: (0, 0)), row, vec, vec,
                  pl.BlockSpec((D_MODEL, LANES), lambda i: (0, 0)), pl.BlockSpec((1, LANES), lambda i: (0, 0))],
        out_specs=[pl.BlockSpec((tm * PACK_ROWS, LANES), lambda i: (ep_tile(i), 0)), lane_row, lane_row],
        out_shape=[jax.ShapeDtypeStruct((n * PACK_ROWS, LANES), jnp.int32),
                   jax.ShapeDtypeStruct((n, LANES), F32), jax.ShapeDtypeStruct((n, LANES), jnp.int32)],
        scratch_shapes=[pltpu.VMEM((tm, D_MODEL), F32), pltpu.VMEM((tm, D_MODEL), F32)],
        compiler_params=_params("arbitrary"),
        name="mix_ln_router",
    )(h_p, h_s, w, x, g[None, :], b[None, :], rw, rb[None, :])


def _add_ln_body(x_ref, y_ref, g_ref, b_ref, *rest):
    t = rest[-1].shape[0]
    xn = _ln(ALPHA * _load_packed(x_ref, t) + _load_packed(y_ref, t), g_ref[...], b_ref[...])
    if len(rest) == 2:
        xo_ref, xbo_ref = rest
    else:
        wg_ref, xo_ref, xbo_ref, go_ref = rest
        go_ref[...] = _dot_hilo(xn, wg_ref[...])
    xo_ref[...] = xn
    xbo_ref[...] = xn.astype(BF16)


def residual_ln(x_packed, y_packed, g, b, w_gate=None, tm=512):
    n = x_packed.shape[0] // PACK_ROWS
    row = pl.BlockSpec((tm, D_MODEL), lambda i: (i, 0))
    vec = pl.BlockSpec((1, D_MODEL), lambda i: (0, 0))
    packed = pl.BlockSpec((tm * PACK_ROWS, LANES), lambda i: (i, 0))
    in_specs = [packed, packed, vec, vec]
    out_specs = [row, row]
    out_shape = [jax.ShapeDtypeStruct((n, D_MODEL), F32), jax.ShapeDtypeStruct((n, D_MODEL), BF16)]
    args = [x_packed, y_packed, g[None, :], b[None, :]]
    if w_gate is not None:
        in_specs.append(pl.BlockSpec((D_MODEL, LANES), lambda i: (0, 0)))
        out_specs.append(pl.BlockSpec((tm, LANES), lambda i: (i, 0)))
        out_shape.append(jax.ShapeDtypeStruct((n, LANES), F32))
        args.append(jnp.pad(w_gate, ((0, 0), (0, LANES - w_gate.shape[1]))))
    return pl.pallas_call(
        _add_ln_body,
        grid=(n // tm,),
        in_specs=in_specs,
        out_specs=out_specs,
        out_shape=out_shape,
        compiler_params=_params("parallel"),
        name="residual_ln",
    )(*args)


def _embed_body(xp_ref, xs_ref, wg_ref, xo_ref, xbo_ref, go_ref, *, prompt_tiles):
    x = jnp.where(pl.program_id(0) < prompt_tiles, xp_ref[...], xs_ref[...])
    xo_ref[...] = x
    xbo_ref[...] = x.astype(BF16)
    go_ref[...] = _dot_hilo(x, wg_ref[...])


def embed(x_p, x_s, w_gate, tm=512):
    n = x_p.shape[0] + x_s.shape[0]
    prompt_tiles = x_p.shape[0] // tm
    assert prompt_tiles * tm == x_p.shape[0] and n % tm == 0
    row = pl.BlockSpec((tm, D_MODEL), lambda i: (i, 0))
    wg = jnp.pad(w_gate, ((0, 0), (0, LANES - w_gate.shape[1])))
    return pl.pallas_call(
        functools.partial(_embed_body, prompt_tiles=prompt_tiles),
        grid=(n // tm,),
        in_specs=[pl.BlockSpec((tm, D_MODEL), lambda i: (jnp.minimum(i, prompt_tiles - 1), 0)),
                  pl.BlockSpec((tm, D_MODEL), lambda i: (jnp.maximum(i - prompt_tiles, 0), 0)),
                  pl.BlockSpec((D_MODEL, LANES), lambda i: (0, 0))],
        out_specs=[row, row, pl.BlockSpec((tm, LANES), lambda i: (i, 0))],
        out_shape=[jax.ShapeDtypeStruct((n, D_MODEL), F32), jax.ShapeDtypeStruct((n, D_MODEL), BF16),
                   jax.ShapeDtypeStruct((n, LANES), F32)],
        compiler_params=_params("parallel"),
        name="embed",
    )(x_p, x_s, wg)


def _add_ln_split_body(x_ref, y_ref, g_ref, b_ref, op_ref, os_ref, *, prompt_tiles):
    t = op_ref.shape[0]
    xn = _ln(ALPHA * _load_packed(x_ref, t) + _load_packed(y_ref, t), g_ref[...], b_ref[...])
    i = pl.program_id(0)

    @pl.when(i < prompt_tiles)
    def _():
        op_ref[...] = xn

    @pl.when(i >= prompt_tiles)
    def _():
        os_ref[...] = xn


def residual_ln_split(x_packed, y_packed, g, b, n_prompt, tm=512):
    n = x_packed.shape[0] // PACK_ROWS
    prompt_tiles = n_prompt // tm
    assert prompt_tiles * tm == n_prompt and n % tm == 0
    vec = pl.BlockSpec((1, D_MODEL), lambda i: (0, 0))
    packed = pl.BlockSpec((tm * PACK_ROWS, LANES), lambda i: (i, 0))
    return pl.pallas_call(
        functools.partial(_add_ln_split_body, prompt_tiles=prompt_tiles),
        grid=(n // tm,),
        in_specs=[packed, packed, vec, vec],
        out_specs=[pl.BlockSpec((tm, D_MODEL), lambda i: (jnp.minimum(i, prompt_tiles - 1), 0)),
                   pl.BlockSpec((tm, D_MODEL), lambda i: (jnp.maximum(i - prompt_tiles, 0), 0))],
        out_shape=[jax.ShapeDtypeStruct((n_prompt, D_MODEL), F32),
                   jax.ShapeDtypeStruct((n - n_prompt, D_MODEL), F32)],
        compiler_params=_params("arbitrary"),
        name="residual_ln_split",
    )(x_packed, y_packed, g[None, :], b[None, :])


ROUTER_E0 = N_GROUPS


def _route(logits, wt_ref, cls_ref):
    lane = lax.broadcasted_iota(jnp.int32, logits.shape, 1).astype(F32)
    big = float(LANES)
    rmax = lambda a: jnp.max(a, axis=-1, keepdims=True)
    rmin = lambda a: jnp.min(a, axis=-1, keepdims=True)
    is_g = lane < N_GROUPS
    gl = jnp.where(is_g, logits, -jnp.inf)
    gmax = rmax(gl)
    gsel = rmin(jnp.where(gl == gmax, lane, big))
    gsum = jnp.sum(jnp.where(is_g, jnp.exp(logits - gmax), 0.0), axis=-1, keepdims=True)
    g_w = 1.0 / gsum
    e_lo = ROUTER_E0 + EXP_PER_GROUP * gsel
    in_grp = (lane >= e_lo) & (lane < e_lo + EXP_PER_GROUP)
    el = jnp.where(in_grp, logits, -jnp.inf)
    e1 = rmax(el)
    i1 = rmin(jnp.where(el == e1, lane, big))
    el2 = jnp.where(lane == i1, -jnp.inf, el)
    e2 = rmax(el2)
    i2 = rmin(jnp.where(el2 == e2, lane, big))
    t = jnp.exp(e2 - e1)
    p1 = 1.0 / (1.0 + t)
    w1 = p1 * g_w
    w2 = (t * p1) * g_w
    a1 = i1 - e_lo
    a2 = i2 - e_lo
    first_low = a1 < a2
    lo = jnp.where(first_low, a1, a2)
    hi = jnp.where(first_low, a2, a1)
    w_lo = jnp.where(first_low, w1, w2)
    w_hi = jnp.where(first_low, w2, w1)
    off = jnp.where(lo == 0.0, 0.0, jnp.where(lo == 1.0, 3.0, 5.0))
    cls = gsel * float(len(PAIRS)) + off + (hi - lo - 1.0)
    wt_ref[...] = jnp.where(lane < LANES // 2, w_lo, w_hi)
    cls_ref[...] = jnp.broadcast_to(cls, logits.shape).astype(jnp.int32)


def _expert_body(ea_ref, eb_ref, valid_ref, x_ref, wt_ref, wga_ref, wua_ref, wda_ref,
                 wgb_ref, wub_ref, wdb_ref, y_ref):
    i = pl.program_id(0)

    @pl.when(valid_ref[i] == 0)
    def _():
        y_ref[...] = jnp.zeros_like(y_ref)

    @pl.when(valid_ref[i] != 0)
    def _():
        x = _load_packed(x_ref, MOE_TILE).astype(BF16)
        dot = functools.partial(jnp.dot, preferred_element_type=F32)

        ga, ua = dot(x, wga_ref[...]), dot(x, wua_ref[...])
        gb, ub = dot(x, wgb_ref[...]), dot(x, wub_ref[...])
        ya = dot((jax.nn.silu(ga) * ua).astype(BF16), wda_ref[...])
        yb = dot((jax.nn.silu(gb) * ub).astype(BF16), wdb_ref[...])
        w_lo = wt_ref[:, 0:1]
        w_hi = wt_ref[:, LANES // 2:LANES // 2 + 1]
        _store_packed(y_ref, w_lo * ya + w_hi * yb)


def experts(xs, wts, tile_ea, tile_eb, tile_valid, w_gate, w_up, w_down):
    wspec = lambda shape, which: pl.BlockSpec(
        (None,) + shape, lambda i, ea, eb, va: ((ea, eb)[which][i], 0, 0))
    up = (D_MODEL, D_EXPERT)
    down = (D_EXPERT, D_MODEL)
    packed = pl.BlockSpec((MOE_TILE * PACK_ROWS, LANES), lambda i, ea, eb, va: (i, 0))
    grid_spec = pltpu.PrefetchScalarGridSpec(
        num_scalar_prefetch=3,
        grid=(MOE_TILES,),
        in_specs=[packed,
                  pl.BlockSpec((MOE_TILE, LANES), lambda i, ea, eb, va: (i, 0)),
                  wspec(up, 0), wspec(up, 0), wspec(down, 0),
                  wspec(up, 1), wspec(up, 1), wspec(down, 1)],
        out_specs=packed,
    )
    return pl.pallas_call(
        _expert_body,
        grid_spec=grid_spec,
        out_shape=jax.ShapeDtypeStruct((MOE_ROWS * PACK_ROWS, LANES), jnp.int32),
        compiler_params=_params("arbitrary"),
        name="experts",
    )(tile_ea, tile_eb, tile_valid, xs, wts, w_gate, w_up, w_down, w_gate, w_up, w_down)


def sc_dispatch(xp, wts, pos, fill_rows):
    n = pos.shape[0]
    n_fill = fill_rows.shape[0]
    info = plsc.get_sparse_core_info()
    n_workers = info.num_cores * info.num_subcores
    assert n_workers == SC_WORKERS
    per_worker = n // n_workers
    steps = per_worker // SC_TOKENS
    fill_steps = n_fill // n_workers // SC_FILL
    assert steps * SC_TOKENS * n_workers == n and steps % 2 == 0
    assert fill_steps * SC_FILL * n_workers == n_fill
    mesh = plsc.VectorSubcoreMesh(core_axis_name="core", subcore_axis_name="subcore")
    dma = pltpu.SemaphoreType.DMA

    @functools.partial(
        pl.kernel,
        out_type=[jax.ShapeDtypeStruct((n + n_fill, PACK_ROWS, LANES), xp.dtype),
                  jax.ShapeDtypeStruct((n + n_fill, LANES), wts.dtype)],
        mesh=mesh,
        scratch_types=[pltpu.VMEM((steps, SC_TOKENS), jnp.int32),
                       pltpu.VMEM((fill_steps, SC_FILL), jnp.int32),
                       pltpu.VMEM((2, SC_TOKENS, PACK_ROWS, LANES), xp.dtype),
                       pltpu.VMEM((2, SC_TOKENS, LANES), wts.dtype),
                       dma((2,)), dma((2,)), dma((2,)), dma((2,))],
    )
    def scatter(x_hbm, w_hbm, pos_hbm, fill_hbm, zx_hbm, zw_hbm, xs_hbm, ws_hbm,
                pos_v, fill_v, xbuf, wbuf, read_x, read_w, write_x, write_w):
        worker = lax.axis_index("subcore") * info.num_cores + lax.axis_index("core")
        base = worker * per_worker
        pltpu.sync_copy(pos_hbm.at[worker], pos_v)
        pltpu.sync_copy(fill_hbm.at[worker], fill_v)

        def reads(s, b):
            rows = pl.ds(base + s * SC_TOKENS, SC_TOKENS)
            return (pltpu.make_async_copy(x_hbm.at[rows], xbuf.at[b], read_x.at[b]),
                    pltpu.make_async_copy(w_hbm.at[rows], wbuf.at[b], read_w.at[b]))

        def writes(s, b):
            return (pltpu.make_async_copy(xbuf.at[b], xs_hbm.at[pos_v.at[s]], write_x.at[b]),
                    pltpu.make_async_copy(wbuf.at[b], ws_hbm.at[pos_v.at[s]], write_w.at[b]))

        def start(copies):
            for c in copies:
                c.start()

        def wait(copies):
            for c in copies:
                c.wait()

        start(reads(0, 0))

        @pl.loop(0, steps, step=2)
        def _(s):
            wait(reads(s, 0))
            start(reads(s + 1, 1))
            start(writes(s, 0))
            wait(reads(s + 1, 1))
            wait(writes(s, 0))

            @pl.when(s + 2 < steps)
            def _():
                start(reads(s + 2, 0))

            start(writes(s + 1, 1))
            wait(writes(s + 1, 1))

        zx = xbuf.at[0, pl.ds(0, SC_FILL)]
        zw = wbuf.at[0, pl.ds(0, SC_FILL)]
        pltpu.sync_copy(zx_hbm, zx)
        pltpu.sync_copy(zw_hbm, zw)
        fills = [(pltpu.make_async_copy(zx, xs_hbm.at[fill_v.at[s]], write_x.at[0]),
                  pltpu.make_async_copy(zw, ws_hbm.at[fill_v.at[s]], write_w.at[0])) for s in range(fill_steps)]
        for f in fills:
            start(f)
        for f in fills:
            wait(f)

    xs, ws = scatter(xp.reshape(n, PACK_ROWS, LANES), wts,
                     pos.reshape(n_workers, steps, SC_TOKENS), fill_rows.reshape(n_workers, fill_steps, SC_FILL),
                     jnp.zeros((SC_FILL, PACK_ROWS, LANES), xp.dtype), jnp.zeros((SC_FILL, LANES), wts.dtype))
    return xs.reshape((n + n_fill) * PACK_ROWS, LANES), ws


def sc_gather_tokens(x, idx):
    n = x.shape[0] // PACK_ROWS
    p = idx.shape[0]
    info = plsc.get_sparse_core_info()
    n_workers = info.num_cores * info.num_subcores
    assert n_workers == SC_WORKERS
    per_worker = p // n_workers
    steps = per_worker // SC_TOKENS
    assert steps * SC_TOKENS * n_workers == p and steps % 2 == 0
    mesh = plsc.VectorSubcoreMesh(core_axis_name="core", subcore_axis_name="subcore")
    dma = pltpu.SemaphoreType.DMA

    @functools.partial(
        pl.kernel,
        out_type=jax.ShapeDtypeStruct((p, PACK_ROWS, LANES), x.dtype),
        mesh=mesh,
        scratch_types=[pltpu.VMEM((steps, SC_TOKENS), jnp.int32),
                       pltpu.VMEM((2, SC_TOKENS, PACK_ROWS, LANES), x.dtype),
                       dma((2,)), dma((2,))],
    )
    def gather(x_hbm, i_hbm, o_hbm, idx_v, buf, read_sem, write_sem):
        worker = lax.axis_index("subcore") * info.num_cores + lax.axis_index("core")
        base = worker * per_worker
        pltpu.sync_copy(i_hbm.at[worker], idx_v)

        def read(s, b):
            return pltpu.make_async_copy(x_hbm.at[idx_v.at[s]], buf.at[b], read_sem.at[b])

        def write(s, b):
            return pltpu.make_async_copy(buf.at[b], o_hbm.at[pl.ds(base + s * SC_TOKENS, SC_TOKENS)], write_sem.at[b])

        read(0, 0).start()

        @pl.loop(0, steps, step=2)
        def _(s):
            read(s, 0).wait()
            read(s + 1, 1).start()
            write(s, 0).start()
            read(s + 1, 1).wait()
            write(s, 0).wait()

            @pl.when(s + 2 < steps)
            def _():
                read(s + 2, 0).start()

            write(s + 1, 1).start()
            write(s + 1, 1).wait()

    out = gather(x.reshape(n, PACK_ROWS, LANES), idx.reshape(n_workers, steps, SC_TOKENS))
    return out.reshape(p * PACK_ROWS, LANES)


_CLASS_EA = [g * EXP_PER_GROUP + lo for g in range(N_GROUPS) for lo, hi in PAIRS]
_CLASS_EB = [g * EXP_PER_GROUP + hi for g in range(N_GROUPS) for lo, hi in PAIRS]


def dispatch_plan(cls):
    onehot = (cls[:, None] == jnp.arange(N_CLASSES, dtype=jnp.int32)[None, :]).astype(jnp.int32)
    csum = jnp.cumsum(onehot, axis=0)
    rank = jnp.sum(csum * onehot, axis=1) - 1
    counts = csum[-1]
    tiles_per = (counts + MOE_TILE - 1) // MOE_TILE
    tile_end = jnp.cumsum(tiles_per)
    tile_start = tile_end - tiles_per
    pos = jnp.sum(onehot * tile_start[None, :], axis=1) * MOE_TILE + rank
    n = cls.shape[0]
    pad = tiles_per * MOE_TILE - counts
    pad_end = jnp.cumsum(pad)
    k = jnp.arange(MOE_ROWS - n, dtype=jnp.int32)
    k_cls = jnp.sum((k[:, None] >= pad_end[None, :]).astype(jnp.int32), axis=1)
    k_hot = (k_cls[:, None] == jnp.arange(N_CLASSES, dtype=jnp.int32)[None, :]).astype(jnp.int32)
    pad_row0 = tile_start * MOE_TILE + counts - (pad_end - pad)
    fill_rows = k + jnp.where(k_cls < N_CLASSES, jnp.sum(k_hot * pad_row0[None, :], axis=1),
                              tile_end[-1] * MOE_TILE - pad_end[-1])
    t = jnp.arange(MOE_TILES, dtype=jnp.int32)
    tile_cls = jnp.sum((t[:, None] >= tile_end[None, :]).astype(jnp.int32), axis=1)
    valid = (tile_cls < N_CLASSES).astype(jnp.int32)
    last_cls = jnp.max(jnp.where(counts > 0, jnp.arange(N_CLASSES, dtype=jnp.int32), 0))
    tile_cls = jnp.where(valid == 1, tile_cls, last_cls)
    ea = jnp.asarray(_CLASS_EA, jnp.int32)[tile_cls]
    eb = jnp.asarray(_CLASS_EB, jnp.int32)[tile_cls]
    return pos, fill_rows, ea, eb, valid


def moe(xp, wts, cls, e_w_gate, e_w_up, w_down, e_w_down, layer):
    pos, fill_rows, ea, eb, valid = dispatch_plan(cls[:, 0])
    xs, ws = sc_dispatch(xp, wts, pos, fill_rows)
    w_gate, w_up = cast_layer(e_w_gate, layer, after=pos), cast_layer(e_w_up, layer, after=pos)
    ys = experts(xs, ws, ea, eb, valid, w_gate, w_up, w_down)
    y = sc_gather_tokens(ys, pos)
    if layer + 1 == DEPTH:
        return y, None
    return lax.optimization_barrier((y, cast_layer(e_w_down, layer + 1, after=ys)))


def _rows(a):
    return a.reshape(-1, a.shape[-1])


def kernel(x_prompt, x_sample, state_C, state_n, state_m, cache_k, cache_v, a_w_in, a_b_gate, a_norm, a_w_out, b_w_in, b_rel, b_w_out, ln1_g, ln1_b, ln2_g, ln2_b, r_w_group, r_b_group, r_w_expert, r_b_expert, e_w_gate, e_w_up, e_w_down):
    hk = A_HEADS * A_DK
    n_main = 2 * hk + 2 * D_MODEL
    gate_w = lambda layer: a_w_in[layer // 2, :, n_main:]
    x, xb, g = embed(_rows(x_prompt), _rows(x_sample), gate_w(0))
    w_down = cast_layer(e_w_down, 0)
    outs = {k: [] for k in ("Cp", "np", "mp", "kp", "vp", "Cs", "ns", "ms", "ks", "vs")}
    for layer in range(DEPTH):
        j = layer // 2
        if layer % 2 == 0:
            proj = matmul(xb, a_w_in, j, n_main, BF16)
            zc = jnp.zeros((BATCH, A_HEADS, A_DK, A_DV), F32)
            zn = jnp.zeros((BATCH, A_HEADS, A_DK), F32)
            zm = jnp.zeros((BATCH, A_HEADS), F32)
            h_p, c_p, n_p, m_p = mlstm(proj, g, a_b_gate[j], a_norm[j], zc, zn, zm,
                                       batch=BATCH, seq=SEQ, L=MLSTM_CHUNK, row0=0)
            h_s, c_s, n_s, m_s = mlstm(proj, g, a_b_gate[j], a_norm[j],
                                       state_C[j], state_n[j], state_m[j],
                                       batch=DEC_BATCH, seq=DEC_SEQ, L=DEC_SEQ, row0=N_PROMPT)
            outs["Cp"].append(c_p); outs["np"].append(n_p); outs["mp"].append(m_p)
            outs["Cs"].append(c_s); outs["ns"].append(n_s); outs["ms"].append(m_s)
            w_out = a_w_out[j]
        else:
            qkv = matmul(xb, b_w_in, j, 3 * D_MODEL, BF16)
            h_p, k_p, v_p = attn_prompt(qkv, b_rel[j])
            h_s, k_s, v_s = attn_sample(qkv, cache_k, cache_v, j, b_rel[j])
            heads = lambda a: a.reshape(a.shape[:2] + (B_HEADS, B_DH))
            outs["kp"].append(heads(k_p)); outs["vp"].append(heads(v_p))
            outs["ks"].append(heads(k_s)); outs["vs"].append(heads(v_s))
            w_out = b_w_out[j]
        xp, wts, cls = mix_ln_router(h_p, h_s, w_out.astype(BF16), x, ln1_g[layer], ln1_b[layer],
                                     r_w_group[layer], r_b_group[layer], r_w_expert[layer], r_b_expert[layer])
        y, w_down = moe(xp, wts, cls, e_w_gate, e_w_up, w_down, e_w_down, layer)
        if layer < DEPTH - 1:
            if (layer + 1) % 2 == 0:
                x, xb, g = residual_ln(xp, y, ln2_g[layer], ln2_b[layer], gate_w(layer + 1))
            else:
                x, xb = residual_ln(xp, y, ln2_g[layer], ln2_b[layer])
        else:
            y_p, y_s = residual_ln_split(xp, y, ln2_g[layer], ln2_b[layer], N_PROMPT)
    st = lambda k: jnp.stack(outs[k])
    return (y_p.reshape(BATCH, SEQ, D_MODEL), y_s.reshape(DEC_BATCH, DEC_SEQ, D_MODEL),
            st("Cp"), st("np"), st("mp"), st("kp"), st("vp"),
            st("Cs"), st("ns"), st("ms"), st("ks"), st("vs"))
```

```python
import functools

import jax
import jax.numpy as jnp
from jax import lax
from jax.experimental import pallas as pl
from jax.experimental.pallas import tpu as pltpu
from jax.experimental.pallas import tpu_sc as plsc

F32 = jnp.float32
BF16 = jnp.bfloat16

D_MODEL = 2048
BATCH = 8
SEQ = 4096
DEPTH = 4
DEC_BATCH = 32
DEC_SEQ = 32
PAST_LEN = 1024
CHUNK = 64
A_HEADS = 8
A_DK = 128
A_DV = D_MODEL // A_HEADS
GATE_CAP = 15.0
B_HEADS = 16
B_DH = D_MODEL // B_HEADS
PREV_CHUNKS = 8
REACH = PREV_CHUNKS * CHUNK
REL_CLIP = 256
N_GROUPS = 4
EXP_PER_GROUP = 4
N_EXPERTS = N_GROUPS * EXP_PER_GROUP
D_EXPERT = D_MODEL // 4
ALPHA = (2 * DEPTH) ** 0.25
LN_EPS = 1e-5
RMS_EPS = 1e-6

N_PROMPT = BATCH * SEQ
N_SAMPLE = DEC_BATCH * DEC_SEQ
N_TOK = N_PROMPT + N_SAMPLE

VMEM_LIMIT = 56 * 1024 * 1024
LANES = 128

PAIRS = ((0, 1), (0, 2), (0, 3), (1, 2), (1, 3), (2, 3))
N_CLASSES = N_GROUPS * len(PAIRS)
MOE_TILE = 256
SC_WORKERS = 32
MOE_TILES = -(-(N_TOK // MOE_TILE + N_CLASSES) // SC_WORKERS) * SC_WORKERS
MOE_ROWS = MOE_TILES * MOE_TILE

MLSTM_CHUNK = 128
ATT_TQ = 256
ATT_TK = ATT_TQ + REACH
NEG = -1e30

SC_TOKENS = 48
SC_FILL = 32


def _params(*sem):
    return pltpu.CompilerParams(dimension_semantics=sem, vmem_limit_bytes=VMEM_LIMIT)


def _mm_body(x_ref, w_ref, o_ref, wb):
    @pl.when(pl.program_id(1) == 0)
    def _():
        wb[...] = w_ref[...].astype(BF16)

    o_ref[...] = jnp.dot(x_ref[...], wb[...], preferred_element_type=F32).astype(o_ref.dtype)


def matmul(x, w_all, layer, m, out_dtype, tm=1536, tn=1024):
    n, k = x.shape
    return pl.pallas_call(
        _mm_body,
        grid=(m // tn, n // tm),
        in_specs=[pl.BlockSpec((tm, k), lambda j, i: (i, 0)),
                  pl.BlockSpec((None, k, tn), lambda j, i: (layer, 0, j))],
        out_specs=pl.BlockSpec((tm, tn), lambda j, i: (i, j)),
        out_shape=jax.ShapeDtypeStruct((n, m), out_dtype),
        scratch_shapes=[pltpu.VMEM((k, tn), BF16)],
        compiler_params=_params("parallel", "arbitrary"),
        name="matmul",
    )(x, w_all)


def _cast_body(w_ref, *rest):
    o_ref = rest[-1]
    o_ref[...] = w_ref[...].astype(BF16)


def cast_layer(w_all, layer, after=None):
    e, a, b = w_all.shape[1:]
    per_step = 2
    in_specs = [pl.BlockSpec((None, per_step, a, b), lambda i: (layer, i, 0, 0))]
    args = [w_all]
    if after is not None:
        in_specs.append(pl.BlockSpec(memory_space=pl.ANY))
        args.append(after)
    return pl.pallas_call(
        _cast_body,
        grid=(e // per_step,),
        in_specs=in_specs,
        out_specs=pl.BlockSpec((per_step, a, b), lambda i: (i, 0, 0)),
        out_shape=jax.ShapeDtypeStruct((e, a, b), BF16),
        compiler_params=_params("parallel"),
        name="cast_layer",
    )(*args)


def _split2(x):
    hi = x.astype(BF16)
    lo = (x - hi.astype(F32)).astype(BF16)
    return hi, lo


_NT = (((1,), (1,)), ((), ()))
_TN = (((0,), (0,)), ((), ()))


def _dot_hilo(x, w):
    t = x.shape[0]
    xh, xl = _split2(x)
    wh, wl = _split2(w)
    r = jnp.dot(jnp.concatenate([xh, xl], axis=0), jnp.concatenate([wh, wl], axis=1), preferred_element_type=F32)
    return r[:t, :LANES] + (r[:t, LANES:] + r[t:, :LANES]) + r[t:, LANES:]


def _log_sigmoid(x):
    return jnp.minimum(x, 0.0) - jnp.log(1.0 + jnp.exp(-jnp.abs(x)))


def _split3(x):
    a = x.astype(BF16)
    r = x - a.astype(F32)
    b = r.astype(BF16)
    c = (r - b.astype(F32)).astype(BF16)
    return a, b, c


A_STATE = A_DV + LANES


def _cummax_rows(x):
    n = x.shape[0]
    row = lax.broadcasted_iota(jnp.int32, x.shape, 0)
    k = 1
    while k < n:
        x = jnp.maximum(x, jnp.where(row >= k, pltpu.roll(x, k, axis=0), -jnp.inf))
        k *= 2
    return x


def _mlstm_body(q_ref, k_ref, v_ref, o_ref, g_ref, bias_ref, gain_ref, c0_ref, n0_ref, m0_ref,
                h_ref, c_out_ref, n_out_ref, m_ref, state, *, L, chunks):
    step = pl.program_id(1)

    @pl.when(step == 0)
    def _():
        state[:, :, :A_DV] = c0_ref[0]
        state[:, :, A_DV:] = n0_ref[0]
        m_ref[...] = m0_ref[...]

    for c in range(chunks):
        rows = pl.ds(c * L, L)
        _mlstm_chunk(q_ref.at[rows], k_ref.at[rows], v_ref.at[rows], o_ref.at[rows], g_ref.at[rows],
                     bias_ref, gain_ref, h_ref.at[rows], m_ref, state, L=L)

    @pl.when(step == pl.num_programs(1) - 1)
    def _():
        c_out_ref[0] = state[:, :, :A_DV]
        n_out_ref[0] = state[:, :, A_DV:]


def _mlstm_chunk(q_ref, k_ref, v_ref, o_ref, g_ref, bias_ref, gain_ref, h_ref, m_ref, state, *, L):
    dot = functools.partial(jnp.dot, preferred_element_type=F32)
    row = lax.broadcasted_iota(jnp.int32, (L, L), 0)
    col = lax.broadcasted_iota(jnp.int32, (L, L), 1)
    causal = col <= row
    tril = jnp.where(causal, 1.0, 0.0).astype(BF16)

    pre = GATE_CAP * jnp.tanh((g_ref[...] + bias_ref[...]) * (1.0 / GATE_CAP))
    ig = pre
    lf = _log_sigmoid(pltpu.roll(pre, LANES - A_HEADS, axis=1))
    a3, b3, c3 = _split3(lf)
    bt = dot(tril, a3) + (dot(tril, b3) + dot(tril, c3))
    a = ig - bt
    m_prev = m_ref[0]
    m_t = bt + jnp.maximum(m_prev, _cummax_rows(a))
    u = bt - m_t
    inter = jnp.exp(bt + m_prev - m_t)
    eminus = jnp.exp(-m_t)
    m_new = m_t[L - 1:L, :]
    bt_last = bt[L - 1:L, :]
    w_c = jnp.exp((bt_last - m_new) + a)
    decay = jnp.exp(bt_last + m_prev - m_new)
    m_ref[0] = m_new
    a_pad = a if L == LANES else jnp.concatenate([a, jnp.zeros((LANES - L, LANES), F32)], axis=0)
    a_t = jnp.transpose(a_pad)
    ie = jnp.concatenate([inter, eminus], axis=1).astype(BF16)
    sel_r = lax.broadcasted_iota(jnp.int32, (2 * LANES, 2 * LANES), 0)
    sel_c = lax.broadcasted_iota(jnp.int32, (2 * LANES, 2 * LANES), 1)
    same_half = (sel_r >= LANES) == (sel_c >= LANES)
    ones_l = jnp.ones((L, LANES), BF16)
    zeros_l = jnp.zeros((L, LANES), BF16)
    zeros_k = jnp.zeros((A_DK, LANES), BF16)
    mean_cols = jnp.full((A_DV, LANES), 1.0, BF16)

    scale = A_DK ** -0.5
    heads = range(A_HEADS)
    dk_sl = lambda h: slice(h * A_DK, (h + 1) * A_DK)
    dv_sl = lambda h: slice(h * A_DV, (h + 1) * A_DV)
    ie_b, qk = [], []
    for h in heads:
        sel = jnp.where(same_half & ((sel_r & (LANES - 1)) == h), 1.0, 0.0).astype(BF16)
        ie_b.append(dot(ie, sel))
        qk.append(lax.dot_general(q_ref[:, dk_sl(h)], k_ref[:, dk_sl(h)], _NT, preferred_element_type=F32))
    out = []
    for h in heads:
        d_mat = jnp.where(causal, jnp.exp(u[:, h:h + 1] + a_t[h:h + 1, :L]), 0.0)
        s = qk[h] * (d_mat * scale)
        q_i = (q_ref[:, dk_sl(h)].astype(F32) * ie_b[h][:, :LANES]).astype(BF16)
        c_b = state[h].astype(BF16)
        rhs = jnp.concatenate([
            jnp.concatenate([v_ref[:, dv_sl(h)], ones_l, zeros_l], axis=1),
            jnp.concatenate([c_b[:, :A_DV], zeros_k, c_b[:, A_DV:]], axis=1)], axis=0)
        out.append(dot(jnp.concatenate([s.astype(BF16), q_i], axis=1), rhs))
    hh, ms = [], []
    for h in heads:
        den = jnp.maximum(jnp.abs(out[h][:, A_DV:A_DV + LANES] + out[h][:, A_DV + LANES:]), ie_b[h][:, LANES:])
        r = 1.0 / den
        hh.append(out[h][:, :A_DV] * jnp.concatenate([r, r], axis=1))
        ms.append(dot((hh[h] * hh[h]).astype(BF16), mean_cols))
    upd = []
    for h in heads:
        rs = lax.rsqrt(ms[h] * (1.0 / A_DV) + RMS_EPS)
        og = o_ref[:, dv_sl(h)].astype(F32)
        y = hh[h] * jnp.concatenate([rs, rs], axis=1) * gain_ref[:, dv_sl(h)] * jax.nn.sigmoid(og)
        h_ref[:, dv_sl(h)] = y.astype(h_ref.dtype)
        wk = (k_ref[:, dk_sl(h)].astype(F32) * (w_c[:, h:h + 1] * scale)).astype(BF16)
        upd.append(lax.dot_general(wk, jnp.concatenate([v_ref[:, dv_sl(h)], ones_l], axis=1), _TN,
                                   preferred_element_type=F32))
    for h in heads:
        state[h] = decay[:, h:h + 1] * state[h] + upd[h]


def mlstm(proj, g, b_gate, gain, c0, n0, m0, *, batch, seq, L, row0):
    chunks = 2 if seq % (2 * L) == 0 else 1
    rows = chunks * L
    nc = seq // rows
    bias = jnp.pad(b_gate.astype(F32), (0, LANES - 2 * A_HEADS))
    row_blk = lambda b, c: row0 // rows + b * nc + c
    st = lambda *s: pl.BlockSpec((1,) + s, lambda b, c: (b,) + (0,) * len(s))
    n0_lanes = jnp.broadcast_to(n0[..., None], n0.shape + (LANES,))
    m_lanes = jnp.pad(m0, ((0, 0), (0, LANES - A_HEADS)))[:, None, :]
    h, c, n, m = pl.pallas_call(
        functools.partial(_mlstm_body, L=L, chunks=chunks),
        grid=(batch, nc),
        in_specs=[pl.BlockSpec((rows, A_HEADS * A_DK), lambda b, c: (row_blk(b, c), 0)),
                  pl.BlockSpec((rows, A_HEADS * A_DK), lambda b, c: (row_blk(b, c), 1)),
                  pl.BlockSpec((rows, D_MODEL), lambda b, c: (row_blk(b, c), 1)),
                  pl.BlockSpec((rows, D_MODEL), lambda b, c: (row_blk(b, c), 2)),
                  pl.BlockSpec((rows, LANES), lambda b, c: (row_blk(b, c), 0)),
                  pl.BlockSpec((1, LANES), lambda b, c: (0, 0)),
                  pl.BlockSpec((1, D_MODEL), lambda b, c: (0, 0)),
                  st(A_HEADS, A_DK, A_DV), st(A_HEADS, A_DK, LANES), st(1, LANES)],
        out_specs=[pl.BlockSpec((rows, D_MODEL), lambda b, c: (b * nc + c, 0)),
                   st(A_HEADS, A_DK, A_DV), st(A_HEADS, A_DK, LANES), st(1, LANES)],
        out_shape=[jax.ShapeDtypeStruct((batch * seq, D_MODEL), BF16),
                   jax.ShapeDtypeStruct((batch, A_HEADS, A_DK, A_DV), F32),
                   jax.ShapeDtypeStruct((batch, A_HEADS, A_DK, LANES), F32),
                   jax.ShapeDtypeStruct((batch, 1, LANES), F32)],
        scratch_shapes=[pltpu.VMEM((A_HEADS, A_DK, A_STATE), F32)],
        compiler_params=_params("parallel", "arbitrary"),
        name="mlstm",
    )(proj, proj, proj, proj, g, bias[None, :], gain.astype(F32)[None, :], c0, n0_lanes, m_lanes)
    return h, c, n[..., 0], m[:, 0, :A_HEADS]


LOG2E = 1.4426950408889634
Q_SCALE = B_DH ** -0.5 * LOG2E


def _scale_q(q):
    return (q.astype(F32) * Q_SCALE).astype(BF16)


def _attn_prompt_body(q_ref, k_ref, v_ref, bias_ref, o_ref, kt_ref, vt_ref, qs, kpad, vpad):
    kt_ref[0] = k_ref[SEQ - REACH:, :].astype(F32)
    vt_ref[0] = v_ref[SEQ - REACH:, :].astype(F32)
    qs[...] = _scale_q(q_ref[...])
    kpad[0:REACH, :] = jnp.zeros((REACH, B_DH), BF16)
    kpad[REACH:, :] = k_ref[...]
    vpad[0:REACH, 0:B_DH] = jnp.zeros((REACH, B_DH), BF16)
    vpad[REACH:, 0:B_DH] = v_ref[...]
    vpad[:, B_DH:] = jnp.ones((SEQ + REACH, B_DH), BF16)

    def scores(t):
        r0 = t * ATT_TQ
        kb = kpad[r0:r0 + ATT_TK, :]
        s = lax.dot_general(qs[r0:r0 + ATT_TQ, :], kb, _NT, preferred_element_type=F32) + bias_ref[0]
        if r0 < REACH:
            j = lax.broadcasted_iota(jnp.int32, (ATT_TQ, ATT_TK), 1)
            s = jnp.where(j + r0 >= REACH, s, NEG)
        return s

    def finish(t, s):
        r0 = t * ATT_TQ
        p = jnp.exp2(s - jnp.max(s, axis=-1, keepdims=True)).astype(BF16)
        acc = jnp.dot(p, vpad[r0:r0 + ATT_TK, :], preferred_element_type=F32)
        o_ref[r0:r0 + ATT_TQ, :] = (acc[:, :B_DH] / acc[:, B_DH:]).astype(o_ref.dtype)

    n_tiles = SEQ // ATT_TQ
    s_next = scores(0)
    for t in range(n_tiles):
        s_cur = s_next
        if t + 1 < n_tiles:
            s_next = scores(t + 1)
        finish(t, s_cur)


def band_bias(rel_table, nq, nk):
    w = nq + nk
    rel = jnp.clip(REACH + nq - 1 - jnp.arange(w), -REL_CLIP, REL_CLIP) + REL_CLIP
    r = rel_table.astype(F32)[:, rel]
    heads = r.shape[0]
    skew = jnp.broadcast_to(r[:, None, :], (heads, nq, w)).reshape(heads, nq * w)
    skew = skew[:, :nq * (w - 1)].reshape(heads, nq, w - 1)
    bias = skew[:, :, nq - 1:nq - 1 + nk]
    i = jnp.arange(nq)[:, None]
    j = jnp.arange(nk)[None, :]
    dc = j // CHUNK - i // CHUNK
    allowed = (dc >= 0) & (dc <= PREV_CHUNKS)
    return jnp.where(allowed[None], bias * LOG2E, NEG)


def attn_prompt(qkv, rel_table):
    bias = band_bias(rel_table, ATT_TQ, ATT_TK)
    tail = pl.BlockSpec((1, REACH, B_DH), lambda b, h: (b, 0, h))
    return pl.pallas_call(
        _attn_prompt_body,
        grid=(BATCH, B_HEADS),
        in_specs=[pl.BlockSpec((SEQ, B_DH), lambda b, h: (b, h)),
                  pl.BlockSpec((SEQ, B_DH), lambda b, h: (b, B_HEADS + h)),
                  pl.BlockSpec((SEQ, B_DH), lambda b, h: (b, 2 * B_HEADS + h)),
                  pl.BlockSpec((1, ATT_TQ, ATT_TK), lambda b, h: (h, 0, 0))],
        out_specs=[pl.BlockSpec((SEQ, B_DH), lambda b, h: (b, h)), tail, tail],
        out_shape=[jax.ShapeDtypeStruct((N_PROMPT, D_MODEL), BF16),
                   jax.ShapeDtypeStruct((BATCH, REACH, D_MODEL), F32),
                   jax.ShapeDtypeStruct((BATCH, REACH, D_MODEL), F32)],
        scratch_shapes=[pltpu.VMEM((SEQ, B_DH), BF16),
                        pltpu.VMEM((SEQ + REACH, B_DH), BF16),
                        pltpu.VMEM((SEQ + REACH, 2 * B_DH), BF16)],
        compiler_params=_params("parallel", "parallel"),
        name="attn_prompt",
    )(qkv, qkv, qkv, bias)


def _attn_sample_body(q_ref, kn_ref, vn_ref, ck_ref, cv_ref, bias_c_ref, bias_n_ref, o_ref, ks_ref, vs_ref):
    ks_ref[0] = kn_ref[...].astype(F32)
    vs_ref[0] = vn_ref[...].astype(F32)
    for h in range(B_HEADS):
        sl = slice(h * B_DH, (h + 1) * B_DH)
        q = _scale_q(q_ref[:, sl])
        kc = ck_ref[pl.ds(h, REACH, stride=B_HEADS), :].astype(BF16)
        vc = cv_ref[pl.ds(h, REACH, stride=B_HEADS), :].astype(BF16)
        s_c = lax.dot_general(q, kc, _NT, preferred_element_type=F32) + bias_c_ref[h]
        s_n = lax.dot_general(q, kn_ref[:, sl], _NT, preferred_element_type=F32) + bias_n_ref[h]
        m = jnp.maximum(jnp.max(s_c, axis=-1, keepdims=True), jnp.max(s_n, axis=-1, keepdims=True))
        p_c = jnp.exp2(s_c - m)
        p_n = jnp.exp2(s_n - m)
        l = jnp.sum(p_c, axis=-1, keepdims=True) + jnp.sum(p_n, axis=-1, keepdims=True)
        dot = functools.partial(jnp.dot, preferred_element_type=F32)
        o_ref[:, sl] = ((dot(p_c.astype(BF16), vc) + dot(p_n.astype(BF16), vn_ref[:, sl])) / l).astype(o_ref.dtype)


def attn_sample(qkv, cache_k, cache_v, layer, rel_table):
    bias = band_bias(rel_table, DEC_SEQ, REACH + DEC_SEQ)
    cache_rows = REACH * B_HEADS
    ck = cache_k.reshape(cache_k.shape[0], DEC_BATCH, cache_rows, B_DH)
    cv = cache_v.reshape(cache_v.shape[0], DEC_BATCH, cache_rows, B_DH)
    row = lambda c: pl.BlockSpec((DEC_SEQ, D_MODEL), lambda b: (N_PROMPT // DEC_SEQ + b, c))
    cache = pl.BlockSpec((None, None, cache_rows, B_DH), lambda b: (layer, b, 0, 0))
    new = pl.BlockSpec((1, DEC_SEQ, D_MODEL), lambda b: (b, 0, 0))
    return pl.pallas_call(
        _attn_sample_body,
        grid=(DEC_BATCH,),
        in_specs=[row(0), row(1), row(2), cache, cache,
                  pl.BlockSpec((B_HEADS, DEC_SEQ, REACH), lambda b: (0, 0, 0)),
                  pl.BlockSpec((B_HEADS, DEC_SEQ, DEC_SEQ), lambda b: (0, 0, 0))],
        out_specs=[pl.BlockSpec((DEC_SEQ, D_MODEL), lambda b: (b, 0)), new, new],
        out_shape=[jax.ShapeDtypeStruct((N_SAMPLE, D_MODEL), BF16),
                   jax.ShapeDtypeStruct((DEC_BATCH, DEC_SEQ, D_MODEL), F32),
                   jax.ShapeDtypeStruct((DEC_BATCH, DEC_SEQ, D_MODEL), F32)],
        compiler_params=_params("parallel"),
        name="attn_sample",
    )(qkv, qkv, qkv, ck, cv, bias[:, :, :REACH], bias[:, :, REACH:])


def _ln(z, g, b):
    mu = jnp.mean(z, axis=-1, keepdims=True)
    zc = z - mu
    var = jnp.mean(zc * zc, axis=-1, keepdims=True)
    return zc * lax.rsqrt(var + LN_EPS) * g + b


PACK_ROWS = D_MODEL // 2 // LANES
HI_MASK = -65536


def _store_packed(ref, v, token0=0):
    t = v.shape[0]
    half = D_MODEL // 2
    hi = lax.bitcast_convert_type(v[:, :half].astype(BF16).astype(F32), jnp.int32)
    lo = lax.bitcast_convert_type(v[:, half:].astype(BF16).astype(F32), jnp.int32)
    words = (hi & HI_MASK) | lax.shift_right_logical(lo, 16)
    for a in range(PACK_ROWS):
        ref[pl.ds(token0 * PACK_ROWS + a, t, stride=PACK_ROWS), :] = words[:, a * LANES:(a + 1) * LANES]


def _load_packed(ref, t):
    words = jnp.concatenate([ref[pl.ds(a, t, stride=PACK_ROWS), :] for a in range(PACK_ROWS)], axis=1)
    hi = lax.bitcast_convert_type(words & HI_MASK, F32)
    lo = lax.bitcast_convert_type(lax.shift_left(words, 16), F32)
    return jnp.concatenate([hi, lo], axis=1)


def _mix_ln_router_body(hp_ref, hs_ref, w_ref, x_ref, g_ref, b_ref, rw_ref, rb_ref,
                        xp_ref, wt_ref, cls_ref, y_even, y_odd, *, prompt_tiles, n_tiles):
    i = pl.program_id(0)
    from_prompt = jnp.minimum(i, n_tiles - 1) < prompt_tiles

    @pl.when(i == 0)
    def _():
        y_odd[...] = jnp.zeros_like(y_odd)

    half = D_MODEL // 2

    def run(y_new, y_old):
        h = jnp.where(from_prompt, hp_ref[...], hs_ref[...])
        y_new[:, :half] = jnp.dot(h, w_ref[:, :half], preferred_element_type=F32)
        xn = _ln(ALPHA * x_ref[...] + y_old[...], g_ref[...], b_ref[...])
        logits = _dot_hilo(xn, rw_ref[...]) + rb_ref[...]
        y_new[:, half:] = jnp.dot(h, w_ref[:, half:], preferred_element_type=F32)
        _store_packed(xp_ref, xn)
        _route(logits, wt_ref, cls_ref)

    @pl.when(i % 2 == 0)
    def _():
        run(y_even, y_odd)

    @pl.when(i % 2 == 1)
    def _():
        run(y_odd, y_even)


def mix_ln_router(h_p, h_s, w, x, g, b, w_group, b_group, w_expert, b_expert, tm=512):
    n = x.shape[0]
    n_tiles = n // tm
    prompt_tiles = h_p.shape[0] // tm
    sample_tiles = h_s.shape[0] // tm
    assert prompt_tiles * tm == h_p.shape[0] and (prompt_tiles + sample_tiles) * tm == n
    mm_tile = lambda i: jnp.minimum(i, n_tiles - 1)
    ep_tile = lambda i: jnp.maximum(i - 1, 0)
    hp_spec = pl.BlockSpec((tm, D_MODEL), lambda i: (jnp.minimum(mm_tile(i), prompt_tiles - 1), 0))
    hs_spec = pl.BlockSpec((tm, D_MODEL), lambda i: (jnp.maximum(mm_tile(i) - prompt_tiles, 0), 0))
    rw = jnp.pad(jnp.concatenate([w_group, w_expert], axis=1), ((0, 0), (0, LANES - N_GROUPS - N_EXPERTS)))
    rb = jnp.pad(jnp.concatenate([b_group, b_expert]).astype(F32), (0, LANES - N_GROUPS - N_EXPERTS))
    row = pl.BlockSpec((tm, D_MODEL), lambda i: (ep_tile(i), 0))
    vec = pl.BlockSpec((1, D_MODEL), lambda i: (0, 0))
    lane_row = pl.BlockSpec((tm, LANES), lambda i: (ep_tile(i), 0))
    return pl.pallas_call(
        functools.partial(_mix_ln_router_body, prompt_tiles=prompt_tiles, n_tiles=n_tiles),
        grid=(n_tiles + 1,),
        in_specs=[hp_spec, hs_spec, pl.BlockSpec((D_MODEL, D_MODEL), lambda i: (0, 0)), row, vec, vec,
                  pl.BlockSpec((D_MODEL, LANES), lambda i: (0, 0)), pl.BlockSpec((1, LANES), lambda i: (0, 0))],
        out_specs=[pl.BlockSpec((tm * PACK_ROWS, LANES), lambda i: (ep_tile(i), 0)), lane_row, lane_row],
        out_shape=[jax.ShapeDtypeStruct((n * PACK_ROWS, LANES), jnp.int32),
                   jax.ShapeDtypeStruct((n, LANES), F32), jax.ShapeDtypeStruct((n, LANES), jnp.int32)],
        scratch_shapes=[pltpu.VMEM((tm, D_MODEL), F32), pltpu.VMEM((tm, D_MODEL), F32)],
        compiler_params=_params("arbitrary"),
        name="mix_ln_router",
    )(h_p, h_s, w, x, g[None, :], b[None, :], rw, rb[None, :])


GATE_COL_BLOCK = (2 * A_HEADS * A_DK + 2 * D_MODEL) // LANES


def _gate_spec(mlstm_layer):
    return pl.BlockSpec((None, D_MODEL, LANES), lambda i: (mlstm_layer, 0, GATE_COL_BLOCK))


def _gate_preacts(x, wg_ref):
    lane = lax.broadcasted_iota(jnp.int32, wg_ref.shape, 1)
    return _dot_hilo(x, jnp.where(lane < 2 * A_HEADS, wg_ref[...], 0.0))


def _add_ln_body(x_ref, y_ref, g_ref, b_ref, *rest):
    t = rest[-1].shape[0]
    xn = _ln(ALPHA * _load_packed(x_ref, t) + _load_packed(y_ref, t), g_ref[...], b_ref[...])
    if len(rest) == 2:
        xo_ref, xbo_ref = rest
    else:
        wg_ref, xo_ref, xbo_ref, go_ref = rest
        go_ref[...] = _gate_preacts(xn, wg_ref)
    xo_ref[...] = xn
    xbo_ref[...] = xn.astype(BF16)


def residual_ln(x_packed, y_packed, g, b, a_w_in=None, mlstm_layer=None, tm=512):
    n = x_packed.shape[0] // PACK_ROWS
    row = pl.BlockSpec((tm, D_MODEL), lambda i: (i, 0))
    vec = pl.BlockSpec((1, D_MODEL), lambda i: (0, 0))
    packed = pl.BlockSpec((tm * PACK_ROWS, LANES), lambda i: (i, 0))
    in_specs = [packed, packed, vec, vec]
    out_specs = [row, row]
    out_shape = [jax.ShapeDtypeStruct((n, D_MODEL), F32), jax.ShapeDtypeStruct((n, D_MODEL), BF16)]
    args = [x_packed, y_packed, g[None, :], b[None, :]]
    if a_w_in is not None:
        in_specs.append(_gate_spec(mlstm_layer))
        out_specs.append(pl.BlockSpec((tm, LANES), lambda i: (i, 0)))
        out_shape.append(jax.ShapeDtypeStruct((n, LANES), F32))
        args.append(a_w_in)
    return pl.pallas_call(
        _add_ln_body,
        grid=(n // tm,),
        in_specs=in_specs,
        out_specs=out_specs,
        out_shape=out_shape,
        compiler_params=_params("parallel"),
        name="residual_ln",
    )(*args)


def _embed_body(xp_ref, xs_ref, wg_ref, xo_ref, xbo_ref, go_ref, *, prompt_tiles):
    x = jnp.where(pl.program_id(0) < prompt_tiles, xp_ref[...], xs_ref[...])
    xo_ref[...] = x
    xbo_ref[...] = x.astype(BF16)
    go_ref[...] = _gate_preacts(x, wg_ref)


def embed(x_p, x_s, a_w_in, tm=512):
    n = x_p.shape[0] + x_s.shape[0]
    prompt_tiles = x_p.shape[0] // tm
    assert prompt_tiles * tm == x_p.shape[0] and n % tm == 0
    row = pl.BlockSpec((tm, D_MODEL), lambda i: (i, 0))
    return pl.pallas_call(
        functools.partial(_embed_body, prompt_tiles=prompt_tiles),
        grid=(n // tm,),
        in_specs=[pl.BlockSpec((tm, D_MODEL), lambda i: (jnp.minimum(i, prompt_tiles - 1), 0)),
                  pl.BlockSpec((tm, D_MODEL), lambda i: (jnp.maximum(i - prompt_tiles, 0), 0)),
                  _gate_spec(0)],
        out_specs=[row, row, pl.BlockSpec((tm, LANES), lambda i: (i, 0))],
        out_shape=[jax.ShapeDtypeStruct((n, D_MODEL), F32), jax.ShapeDtypeStruct((n, D_MODEL), BF16),
                   jax.ShapeDtypeStruct((n, LANES), F32)],
        compiler_params=_params("parallel"),
        name="embed",
    )(x_p, x_s, a_w_in)


def _add_ln_split_body(x_ref, y_ref, g_ref, b_ref, op_ref, os_ref, *, prompt_tiles):
    t = op_ref.shape[0]
    xn = _ln(ALPHA * _load_packed(x_ref, t) + _load_packed(y_ref, t), g_ref[...], b_ref[...])
    i = pl.program_id(0)

    @pl.when(i < prompt_tiles)
    def _():
        op_ref[...] = xn

    @pl.when(i >= prompt_tiles)
    def _():
        os_ref[...] = xn


def residual_ln_split(x_packed, y_packed, g, b, n_prompt, tm=512):
    n = x_packed.shape[0] // PACK_ROWS
    prompt_tiles = n_prompt // tm
    assert prompt_tiles * tm == n_prompt and n % tm == 0
    vec = pl.BlockSpec((1, D_MODEL), lambda i: (0, 0))
    packed = pl.BlockSpec((tm * PACK_ROWS, LANES), lambda i: (i, 0))
    return pl.pallas_call(
        functools.partial(_add_ln_split_body, prompt_tiles=prompt_tiles),
        grid=(n // tm,),
        in_specs=[packed, packed, vec, vec],
        out_specs=[pl.BlockSpec((tm, D_MODEL), lambda i: (jnp.minimum(i, prompt_tiles - 1), 0)),
                   pl.BlockSpec((tm, D_MODEL), lambda i: (jnp.maximum(i - prompt_tiles, 0), 0))],
        out_shape=[jax.ShapeDtypeStruct((n_prompt, D_MODEL), F32),
                   jax.ShapeDtypeStruct((n - n_prompt, D_MODEL), F32)],
        compiler_params=_params("arbitrary"),
        name="residual_ln_split",
    )(x_packed, y_packed, g[None, :], b[None, :])


ROUTER_E0 = N_GROUPS


def _route(logits, wt_ref, cls_ref):
    lane = lax.broadcasted_iota(jnp.int32, logits.shape, 1).astype(F32)
    big = float(LANES)
    rmax = lambda a: jnp.max(a, axis=-1, keepdims=True)
    rmin = lambda a: jnp.min(a, axis=-1, keepdims=True)
    is_g = lane < N_GROUPS
    gl = jnp.where(is_g, logits, -jnp.inf)
    gmax = rmax(gl)
    gsel = rmin(jnp.where(gl == gmax, lane, big))
    gsum = jnp.sum(jnp.where(is_g, jnp.exp(logits - gmax), 0.0), axis=-1, keepdims=True)
    g_w = 1.0 / gsum
    e_lo = ROUTER_E0 + EXP_PER_GROUP * gsel
    in_grp = (lane >= e_lo) & (lane < e_lo + EXP_PER_GROUP)
    el = jnp.where(in_grp, logits, -jnp.inf)
    e1 = rmax(el)
    i1 = rmin(jnp.where(el == e1, lane, big))
    el2 = jnp.where(lane == i1, -jnp.inf, el)
    e2 = rmax(el2)
    i2 = rmin(jnp.where(el2 == e2, lane, big))
    t = jnp.exp(e2 - e1)
    p1 = 1.0 / (1.0 + t)
    w1 = p1 * g_w
    w2 = (t * p1) * g_w
    a1 = i1 - e_lo
    a2 = i2 - e_lo
    first_low = a1 < a2
    lo = jnp.where(first_low, a1, a2)
    hi = jnp.where(first_low, a2, a1)
    w_lo = jnp.where(first_low, w1, w2)
    w_hi = jnp.where(first_low, w2, w1)
    off = jnp.where(lo == 0.0, 0.0, jnp.where(lo == 1.0, 3.0, 5.0))
    cls = gsel * float(len(PAIRS)) + off + (hi - lo - 1.0)
    wt_ref[...] = jnp.where(lane < LANES // 2, w_lo, w_hi)
    cls_ref[...] = jnp.broadcast_to(cls, logits.shape).astype(jnp.int32)


def _expert_body(ea_ref, eb_ref, valid_ref, x_ref, wt_ref, wga_ref, wua_ref, wda_ref,
                 wgb_ref, wub_ref, wdb_ref, y_ref):
    i = pl.program_id(0)

    @pl.when(valid_ref[i] == 0)
    def _():
        y_ref[...] = jnp.zeros_like(y_ref)

    @pl.when(valid_ref[i] != 0)
    def _():
        x = _load_packed(x_ref, MOE_TILE).astype(BF16)
        dot = functools.partial(jnp.dot, preferred_element_type=F32)

        ga, ua = dot(x, wga_ref[...]), dot(x, wua_ref[...])
        gb, ub = dot(x, wgb_ref[...]), dot(x, wub_ref[...])
        ya = dot((jax.nn.silu(ga) * ua).astype(BF16), wda_ref[...])
        yb = dot((jax.nn.silu(gb) * ub).astype(BF16), wdb_ref[...])
        w_lo = wt_ref[:, 0:1]
        w_hi = wt_ref[:, LANES // 2:LANES // 2 + 1]
        _store_packed(y_ref, w_lo * ya + w_hi * yb)


def experts(xs, wts, tile_ea, tile_eb, tile_valid, w_gate, w_up, w_down):
    wspec = lambda shape, which: pl.BlockSpec(
        (None,) + shape, lambda i, ea, eb, va: ((ea, eb)[which][i], 0, 0))
    up = (D_MODEL, D_EXPERT)
    down = (D_EXPERT, D_MODEL)
    packed = pl.BlockSpec((MOE_TILE * PACK_ROWS, LANES), lambda i, ea, eb, va: (i, 0))
    grid_spec = pltpu.PrefetchScalarGridSpec(
        num_scalar_prefetch=3,
        grid=(MOE_TILES,),
        in_specs=[packed,
                  pl.BlockSpec((MOE_TILE, LANES), lambda i, ea, eb, va: (i, 0)),
                  wspec(up, 0), wspec(up, 0), wspec(down, 0),
                  wspec(up, 1), wspec(up, 1), wspec(down, 1)],
        out_specs=packed,
    )
    return pl.pallas_call(
        _expert_body,
        grid_spec=grid_spec,
        out_shape=jax.ShapeDtypeStruct((MOE_ROWS * PACK_ROWS, LANES), jnp.int32),
        compiler_params=_params("arbitrary"),
        name="experts",
    )(tile_ea, tile_eb, tile_valid, xs, wts, w_gate, w_up, w_down, w_gate, w_up, w_down)


def sc_dispatch(xp, wts, pos, fill_rows):
    n = pos.shape[0]
    n_fill = fill_rows.shape[0]
    info = plsc.get_sparse_core_info()
    n_workers = info.num_cores * info.num_subcores
    assert n_workers == SC_WORKERS
    per_worker = n // n_workers
    steps = per_worker // SC_TOKENS
    fill_steps = n_fill // n_workers // SC_FILL
    assert steps * SC_TOKENS * n_workers == n and steps % 2 == 0
    assert fill_steps * SC_FILL * n_workers == n_fill
    mesh = plsc.VectorSubcoreMesh(core_axis_name="core", subcore_axis_name="subcore")
    dma = pltpu.SemaphoreType.DMA

    @functools.partial(
        pl.kernel,
        out_type=[jax.ShapeDtypeStruct((n + n_fill, PACK_ROWS, LANES), xp.dtype),
                  jax.ShapeDtypeStruct((n + n_fill, LANES), wts.dtype)],
        mesh=mesh,
        scratch_types=[pltpu.VMEM((steps, SC_TOKENS), jnp.int32),
                       pltpu.VMEM((fill_steps, SC_FILL), jnp.int32),
                       pltpu.VMEM((2, SC_TOKENS, PACK_ROWS, LANES), xp.dtype),
                       pltpu.VMEM((2, SC_TOKENS, LANES), wts.dtype),
                       dma((2,)), dma((2,)), dma((2,)), dma((2,))],
    )
    def scatter(x_hbm, w_hbm, pos_hbm, fill_hbm, zx_hbm, zw_hbm, xs_hbm, ws_hbm,
                pos_v, fill_v, xbuf, wbuf, read_x, read_w, write_x, write_w):
        worker = lax.axis_index("subcore") * info.num_cores + lax.axis_index("core")
        base = worker * per_worker
        pltpu.sync_copy(pos_hbm.at[worker], pos_v)
        pltpu.sync_copy(fill_hbm.at[worker], fill_v)

        def reads(s, b):
            rows = pl.ds(base + s * SC_TOKENS, SC_TOKENS)
            return (pltpu.make_async_copy(x_hbm.at[rows], xbuf.at[b], read_x.at[b]),
                    pltpu.make_async_copy(w_hbm.at[rows], wbuf.at[b], read_w.at[b]))

        def writes(s, b):
            return (pltpu.make_async_copy(xbuf.at[b], xs_hbm.at[pos_v.at[s]], write_x.at[b]),
                    pltpu.make_async_copy(wbuf.at[b], ws_hbm.at[pos_v.at[s]], write_w.at[b]))

        def start(copies):
            for c in copies:
                c.start()

        def wait(copies):
            for c in copies:
                c.wait()

        start(reads(0, 0))

        @pl.loop(0, steps, step=2)
        def _(s):
            wait(reads(s, 0))
            start(reads(s + 1, 1))
            start(writes(s, 0))
            wait(reads(s + 1, 1))
            wait(writes(s, 0))

            @pl.when(s + 2 < steps)
            def _():
                start(reads(s + 2, 0))

            start(writes(s + 1, 1))
            wait(writes(s + 1, 1))

        zx = xbuf.at[0, pl.ds(0, SC_FILL)]
        zw = wbuf.at[0, pl.ds(0, SC_FILL)]
        pltpu.sync_copy(zx_hbm, zx)
        pltpu.sync_copy(zw_hbm, zw)
        fills = [(pltpu.make_async_copy(zx, xs_hbm.at[fill_v.at[s]], write_x.at[0]),
                  pltpu.make_async_copy(zw, ws_hbm.at[fill_v.at[s]], write_w.at[0])) for s in range(fill_steps)]
        for f in fills:
            start(f)
        for f in fills:
            wait(f)

    xs, ws = scatter(xp.reshape(n, PACK_ROWS, LANES), wts,
                     pos.reshape(n_workers, steps, SC_TOKENS), fill_rows.reshape(n_workers, fill_steps, SC_FILL),
                     jnp.zeros((SC_FILL, PACK_ROWS, LANES), xp.dtype), jnp.zeros((SC_FILL, LANES), wts.dtype))
    return xs.reshape((n + n_fill) * PACK_ROWS, LANES), ws


def sc_gather_tokens(x, idx):
    n = x.shape[0] // PACK_ROWS
    p = idx.shape[0]
    info = plsc.get_sparse_core_info()
    n_workers = info.num_cores * info.num_subcores
    assert n_workers == SC_WORKERS
    per_worker = p // n_workers
    steps = per_worker // SC_TOKENS
    assert steps * SC_TOKENS * n_workers == p and steps % 2 == 0
    mesh = plsc.VectorSubcoreMesh(core_axis_name="core", subcore_axis_name="subcore")
    dma = pltpu.SemaphoreType.DMA

    @functools.partial(
        pl.kernel,
        out_type=jax.ShapeDtypeStruct((p, PACK_ROWS, LANES), x.dtype),
        mesh=mesh,
        scratch_types=[pltpu.VMEM((steps, SC_TOKENS), jnp.int32),
                       pltpu.VMEM((2, SC_TOKENS, PACK_ROWS, LANES), x.dtype),
                       dma((2,)), dma((2,))],
    )
    def gather(x_hbm, i_hbm, o_hbm, idx_v, buf, read_sem, write_sem):
        worker = lax.axis_index("subcore") * info.num_cores + lax.axis_index("core")
        base = worker * per_worker
        pltpu.sync_copy(i_hbm.at[worker], idx_v)

        def read(s, b):
            return pltpu.make_async_copy(x_hbm.at[idx_v.at[s]], buf.at[b], read_sem.at[b])

        def write(s, b):
            return pltpu.make_async_copy(buf.at[b], o_hbm.at[pl.ds(base + s * SC_TOKENS, SC_TOKENS)], write_sem.at[b])

        read(0, 0).start()

        @pl.loop(0, steps, step=2)
        def _(s):
            read(s, 0).wait()
            read(s + 1, 1).start()
            write(s, 0).start()
            read(s + 1, 1).wait()
            write(s, 0).wait()

            @pl.when(s + 2 < steps)
            def _():
                read(s + 2, 0).start()

            write(s + 1, 1).start()
            write(s + 1, 1).wait()

    out = gather(x.reshape(n, PACK_ROWS, LANES), idx.reshape(n_workers, steps, SC_TOKENS))
    return out.reshape(p * PACK_ROWS, LANES)


_CLASS_EA = [g * EXP_PER_GROUP + lo for g in range(N_GROUPS) for lo, hi in PAIRS]
_CLASS_EB = [g * EXP_PER_GROUP + hi for g in range(N_GROUPS) for lo, hi in PAIRS]


def dispatch_plan(cls):
    onehot = (cls[:, None] == jnp.arange(N_CLASSES, dtype=jnp.int32)[None, :]).astype(jnp.int32)
    csum = jnp.cumsum(onehot, axis=0)
    rank = jnp.sum(csum * onehot, axis=1) - 1
    counts = csum[-1]
    tiles_per = (counts + MOE_TILE - 1) // MOE_TILE
    tile_end = jnp.cumsum(tiles_per)
    tile_start = tile_end - tiles_per
    pos = jnp.sum(onehot * tile_start[None, :], axis=1) * MOE_TILE + rank
    n = cls.shape[0]
    pad = tiles_per * MOE_TILE - counts
    pad_end = jnp.cumsum(pad)
    k = jnp.arange(MOE_ROWS - n, dtype=jnp.int32)
    k_cls = jnp.sum((k[:, None] >= pad_end[None, :]).astype(jnp.int32), axis=1)
    k_hot = (k_cls[:, None] == jnp.arange(N_CLASSES, dtype=jnp.int32)[None, :]).astype(jnp.int32)
    pad_row0 = tile_start * MOE_TILE + counts - (pad_end - pad)
    fill_rows = k + jnp.where(k_cls < N_CLASSES, jnp.sum(k_hot * pad_row0[None, :], axis=1),
                              tile_end[-1] * MOE_TILE - pad_end[-1])
    t = jnp.arange(MOE_TILES, dtype=jnp.int32)
    tile_cls = jnp.sum((t[:, None] >= tile_end[None, :]).astype(jnp.int32), axis=1)
    valid = (tile_cls < N_CLASSES).astype(jnp.int32)
    last_cls = jnp.max(jnp.where(counts > 0, jnp.arange(N_CLASSES, dtype=jnp.int32), 0))
    tile_cls = jnp.where(valid == 1, tile_cls, last_cls)
    ea = jnp.asarray(_CLASS_EA, jnp.int32)[tile_cls]
    eb = jnp.asarray(_CLASS_EB, jnp.int32)[tile_cls]
    return pos, fill_rows, ea, eb, valid


def moe(xp, wts, cls, e_w_gate, e_w_up, w_down, e_w_down, layer):
    pos, fill_rows, ea, eb, valid = dispatch_plan(cls[:, 0])
    xs, ws = sc_dispatch(xp, wts, pos, fill_rows)
    w_gate, w_up = cast_layer(e_w_gate, layer, after=pos), cast_layer(e_w_up, layer, after=pos)
    ys = experts(xs, ws, ea, eb, valid, w_gate, w_up, w_down)
    y = sc_gather_tokens(ys, pos)
    if layer + 1 == DEPTH:
        return y, None
    return lax.optimization_barrier((y, cast_layer(e_w_down, layer + 1, after=ys)))


def _rows(a):
    return a.reshape(-1, a.shape[-1])


def kernel(x_prompt, x_sample, state_C, state_n, state_m, cache_k, cache_v, a_w_in, a_b_gate, a_norm, a_w_out, b_w_in, b_rel, b_w_out, ln1_g, ln1_b, ln2_g, ln2_b, r_w_group, r_b_group, r_w_expert, r_b_expert, e_w_gate, e_w_up, e_w_down):
    hk = A_HEADS * A_DK
    n_main = 2 * hk + 2 * D_MODEL
    assert n_main == GATE_COL_BLOCK * LANES and a_w_in.shape[-1] == n_main + 2 * A_HEADS
    x, xb, g = embed(_rows(x_prompt), _rows(x_sample), a_w_in)
    w_down = cast_layer(e_w_down, 0)
    outs = {k: [] for k in ("Cp", "np", "mp", "kp", "vp", "Cs", "ns", "ms", "ks", "vs")}
    for layer in range(DEPTH):
        j = layer // 2
        if layer % 2 == 0:
            proj = matmul(xb, a_w_in, j, n_main, BF16)
            zc = jnp.zeros((BATCH, A_HEADS, A_DK, A_DV), F32)
            zn = jnp.zeros((BATCH, A_HEADS, A_DK), F32)
            zm = jnp.zeros((BATCH, A_HEADS), F32)
            h_p, c_p, n_p, m_p = mlstm(proj, g, a_b_gate[j], a_norm[j], zc, zn, zm,
                                       batch=BATCH, seq=SEQ, L=MLSTM_CHUNK, row0=0)
            h_s, c_s, n_s, m_s = mlstm(proj, g, a_b_gate[j], a_norm[j],
                                       state_C[j], state_n[j], state_m[j],
                                       batch=DEC_BATCH, seq=DEC_SEQ, L=DEC_SEQ, row0=N_PROMPT)
            outs["Cp"].append(c_p); outs["np"].append(n_p); outs["mp"].append(m_p)
            outs["Cs"].append(c_s); outs["ns"].append(n_s); outs["ms"].append(m_s)
            w_out = a_w_out[j]
        else:
            qkv = matmul(xb, b_w_in, j, 3 * D_MODEL, BF16)
            h_p, k_p, v_p = attn_prompt(qkv, b_rel[j])
            h_s, k_s, v_s = attn_sample(qkv, cache_k, cache_v, j, b_rel[j])
            heads = lambda a: a.reshape(a.shape[:2] + (B_HEADS, B_DH))
            outs["kp"].append(heads(k_p)); outs["vp"].append(heads(v_p))
            outs["ks"].append(heads(k_s)); outs["vs"].append(heads(v_s))
            w_out = b_w_out[j]
        xp, wts, cls = mix_ln_router(h_p, h_s, w_out.astype(BF16), x, ln1_g[layer], ln1_b[layer],
                                     r_w_group[layer], r_b_group[layer], r_w_expert[layer], r_b_expert[layer])
        y, w_down = moe(xp, wts, cls, e_w_gate, e_w_up, w_down, e_w_down, layer)
        if layer < DEPTH - 1:
            if (layer + 1) % 2 == 0:
                x, xb, g = residual_ln(xp, y, ln2_g[layer], ln2_b[layer], a_w_in, (layer + 1) // 2)
            else:
                x, xb = residual_ln(xp, y, ln2_g[layer], ln2_b[layer])
        else:
            y_p, y_s = residual_ln_split(xp, y, ln2_g[layer], ln2_b[layer], N_PROMPT)
    st = lambda k: jnp.stack(outs[k])
    return (y_p.reshape(BATCH, SEQ, D_MODEL), y_s.reshape(DEC_BATCH, DEC_SEQ, D_MODEL),
            st("Cp"), st("np"), st("mp"), st("kp"), st("vp"),
            st("Cs"), st("ns"), st("ms"), st("ks"), st("vs"))
```

```python
import functools

import jax
import jax.numpy as jnp
from jax import lax
from jax.experimental import pallas as pl
from jax.experimental.pallas import tpu as pltpu
from jax.experimental.pallas import tpu_sc as plsc

F32 = jnp.float32
BF16 = jnp.bfloat16

D_MODEL = 2048
BATCH = 8
SEQ = 4096
DEPTH = 4
DEC_BATCH = 32
DEC_SEQ = 32
PAST_LEN = 1024
CHUNK = 64
A_HEADS = 8
A_DK = 128
A_DV = D_MODEL // A_HEADS
GATE_CAP = 15.0
B_HEADS = 16
B_DH = D_MODEL // B_HEADS
PREV_CHUNKS = 8
REACH = PREV_CHUNKS * CHUNK
REL_CLIP = 256
N_GROUPS = 4
EXP_PER_GROUP = 4
N_EXPERTS = N_GROUPS * EXP_PER_GROUP
D_EXPERT = D_MODEL // 4
ALPHA = (2 * DEPTH) ** 0.25
LN_EPS = 1e-5
RMS_EPS = 1e-6

N_PROMPT = BATCH * SEQ
N_SAMPLE = DEC_BATCH * DEC_SEQ
N_TOK = N_PROMPT + N_SAMPLE

VMEM_LIMIT = 56 * 1024 * 1024
LANES = 128

PAIRS = ((0, 1), (0, 2), (0, 3), (1, 2), (1, 3), (2, 3))
N_CLASSES = N_GROUPS * len(PAIRS)
MOE_TILE = 256
SC_WORKERS = 32
MOE_TILES = -(-(N_TOK // MOE_TILE + N_CLASSES) // SC_WORKERS) * SC_WORKERS
MOE_ROWS = MOE_TILES * MOE_TILE

MLSTM_CHUNK = 128
ATT_TQ = 256
ATT_TK = ATT_TQ + REACH
NEG = -1e30

SC_TOKENS = 48
SC_FILL = 32


def _params(*sem):
    return pltpu.CompilerParams(dimension_semantics=sem, vmem_limit_bytes=VMEM_LIMIT)


def _mm_body(x_ref, w_ref, o_ref, wb):
    @pl.when(pl.program_id(1) == 0)
    def _():
        wb[...] = w_ref[...].astype(BF16)

    o_ref[...] = jnp.dot(x_ref[...], wb[...], preferred_element_type=F32).astype(o_ref.dtype)


def matmul(x, w_all, layer, m, out_dtype, tm=1536, tn=1024):
    n, k = x.shape
    return pl.pallas_call(
        _mm_body,
        grid=(m // tn, n // tm),
        in_specs=[pl.BlockSpec((tm, k), lambda j, i: (i, 0)),
                  pl.BlockSpec((None, k, tn), lambda j, i: (layer, 0, j))],
        out_specs=pl.BlockSpec((tm, tn), lambda j, i: (i, j)),
        out_shape=jax.ShapeDtypeStruct((n, m), out_dtype),
        scratch_shapes=[pltpu.VMEM((k, tn), BF16)],
        compiler_params=_params("parallel", "arbitrary"),
        name="matmul",
    )(x, w_all)


def _cast_body(w_ref, *rest):
    o_ref = rest[-1]
    o_ref[...] = w_ref[...].astype(BF16)


def cast_layer(w_all, layer, after=None):
    e, a, b = w_all.shape[1:]
    per_step = 2
    in_specs = [pl.BlockSpec((None, per_step, a, b), lambda i: (layer, i, 0, 0))]
    args = [w_all]
    if after is not None:
        in_specs.append(pl.BlockSpec(memory_space=pl.ANY))
        args.append(after)
    return pl.pallas_call(
        _cast_body,
        grid=(e // per_step,),
        in_specs=in_specs,
        out_specs=pl.BlockSpec((per_step, a, b), lambda i: (i, 0, 0)),
        out_shape=jax.ShapeDtypeStruct((e, a, b), BF16),
        compiler_params=_params("parallel"),
        name="cast_layer",
    )(*args)


def _split2(x):
    hi = x.astype(BF16)
    lo = (x - hi.astype(F32)).astype(BF16)
    return hi, lo


_NT = (((1,), (1,)), ((), ()))
_TN = (((0,), (0,)), ((), ()))


def _dot_hilo(x, w):
    t = x.shape[0]
    xh, xl = _split2(x)
    wh, wl = _split2(w)
    r = jnp.dot(jnp.concatenate([xh, xl], axis=0), jnp.concatenate([wh, wl], axis=1), preferred_element_type=F32)
    return r[:t, :LANES] + (r[:t, LANES:] + r[t:, :LANES]) + r[t:, LANES:]


def _log_sigmoid(x):
    return jnp.minimum(x, 0.0) - jnp.log(1.0 + jnp.exp(-jnp.abs(x)))


def _split3(x):
    a = x.astype(BF16)
    r = x - a.astype(F32)
    b = r.astype(BF16)
    c = (r - b.astype(F32)).astype(BF16)
    return a, b, c


A_STATE = A_DV + LANES


def _cummax_rows(x):
    n = x.shape[0]
    row = lax.broadcasted_iota(jnp.int32, x.shape, 0)
    k = 1
    while k < n:
        x = jnp.maximum(x, jnp.where(row >= k, pltpu.roll(x, k, axis=0), -jnp.inf))
        k *= 2
    return x


def _mlstm_body(q_ref, k_ref, v_ref, o_ref, g_ref, bias_ref, gain_ref, c0_ref, n0_ref, m0_ref,
                h_ref, c_out_ref, n_out_ref, m_ref, state, *, L, chunks):
    step = pl.program_id(1)

    @pl.when(step == 0)
    def _():
        state[:, :, :A_DV] = c0_ref[0]
        state[:, :, A_DV:] = n0_ref[0]
        m_ref[...] = m0_ref[...]

    for c in range(chunks):
        rows = pl.ds(c * L, L)
        _mlstm_chunk(q_ref.at[rows], k_ref.at[rows], v_ref.at[rows], o_ref.at[rows], g_ref.at[rows],
                     bias_ref, gain_ref, h_ref.at[rows], m_ref, state, L=L)

    @pl.when(step == pl.num_programs(1) - 1)
    def _():
        c_out_ref[0] = state[:, :, :A_DV]
        n_out_ref[0] = state[:, :, A_DV:]


def _mlstm_chunk(q_ref, k_ref, v_ref, o_ref, g_ref, bias_ref, gain_ref, h_ref, m_ref, state, *, L):
    dot = functools.partial(jnp.dot, preferred_element_type=F32)
    row = lax.broadcasted_iota(jnp.int32, (L, L), 0)
    col = lax.broadcasted_iota(jnp.int32, (L, L), 1)
    causal = col <= row
    tril = jnp.where(causal, 1.0, 0.0).astype(BF16)

    pre = GATE_CAP * jnp.tanh((g_ref[...] + bias_ref[...]) * (1.0 / GATE_CAP))
    ig = pre
    lf = _log_sigmoid(pltpu.roll(pre, LANES - A_HEADS, axis=1))
    a3, b3, c3 = _split3(lf)
    bt = dot(tril, a3) + (dot(tril, b3) + dot(tril, c3))
    a = ig - bt
    m_prev = m_ref[0]
    m_t = bt + jnp.maximum(m_prev, _cummax_rows(a))
    u = bt - m_t
    inter = jnp.exp(bt + m_prev - m_t)
    eminus = jnp.exp(-m_t)
    m_new = m_t[L - 1:L, :]
    bt_last = bt[L - 1:L, :]
    w_c = jnp.exp((bt_last - m_new) + a)
    decay = jnp.exp(bt_last + m_prev - m_new)
    m_ref[0] = m_new
    a_pad = a if L == LANES else jnp.concatenate([a, jnp.zeros((LANES - L, LANES), F32)], axis=0)
    a_t = jnp.transpose(a_pad)
    ie = jnp.concatenate([inter, eminus], axis=1).astype(BF16)
    sel_r = lax.broadcasted_iota(jnp.int32, (2 * LANES, 2 * LANES), 0)
    sel_c = lax.broadcasted_iota(jnp.int32, (2 * LANES, 2 * LANES), 1)
    same_half = (sel_r >= LANES) == (sel_c >= LANES)
    ones_l = jnp.ones((L, LANES), BF16)
    zeros_l = jnp.zeros((L, LANES), BF16)
    zeros_k = jnp.zeros((A_DK, LANES), BF16)
    mean_cols = jnp.full((A_DV, LANES), 1.0, BF16)

    scale = A_DK ** -0.5
    heads = range(A_HEADS)
    dk_sl = lambda h: slice(h * A_DK, (h + 1) * A_DK)
    dv_sl = lambda h: slice(h * A_DV, (h + 1) * A_DV)
    ie_b, qk = [], []
    for h in heads:
        sel = jnp.where(same_half & ((sel_r & (LANES - 1)) == h), 1.0, 0.0).astype(BF16)
        ie_b.append(dot(ie, sel))
        qk.append(lax.dot_general(q_ref[:, dk_sl(h)], k_ref[:, dk_sl(h)], _NT, preferred_element_type=F32))
    out = []
    for h in heads:
        d_mat = jnp.where(causal, jnp.exp(u[:, h:h + 1] + a_t[h:h + 1, :L]), 0.0)
        s = qk[h] * (d_mat * scale)
        q_i = (q_ref[:, dk_sl(h)].astype(F32) * ie_b[h][:, :LANES]).astype(BF16)
        c_b = state[h].astype(BF16)
        rhs = jnp.concatenate([
            jnp.concatenate([v_ref[:, dv_sl(h)], ones_l, zeros_l], axis=1),
            jnp.concatenate([c_b[:, :A_DV], zeros_k, c_b[:, A_DV:]], axis=1)], axis=0)
        out.append(dot(jnp.concatenate([s.astype(BF16), q_i], axis=1), rhs))
    hh, ms = [], []
    for h in heads:
        den = jnp.maximum(jnp.abs(out[h][:, A_DV:A_DV + LANES] + out[h][:, A_DV + LANES:]), ie_b[h][:, LANES:])
        r = 1.0 / den
        hh.append(out[h][:, :A_DV] * jnp.concatenate([r, r], axis=1))
        ms.append(dot((hh[h] * hh[h]).astype(BF16), mean_cols))
    upd = []
    for h in heads:
        rs = lax.rsqrt(ms[h] * (1.0 / A_DV) + RMS_EPS)
        og = o_ref[:, dv_sl(h)].astype(F32)
        y = hh[h] * jnp.concatenate([rs, rs], axis=1) * gain_ref[:, dv_sl(h)] * jax.nn.sigmoid(og)
        h_ref[:, dv_sl(h)] = y.astype(h_ref.dtype)
        wk = (k_ref[:, dk_sl(h)].astype(F32) * (w_c[:, h:h + 1] * scale)).astype(BF16)
        upd.append(lax.dot_general(wk, jnp.concatenate([v_ref[:, dv_sl(h)], ones_l], axis=1), _TN,
                                   preferred_element_type=F32))
    for h in heads:
        state[h] = decay[:, h:h + 1] * state[h] + upd[h]


def mlstm(proj, g, b_gate, gain, c0, n0, m0, *, batch, seq, L, row0):
    chunks = 2 if seq % (2 * L) == 0 else 1
    rows = chunks * L
    nc = seq // rows
    bias = jnp.pad(b_gate.astype(F32), (0, LANES - 2 * A_HEADS))
    row_blk = lambda b, c: row0 // rows + b * nc + c
    st = lambda *s: pl.BlockSpec((1,) + s, lambda b, c: (b,) + (0,) * len(s))
    n0_lanes = jnp.broadcast_to(n0[..., None], n0.shape + (LANES,))
    m_lanes = jnp.pad(m0, ((0, 0), (0, LANES - A_HEADS)))[:, None, :]
    h, c, n, m = pl.pallas_call(
        functools.partial(_mlstm_body, L=L, chunks=chunks),
        grid=(batch, nc),
        in_specs=[pl.BlockSpec((rows, A_HEADS * A_DK), lambda b, c: (row_blk(b, c), 0)),
                  pl.BlockSpec((rows, A_HEADS * A_DK), lambda b, c: (row_blk(b, c), 1)),
                  pl.BlockSpec((rows, D_MODEL), lambda b, c: (row_blk(b, c), 1)),
                  pl.BlockSpec((rows, D_MODEL), lambda b, c: (row_blk(b, c), 2)),
                  pl.BlockSpec((rows, LANES), lambda b, c: (row_blk(b, c), 0)),
                  pl.BlockSpec((1, LANES), lambda b, c: (0, 0)),
                  pl.BlockSpec((1, D_MODEL), lambda b, c: (0, 0)),
                  st(A_HEADS, A_DK, A_DV), st(A_HEADS, A_DK, LANES), st(1, LANES)],
        out_specs=[pl.BlockSpec((rows, D_MODEL), lambda b, c: (b * nc + c, 0)),
                   st(A_HEADS, A_DK, A_DV), st(A_HEADS, A_DK, LANES), st(1, LANES)],
        out_shape=[jax.ShapeDtypeStruct((batch * seq, D_MODEL), BF16),
                   jax.ShapeDtypeStruct((batch, A_HEADS, A_DK, A_DV), F32),
                   jax.ShapeDtypeStruct((batch, A_HEADS, A_DK, LANES), F32),
                   jax.ShapeDtypeStruct((batch, 1, LANES), F32)],
        scratch_shapes=[pltpu.VMEM((A_HEADS, A_DK, A_STATE), F32)],
        compiler_params=_params("parallel", "arbitrary"),
        name="mlstm",
    )(proj, proj, proj, proj, g, bias[None, :], gain.astype(F32)[None, :], c0, n0_lanes, m_lanes)
    return h, c, n[..., 0], m[:, 0, :A_HEADS]


LOG2E = 1.4426950408889634
Q_SCALE = B_DH ** -0.5 * LOG2E


def _scale_q(q):
    return (q.astype(F32) * Q_SCALE).astype(BF16)


def _attn_prompt_body(q_ref, k_ref, v_ref, bias_ref, o_ref, kt_ref, vt_ref, qs, kpad, vpad):
    kt_ref[0] = k_ref[SEQ - REACH:, :].astype(F32)
    vt_ref[0] = v_ref[SEQ - REACH:, :].astype(F32)
    qs[...] = _scale_q(q_ref[...])
    kpad[0:REACH, :] = jnp.zeros((REACH, B_DH), BF16)
    kpad[REACH:, :] = k_ref[...]
    vpad[0:REACH, 0:B_DH] = jnp.zeros((REACH, B_DH), BF16)
    vpad[REACH:, 0:B_DH] = v_ref[...]
    vpad[:, B_DH:] = jnp.ones((SEQ + REACH, B_DH), BF16)

    def scores(t):
        r0 = t * ATT_TQ
        kb = kpad[r0:r0 + ATT_TK, :]
        s = lax.dot_general(qs[r0:r0 + ATT_TQ, :], kb, _NT, preferred_element_type=F32) + bias_ref[0]
        if r0 < REACH:
            j = lax.broadcasted_iota(jnp.int32, (ATT_TQ, ATT_TK), 1)
            s = jnp.where(j + r0 >= REACH, s, NEG)
        return s

    def finish(t, s):
        r0 = t * ATT_TQ
        p = jnp.exp2(s - jnp.max(s, axis=-1, keepdims=True)).astype(BF16)
        acc = jnp.dot(p, vpad[r0:r0 + ATT_TK, :], preferred_element_type=F32)
        o_ref[r0:r0 + ATT_TQ, :] = (acc[:, :B_DH] / acc[:, B_DH:]).astype(o_ref.dtype)

    n_tiles = SEQ // ATT_TQ
    s_next = scores(0)
    for t in range(n_tiles):
        s_cur = s_next
        if t + 1 < n_tiles:
            s_next = scores(t + 1)
        finish(t, s_cur)


def band_bias(rel_table, nq, nk):
    w = nq + nk
    rel = jnp.clip(REACH + nq - 1 - jnp.arange(w), -REL_CLIP, REL_CLIP) + REL_CLIP
    r = rel_table.astype(F32)[:, rel]
    heads = r.shape[0]
    skew = jnp.broadcast_to(r[:, None, :], (heads, nq, w)).reshape(heads, nq * w)
    skew = skew[:, :nq * (w - 1)].reshape(heads, nq, w - 1)
    bias = skew[:, :, nq - 1:nq - 1 + nk]
    i = jnp.arange(nq)[:, None]
    j = jnp.arange(nk)[None, :]
    dc = j // CHUNK - i // CHUNK
    allowed = (dc >= 0) & (dc <= PREV_CHUNKS)
    return jnp.where(allowed[None], bias * LOG2E, NEG)


def attn_prompt(qkv, rel_table):
    bias = band_bias(rel_table, ATT_TQ, ATT_TK)
    tail = pl.BlockSpec((1, REACH, B_DH), lambda b, h: (b, 0, h))
    return pl.pallas_call(
        _attn_prompt_body,
        grid=(BATCH, B_HEADS),
        in_specs=[pl.BlockSpec((SEQ, B_DH), lambda b, h: (b, h)),
                  pl.BlockSpec((SEQ, B_DH), lambda b, h: (b, B_HEADS + h)),
                  pl.BlockSpec((SEQ, B_DH), lambda b, h: (b, 2 * B_HEADS + h)),
                  pl.BlockSpec((1, ATT_TQ, ATT_TK), lambda b, h: (h, 0, 0))],
        out_specs=[pl.BlockSpec((SEQ, B_DH), lambda b, h: (b, h)), tail, tail],
        out_shape=[jax.ShapeDtypeStruct((N_PROMPT, D_MODEL), BF16),
                   jax.ShapeDtypeStruct((BATCH, REACH, D_MODEL), F32),
                   jax.ShapeDtypeStruct((BATCH, REACH, D_MODEL), F32)],
        scratch_shapes=[pltpu.VMEM((SEQ, B_DH), BF16),
                        pltpu.VMEM((SEQ + REACH, B_DH), BF16),
                        pltpu.VMEM((SEQ + REACH, 2 * B_DH), BF16)],
        compiler_params=_params("parallel", "parallel"),
        name="attn_prompt",
    )(qkv, qkv, qkv, bias)


def _attn_sample_body(q_ref, kn_ref, vn_ref, ck_ref, cv_ref, bias_c_ref, bias_n_ref, o_ref, ks_ref, vs_ref):
    ks_ref[0] = kn_ref[...].astype(F32)
    vs_ref[0] = vn_ref[...].astype(F32)
    for h in range(B_HEADS):
        sl = slice(h * B_DH, (h + 1) * B_DH)
        q = _scale_q(q_ref[:, sl])
        kc = ck_ref[pl.ds(h, REACH, stride=B_HEADS), :].astype(BF16)
        vc = cv_ref[pl.ds(h, REACH, stride=B_HEADS), :].astype(BF16)
        s_c = lax.dot_general(q, kc, _NT, preferred_element_type=F32) + bias_c_ref[h]
        s_n = lax.dot_general(q, kn_ref[:, sl], _NT, preferred_element_type=F32) + bias_n_ref[h]
        m = jnp.maximum(jnp.max(s_c, axis=-1, keepdims=True), jnp.max(s_n, axis=-1, keepdims=True))
        p_c = jnp.exp2(s_c - m)
        p_n = jnp.exp2(s_n - m)
        l = jnp.sum(p_c, axis=-1, keepdims=True) + jnp.sum(p_n, axis=-1, keepdims=True)
        dot = functools.partial(jnp.dot, preferred_element_type=F32)
        o_ref[:, sl] = ((dot(p_c.astype(BF16), vc) + dot(p_n.astype(BF16), vn_ref[:, sl])) / l).astype(o_ref.dtype)


def attn_sample(qkv, cache_k, cache_v, layer, rel_table):
    bias = band_bias(rel_table, DEC_SEQ, REACH + DEC_SEQ)
    cache_rows = REACH * B_HEADS
    ck = cache_k.reshape(cache_k.shape[0], DEC_BATCH, cache_rows, B_DH)
    cv = cache_v.reshape(cache_v.shape[0], DEC_BATCH, cache_rows, B_DH)
    row = lambda c: pl.BlockSpec((DEC_SEQ, D_MODEL), lambda b: (N_PROMPT // DEC_SEQ + b, c))
    cache = pl.BlockSpec((None, None, cache_rows, B_DH), lambda b: (layer, b, 0, 0))
    new = pl.BlockSpec((1, DEC_SEQ, D_MODEL), lambda b: (b, 0, 0))
    return pl.pallas_call(
        _attn_sample_body,
        grid=(DEC_BATCH,),
        in_specs=[row(0), row(1), row(2), cache, cache,
                  pl.BlockSpec((B_HEADS, DEC_SEQ, REACH), lambda b: (0, 0, 0)),
                  pl.BlockSpec((B_HEADS, DEC_SEQ, DEC_SEQ), lambda b: (0, 0, 0))],
        out_specs=[pl.BlockSpec((DEC_SEQ, D_MODEL), lambda b: (b, 0)), new, new],
        out_shape=[jax.ShapeDtypeStruct((N_SAMPLE, D_MODEL), BF16),
                   jax.ShapeDtypeStruct((DEC_BATCH, DEC_SEQ, D_MODEL), F32),
                   jax.ShapeDtypeStruct((DEC_BATCH, DEC_SEQ, D_MODEL), F32)],
        compiler_params=_params("parallel"),
        name="attn_sample",
    )(qkv, qkv, qkv, ck, cv, bias[:, :, :REACH], bias[:, :, REACH:])


def _ln(z, g, b):
    mu = jnp.mean(z, axis=-1, keepdims=True)
    zc = z - mu
    var = jnp.mean(zc * zc, axis=-1, keepdims=True)
    return zc * lax.rsqrt(var + LN_EPS) * g + b


PACK_ROWS = D_MODEL // 2 // LANES
HI_MASK = -65536


def _store_packed(ref, v, token0=0):
    t = v.shape[0]
    half = D_MODEL // 2
    hi = lax.bitcast_convert_type(v[:, :half].astype(BF16).astype(F32), jnp.int32)
    lo = lax.bitcast_convert_type(v[:, half:].astype(BF16).astype(F32), jnp.int32)
    words = (hi & HI_MASK) | lax.shift_right_logical(lo, 16)
    for a in range(PACK_ROWS):
        ref[pl.ds(token0 * PACK_ROWS + a, t, stride=PACK_ROWS), :] = words[:, a * LANES:(a + 1) * LANES]


def _load_packed(ref, t):
    words = jnp.concatenate([ref[pl.ds(a, t, stride=PACK_ROWS), :] for a in range(PACK_ROWS)], axis=1)
    hi = lax.bitcast_convert_type(words & HI_MASK, F32)
    lo = lax.bitcast_convert_type(lax.shift_left(words, 16), F32)
    return jnp.concatenate([hi, lo], axis=1)


def _mix_ln_router_body(hp_ref, hs_ref, w_ref, xa_ref, xb_ref, g_ref, b_ref, rw_ref, rb_ref,
                        xp_ref, wt_ref, cls_ref, y_even, y_odd, *, prompt_tiles, n_tiles, x_split):
    i = pl.program_id(0)
    from_prompt = jnp.minimum(i, n_tiles - 1) < prompt_tiles

    @pl.when(i == 0)
    def _():
        y_odd[...] = jnp.zeros_like(y_odd)

    half = D_MODEL // 2

    def run(y_new, y_old):
        h = jnp.where(from_prompt, hp_ref[...], hs_ref[...])
        y_new[:, :half] = jnp.dot(h, w_ref[:, :half], preferred_element_type=F32)
        x = xa_ref[...]
        if x_split:
            x = jnp.where(jnp.maximum(i - 1, 0) < prompt_tiles, x, xb_ref[...])
        xn = _ln(ALPHA * x + y_old[...], g_ref[...], b_ref[...])
        logits = _dot_hilo(xn, rw_ref[...]) + rb_ref[...]
        y_new[:, half:] = jnp.dot(h, w_ref[:, half:], preferred_element_type=F32)
        _store_packed(xp_ref, xn)
        _route(logits, wt_ref, cls_ref)

    @pl.when(i % 2 == 0)
    def _():
        run(y_even, y_odd)

    @pl.when(i % 2 == 1)
    def _():
        run(y_odd, y_even)


def mix_ln_router(h_p, h_s, w, x, g, b, w_group, b_group, w_expert, b_expert, x_tail=None, tm=512):
    n = h_p.shape[0] + h_s.shape[0]
    n_tiles = n // tm
    prompt_tiles = h_p.shape[0] // tm
    sample_tiles = h_s.shape[0] // tm
    assert prompt_tiles * tm == h_p.shape[0] and (prompt_tiles + sample_tiles) * tm == n
    mm_tile = lambda i: jnp.minimum(i, n_tiles - 1)
    ep_tile = lambda i: jnp.maximum(i - 1, 0)
    hp_spec = pl.BlockSpec((tm, D_MODEL), lambda i: (jnp.minimum(mm_tile(i), prompt_tiles - 1), 0))
    hs_spec = pl.BlockSpec((tm, D_MODEL), lambda i: (jnp.maximum(mm_tile(i) - prompt_tiles, 0), 0))
    rw = jnp.pad(jnp.concatenate([w_group, w_expert], axis=1), ((0, 0), (0, LANES - N_GROUPS - N_EXPERTS)))
    rb = jnp.pad(jnp.concatenate([b_group, b_expert]).astype(F32), (0, LANES - N_GROUPS - N_EXPERTS))
    vec = pl.BlockSpec((1, D_MODEL), lambda i: (0, 0))
    lane_row = pl.BlockSpec((tm, LANES), lambda i: (ep_tile(i), 0))
    if x_tail is None:
        xa, xa_spec = x, pl.BlockSpec((tm, D_MODEL), lambda i: (ep_tile(i), 0))
        xb, xb_spec = x, pl.BlockSpec(memory_space=pl.ANY)
    else:
        assert x.shape[0] == h_p.shape[0] and x_tail.shape[0] == h_s.shape[0]
        xa, xa_spec = x, pl.BlockSpec((tm, D_MODEL), lambda i: (jnp.minimum(ep_tile(i), prompt_tiles - 1), 0))
        xb, xb_spec = x_tail, pl.BlockSpec((tm, D_MODEL), lambda i: (jnp.maximum(ep_tile(i) - prompt_tiles, 0), 0))
    return pl.pallas_call(
        functools.partial(_mix_ln_router_body, prompt_tiles=prompt_tiles, n_tiles=n_tiles,
                          x_split=x_tail is not None),
        grid=(n_tiles + 1,),
        in_specs=[hp_spec, hs_spec, pl.BlockSpec((D_MODEL, D_MODEL), lambda i: (0, 0)), xa_spec, xb_spec, vec, vec,
                  pl.BlockSpec((D_MODEL, LANES), lambda i: (0, 0)), pl.BlockSpec((1, LANES), lambda i: (0, 0))],
        out_specs=[pl.BlockSpec((tm * PACK_ROWS, LANES), lambda i: (ep_tile(i), 0)), lane_row, lane_row],
        out_shape=[jax.ShapeDtypeStruct((n * PACK_ROWS, LANES), jnp.int32),
                   jax.ShapeDtypeStruct((n, LANES), F32), jax.ShapeDtypeStruct((n, LANES), jnp.int32)],
        scratch_shapes=[pltpu.VMEM((tm, D_MODEL), F32), pltpu.VMEM((tm, D_MODEL), F32)],
        compiler_params=_params("arbitrary"),
        name="mix_ln_router",
    )(h_p, h_s, w, xa, xb, g[None, :], b[None, :], rw, rb[None, :])


GATE_COL_BLOCK = (2 * A_HEADS * A_DK + 2 * D_MODEL) // LANES


def _gate_spec(mlstm_layer):
    return pl.BlockSpec((None, D_MODEL, LANES), lambda i: (mlstm_layer, 0, GATE_COL_BLOCK))


def _gate_preacts(x, wg_ref):
    lane = lax.broadcasted_iota(jnp.int32, wg_ref.shape, 1)
    return _dot_hilo(x, jnp.where(lane < 2 * A_HEADS, wg_ref[...], 0.0))


def _add_ln_body(x_ref, y_ref, g_ref, b_ref, *rest):
    t = rest[-1].shape[0]
    xn = _ln(ALPHA * _load_packed(x_ref, t) + _load_packed(y_ref, t), g_ref[...], b_ref[...])
    if len(rest) == 2:
        xo_ref, xbo_ref = rest
    else:
        wg_ref, xo_ref, xbo_ref, go_ref = rest
        go_ref[...] = _gate_preacts(xn, wg_ref)
    xo_ref[...] = xn
    xbo_ref[...] = xn.astype(BF16)


def residual_ln(x_packed, y_packed, g, b, a_w_in=None, mlstm_layer=None, tm=512):
    n = x_packed.shape[0] // PACK_ROWS
    row = pl.BlockSpec((tm, D_MODEL), lambda i: (i, 0))
    vec = pl.BlockSpec((1, D_MODEL), lambda i: (0, 0))
    packed = pl.BlockSpec((tm * PACK_ROWS, LANES), lambda i: (i, 0))
    in_specs = [packed, packed, vec, vec]
    out_specs = [row, row]
    out_shape = [jax.ShapeDtypeStruct((n, D_MODEL), F32), jax.ShapeDtypeStruct((n, D_MODEL), BF16)]
    args = [x_packed, y_packed, g[None, :], b[None, :]]
    if a_w_in is not None:
        in_specs.append(_gate_spec(mlstm_layer))
        out_specs.append(pl.BlockSpec((tm, LANES), lambda i: (i, 0)))
        out_shape.append(jax.ShapeDtypeStruct((n, LANES), F32))
        args.append(a_w_in)
    return pl.pallas_call(
        _add_ln_body,
        grid=(n // tm,),
        in_specs=in_specs,
        out_specs=out_specs,
        out_shape=out_shape,
        compiler_params=_params("parallel"),
        name="residual_ln",
    )(*args)


def _embed_body(xp_ref, xs_ref, wg_ref, xbo_ref, go_ref, *, prompt_tiles):
    x = jnp.where(pl.program_id(0) < prompt_tiles, xp_ref[...], xs_ref[...])
    xbo_ref[...] = x.astype(BF16)
    go_ref[...] = _gate_preacts(x, wg_ref)


def embed(x_p, x_s, a_w_in, tm=512):
    n = x_p.shape[0] + x_s.shape[0]
    prompt_tiles = x_p.shape[0] // tm
    assert prompt_tiles * tm == x_p.shape[0] and n % tm == 0
    row = pl.BlockSpec((tm, D_MODEL), lambda i: (i, 0))
    return pl.pallas_call(
        functools.partial(_embed_body, prompt_tiles=prompt_tiles),
        grid=(n // tm,),
        in_specs=[pl.BlockSpec((tm, D_MODEL), lambda i: (jnp.minimum(i, prompt_tiles - 1), 0)),
                  pl.BlockSpec((tm, D_MODEL), lambda i: (jnp.maximum(i - prompt_tiles, 0), 0)),
                  _gate_spec(0)],
        out_specs=[row, pl.BlockSpec((tm, LANES), lambda i: (i, 0))],
        out_shape=[jax.ShapeDtypeStruct((n, D_MODEL), BF16), jax.ShapeDtypeStruct((n, LANES), F32)],
        compiler_params=_params("parallel"),
        name="embed",
    )(x_p, x_s, a_w_in)


def _add_ln_split_body(x_ref, y_ref, g_ref, b_ref, op_ref, os_ref, *, prompt_tiles):
    t = op_ref.shape[0]
    xn = _ln(ALPHA * _load_packed(x_ref, t) + _load_packed(y_ref, t), g_ref[...], b_ref[...])
    i = pl.program_id(0)

    @pl.when(i < prompt_tiles)
    def _():
        op_ref[...] = xn

    @pl.when(i >= prompt_tiles)
    def _():
        os_ref[...] = xn


def residual_ln_split(x_packed, y_packed, g, b, n_prompt, tm=512):
    n = x_packed.shape[0] // PACK_ROWS
    prompt_tiles = n_prompt // tm
    assert prompt_tiles * tm == n_prompt and n % tm == 0
    vec = pl.BlockSpec((1, D_MODEL), lambda i: (0, 0))
    packed = pl.BlockSpec((tm * PACK_ROWS, LANES), lambda i: (i, 0))
    return pl.pallas_call(
        functools.partial(_add_ln_split_body, prompt_tiles=prompt_tiles),
        grid=(n // tm,),
        in_specs=[packed, packed, vec, vec],
        out_specs=[pl.BlockSpec((tm, D_MODEL), lambda i: (jnp.minimum(i, prompt_tiles - 1), 0)),
                   pl.BlockSpec((tm, D_MODEL), lambda i: (jnp.maximum(i - prompt_tiles, 0), 0))],
        out_shape=[jax.ShapeDtypeStruct((n_prompt, D_MODEL), F32),
                   jax.ShapeDtypeStruct((n - n_prompt, D_MODEL), F32)],
        compiler_params=_params("arbitrary"),
        name="residual_ln_split",
    )(x_packed, y_packed, g[None, :], b[None, :])


ROUTER_E0 = N_GROUPS


def _route(logits, wt_ref, cls_ref):
    lane = lax.broadcasted_iota(jnp.int32, logits.shape, 1).astype(F32)
    big = float(LANES)
    rmax = lambda a: jnp.max(a, axis=-1, keepdims=True)
    rmin = lambda a: jnp.min(a, axis=-1, keepdims=True)
    is_g = lane < N_GROUPS
    gl = jnp.where(is_g, logits, -jnp.inf)
    gmax = rmax(gl)
    gsel = rmin(jnp.where(gl == gmax, lane, big))
    gsum = jnp.sum(jnp.where(is_g, jnp.exp(logits - gmax), 0.0), axis=-1, keepdims=True)
    g_w = 1.0 / gsum
    e_lo = ROUTER_E0 + EXP_PER_GROUP * gsel
    in_grp = (lane >= e_lo) & (lane < e_lo + EXP_PER_GROUP)
    el = jnp.where(in_grp, logits, -jnp.inf)
    e1 = rmax(el)
    i1 = rmin(jnp.where(el == e1, lane, big))
    el2 = jnp.where(lane == i1, -jnp.inf, el)
    e2 = rmax(el2)
    i2 = rmin(jnp.where(el2 == e2, lane, big))
    t = jnp.exp(e2 - e1)
    p1 = 1.0 / (1.0 + t)
    w1 = p1 * g_w
    w2 = (t * p1) * g_w
    a1 = i1 - e_lo
    a2 = i2 - e_lo
    first_low = a1 < a2
    lo = jnp.where(first_low, a1, a2)
    hi = jnp.where(first_low, a2, a1)
    w_lo = jnp.where(first_low, w1, w2)
    w_hi = jnp.where(first_low, w2, w1)
    off = jnp.where(lo == 0.0, 0.0, jnp.where(lo == 1.0, 3.0, 5.0))
    cls = gsel * float(len(PAIRS)) + off + (hi - lo - 1.0)
    wt_ref[...] = jnp.where(lane < LANES // 2, w_lo, w_hi)
    cls_ref[...] = jnp.broadcast_to(cls, logits.shape).astype(jnp.int32)


def _expert_body(ea_ref, eb_ref, valid_ref, x_ref, wt_ref, wga_ref, wua_ref, wda_ref,
                 wgb_ref, wub_ref, wdb_ref, y_ref):
    i = pl.program_id(0)

    @pl.when(valid_ref[i] == 0)
    def _():
        y_ref[...] = jnp.zeros_like(y_ref)

    @pl.when(valid_ref[i] != 0)
    def _():
        x = _load_packed(x_ref, MOE_TILE).astype(BF16)
        dot = functools.partial(jnp.dot, preferred_element_type=F32)

        ga, ua = dot(x, wga_ref[...]), dot(x, wua_ref[...])
        gb, ub = dot(x, wgb_ref[...]), dot(x, wub_ref[...])
        ya = dot((jax.nn.silu(ga) * ua).astype(BF16), wda_ref[...])
        yb = dot((jax.nn.silu(gb) * ub).astype(BF16), wdb_ref[...])
        w_lo = wt_ref[:, 0:1]
        w_hi = wt_ref[:, LANES // 2:LANES // 2 + 1]
        _store_packed(y_ref, w_lo * ya + w_hi * yb)


def experts(xs, wts, tile_ea, tile_eb, tile_valid, w_gate, w_up, w_down):
    wspec = lambda shape, which: pl.BlockSpec(
        (None,) + shape, lambda i, ea, eb, va: ((ea, eb)[which][i], 0, 0))
    up = (D_MODEL, D_EXPERT)
    down = (D_EXPERT, D_MODEL)
    packed = pl.BlockSpec((MOE_TILE * PACK_ROWS, LANES), lambda i, ea, eb, va: (i, 0))
    grid_spec = pltpu.PrefetchScalarGridSpec(
        num_scalar_prefetch=3,
        grid=(MOE_TILES,),
        in_specs=[packed,
                  pl.BlockSpec((MOE_TILE, LANES), lambda i, ea, eb, va: (i, 0)),
                  wspec(up, 0), wspec(up, 0), wspec(down, 0),
                  wspec(up, 1), wspec(up, 1), wspec(down, 1)],
        out_specs=packed,
    )
    return pl.pallas_call(
        _expert_body,
        grid_spec=grid_spec,
        out_shape=jax.ShapeDtypeStruct((MOE_ROWS * PACK_ROWS, LANES), jnp.int32),
        compiler_params=_params("arbitrary"),
        name="experts",
    )(tile_ea, tile_eb, tile_valid, xs, wts, w_gate, w_up, w_down, w_gate, w_up, w_down)


def sc_dispatch(xp, wts, pos, fill_rows):
    n = pos.shape[0]
    n_fill = fill_rows.shape[0]
    info = plsc.get_sparse_core_info()
    n_workers = info.num_cores * info.num_subcores
    assert n_workers == SC_WORKERS
    per_worker = n // n_workers
    steps = per_worker // SC_TOKENS
    fill_steps = n_fill // n_workers // SC_FILL
    assert steps * SC_TOKENS * n_workers == n and steps % 2 == 0
    assert fill_steps * SC_FILL * n_workers == n_fill
    mesh = plsc.VectorSubcoreMesh(core_axis_name="core", subcore_axis_name="subcore")
    dma = pltpu.SemaphoreType.DMA

    @functools.partial(
        pl.kernel,
        out_type=[jax.ShapeDtypeStruct((n + n_fill, PACK_ROWS, LANES), xp.dtype),
                  jax.ShapeDtypeStruct((n + n_fill, LANES), wts.dtype)],
        mesh=mesh,
        scratch_types=[pltpu.VMEM((steps, SC_TOKENS), jnp.int32),
                       pltpu.VMEM((fill_steps, SC_FILL), jnp.int32),
                       pltpu.VMEM((2, SC_TOKENS, PACK_ROWS, LANES), xp.dtype),
                       pltpu.VMEM((2, SC_TOKENS, LANES), wts.dtype),
                       dma((2,)), dma((2,)), dma((2,)), dma((2,))],
    )
    def scatter(x_hbm, w_hbm, pos_hbm, fill_hbm, zx_hbm, zw_hbm, xs_hbm, ws_hbm,
                pos_v, fill_v, xbuf, wbuf, read_x, read_w, write_x, write_w):
        worker = lax.axis_index("subcore") * info.num_cores + lax.axis_index("core")
        base = worker * per_worker
        pltpu.sync_copy(pos_hbm.at[worker], pos_v)
        pltpu.sync_copy(fill_hbm.at[worker], fill_v)

        def reads(s, b):
            rows = pl.ds(base + s * SC_TOKENS, SC_TOKENS)
            return (pltpu.make_async_copy(x_hbm.at[rows], xbuf.at[b], read_x.at[b]),
                    pltpu.make_async_copy(w_hbm.at[rows], wbuf.at[b], read_w.at[b]))

        def writes(s, b):
            return (pltpu.make_async_copy(xbuf.at[b], xs_hbm.at[pos_v.at[s]], write_x.at[b]),
                    pltpu.make_async_copy(wbuf.at[b], ws_hbm.at[pos_v.at[s]], write_w.at[b]))

        def start(copies):
            for c in copies:
                c.start()

        def wait(copies):
            for c in copies:
                c.wait()

        start(reads(0, 0))

        @pl.loop(0, steps, step=2)
        def _(s):
            wait(reads(s, 0))
            start(reads(s + 1, 1))
            start(writes(s, 0))
            wait(reads(s + 1, 1))
            wait(writes(s, 0))

            @pl.when(s + 2 < steps)
            def _():
                start(reads(s + 2, 0))

            start(writes(s + 1, 1))
            wait(writes(s + 1, 1))

        zx = xbuf.at[0, pl.ds(0, SC_FILL)]
        zw = wbuf.at[0, pl.ds(0, SC_FILL)]
        pltpu.sync_copy(zx_hbm, zx)
        pltpu.sync_copy(zw_hbm, zw)
        fills = [(pltpu.make_async_copy(zx, xs_hbm.at[fill_v.at[s]], write_x.at[0]),
                  pltpu.make_async_copy(zw, ws_hbm.at[fill_v.at[s]], write_w.at[0])) for s in range(fill_steps)]
        for f in fills:
            start(f)
        for f in fills:
            wait(f)

    xs, ws = scatter(xp.reshape(n, PACK_ROWS, LANES), wts,
                     pos.reshape(n_workers, steps, SC_TOKENS), fill_rows.reshape(n_workers, fill_steps, SC_FILL),
                     jnp.zeros((SC_FILL, PACK_ROWS, LANES), xp.dtype), jnp.zeros((SC_FILL, LANES), wts.dtype))
    return xs.reshape((n + n_fill) * PACK_ROWS, LANES), ws


def sc_gather_tokens(x, idx):
    n = x.shape[0] // PACK_ROWS
    p = idx.shape[0]
    info = plsc.get_sparse_core_info()
    n_workers = info.num_cores * info.num_subcores
    assert n_workers == SC_WORKERS
    per_worker = p // n_workers
    steps = per_worker // SC_TOKENS
    assert steps * SC_TOKENS * n_workers == p and steps % 2 == 0
    mesh = plsc.VectorSubcoreMesh(core_axis_name="core", subcore_axis_name="subcore")
    dma = pltpu.SemaphoreType.DMA

    @functools.partial(
        pl.kernel,
        out_type=jax.ShapeDtypeStruct((p, PACK_ROWS, LANES), x.dtype),
        mesh=mesh,
        scratch_types=[pltpu.VMEM((steps, SC_TOKENS), jnp.int32),
                       pltpu.VMEM((2, SC_TOKENS, PACK_ROWS, LANES), x.dtype),
                       dma((2,)), dma((2,))],
    )
    def gather(x_hbm, i_hbm, o_hbm, idx_v, buf, read_sem, write_sem):
        worker = lax.axis_index("subcore") * info.num_cores + lax.axis_index("core")
        base = worker * per_worker
        pltpu.sync_copy(i_hbm.at[worker], idx_v)

        def read(s, b):
            return pltpu.make_async_copy(x_hbm.at[idx_v.at[s]], buf.at[b], read_sem.at[b])

        def write(s, b):
            return pltpu.make_async_copy(buf.at[b], o_hbm.at[pl.ds(base + s * SC_TOKENS, SC_TOKENS)], write_sem.at[b])

        read(0, 0).start()

        @pl.loop(0, steps, step=2)
        def _(s):
            read(s, 0).wait()
            read(s + 1, 1).start()
            write(s, 0).start()
            read(s + 1, 1).wait()
            write(s, 0).wait()

            @pl.when(s + 2 < steps)
            def _():
                read(s + 2, 0).start()

            write(s + 1, 1).start()
            write(s + 1, 1).wait()

    out = gather(x.reshape(n, PACK_ROWS, LANES), idx.reshape(n_workers, steps, SC_TOKENS))
    return out.reshape(p * PACK_ROWS, LANES)


_CLASS_EA = [g * EXP_PER_GROUP + lo for g in range(N_GROUPS) for lo, hi in PAIRS]
_CLASS_EB = [g * EXP_PER_GROUP + hi for g in range(N_GROUPS) for lo, hi in PAIRS]


def dispatch_plan(cls):
    onehot = (cls[:, None] == jnp.arange(N_CLASSES, dtype=jnp.int32)[None, :]).astype(jnp.int32)
    csum = jnp.cumsum(onehot, axis=0)
    rank = jnp.sum(csum * onehot, axis=1) - 1
    counts = csum[-1]
    tiles_per = (counts + MOE_TILE - 1) // MOE_TILE
    tile_end = jnp.cumsum(tiles_per)
    tile_start = tile_end - tiles_per
    pos = jnp.sum(onehot * tile_start[None, :], axis=1) * MOE_TILE + rank
    n = cls.shape[0]
    pad = tiles_per * MOE_TILE - counts
    pad_end = jnp.cumsum(pad)
    k = jnp.arange(MOE_ROWS - n, dtype=jnp.int32)
    k_cls = jnp.sum((k[:, None] >= pad_end[None, :]).astype(jnp.int32), axis=1)
    k_hot = (k_cls[:, None] == jnp.arange(N_CLASSES, dtype=jnp.int32)[None, :]).astype(jnp.int32)
    pad_row0 = tile_start * MOE_TILE + counts - (pad_end - pad)
    fill_rows = k + jnp.where(k_cls < N_CLASSES, jnp.sum(k_hot * pad_row0[None, :], axis=1),
                              tile_end[-1] * MOE_TILE - pad_end[-1])
    t = jnp.arange(MOE_TILES, dtype=jnp.int32)
    tile_cls = jnp.sum((t[:, None] >= tile_end[None, :]).astype(jnp.int32), axis=1)
    valid = (tile_cls < N_CLASSES).astype(jnp.int32)
    last_cls = jnp.max(jnp.where(counts > 0, jnp.arange(N_CLASSES, dtype=jnp.int32), 0))
    tile_cls = jnp.where(valid == 1, tile_cls, last_cls)
    ea = jnp.asarray(_CLASS_EA, jnp.int32)[tile_cls]
    eb = jnp.asarray(_CLASS_EB, jnp.int32)[tile_cls]
    return pos, fill_rows, ea, eb, valid


def moe(xp, wts, cls, e_w_gate, e_w_up, w_down, e_w_down, layer):
    pos, fill_rows, ea, eb, valid = dispatch_plan(cls[:, 0])
    xs, ws = sc_dispatch(xp, wts, pos, fill_rows)
    w_gate, w_up = cast_layer(e_w_gate, layer, after=pos), cast_layer(e_w_up, layer, after=pos)
    ys = experts(xs, ws, ea, eb, valid, w_gate, w_up, w_down)
    y = sc_gather_tokens(ys, pos)
    if layer + 1 == DEPTH:
        return y, None
    return lax.optimization_barrier((y, cast_layer(e_w_down, layer + 1, after=ys)))


def _rows(a):
    return a.reshape(-1, a.shape[-1])


def kernel(x_prompt, x_sample, state_C, state_n, state_m, cache_k, cache_v, a_w_in, a_b_gate, a_norm, a_w_out, b_w_in, b_rel, b_w_out, ln1_g, ln1_b, ln2_g, ln2_b, r_w_group, r_b_group, r_w_expert, r_b_expert, e_w_gate, e_w_up, e_w_down):
    hk = A_HEADS * A_DK
    n_main = 2 * hk + 2 * D_MODEL
    assert n_main == GATE_COL_BLOCK * LANES and a_w_in.shape[-1] == n_main + 2 * A_HEADS
    x, x_tail = _rows(x_prompt), _rows(x_sample)
    xb, g = embed(x, x_tail, a_w_in)
    w_down = cast_layer(e_w_down, 0)
    outs = {k: [] for k in ("Cp", "np", "mp", "kp", "vp", "Cs", "ns", "ms", "ks", "vs")}
    for layer in range(DEPTH):
        j = layer // 2
        if layer % 2 == 0:
            proj = matmul(xb, a_w_in, j, n_main, BF16)
            zc = jnp.zeros((BATCH, A_HEADS, A_DK, A_DV), F32)
            zn = jnp.zeros((BATCH, A_HEADS, A_DK), F32)
            zm = jnp.zeros((BATCH, A_HEADS), F32)
            h_p, c_p, n_p, m_p = mlstm(proj, g, a_b_gate[j], a_norm[j], zc, zn, zm,
                                       batch=BATCH, seq=SEQ, L=MLSTM_CHUNK, row0=0)
            h_s, c_s, n_s, m_s = mlstm(proj, g, a_b_gate[j], a_norm[j],
                                       state_C[j], state_n[j], state_m[j],
                                       batch=DEC_BATCH, seq=DEC_SEQ, L=DEC_SEQ, row0=N_PROMPT)
            outs["Cp"].append(c_p); outs["np"].append(n_p); outs["mp"].append(m_p)
            outs["Cs"].append(c_s); outs["ns"].append(n_s); outs["ms"].append(m_s)
            w_out = a_w_out[j]
        else:
            qkv = matmul(xb, b_w_in, j, 3 * D_MODEL, BF16)
            h_p, k_p, v_p = attn_prompt(qkv, b_rel[j])
            h_s, k_s, v_s = attn_sample(qkv, cache_k, cache_v, j, b_rel[j])
            heads = lambda a: a.reshape(a.shape[:2] + (B_HEADS, B_DH))
            outs["kp"].append(heads(k_p)); outs["vp"].append(heads(v_p))
            outs["ks"].append(heads(k_s)); outs["vs"].append(heads(v_s))
            w_out = b_w_out[j]
        xp, wts, cls = mix_ln_router(h_p, h_s, w_out.astype(BF16), x, ln1_g[layer], ln1_b[layer],
                                     r_w_group[layer], r_b_group[layer], r_w_expert[layer], r_b_expert[layer],
                                     x_tail=x_tail if layer == 0 else None)
        y, w_down = moe(xp, wts, cls, e_w_gate, e_w_up, w_down, e_w_down, layer)
        if layer < DEPTH - 1:
            if (layer + 1) % 2 == 0:
                x, xb, g = residual_ln(xp, y, ln2_g[layer], ln2_b[layer], a_w_in, (layer + 1) // 2)
            else:
                x, xb = residual_ln(xp, y, ln2_g[layer], ln2_b[layer])
        else:
            y_p, y_s = residual_ln_split(xp, y, ln2_g[layer], ln2_b[layer], N_PROMPT)
    st = lambda k: jnp.stack(outs[k])
    return (y_p.reshape(BATCH, SEQ, D_MODEL), y_s.reshape(DEC_BATCH, DEC_SEQ, D_MODEL),
            st("Cp"), st("np"), st("mp"), st("kp"), st("vp"),
            st("Cs"), st("ns"), st("ms"), st("ks"), st("vs"))
```

```python
import functools

import jax
import jax.numpy as jnp
from jax import lax
from jax.experimental import pallas as pl
from jax.experimental.pallas import tpu as pltpu
from jax.experimental.pallas import tpu_sc as plsc

F32 = jnp.float32
BF16 = jnp.bfloat16

D_MODEL = 2048
BATCH = 8
SEQ = 4096
DEPTH = 4
DEC_BATCH = 32
DEC_SEQ = 32
PAST_LEN = 1024
CHUNK = 64
A_HEADS = 8
A_DK = 128
A_DV = D_MODEL // A_HEADS
GATE_CAP = 15.0
B_HEADS = 16
B_DH = D_MODEL // B_HEADS
PREV_CHUNKS = 8
REACH = PREV_CHUNKS * CHUNK
REL_CLIP = 256
N_GROUPS = 4
EXP_PER_GROUP = 4
N_EXPERTS = N_GROUPS * EXP_PER_GROUP
D_EXPERT = D_MODEL // 4
ALPHA = (2 * DEPTH) ** 0.25
LN_EPS = 1e-5
RMS_EPS = 1e-6

N_PROMPT = BATCH * SEQ
N_SAMPLE = DEC_BATCH * DEC_SEQ
N_TOK = N_PROMPT + N_SAMPLE

VMEM_LIMIT = 56 * 1024 * 1024
LANES = 128

PAIRS = ((0, 1), (0, 2), (0, 3), (1, 2), (1, 3), (2, 3))
N_CLASSES = N_GROUPS * len(PAIRS)
MOE_TILE = 256
SC_WORKERS = 32
MOE_TILES = -(-(N_TOK // MOE_TILE + N_CLASSES) // SC_WORKERS) * SC_WORKERS
MOE_ROWS = MOE_TILES * MOE_TILE

MLSTM_CHUNK = 128
ATT_TQ = 256
ATT_TK = ATT_TQ + REACH
NEG = -1e30

SC_TOKENS = 48
SC_FILL = 32


def _params(*sem):
    return pltpu.CompilerParams(dimension_semantics=sem, vmem_limit_bytes=VMEM_LIMIT)


def _mm_body(x_ref, w_ref, o_ref, wb):
    @pl.when(pl.program_id(1) == 0)
    def _():
        wb[...] = w_ref[...].astype(BF16)

    o_ref[...] = jnp.dot(x_ref[...], wb[...], preferred_element_type=F32).astype(o_ref.dtype)


def matmul(x, w_all, layer, m, out_dtype, tm=1536, tn=1024):
    n, k = x.shape
    return pl.pallas_call(
        _mm_body,
        grid=(m // tn, n // tm),
        in_specs=[pl.BlockSpec((tm, k), lambda j, i: (i, 0)),
                  pl.BlockSpec((None, k, tn), lambda j, i: (layer, 0, j))],
        out_specs=pl.BlockSpec((tm, tn), lambda j, i: (i, j)),
        out_shape=jax.ShapeDtypeStruct((n, m), out_dtype),
        scratch_shapes=[pltpu.VMEM((k, tn), BF16)],
        compiler_params=_params("parallel", "arbitrary"),
        name="matmul",
    )(x, w_all)


def _cast_body(w_ref, *rest):
    o_ref = rest[-1]
    o_ref[...] = w_ref[...].astype(BF16)


def cast_layer(w_all, layer, after=None):
    e, a, b = w_all.shape[1:]
    per_step = 4
    in_specs = [pl.BlockSpec((None, per_step, a, b), lambda i: (layer, i, 0, 0))]
    args = [w_all]
    if after is not None:
        in_specs.append(pl.BlockSpec(memory_space=pl.ANY))
        args.append(after)
    return pl.pallas_call(
        _cast_body,
        grid=(e // per_step,),
        in_specs=in_specs,
        out_specs=pl.BlockSpec((per_step, a, b), lambda i: (i, 0, 0)),
        out_shape=jax.ShapeDtypeStruct((e, a, b), BF16),
        compiler_params=_params("parallel"),
        name="cast_layer",
    )(*args)


def _split2(x):
    hi = x.astype(BF16)
    lo = (x - hi.astype(F32)).astype(BF16)
    return hi, lo


_NT = (((1,), (1,)), ((), ()))
_TN = (((0,), (0,)), ((), ()))


def _dot_hilo(x, w):
    t = x.shape[0]
    xh, xl = _split2(x)
    wh, wl = _split2(w)
    r = jnp.dot(jnp.concatenate([xh, xl], axis=0), jnp.concatenate([wh, wl], axis=1), preferred_element_type=F32)
    return r[:t, :LANES] + (r[:t, LANES:] + r[t:, :LANES]) + r[t:, LANES:]


def _log_sigmoid(x):
    return jnp.minimum(x, 0.0) - jnp.log(1.0 + jnp.exp(-jnp.abs(x)))


def _split3(x):
    a = x.astype(BF16)
    r = x - a.astype(F32)
    b = r.astype(BF16)
    c = (r - b.astype(F32)).astype(BF16)
    return a, b, c


A_STATE = A_DV + LANES


def _cummax_rows(x):
    n = x.shape[0]
    row = lax.broadcasted_iota(jnp.int32, x.shape, 0)
    k = 1
    while k < n:
        x = jnp.maximum(x, jnp.where(row >= k, pltpu.roll(x, k, axis=0), -jnp.inf))
        k *= 2
    return x


def _mlstm_body(q_ref, k_ref, v_ref, o_ref, g_ref, bias_ref, gain_ref, c0_ref, n0_ref, m0_ref,
                h_ref, c_out_ref, n_out_ref, m_ref, state, *, L, chunks):
    step = pl.program_id(1)

    @pl.when(step == 0)
    def _():
        state[:, :, :A_DV] = c0_ref[0]
        state[:, :, A_DV:] = n0_ref[0]
        m_ref[...] = m0_ref[...]

    for c in range(chunks):
        rows = pl.ds(c * L, L)
        _mlstm_chunk(q_ref.at[rows], k_ref.at[rows], v_ref.at[rows], o_ref.at[rows], g_ref.at[rows],
                     bias_ref, gain_ref, h_ref.at[rows], m_ref, state, L=L)

    @pl.when(step == pl.num_programs(1) - 1)
    def _():
        c_out_ref[0] = state[:, :, :A_DV]
        n_out_ref[0] = state[:, :, A_DV:]


def _mlstm_chunk(q_ref, k_ref, v_ref, o_ref, g_ref, bias_ref, gain_ref, h_ref, m_ref, state, *, L):
    dot = functools.partial(jnp.dot, preferred_element_type=F32)
    row = lax.broadcasted_iota(jnp.int32, (L, L), 0)
    col = lax.broadcasted_iota(jnp.int32, (L, L), 1)
    causal = col <= row
    tril = jnp.where(causal, 1.0, 0.0).astype(BF16)

    pre = GATE_CAP * jnp.tanh((g_ref[...] + bias_ref[...]) * (1.0 / GATE_CAP))
    ig = pre
    lf = _log_sigmoid(pltpu.roll(pre, LANES - A_HEADS, axis=1))
    a3, b3, c3 = _split3(lf)
    bt = dot(tril, a3) + (dot(tril, b3) + dot(tril, c3))
    a = ig - bt
    m_prev = m_ref[0]
    m_t = bt + jnp.maximum(m_prev, _cummax_rows(a))
    u = bt - m_t
    inter = jnp.exp(bt + m_prev - m_t)
    eminus = jnp.exp(-m_t)
    m_new = m_t[L - 1:L, :]
    bt_last = bt[L - 1:L, :]
    w_c = jnp.exp((bt_last - m_new) + a)
    decay = jnp.exp(bt_last + m_prev - m_new)
    m_ref[0] = m_new
    a_pad = a if L == LANES else jnp.concatenate([a, jnp.zeros((LANES - L, LANES), F32)], axis=0)
    a_t = jnp.transpose(a_pad)
    ie = jnp.concatenate([inter, eminus], axis=1).astype(BF16)
    sel_r = lax.broadcasted_iota(jnp.int32, (2 * LANES, 2 * LANES), 0)
    sel_c = lax.broadcasted_iota(jnp.int32, (2 * LANES, 2 * LANES), 1)
    same_half = (sel_r >= LANES) == (sel_c >= LANES)
    ones_l = jnp.ones((L, LANES), BF16)
    zeros_l = jnp.zeros((L, LANES), BF16)
    zeros_k = jnp.zeros((A_DK, LANES), BF16)
    mean_cols = jnp.full((A_DV, LANES), 1.0, BF16)

    scale = A_DK ** -0.5
    heads = range(A_HEADS)
    dk_sl = lambda h: slice(h * A_DK, (h + 1) * A_DK)
    dv_sl = lambda h: slice(h * A_DV, (h + 1) * A_DV)
    ie_b, qk = [], []
    for h in heads:
        sel = jnp.where(same_half & ((sel_r & (LANES - 1)) == h), 1.0, 0.0).astype(BF16)
        ie_b.append(dot(ie, sel))
        qk.append(lax.dot_general(q_ref[:, dk_sl(h)], k_ref[:, dk_sl(h)], _NT, preferred_element_type=F32))
    out = []
    for h in heads:
        d_mat = jnp.where(causal, jnp.exp(u[:, h:h + 1] + a_t[h:h + 1, :L]), 0.0)
        s = qk[h] * (d_mat * scale)
        q_i = (q_ref[:, dk_sl(h)].astype(F32) * ie_b[h][:, :LANES]).astype(BF16)
        c_b = state[h].astype(BF16)
        rhs = jnp.concatenate([
            jnp.concatenate([v_ref[:, dv_sl(h)], ones_l, zeros_l], axis=1),
            jnp.concatenate([c_b[:, :A_DV], zeros_k, c_b[:, A_DV:]], axis=1)], axis=0)
        out.append(dot(jnp.concatenate([s.astype(BF16), q_i], axis=1), rhs))
    hh, ms = [], []
    for h in heads:
        den = jnp.maximum(jnp.abs(out[h][:, A_DV:A_DV + LANES] + out[h][:, A_DV + LANES:]), ie_b[h][:, LANES:])
        r = 1.0 / den
        hh.append(out[h][:, :A_DV] * jnp.concatenate([r, r], axis=1))
        ms.append(dot((hh[h] * hh[h]).astype(BF16), mean_cols))
    upd = []
    for h in heads:
        rs = lax.rsqrt(ms[h] * (1.0 / A_DV) + RMS_EPS)
        og = o_ref[:, dv_sl(h)].astype(F32)
        y = hh[h] * jnp.concatenate([rs, rs], axis=1) * gain_ref[:, dv_sl(h)] * jax.nn.sigmoid(og)
        h_ref[:, dv_sl(h)] = y.astype(h_ref.dtype)
        wk = (k_ref[:, dk_sl(h)].astype(F32) * (w_c[:, h:h + 1] * scale)).astype(BF16)
        upd.append(lax.dot_general(wk, jnp.concatenate([v_ref[:, dv_sl(h)], ones_l], axis=1), _TN,
                                   preferred_element_type=F32))
    for h in heads:
        state[h] = decay[:, h:h + 1] * state[h] + upd[h]


def mlstm(proj, g, b_gate, gain, c0, n0, m0, *, batch, seq, L, row0):
    chunks = 2 if seq % (2 * L) == 0 else 1
    rows = chunks * L
    nc = seq // rows
    bias = jnp.pad(b_gate.astype(F32), (0, LANES - 2 * A_HEADS))
    row_blk = lambda b, c: row0 // rows + b * nc + c
    st = lambda *s: pl.BlockSpec((1,) + s, lambda b, c: (b,) + (0,) * len(s))
    n0_lanes = jnp.broadcast_to(n0[..., None], n0.shape + (LANES,))
    m_lanes = jnp.pad(m0, ((0, 0), (0, LANES - A_HEADS)))[:, None, :]
    h, c, n, m = pl.pallas_call(
        functools.partial(_mlstm_body, L=L, chunks=chunks),
        grid=(batch, nc),
        in_specs=[pl.BlockSpec((rows, A_HEADS * A_DK), lambda b, c: (row_blk(b, c), 0)),
                  pl.BlockSpec((rows, A_HEADS * A_DK), lambda b, c: (row_blk(b, c), 1)),
                  pl.BlockSpec((rows, D_MODEL), lambda b, c: (row_blk(b, c), 1)),
                  pl.BlockSpec((rows, D_MODEL), lambda b, c: (row_blk(b, c), 2)),
                  pl.BlockSpec((rows, LANES), lambda b, c: (row_blk(b, c), 0)),
                  pl.BlockSpec((1, LANES), lambda b, c: (0, 0)),
                  pl.BlockSpec((1, D_MODEL), lambda b, c: (0, 0)),
                  st(A_HEADS, A_DK, A_DV), st(A_HEADS, A_DK, LANES), st(1, LANES)],
        out_specs=[pl.BlockSpec((rows, D_MODEL), lambda b, c: (b * nc + c, 0)),
                   st(A_HEADS, A_DK, A_DV), st(A_HEADS, A_DK, LANES), st(1, LANES)],
        out_shape=[jax.ShapeDtypeStruct((batch * seq, D_MODEL), BF16),
                   jax.ShapeDtypeStruct((batch, A_HEADS, A_DK, A_DV), F32),
                   jax.ShapeDtypeStruct((batch, A_HEADS, A_DK, LANES), F32),
                   jax.ShapeDtypeStruct((batch, 1, LANES), F32)],
        scratch_shapes=[pltpu.VMEM((A_HEADS, A_DK, A_STATE), F32)],
        compiler_params=_params("parallel", "arbitrary"),
        name="mlstm",
    )(proj, proj, proj, proj, g, bias[None, :], gain.astype(F32)[None, :], c0, n0_lanes, m_lanes)
    return h, c, n[..., 0], m[:, 0, :A_HEADS]


LOG2E = 1.4426950408889634
Q_SCALE = B_DH ** -0.5 * LOG2E


def _scale_q(q):
    return (q.astype(F32) * Q_SCALE).astype(BF16)


def _attn_prompt_body(q_ref, k_ref, v_ref, bias_ref, o_ref, kt_ref, vt_ref, qs, kpad, vpad):
    kt_ref[0] = k_ref[SEQ - REACH:, :].astype(F32)
    vt_ref[0] = v_ref[SEQ - REACH:, :].astype(F32)
    qs[...] = _scale_q(q_ref[...])
    kpad[0:REACH, :] = jnp.zeros((REACH, B_DH), BF16)
    kpad[REACH:, :] = k_ref[...]
    vpad[0:REACH, 0:B_DH] = jnp.zeros((REACH, B_DH), BF16)
    vpad[REACH:, 0:B_DH] = v_ref[...]
    vpad[:, B_DH:] = jnp.ones((SEQ + REACH, B_DH), BF16)

    def scores(t):
        r0 = t * ATT_TQ
        kb = kpad[r0:r0 + ATT_TK, :]
        s = lax.dot_general(qs[r0:r0 + ATT_TQ, :], kb, _NT, preferred_element_type=F32) + bias_ref[0]
        if r0 < REACH:
            j = lax.broadcasted_iota(jnp.int32, (ATT_TQ, ATT_TK), 1)
            s = jnp.where(j + r0 >= REACH, s, NEG)
        return s

    def finish(t, s):
        r0 = t * ATT_TQ
        p = jnp.exp2(s - jnp.max(s, axis=-1, keepdims=True)).astype(BF16)
        acc = jnp.dot(p, vpad[r0:r0 + ATT_TK, :], preferred_element_type=F32)
        o_ref[r0:r0 + ATT_TQ, :] = (acc[:, :B_DH] / acc[:, B_DH:]).astype(o_ref.dtype)

    n_tiles = SEQ // ATT_TQ
    s_next = scores(0)
    for t in range(n_tiles):
        s_cur = s_next
        if t + 1 < n_tiles:
            s_next = scores(t + 1)
        finish(t, s_cur)


def band_bias(rel_table, nq, nk):
    w = nq + nk
    rel = jnp.clip(REACH + nq - 1 - jnp.arange(w), -REL_CLIP, REL_CLIP) + REL_CLIP
    r = rel_table.astype(F32)[:, rel]
    heads = r.shape[0]
    skew = jnp.broadcast_to(r[:, None, :], (heads, nq, w)).reshape(heads, nq * w)
    skew = skew[:, :nq * (w - 1)].reshape(heads, nq, w - 1)
    bias = skew[:, :, nq - 1:nq - 1 + nk]
    i = jnp.arange(nq)[:, None]
    j = jnp.arange(nk)[None, :]
    dc = j // CHUNK - i // CHUNK
    allowed = (dc >= 0) & (dc <= PREV_CHUNKS)
    return jnp.where(allowed[None], bias * LOG2E, NEG)


def attn_prompt(qkv, rel_table):
    bias = band_bias(rel_table, ATT_TQ, ATT_TK)
    tail = pl.BlockSpec((1, REACH, B_DH), lambda b, h: (b, 0, h))
    return pl.pallas_call(
        _attn_prompt_body,
        grid=(BATCH, B_HEADS),
        in_specs=[pl.BlockSpec((SEQ, B_DH), lambda b, h: (b, h)),
                  pl.BlockSpec((SEQ, B_DH), lambda b, h: (b, B_HEADS + h)),
                  pl.BlockSpec((SEQ, B_DH), lambda b, h: (b, 2 * B_HEADS + h)),
                  pl.BlockSpec((1, ATT_TQ, ATT_TK), lambda b, h: (h, 0, 0))],
        out_specs=[pl.BlockSpec((SEQ, B_DH), lambda b, h: (b, h)), tail, tail],
        out_shape=[jax.ShapeDtypeStruct((N_PROMPT, D_MODEL), BF16),
                   jax.ShapeDtypeStruct((BATCH, REACH, D_MODEL), F32),
                   jax.ShapeDtypeStruct((BATCH, REACH, D_MODEL), F32)],
        scratch_shapes=[pltpu.VMEM((SEQ, B_DH), BF16),
                        pltpu.VMEM((SEQ + REACH, B_DH), BF16),
                        pltpu.VMEM((SEQ + REACH, 2 * B_DH), BF16)],
        compiler_params=_params("parallel", "parallel"),
        name="attn_prompt",
    )(qkv, qkv, qkv, bias)


def _attn_sample_body(q_ref, kn_ref, vn_ref, ck_ref, cv_ref, bias_c_ref, bias_n_ref, o_ref, ks_ref, vs_ref):
    ks_ref[0] = kn_ref[...].astype(F32)
    vs_ref[0] = vn_ref[...].astype(F32)
    for h in range(B_HEADS):
        sl = slice(h * B_DH, (h + 1) * B_DH)
        q = _scale_q(q_ref[:, sl])
        kc = ck_ref[pl.ds(h, REACH, stride=B_HEADS), :].astype(BF16)
        vc = cv_ref[pl.ds(h, REACH, stride=B_HEADS), :].astype(BF16)
        s_c = lax.dot_general(q, kc, _NT, preferred_element_type=F32) + bias_c_ref[h]
        s_n = lax.dot_general(q, kn_ref[:, sl], _NT, preferred_element_type=F32) + bias_n_ref[h]
        m = jnp.maximum(jnp.max(s_c, axis=-1, keepdims=True), jnp.max(s_n, axis=-1, keepdims=True))
        p_c = jnp.exp2(s_c - m)
        p_n = jnp.exp2(s_n - m)
        l = jnp.sum(p_c, axis=-1, keepdims=True) + jnp.sum(p_n, axis=-1, keepdims=True)
        dot = functools.partial(jnp.dot, preferred_element_type=F32)
        o_ref[:, sl] = ((dot(p_c.astype(BF16), vc) + dot(p_n.astype(BF16), vn_ref[:, sl])) / l).astype(o_ref.dtype)


def attn_sample(qkv, cache_k, cache_v, layer, rel_table):
    bias = band_bias(rel_table, DEC_SEQ, REACH + DEC_SEQ)
    cache_rows = REACH * B_HEADS
    ck = cache_k.reshape(cache_k.shape[0], DEC_BATCH, cache_rows, B_DH)
    cv = cache_v.reshape(cache_v.shape[0], DEC_BATCH, cache_rows, B_DH)
    row = lambda c: pl.BlockSpec((DEC_SEQ, D_MODEL), lambda b: (N_PROMPT // DEC_SEQ + b, c))
    cache = pl.BlockSpec((None, None, cache_rows, B_DH), lambda b: (layer, b, 0, 0))
    new = pl.BlockSpec((1, DEC_SEQ, D_MODEL), lambda b: (b, 0, 0))
    return pl.pallas_call(
        _attn_sample_body,
        grid=(DEC_BATCH,),
        in_specs=[row(0), row(1), row(2), cache, cache,
                  pl.BlockSpec((B_HEADS, DEC_SEQ, REACH), lambda b: (0, 0, 0)),
                  pl.BlockSpec((B_HEADS, DEC_SEQ, DEC_SEQ), lambda b: (0, 0, 0))],
        out_specs=[pl.BlockSpec((DEC_SEQ, D_MODEL), lambda b: (b, 0)), new, new],
        out_shape=[jax.ShapeDtypeStruct((N_SAMPLE, D_MODEL), BF16),
                   jax.ShapeDtypeStruct((DEC_BATCH, DEC_SEQ, D_MODEL), F32),
                   jax.ShapeDtypeStruct((DEC_BATCH, DEC_SEQ, D_MODEL), F32)],
        compiler_params=_params("parallel"),
        name="attn_sample",
    )(qkv, qkv, qkv, ck, cv, bias[:, :, :REACH], bias[:, :, REACH:])


def _ln(z, g, b):
    mu = jnp.mean(z, axis=-1, keepdims=True)
    zc = z - mu
    var = jnp.mean(zc * zc, axis=-1, keepdims=True)
    return zc * lax.rsqrt(var + LN_EPS) * g + b


PACK_ROWS = D_MODEL // 2 // LANES
HI_MASK = -65536


def _store_packed(ref, v, token0=0):
    t = v.shape[0]
    half = D_MODEL // 2
    hi = lax.bitcast_convert_type(v[:, :half].astype(BF16).astype(F32), jnp.int32)
    lo = lax.bitcast_convert_type(v[:, half:].astype(BF16).astype(F32), jnp.int32)
    words = (hi & HI_MASK) | lax.shift_right_logical(lo, 16)
    for a in range(PACK_ROWS):
        ref[pl.ds(token0 * PACK_ROWS + a, t, stride=PACK_ROWS), :] = words[:, a * LANES:(a + 1) * LANES]


def _load_packed(ref, t):
    words = jnp.concatenate([ref[pl.ds(a, t, stride=PACK_ROWS), :] for a in range(PACK_ROWS)], axis=1)
    hi = lax.bitcast_convert_type(words & HI_MASK, F32)
    lo = lax.bitcast_convert_type(lax.shift_left(words, 16), F32)
    return jnp.concatenate([hi, lo], axis=1)


def _mix_ln_router_body(hp_ref, hs_ref, w_ref, xa_ref, xb_ref, g_ref, b_ref, rw_ref, rb_ref,
                        xp_ref, wt_ref, cls_ref, y_even, y_odd, *, prompt_tiles, n_tiles, x_split):
    i = pl.program_id(0)
    from_prompt = jnp.minimum(i, n_tiles - 1) < prompt_tiles

    @pl.when(i == 0)
    def _():
        y_odd[...] = jnp.zeros_like(y_odd)

    half = D_MODEL // 2

    def run(y_new, y_old):
        h = jnp.where(from_prompt, hp_ref[...], hs_ref[...])
        y_new[:, :half] = jnp.dot(h, w_ref[:, :half], preferred_element_type=F32)
        x = xa_ref[...]
        if x_split:
            x = jnp.where(jnp.maximum(i - 1, 0) < prompt_tiles, x, xb_ref[...])
        xn = _ln(ALPHA * x + y_old[...], g_ref[...], b_ref[...])
        logits = _dot_hilo(xn, rw_ref[...]) + rb_ref[...]
        y_new[:, half:] = jnp.dot(h, w_ref[:, half:], preferred_element_type=F32)
        _store_packed(xp_ref, xn)
        _route(logits, wt_ref, cls_ref)

    @pl.when(i % 2 == 0)
    def _():
        run(y_even, y_odd)

    @pl.when(i % 2 == 1)
    def _():
        run(y_odd, y_even)


def mix_ln_router(h_p, h_s, w, x, g, b, w_group, b_group, w_expert, b_expert, x_tail=None, tm=512):
    n = h_p.shape[0] + h_s.shape[0]
    n_tiles = n // tm
    prompt_tiles = h_p.shape[0] // tm
    sample_tiles = h_s.shape[0] // tm
    assert prompt_tiles * tm == h_p.shape[0] and (prompt_tiles + sample_tiles) * tm == n
    mm_tile = lambda i: jnp.minimum(i, n_tiles - 1)
    ep_tile = lambda i: jnp.maximum(i - 1, 0)
    hp_spec = pl.BlockSpec((tm, D_MODEL), lambda i: (jnp.minimum(mm_tile(i), prompt_tiles - 1), 0))
    hs_spec = pl.BlockSpec((tm, D_MODEL), lambda i: (jnp.maximum(mm_tile(i) - prompt_tiles, 0), 0))
    rw = jnp.pad(jnp.concatenate([w_group, w_expert], axis=1), ((0, 0), (0, LANES - N_GROUPS - N_EXPERTS)))
    rb = jnp.pad(jnp.concatenate([b_group, b_expert]).astype(F32), (0, LANES - N_GROUPS - N_EXPERTS))
    vec = pl.BlockSpec((1, D_MODEL), lambda i: (0, 0))
    lane_row = pl.BlockSpec((tm, LANES), lambda i: (ep_tile(i), 0))
    if x_tail is None:
        xa, xa_spec = x, pl.BlockSpec((tm, D_MODEL), lambda i: (ep_tile(i), 0))
        xb, xb_spec = x, pl.BlockSpec(memory_space=pl.ANY)
    else:
        assert x.shape[0] == h_p.shape[0] and x_tail.shape[0] == h_s.shape[0]
        xa, xa_spec = x, pl.BlockSpec((tm, D_MODEL), lambda i: (jnp.minimum(ep_tile(i), prompt_tiles - 1), 0))
        xb, xb_spec = x_tail, pl.BlockSpec((tm, D_MODEL), lambda i: (jnp.maximum(ep_tile(i) - prompt_tiles, 0), 0))
    return pl.pallas_call(
        functools.partial(_mix_ln_router_body, prompt_tiles=prompt_tiles, n_tiles=n_tiles,
                          x_split=x_tail is not None),
        grid=(n_tiles + 1,),
        in_specs=[hp_spec, hs_spec, pl.BlockSpec((D_MODEL, D_MODEL), lambda i: (0, 0)), xa_spec, xb_spec, vec, vec,
                  pl.BlockSpec((D_MODEL, LANES), lambda i: (0, 0)), pl.BlockSpec((1, LANES), lambda i: (0, 0))],
        out_specs=[pl.BlockSpec((tm * PACK_ROWS, LANES), lambda i: (ep_tile(i), 0)), lane_row, lane_row],
        out_shape=[jax.ShapeDtypeStruct((n * PACK_ROWS, LANES), jnp.int32),
                   jax.ShapeDtypeStruct((n, LANES), F32), jax.ShapeDtypeStruct((n, LANES), jnp.int32)],
        scratch_shapes=[pltpu.VMEM((tm, D_MODEL), F32), pltpu.VMEM((tm, D_MODEL), F32)],
        compiler_params=_params("arbitrary"),
        name="mix_ln_router",
    )(h_p, h_s, w, xa, xb, g[None, :], b[None, :], rw, rb[None, :])


GATE_COL_BLOCK = (2 * A_HEADS * A_DK + 2 * D_MODEL) // LANES


def _gate_spec(mlstm_layer):
    return pl.BlockSpec((None, D_MODEL, LANES), lambda i: (mlstm_layer, 0, GATE_COL_BLOCK))


def _gate_preacts(x, wg_ref):
    lane = lax.broadcasted_iota(jnp.int32, wg_ref.shape, 1)
    return _dot_hilo(x, jnp.where(lane < 2 * A_HEADS, wg_ref[...], 0.0))


def _add_ln_body(x_ref, y_ref, g_ref, b_ref, *rest):
    t = rest[-1].shape[0]
    xn = _ln(ALPHA * _load_packed(x_ref, t) + _load_packed(y_ref, t), g_ref[...], b_ref[...])
    if len(rest) == 2:
        xo_ref, xbo_ref = rest
    else:
        wg_ref, xo_ref, xbo_ref, go_ref = rest
        go_ref[...] = _gate_preacts(xn, wg_ref)
    xo_ref[...] = xn
    xbo_ref[...] = xn.astype(BF16)


def residual_ln(x_packed, y_packed, g, b, a_w_in=None, mlstm_layer=None, tm=512):
    n = x_packed.shape[0] // PACK_ROWS
    row = pl.BlockSpec((tm, D_MODEL), lambda i: (i, 0))
    vec = pl.BlockSpec((1, D_MODEL), lambda i: (0, 0))
    packed = pl.BlockSpec((tm * PACK_ROWS, LANES), lambda i: (i, 0))
    in_specs = [packed, packed, vec, vec]
    out_specs = [row, row]
    out_shape = [jax.ShapeDtypeStruct((n, D_MODEL), F32), jax.ShapeDtypeStruct((n, D_MODEL), BF16)]
    args = [x_packed, y_packed, g[None, :], b[None, :]]
    if a_w_in is not None:
        in_specs.append(_gate_spec(mlstm_layer))
        out_specs.append(pl.BlockSpec((tm, LANES), lambda i: (i, 0)))
        out_shape.append(jax.ShapeDtypeStruct((n, LANES), F32))
        args.append(a_w_in)
    return pl.pallas_call(
        _add_ln_body,
        grid=(n // tm,),
        in_specs=in_specs,
        out_specs=out_specs,
        out_shape=out_shape,
        compiler_params=_params("parallel"),
        name="residual_ln",
    )(*args)


def _embed_body(xp_ref, xs_ref, wg_ref, xbo_ref, go_ref, *, prompt_tiles):
    x = jnp.where(pl.program_id(0) < prompt_tiles, xp_ref[...], xs_ref[...])
    xbo_ref[...] = x.astype(BF16)
    go_ref[...] = _gate_preacts(x, wg_ref)


def embed(x_p, x_s, a_w_in, tm=512):
    n = x_p.shape[0] + x_s.shape[0]
    prompt_tiles = x_p.shape[0] // tm
    assert prompt_tiles * tm == x_p.shape[0] and n % tm == 0
    row = pl.BlockSpec((tm, D_MODEL), lambda i: (i, 0))
    return pl.pallas_call(
        functools.partial(_embed_body, prompt_tiles=prompt_tiles),
        grid=(n // tm,),
        in_specs=[pl.BlockSpec((tm, D_MODEL), lambda i: (jnp.minimum(i, prompt_tiles - 1), 0)),
                  pl.BlockSpec((tm, D_MODEL), lambda i: (jnp.maximum(i - prompt_tiles, 0), 0)),
                  _gate_spec(0)],
        out_specs=[row, pl.BlockSpec((tm, LANES), lambda i: (i, 0))],
        out_shape=[jax.ShapeDtypeStruct((n, D_MODEL), BF16), jax.ShapeDtypeStruct((n, LANES), F32)],
        compiler_params=_params("parallel"),
        name="embed",
    )(x_p, x_s, a_w_in)


def _add_ln_split_body(x_ref, y_ref, g_ref, b_ref, op_ref, os_ref, *, prompt_tiles):
    t = op_ref.shape[0]
    xn = _ln(ALPHA * _load_packed(x_ref, t) + _load_packed(y_ref, t), g_ref[...], b_ref[...])
    i = pl.program_id(0)

    @pl.when(i < prompt_tiles)
    def _():
        op_ref[...] = xn

    @pl.when(i >= prompt_tiles)
    def _():
        os_ref[...] = xn


def residual_ln_split(x_packed, y_packed, g, b, n_prompt, tm=512):
    n = x_packed.shape[0] // PACK_ROWS
    prompt_tiles = n_prompt // tm
    assert prompt_tiles * tm == n_prompt and n % tm == 0
    vec = pl.BlockSpec((1, D_MODEL), lambda i: (0, 0))
    packed = pl.BlockSpec((tm * PACK_ROWS, LANES), lambda i: (i, 0))
    return pl.pallas_call(
        functools.partial(_add_ln_split_body, prompt_tiles=prompt_tiles),
        grid=(n // tm,),
        in_specs=[packed, packed, vec, vec],
        out_specs=[pl.BlockSpec((tm, D_MODEL), lambda i: (jnp.minimum(i, prompt_tiles - 1), 0)),
                   pl.BlockSpec((tm, D_MODEL), lambda i: (jnp.maximum(i - prompt_tiles, 0), 0))],
        out_shape=[jax.ShapeDtypeStruct((n_prompt, D_MODEL), F32),
                   jax.ShapeDtypeStruct((n - n_prompt, D_MODEL), F32)],
        compiler_params=_params("arbitrary"),
        name="residual_ln_split",
    )(x_packed, y_packed, g[None, :], b[None, :])


ROUTER_E0 = N_GROUPS


def _route(logits, wt_ref, cls_ref):
    lane = lax.broadcasted_iota(jnp.int32, logits.shape, 1).astype(F32)
    big = float(LANES)
    rmax = lambda a: jnp.max(a, axis=-1, keepdims=True)
    rmin = lambda a: jnp.min(a, axis=-1, keepdims=True)
    is_g = lane < N_GROUPS
    gl = jnp.where(is_g, logits, -jnp.inf)
    gmax = rmax(gl)
    gsel = rmin(jnp.where(gl == gmax, lane, big))
    gsum = jnp.sum(jnp.where(is_g, jnp.exp(logits - gmax), 0.0), axis=-1, keepdims=True)
    g_w = 1.0 / gsum
    e_lo = ROUTER_E0 + EXP_PER_GROUP * gsel
    in_grp = (lane >= e_lo) & (lane < e_lo + EXP_PER_GROUP)
    el = jnp.where(in_grp, logits, -jnp.inf)
    e1 = rmax(el)
    i1 = rmin(jnp.where(el == e1, lane, big))
    el2 = jnp.where(lane == i1, -jnp.inf, el)
    e2 = rmax(el2)
    i2 = rmin(jnp.where(el2 == e2, lane, big))
    t = jnp.exp(e2 - e1)
    p1 = 1.0 / (1.0 + t)
    w1 = p1 * g_w
    w2 = (t * p1) * g_w
    a1 = i1 - e_lo
    a2 = i2 - e_lo
    first_low = a1 < a2
    lo = jnp.where(first_low, a1, a2)
    hi = jnp.where(first_low, a2, a1)
    w_lo = jnp.where(first_low, w1, w2)
    w_hi = jnp.where(first_low, w2, w1)
    off = jnp.where(lo == 0.0, 0.0, jnp.where(lo == 1.0, 3.0, 5.0))
    cls = gsel * float(len(PAIRS)) + off + (hi - lo - 1.0)
    wt_ref[...] = jnp.where(lane < LANES // 2, w_lo, w_hi)
    cls_ref[...] = jnp.broadcast_to(cls, logits.shape).astype(jnp.int32)


def _expert_body(ea_ref, eb_ref, valid_ref, x_ref, wt_ref, wga_ref, wua_ref, wda_ref,
                 wgb_ref, wub_ref, wdb_ref, y_ref):
    i = pl.program_id(0)

    @pl.when(valid_ref[i] == 0)
    def _():
        y_ref[...] = jnp.zeros_like(y_ref)

    @pl.when(valid_ref[i] != 0)
    def _():
        x = _load_packed(x_ref, MOE_TILE).astype(BF16)
        dot = functools.partial(jnp.dot, preferred_element_type=F32)

        ga, ua = dot(x, wga_ref[...]), dot(x, wua_ref[...])
        gb, ub = dot(x, wgb_ref[...]), dot(x, wub_ref[...])
        ya = dot((jax.nn.silu(ga) * ua).astype(BF16), wda_ref[...])
        yb = dot((jax.nn.silu(gb) * ub).astype(BF16), wdb_ref[...])
        w_lo = wt_ref[:, 0:1]
        w_hi = wt_ref[:, LANES // 2:LANES // 2 + 1]
        _store_packed(y_ref, w_lo * ya + w_hi * yb)


def experts(xs, wts, tile_ea, tile_eb, tile_valid, w_gate, w_up, w_down):
    wspec = lambda shape, which: pl.BlockSpec(
        (None,) + shape, lambda i, ea, eb, va: ((ea, eb)[which][i], 0, 0))
    up = (D_MODEL, D_EXPERT)
    down = (D_EXPERT, D_MODEL)
    packed = pl.BlockSpec((MOE_TILE * PACK_ROWS, LANES), lambda i, ea, eb, va: (i, 0))
    grid_spec = pltpu.PrefetchScalarGridSpec(
        num_scalar_prefetch=3,
        grid=(MOE_TILES,),
        in_specs=[packed,
                  pl.BlockSpec((MOE_TILE, LANES), lambda i, ea, eb, va: (i, 0)),
                  wspec(up, 0), wspec(up, 0), wspec(down, 0),
                  wspec(up, 1), wspec(up, 1), wspec(down, 1)],
        out_specs=packed,
    )
    return pl.pallas_call(
        _expert_body,
        grid_spec=grid_spec,
        out_shape=jax.ShapeDtypeStruct((MOE_ROWS * PACK_ROWS, LANES), jnp.int32),
        compiler_params=_params("arbitrary"),
        name="experts",
    )(tile_ea, tile_eb, tile_valid, xs, wts, w_gate, w_up, w_down, w_gate, w_up, w_down)


def sc_dispatch(xp, wts, pos, fill_rows):
    n = pos.shape[0]
    n_fill = fill_rows.shape[0]
    info = plsc.get_sparse_core_info()
    n_workers = info.num_cores * info.num_subcores
    assert n_workers == SC_WORKERS
    per_worker = n // n_workers
    steps = per_worker // SC_TOKENS
    fill_steps = n_fill // n_workers // SC_FILL
    assert steps * SC_TOKENS * n_workers == n and steps % 2 == 0
    assert fill_steps * SC_FILL * n_workers == n_fill
    mesh = plsc.VectorSubcoreMesh(core_axis_name="core", subcore_axis_name="subcore")
    dma = pltpu.SemaphoreType.DMA

    @functools.partial(
        pl.kernel,
        out_type=[jax.ShapeDtypeStruct((n + n_fill, PACK_ROWS, LANES), xp.dtype),
                  jax.ShapeDtypeStruct((n + n_fill, LANES), wts.dtype)],
        mesh=mesh,
        scratch_types=[pltpu.VMEM((steps, SC_TOKENS), jnp.int32),
                       pltpu.VMEM((fill_steps, SC_FILL), jnp.int32),
                       pltpu.VMEM((2, SC_TOKENS, PACK_ROWS, LANES), xp.dtype),
                       pltpu.VMEM((2, SC_TOKENS, LANES), wts.dtype),
                       dma((2,)), dma((2,)), dma((2,)), dma((2,))],
    )
    def scatter(x_hbm, w_hbm, pos_hbm, fill_hbm, zx_hbm, zw_hbm, xs_hbm, ws_hbm,
                pos_v, fill_v, xbuf, wbuf, read_x, read_w, write_x, write_w):
        worker = lax.axis_index("subcore") * info.num_cores + lax.axis_index("core")
        base = worker * per_worker
        pltpu.sync_copy(pos_hbm.at[worker], pos_v)
        pltpu.sync_copy(fill_hbm.at[worker], fill_v)

        def reads(s, b):
            rows = pl.ds(base + s * SC_TOKENS, SC_TOKENS)
            return (pltpu.make_async_copy(x_hbm.at[rows], xbuf.at[b], read_x.at[b]),
                    pltpu.make_async_copy(w_hbm.at[rows], wbuf.at[b], read_w.at[b]))

        def writes(s, b):
            return (pltpu.make_async_copy(xbuf.at[b], xs_hbm.at[pos_v.at[s]], write_x.at[b]),
                    pltpu.make_async_copy(wbuf.at[b], ws_hbm.at[pos_v.at[s]], write_w.at[b]))

        def start(copies):
            for c in copies:
                c.start()

        def wait(copies):
            for c in copies:
                c.wait()

        start(reads(0, 0))

        @pl.loop(0, steps, step=2)
        def _(s):
            wait(reads(s, 0))
            start(reads(s + 1, 1))
            start(writes(s, 0))
            wait(reads(s + 1, 1))
            wait(writes(s, 0))

            @pl.when(s + 2 < steps)
            def _():
                start(reads(s + 2, 0))

            start(writes(s + 1, 1))
            wait(writes(s + 1, 1))

        zx = xbuf.at[0, pl.ds(0, SC_FILL)]
        zw = wbuf.at[0, pl.ds(0, SC_FILL)]
        pltpu.sync_copy(zx_hbm, zx)
        pltpu.sync_copy(zw_hbm, zw)
        fills = [(pltpu.make_async_copy(zx, xs_hbm.at[fill_v.at[s]], write_x.at[0]),
                  pltpu.make_async_copy(zw, ws_hbm.at[fill_v.at[s]], write_w.at[0])) for s in range(fill_steps)]
        for f in fills:
            start(f)
        for f in fills:
            wait(f)

    xs, ws = scatter(xp.reshape(n, PACK_ROWS, LANES), wts,
                     pos.reshape(n_workers, steps, SC_TOKENS), fill_rows.reshape(n_workers, fill_steps, SC_FILL),
                     jnp.zeros((SC_FILL, PACK_ROWS, LANES), xp.dtype), jnp.zeros((SC_FILL, LANES), wts.dtype))
    return xs.reshape((n + n_fill) * PACK_ROWS, LANES), ws


def sc_gather_tokens(x, idx):
    n = x.shape[0] // PACK_ROWS
    p = idx.shape[0]
    info = plsc.get_sparse_core_info()
    n_workers = info.num_cores * info.num_subcores
    assert n_workers == SC_WORKERS
    per_worker = p // n_workers
    steps = per_worker // SC_TOKENS
    assert steps * SC_TOKENS * n_workers == p and steps % 2 == 0
    mesh = plsc.VectorSubcoreMesh(core_axis_name="core", subcore_axis_name="subcore")
    dma = pltpu.SemaphoreType.DMA

    @functools.partial(
        pl.kernel,
        out_type=jax.ShapeDtypeStruct((p, PACK_ROWS, LANES), x.dtype),
        mesh=mesh,
        scratch_types=[pltpu.VMEM((steps, SC_TOKENS), jnp.int32),
                       pltpu.VMEM((2, SC_TOKENS, PACK_ROWS, LANES), x.dtype),
                       dma((2,)), dma((2,))],
    )
    def gather(x_hbm, i_hbm, o_hbm, idx_v, buf, read_sem, write_sem):
        worker = lax.axis_index("subcore") * info.num_cores + lax.axis_index("core")
        base = worker * per_worker
        pltpu.sync_copy(i_hbm.at[worker], idx_v)

        def read(s, b):
            return pltpu.make_async_copy(x_hbm.at[idx_v.at[s]], buf.at[b], read_sem.at[b])

        def write(s, b):
            return pltpu.make_async_copy(buf.at[b], o_hbm.at[pl.ds(base + s * SC_TOKENS, SC_TOKENS)], write_sem.at[b])

        read(0, 0).start()

        @pl.loop(0, steps, step=2)
        def _(s):
            read(s, 0).wait()
            read(s + 1, 1).start()
            write(s, 0).start()
            read(s + 1, 1).wait()
            write(s, 0).wait()

            @pl.when(s + 2 < steps)
            def _():
                read(s + 2, 0).start()

            write(s + 1, 1).start()
            write(s + 1, 1).wait()

    out = gather(x.reshape(n, PACK_ROWS, LANES), idx.reshape(n_workers, steps, SC_TOKENS))
    return out.reshape(p * PACK_ROWS, LANES)


_CLASS_EA = [g * EXP_PER_GROUP + lo for g in range(N_GROUPS) for lo, hi in PAIRS]
_CLASS_EB = [g * EXP_PER_GROUP + hi for g in range(N_GROUPS) for lo, hi in PAIRS]


def dispatch_plan(cls):
    onehot = (cls[:, None] == jnp.arange(N_CLASSES, dtype=jnp.int32)[None, :]).astype(jnp.int32)
    csum = jnp.cumsum(onehot, axis=0)
    rank = jnp.sum(csum * onehot, axis=1) - 1
    counts = csum[-1]
    tiles_per = (counts + MOE_TILE - 1) // MOE_TILE
    tile_end = jnp.cumsum(tiles_per)
    tile_start = tile_end - tiles_per
    pos = jnp.sum(onehot * tile_start[None, :], axis=1) * MOE_TILE + rank
    n = cls.shape[0]
    pad = tiles_per * MOE_TILE - counts
    pad_end = jnp.cumsum(pad)
    k = jnp.arange(MOE_ROWS - n, dtype=jnp.int32)
    k_cls = jnp.sum((k[:, None] >= pad_end[None, :]).astype(jnp.int32), axis=1)
    k_hot = (k_cls[:, None] == jnp.arange(N_CLASSES, dtype=jnp.int32)[None, :]).astype(jnp.int32)
    pad_row0 = tile_start * MOE_TILE + counts - (pad_end - pad)
    fill_rows = k + jnp.where(k_cls < N_CLASSES, jnp.sum(k_hot * pad_row0[None, :], axis=1),
                              tile_end[-1] * MOE_TILE - pad_end[-1])
    t = jnp.arange(MOE_TILES, dtype=jnp.int32)
    tile_cls = jnp.sum((t[:, None] >= tile_end[None, :]).astype(jnp.int32), axis=1)
    valid = (tile_cls < N_CLASSES).astype(jnp.int32)
    last_cls = jnp.max(jnp.where(counts > 0, jnp.arange(N_CLASSES, dtype=jnp.int32), 0))
    tile_cls = jnp.where(valid == 1, tile_cls, last_cls)
    ea = jnp.asarray(_CLASS_EA, jnp.int32)[tile_cls]
    eb = jnp.asarray(_CLASS_EB, jnp.int32)[tile_cls]
    return pos, fill_rows, ea, eb, valid


def moe(xp, wts, cls, e_w_gate, e_w_up, w_down, e_w_down, layer):
    pos, fill_rows, ea, eb, valid = dispatch_plan(cls[:, 0])
    xs, ws = sc_dispatch(xp, wts, pos, fill_rows)
    w_gate, w_up = cast_layer(e_w_gate, layer, after=pos), cast_layer(e_w_up, layer, after=pos)
    ys = experts(xs, ws, ea, eb, valid, w_gate, w_up, w_down)
    y = sc_gather_tokens(ys, pos)
    if layer + 1 == DEPTH:
        return y, None
    return lax.optimization_barrier((y, cast_layer(e_w_down, layer + 1, after=ys)))


def _rows(a):
    return a.reshape(-1, a.shape[-1])


def kernel(x_prompt, x_sample, state_C, state_n, state_m, cache_k, cache_v, a_w_in, a_b_gate, a_norm, a_w_out, b_w_in, b_rel, b_w_out, ln1_g, ln1_b, ln2_g, ln2_b, r_w_group, r_b_group, r_w_expert, r_b_expert, e_w_gate, e_w_up, e_w_down):
    hk = A_HEADS * A_DK
    n_main = 2 * hk + 2 * D_MODEL
    assert n_main == GATE_COL_BLOCK * LANES and a_w_in.shape[-1] == n_main + 2 * A_HEADS
    x, x_tail = _rows(x_prompt), _rows(x_sample)
    xb, g = embed(x, x_tail, a_w_in)
    w_down = cast_layer(e_w_down, 0)
    outs = {k: [] for k in ("Cp", "np", "mp", "kp", "vp", "Cs", "ns", "ms", "ks", "vs")}
    for layer in range(DEPTH):
        j = layer // 2
        if layer % 2 == 0:
            proj = matmul(xb, a_w_in, j, n_main, BF16)
            zc = jnp.zeros((BATCH, A_HEADS, A_DK, A_DV), F32)
            zn = jnp.zeros((BATCH, A_HEADS, A_DK), F32)
            zm = jnp.zeros((BATCH, A_HEADS), F32)
            h_p, c_p, n_p, m_p = mlstm(proj, g, a_b_gate[j], a_norm[j], zc, zn, zm,
                                       batch=BATCH, seq=SEQ, L=MLSTM_CHUNK, row0=0)
            h_s, c_s, n_s, m_s = mlstm(proj, g, a_b_gate[j], a_norm[j],
                                       state_C[j], state_n[j], state_m[j],
                                       batch=DEC_BATCH, seq=DEC_SEQ, L=DEC_SEQ, row0=N_PROMPT)
            outs["Cp"].append(c_p); outs["np"].append(n_p); outs["mp"].append(m_p)
            outs["Cs"].append(c_s); outs["ns"].append(n_s); outs["ms"].append(m_s)
            w_out = a_w_out[j]
        else:
            qkv = matmul(xb, b_w_in, j, 3 * D_MODEL, BF16)
            h_p, k_p, v_p = attn_prompt(qkv, b_rel[j])
            h_s, k_s, v_s = attn_sample(qkv, cache_k, cache_v, j, b_rel[j])
            heads = lambda a: a.reshape(a.shape[:2] + (B_HEADS, B_DH))
            outs["kp"].append(heads(k_p)); outs["vp"].append(heads(v_p))
            outs["ks"].append(heads(k_s)); outs["vs"].append(heads(v_s))
            w_out = b_w_out[j]
        xp, wts, cls = mix_ln_router(h_p, h_s, w_out.astype(BF16), x, ln1_g[layer], ln1_b[layer],
                                     r_w_group[layer], r_b_group[layer], r_w_expert[layer], r_b_expert[layer],
                                     x_tail=x_tail if layer == 0 else None)
        y, w_down = moe(xp, wts, cls, e_w_gate, e_w_up, w_down, e_w_down, layer)
        if layer < DEPTH - 1:
            if (layer + 1) % 2 == 0:
                x, xb, g = residual_ln(xp, y, ln2_g[layer], ln2_b[layer], a_w_in, (layer + 1) // 2)
            else:
                x, xb = residual_ln(xp, y, ln2_g[layer], ln2_b[layer])
        else:
            y_p, y_s = residual_ln_split(xp, y, ln2_g[layer], ln2_b[layer], N_PROMPT)
    st = lambda k: jnp.stack(outs[k])
    return (y_p.reshape(BATCH, SEQ, D_MODEL), y_s.reshape(DEC_BATCH, DEC_SEQ, D_MODEL),
            st("Cp"), st("np"), st("mp"), st("kp"), st("vp"),
            st("Cs"), st("ns"), st("ms"), st("ks"), st("vs"))
```

```python
import functools

import jax
import jax.numpy as jnp
from jax import lax
from jax.experimental import pallas as pl
from jax.experimental.pallas import tpu as pltpu
from jax.experimental.pallas import tpu_sc as plsc

F32 = jnp.float32
BF16 = jnp.bfloat16

D_MODEL = 2048
BATCH = 8
SEQ = 4096
DEPTH = 4
DEC_BATCH = 32
DEC_SEQ = 32
PAST_LEN = 1024
CHUNK = 64
A_HEADS = 8
A_DK = 128
A_DV = D_MODEL // A_HEADS
GATE_CAP = 15.0
B_HEADS = 16
B_DH = D_MODEL // B_HEADS
PREV_CHUNKS = 8
REACH = PREV_CHUNKS * CHUNK
REL_CLIP = 256
N_GROUPS = 4
EXP_PER_GROUP = 4
N_EXPERTS = N_GROUPS * EXP_PER_GROUP
D_EXPERT = D_MODEL // 4
ALPHA = (2 * DEPTH) ** 0.25
LN_EPS = 1e-5
RMS_EPS = 1e-6

N_PROMPT = BATCH * SEQ
N_SAMPLE = DEC_BATCH * DEC_SEQ
N_TOK = N_PROMPT + N_SAMPLE

VMEM_LIMIT = 56 * 1024 * 1024
LANES = 128

PAIRS = ((0, 1), (0, 2), (0, 3), (1, 2), (1, 3), (2, 3))
N_CLASSES = N_GROUPS * len(PAIRS)
MOE_TILE = 256
SC_WORKERS = 32
MOE_TILES = -(-(N_TOK // MOE_TILE + N_CLASSES) // SC_WORKERS) * SC_WORKERS
MOE_ROWS = MOE_TILES * MOE_TILE

MLSTM_CHUNK = 128
ATT_TQ = 256
ATT_TK = ATT_TQ + REACH
NEG = -1e30

SC_TOKENS = 48
SC_FILL = 32


def _params(*sem):
    return pltpu.CompilerParams(dimension_semantics=sem, vmem_limit_bytes=VMEM_LIMIT)


def _mm_body(x_ref, w_ref, o_ref, wb):
    @pl.when(pl.program_id(1) == 0)
    def _():
        wb[...] = w_ref[...].astype(BF16)

    o_ref[...] = jnp.dot(x_ref[...], wb[...], preferred_element_type=F32).astype(o_ref.dtype)


def matmul(x, w_all, layer, m, out_dtype, tm=1536, tn=1024):
    n, k = x.shape
    return pl.pallas_call(
        _mm_body,
        grid=(m // tn, n // tm),
        in_specs=[pl.BlockSpec((tm, k), lambda j, i: (i, 0)),
                  pl.BlockSpec((None, k, tn), lambda j, i: (layer, 0, j))],
        out_specs=pl.BlockSpec((tm, tn), lambda j, i: (i, j)),
        out_shape=jax.ShapeDtypeStruct((n, m), out_dtype),
        scratch_shapes=[pltpu.VMEM((k, tn), BF16)],
        compiler_params=_params("parallel", "arbitrary"),
        name="matmul",
    )(x, w_all)


def _cast_body(w_ref, *rest):
    o_ref = rest[-1]
    o_ref[...] = w_ref[...].astype(BF16)


def cast_layer(w_all, layer, after=None):
    e, a, b = w_all.shape[1:]
    per_step = 4
    in_specs = [pl.BlockSpec((None, per_step, a, b), lambda i: (layer, i, 0, 0))]
    args = [w_all]
    if after is not None:
        in_specs.append(pl.BlockSpec(memory_space=pl.ANY))
        args.append(after)
    return pl.pallas_call(
        _cast_body,
        grid=(e // per_step,),
        in_specs=in_specs,
        out_specs=pl.BlockSpec((per_step, a, b), lambda i: (i, 0, 0)),
        out_shape=jax.ShapeDtypeStruct((e, a, b), BF16),
        compiler_params=_params("parallel"),
        name="cast_layer",
    )(*args)


def _split2(x):
    hi = x.astype(BF16)
    lo = (x - hi.astype(F32)).astype(BF16)
    return hi, lo


_NT = (((1,), (1,)), ((), ()))
_TN = (((0,), (0,)), ((), ()))


def _dot_hilo(x, w):
    t = x.shape[0]
    xh, xl = _split2(x)
    wh, wl = _split2(w)
    r = jnp.dot(jnp.concatenate([xh, xl], axis=0), jnp.concatenate([wh, wl], axis=1), preferred_element_type=F32)
    return r[:t, :LANES] + (r[:t, LANES:] + r[t:, :LANES]) + r[t:, LANES:]


def _log_sigmoid(x):
    return jnp.minimum(x, 0.0) - jnp.log(1.0 + jnp.exp(-jnp.abs(x)))


def _split3(x):
    a = x.astype(BF16)
    r = x - a.astype(F32)
    b = r.astype(BF16)
    c = (r - b.astype(F32)).astype(BF16)
    return a, b, c


A_STATE = A_DV + LANES


def _cummax_rows(x):
    n = x.shape[0]
    row = lax.broadcasted_iota(jnp.int32, x.shape, 0)
    k = 1
    while k < n:
        x = jnp.maximum(x, jnp.where(row >= k, pltpu.roll(x, k, axis=0), -jnp.inf))
        k *= 2
    return x


def _mlstm_body(q_ref, k_ref, v_ref, o_ref, g_ref, bias_ref, gain_ref, c0_ref, n0_ref, m0_ref,
                h_ref, c_out_ref, n_out_ref, m_ref, state, *, L, chunks):
    step = pl.program_id(1)

    @pl.when(step == 0)
    def _():
        state[:, :, :A_DV] = c0_ref[0]
        state[:, :, A_DV:] = n0_ref[0]
        m_ref[...] = m0_ref[...]

    for c in range(chunks):
        rows = pl.ds(c * L, L)
        _mlstm_chunk(q_ref.at[rows], k_ref.at[rows], v_ref.at[rows], o_ref.at[rows], g_ref.at[rows],
                     bias_ref, gain_ref, h_ref.at[rows], m_ref, state, L=L)

    @pl.when(step == pl.num_programs(1) - 1)
    def _():
        c_out_ref[0] = state[:, :, :A_DV]
        n_out_ref[0] = state[:, :, A_DV:]


def _mlstm_chunk(q_ref, k_ref, v_ref, o_ref, g_ref, bias_ref, gain_ref, h_ref, m_ref, state, *, L):
    dot = functools.partial(jnp.dot, preferred_element_type=F32)
    row = lax.broadcasted_iota(jnp.int32, (L, L), 0)
    col = lax.broadcasted_iota(jnp.int32, (L, L), 1)
    causal = col <= row
    tril = jnp.where(causal, 1.0, 0.0).astype(BF16)

    pre = GATE_CAP * jnp.tanh((g_ref[...] + bias_ref[...]) * (1.0 / GATE_CAP))
    ig = pre
    lf = _log_sigmoid(pltpu.roll(pre, LANES - A_HEADS, axis=1))
    a3, b3, c3 = _split3(lf)
    bt = dot(tril, a3) + (dot(tril, b3) + dot(tril, c3))
    a = ig - bt
    m_prev = m_ref[0]
    m_t = bt + jnp.maximum(m_prev, _cummax_rows(a))
    u = bt - m_t
    inter = jnp.exp(bt + m_prev - m_t)
    eminus = jnp.exp(-m_t)
    m_new = m_t[L - 1:L, :]
    bt_last = bt[L - 1:L, :]
    w_c = jnp.exp((bt_last - m_new) + a)
    decay = jnp.exp(bt_last + m_prev - m_new)
    m_ref[0] = m_new
    a_pad = a if L == LANES else jnp.concatenate([a, jnp.zeros((LANES - L, LANES), F32)], axis=0)
    a_t = jnp.transpose(a_pad)
    ie = jnp.concatenate([inter, eminus], axis=1).astype(BF16)
    sel_r = lax.broadcasted_iota(jnp.int32, (2 * LANES, 2 * LANES), 0)
    sel_c = lax.broadcasted_iota(jnp.int32, (2 * LANES, 2 * LANES), 1)
    same_half = (sel_r >= LANES) == (sel_c >= LANES)
    ones_l = jnp.ones((L, LANES), BF16)
    zeros_l = jnp.zeros((L, LANES), BF16)
    zeros_k = jnp.zeros((A_DK, LANES), BF16)
    mean_cols = jnp.full((A_DV, LANES), 1.0, BF16)

    scale = A_DK ** -0.5
    heads = range(A_HEADS)
    dk_sl = lambda h: slice(h * A_DK, (h + 1) * A_DK)
    dv_sl = lambda h: slice(h * A_DV, (h + 1) * A_DV)
    ie_b, qk = [], []
    for h in heads:
        sel = jnp.where(same_half & ((sel_r & (LANES - 1)) == h), 1.0, 0.0).astype(BF16)
        ie_b.append(dot(ie, sel))
        qk.append(lax.dot_general(q_ref[:, dk_sl(h)], k_ref[:, dk_sl(h)], _NT, preferred_element_type=F32))
    out = []
    for h in heads:
        d_mat = jnp.where(causal, jnp.exp(u[:, h:h + 1] + a_t[h:h + 1, :L]), 0.0)
        s = qk[h] * (d_mat * scale)
        q_i = (q_ref[:, dk_sl(h)].astype(F32) * ie_b[h][:, :LANES]).astype(BF16)
        c_b = state[h].astype(BF16)
        rhs = jnp.concatenate([
            jnp.concatenate([v_ref[:, dv_sl(h)], ones_l, zeros_l], axis=1),
            jnp.concatenate([c_b[:, :A_DV], zeros_k, c_b[:, A_DV:]], axis=1)], axis=0)
        out.append(dot(jnp.concatenate([s.astype(BF16), q_i], axis=1), rhs))
    hh, ms = [], []
    for h in heads:
        den = jnp.maximum(jnp.abs(out[h][:, A_DV:A_DV + LANES] + out[h][:, A_DV + LANES:]), ie_b[h][:, LANES:])
        r = 1.0 / den
        hh.append(out[h][:, :A_DV] * jnp.concatenate([r, r], axis=1))
        ms.append(dot((hh[h] * hh[h]).astype(BF16), mean_cols))
    upd = []
    for h in heads:
        rs = lax.rsqrt(ms[h] * (1.0 / A_DV) + RMS_EPS)
        og = o_ref[:, dv_sl(h)].astype(F32)
        y = hh[h] * jnp.concatenate([rs, rs], axis=1) * gain_ref[:, dv_sl(h)] * jax.nn.sigmoid(og)
        h_ref[:, dv_sl(h)] = y.astype(h_ref.dtype)
        wk = (k_ref[:, dk_sl(h)].astype(F32) * (w_c[:, h:h + 1] * scale)).astype(BF16)
        upd.append(lax.dot_general(wk, jnp.concatenate([v_ref[:, dv_sl(h)], ones_l], axis=1), _TN,
                                   preferred_element_type=F32))
    for h in heads:
        state[h] = decay[:, h:h + 1] * state[h] + upd[h]


def mlstm(proj, g, b_gate, gain, c0, n0, m0, *, batch, seq, L, row0):
    chunks = 2 if seq % (2 * L) == 0 else 1
    rows = chunks * L
    nc = seq // rows
    bias = jnp.pad(b_gate.astype(F32), (0, LANES - 2 * A_HEADS))
    row_blk = lambda b, c: row0 // rows + b * nc + c
    st = lambda *s: pl.BlockSpec((1,) + s, lambda b, c: (b,) + (0,) * len(s))
    n0_lanes = jnp.broadcast_to(n0[..., None], n0.shape + (LANES,))
    m_lanes = jnp.pad(m0, ((0, 0), (0, LANES - A_HEADS)))[:, None, :]
    h, c, n, m = pl.pallas_call(
        functools.partial(_mlstm_body, L=L, chunks=chunks),
        grid=(batch, nc),
        in_specs=[pl.BlockSpec((rows, A_HEADS * A_DK), lambda b, c: (row_blk(b, c), 0)),
                  pl.BlockSpec((rows, A_HEADS * A_DK), lambda b, c: (row_blk(b, c), 1)),
                  pl.BlockSpec((rows, D_MODEL), lambda b, c: (row_blk(b, c), 1)),
                  pl.BlockSpec((rows, D_MODEL), lambda b, c: (row_blk(b, c), 2)),
                  pl.BlockSpec((rows, LANES), lambda b, c: (row_blk(b, c), 0)),
                  pl.BlockSpec((1, LANES), lambda b, c: (0, 0)),
                  pl.BlockSpec((1, D_MODEL), lambda b, c: (0, 0)),
                  st(A_HEADS, A_DK, A_DV), st(A_HEADS, A_DK, LANES), st(1, LANES)],
        out_specs=[pl.BlockSpec((rows, D_MODEL), lambda b, c: (b * nc + c, 0)),
                   st(A_HEADS, A_DK, A_DV), st(A_HEADS, A_DK, LANES), st(1, LANES)],
        out_shape=[jax.ShapeDtypeStruct((batch * seq, D_MODEL), BF16),
                   jax.ShapeDtypeStruct((batch, A_HEADS, A_DK, A_DV), F32),
                   jax.ShapeDtypeStruct((batch, A_HEADS, A_DK, LANES), F32),
                   jax.ShapeDtypeStruct((batch, 1, LANES), F32)],
        scratch_shapes=[pltpu.VMEM((A_HEADS, A_DK, A_STATE), F32)],
        compiler_params=_params("parallel", "arbitrary"),
        name="mlstm",
    )(proj, proj, proj, proj, g, bias[None, :], gain.astype(F32)[None, :], c0, n0_lanes, m_lanes)
    return h, c, n[..., 0], m[:, 0, :A_HEADS]


LOG2E = 1.4426950408889634
Q_SCALE = B_DH ** -0.5 * LOG2E


def _scale_q(q):
    return (q.astype(F32) * Q_SCALE).astype(BF16)


def _attn_prompt_body(q_ref, k_ref, v_ref, bias_ref, o_ref, kt_ref, vt_ref, qs, kpad, vpad):
    kt_ref[0] = k_ref[SEQ - REACH:, :].astype(F32)
    vt_ref[0] = v_ref[SEQ - REACH:, :].astype(F32)
    qs[...] = _scale_q(q_ref[...])
    kpad[0:REACH, :] = jnp.zeros((REACH, B_DH), BF16)
    kpad[REACH:, :] = k_ref[...]
    vpad[0:REACH, 0:B_DH] = jnp.zeros((REACH, B_DH), BF16)
    vpad[REACH:, 0:B_DH] = v_ref[...]
    vpad[:, B_DH:] = jnp.ones((SEQ + REACH, B_DH), BF16)

    def scores(t):
        r0 = t * ATT_TQ
        kb = kpad[r0:r0 + ATT_TK, :]
        s = lax.dot_general(qs[r0:r0 + ATT_TQ, :], kb, _NT, preferred_element_type=F32) + bias_ref[0]
        if r0 < REACH:
            j = lax.broadcasted_iota(jnp.int32, (ATT_TQ, ATT_TK), 1)
            s = jnp.where(j + r0 >= REACH, s, NEG)
        return s

    def finish(t, s):
        r0 = t * ATT_TQ
        p = jnp.exp2(s - jnp.max(s, axis=-1, keepdims=True)).astype(BF16)
        acc = jnp.dot(p, vpad[r0:r0 + ATT_TK, :], preferred_element_type=F32)
        o_ref[r0:r0 + ATT_TQ, :] = (acc[:, :B_DH] / acc[:, B_DH:]).astype(o_ref.dtype)

    n_tiles = SEQ // ATT_TQ
    s_next = scores(0)
    for t in range(n_tiles):
        s_cur = s_next
        if t + 1 < n_tiles:
            s_next = scores(t + 1)
        finish(t, s_cur)


def band_bias(rel_table, nq, nk):
    w = nq + nk
    rel = jnp.clip(REACH + nq - 1 - jnp.arange(w), -REL_CLIP, REL_CLIP) + REL_CLIP
    r = rel_table.astype(F32)[:, rel]
    heads = r.shape[0]
    skew = jnp.broadcast_to(r[:, None, :], (heads, nq, w)).reshape(heads, nq * w)
    skew = skew[:, :nq * (w - 1)].reshape(heads, nq, w - 1)
    bias = skew[:, :, nq - 1:nq - 1 + nk]
    i = jnp.arange(nq)[:, None]
    j = jnp.arange(nk)[None, :]
    dc = j // CHUNK - i // CHUNK
    allowed = (dc >= 0) & (dc <= PREV_CHUNKS)
    return jnp.where(allowed[None], bias * LOG2E, NEG)


def attn_prompt(qkv, rel_table):
    bias = band_bias(rel_table, ATT_TQ, ATT_TK)
    tail = pl.BlockSpec((1, REACH, B_DH), lambda b, h: (b, 0, h))
    return pl.pallas_call(
        _attn_prompt_body,
        grid=(BATCH, B_HEADS),
        in_specs=[pl.BlockSpec((SEQ, B_DH), lambda b, h: (b, h)),
                  pl.BlockSpec((SEQ, B_DH), lambda b, h: (b, B_HEADS + h)),
                  pl.BlockSpec((SEQ, B_DH), lambda b, h: (b, 2 * B_HEADS + h)),
                  pl.BlockSpec((1, ATT_TQ, ATT_TK), lambda b, h: (h, 0, 0))],
        out_specs=[pl.BlockSpec((SEQ, B_DH), lambda b, h: (b, h)), tail, tail],
        out_shape=[jax.ShapeDtypeStruct((N_PROMPT, D_MODEL), BF16),
                   jax.ShapeDtypeStruct((BATCH, REACH, D_MODEL), F32),
                   jax.ShapeDtypeStruct((BATCH, REACH, D_MODEL), F32)],
        scratch_shapes=[pltpu.VMEM((SEQ, B_DH), BF16),
                        pltpu.VMEM((SEQ + REACH, B_DH), BF16),
                        pltpu.VMEM((SEQ + REACH, 2 * B_DH), BF16)],
        compiler_params=_params("parallel", "parallel"),
        name="attn_prompt",
    )(qkv, qkv, qkv, bias)


def _attn_sample_body(q_ref, kn_ref, vn_ref, ck_ref, cv_ref, bias_c_ref, bias_n_ref, o_ref, ks_ref, vs_ref):
    ks_ref[0] = kn_ref[...].astype(F32)
    vs_ref[0] = vn_ref[...].astype(F32)
    for h in range(B_HEADS):
        sl = slice(h * B_DH, (h + 1) * B_DH)
        q = _scale_q(q_ref[:, sl])
        kc = ck_ref[pl.ds(h, REACH, stride=B_HEADS), :].astype(BF16)
        vc = cv_ref[pl.ds(h, REACH, stride=B_HEADS), :].astype(BF16)
        s_c = lax.dot_general(q, kc, _NT, preferred_element_type=F32) + bias_c_ref[h]
        s_n = lax.dot_general(q, kn_ref[:, sl], _NT, preferred_element_type=F32) + bias_n_ref[h]
        m = jnp.maximum(jnp.max(s_c, axis=-1, keepdims=True), jnp.max(s_n, axis=-1, keepdims=True))
        p_c = jnp.exp2(s_c - m)
        p_n = jnp.exp2(s_n - m)
        l = jnp.sum(p_c, axis=-1, keepdims=True) + jnp.sum(p_n, axis=-1, keepdims=True)
        dot = functools.partial(jnp.dot, preferred_element_type=F32)
        o_ref[:, sl] = ((dot(p_c.astype(BF16), vc) + dot(p_n.astype(BF16), vn_ref[:, sl])) / l).astype(o_ref.dtype)


def attn_sample(qkv, cache_k, cache_v, layer, rel_table):
    bias = band_bias(rel_table, DEC_SEQ, REACH + DEC_SEQ)
    cache_rows = REACH * B_HEADS
    ck = cache_k.reshape(cache_k.shape[0], DEC_BATCH, cache_rows, B_DH)
    cv = cache_v.reshape(cache_v.shape[0], DEC_BATCH, cache_rows, B_DH)
    row = lambda c: pl.BlockSpec((DEC_SEQ, D_MODEL), lambda b: (N_PROMPT // DEC_SEQ + b, c))
    cache = pl.BlockSpec((None, None, cache_rows, B_DH), lambda b: (layer, b, 0, 0))
    new = pl.BlockSpec((1, DEC_SEQ, D_MODEL), lambda b: (b, 0, 0))
    return pl.pallas_call(
        _attn_sample_body,
        grid=(DEC_BATCH,),
        in_specs=[row(0), row(1), row(2), cache, cache,
                  pl.BlockSpec((B_HEADS, DEC_SEQ, REACH), lambda b: (0, 0, 0)),
                  pl.BlockSpec((B_HEADS, DEC_SEQ, DEC_SEQ), lambda b: (0, 0, 0))],
        out_specs=[pl.BlockSpec((DEC_SEQ, D_MODEL), lambda b: (b, 0)), new, new],
        out_shape=[jax.ShapeDtypeStruct((N_SAMPLE, D_MODEL), BF16),
                   jax.ShapeDtypeStruct((DEC_BATCH, DEC_SEQ, D_MODEL), F32),
                   jax.ShapeDtypeStruct((DEC_BATCH, DEC_SEQ, D_MODEL), F32)],
        compiler_params=_params("parallel"),
        name="attn_sample",
    )(qkv, qkv, qkv, ck, cv, bias[:, :, :REACH], bias[:, :, REACH:])


def _ln(z, g, b):
    mu = jnp.mean(z, axis=-1, keepdims=True)
    zc = z - mu
    var = jnp.mean(zc * zc, axis=-1, keepdims=True)
    return zc * lax.rsqrt(var + LN_EPS) * g + b


PACK_ROWS = D_MODEL // 2 // LANES
HI_MASK = -65536


def _store_packed(ref, v, token0=0):
    t = v.shape[0]
    half = D_MODEL // 2
    hi = lax.bitcast_convert_type(v[:, :half].astype(BF16).astype(F32), jnp.int32)
    lo = lax.bitcast_convert_type(v[:, half:].astype(BF16).astype(F32), jnp.int32)
    words = (hi & HI_MASK) | lax.shift_right_logical(lo, 16)
    for a in range(PACK_ROWS):
        ref[pl.ds(token0 * PACK_ROWS + a, t, stride=PACK_ROWS), :] = words[:, a * LANES:(a + 1) * LANES]


def _load_packed(ref, t, token0=0):
    words = jnp.concatenate([ref[pl.ds(token0 * PACK_ROWS + a, t, stride=PACK_ROWS), :] for a in range(PACK_ROWS)],
                            axis=1)
    hi = lax.bitcast_convert_type(words & HI_MASK, F32)
    lo = lax.bitcast_convert_type(lax.shift_left(words, 16), F32)
    return jnp.concatenate([hi, lo], axis=1)


def _mix_ln_router_body(hp_ref, hs_ref, w_ref, xa_ref, xb_ref, g_ref, b_ref, rw_ref, rb_ref,
                        xp_ref, wt_ref, cls_ref, y_even, y_odd, *, prompt_tiles, n_tiles, x_split):
    i = pl.program_id(0)
    from_prompt = jnp.minimum(i, n_tiles - 1) < prompt_tiles

    @pl.when(i == 0)
    def _():
        y_odd[...] = jnp.zeros_like(y_odd)

    half = D_MODEL // 2

    def run(y_new, y_old):
        h = jnp.where(from_prompt, hp_ref[...], hs_ref[...])
        y_new[:, :half] = jnp.dot(h, w_ref[:, :half], preferred_element_type=F32)
        x = xa_ref[...]
        if x_split:
            x = jnp.where(jnp.maximum(i - 1, 0) < prompt_tiles, x, xb_ref[...])
        xn = _ln(ALPHA * x + y_old[...], g_ref[...], b_ref[...])
        logits = _dot_hilo(xn, rw_ref[...]) + rb_ref[...]
        y_new[:, half:] = jnp.dot(h, w_ref[:, half:], preferred_element_type=F32)
        _store_packed(xp_ref, xn)
        _route(logits, wt_ref, cls_ref)

    @pl.when(i % 2 == 0)
    def _():
        run(y_even, y_odd)

    @pl.when(i % 2 == 1)
    def _():
        run(y_odd, y_even)


def mix_ln_router(h_p, h_s, w, x, g, b, w_group, b_group, w_expert, b_expert, x_tail=None, tm=512):
    n = h_p.shape[0] + h_s.shape[0]
    n_tiles = n // tm
    prompt_tiles = h_p.shape[0] // tm
    sample_tiles = h_s.shape[0] // tm
    assert prompt_tiles * tm == h_p.shape[0] and (prompt_tiles + sample_tiles) * tm == n
    mm_tile = lambda i: jnp.minimum(i, n_tiles - 1)
    ep_tile = lambda i: jnp.maximum(i - 1, 0)
    hp_spec = pl.BlockSpec((tm, D_MODEL), lambda i: (jnp.minimum(mm_tile(i), prompt_tiles - 1), 0))
    hs_spec = pl.BlockSpec((tm, D_MODEL), lambda i: (jnp.maximum(mm_tile(i) - prompt_tiles, 0), 0))
    rw = jnp.pad(jnp.concatenate([w_group, w_expert], axis=1), ((0, 0), (0, LANES - N_GROUPS - N_EXPERTS)))
    rb = jnp.pad(jnp.concatenate([b_group, b_expert]).astype(F32), (0, LANES - N_GROUPS - N_EXPERTS))
    vec = pl.BlockSpec((1, D_MODEL), lambda i: (0, 0))
    lane_row = pl.BlockSpec((tm, LANES), lambda i: (ep_tile(i), 0))
    if x_tail is None:
        xa, xa_spec = x, pl.BlockSpec((tm, D_MODEL), lambda i: (ep_tile(i), 0))
        xb, xb_spec = x, pl.BlockSpec(memory_space=pl.ANY)
    else:
        assert x.shape[0] == h_p.shape[0] and x_tail.shape[0] == h_s.shape[0]
        xa, xa_spec = x, pl.BlockSpec((tm, D_MODEL), lambda i: (jnp.minimum(ep_tile(i), prompt_tiles - 1), 0))
        xb, xb_spec = x_tail, pl.BlockSpec((tm, D_MODEL), lambda i: (jnp.maximum(ep_tile(i) - prompt_tiles, 0), 0))
    return pl.pallas_call(
        functools.partial(_mix_ln_router_body, prompt_tiles=prompt_tiles, n_tiles=n_tiles,
                          x_split=x_tail is not None),
        grid=(n_tiles + 1,),
        in_specs=[hp_spec, hs_spec, pl.BlockSpec((D_MODEL, D_MODEL), lambda i: (0, 0)), xa_spec, xb_spec, vec, vec,
                  pl.BlockSpec((D_MODEL, LANES), lambda i: (0, 0)), pl.BlockSpec((1, LANES), lambda i: (0, 0))],
        out_specs=[pl.BlockSpec((tm * PACK_ROWS, LANES), lambda i: (ep_tile(i), 0)), lane_row, lane_row],
        out_shape=[jax.ShapeDtypeStruct((n * PACK_ROWS, LANES), jnp.int32),
                   jax.ShapeDtypeStruct((n, LANES), F32), jax.ShapeDtypeStruct((n, LANES), jnp.int32)],
        scratch_shapes=[pltpu.VMEM((tm, D_MODEL), F32), pltpu.VMEM((tm, D_MODEL), F32)],
        compiler_params=_params("arbitrary"),
        name="mix_ln_router",
    )(h_p, h_s, w, xa, xb, g[None, :], b[None, :], rw, rb[None, :])


GATE_COL_BLOCK = (2 * A_HEADS * A_DK + 2 * D_MODEL) // LANES


def _gate_spec(mlstm_layer):
    return pl.BlockSpec((None, D_MODEL, LANES), lambda i: (mlstm_layer, 0, GATE_COL_BLOCK))


def _gate_preacts(x, wg_ref):
    lane = lax.broadcasted_iota(jnp.int32, wg_ref.shape, 1)
    return _dot_hilo(x, jnp.where(lane < 2 * A_HEADS, wg_ref[...], 0.0))


LN_ROWS = 128


def _add_ln_body(x_ref, y_ref, g_ref, b_ref, *rest):
    if len(rest) == 2:
        (xo_ref, xbo_ref), wg_ref, go_ref = rest, None, None
    else:
        wg_ref, xo_ref, xbo_ref, go_ref = rest
    for r0 in range(0, xo_ref.shape[0], LN_ROWS):
        rows = pl.ds(r0, LN_ROWS)
        xn = _ln(ALPHA * _load_packed(x_ref, LN_ROWS, r0) + _load_packed(y_ref, LN_ROWS, r0), g_ref[...], b_ref[...])
        if go_ref is not None:
            go_ref[rows, :] = _gate_preacts(xn, wg_ref)
        xo_ref[rows, :] = xn
        xbo_ref[rows, :] = xn.astype(BF16)


def residual_ln(x_packed, y_packed, g, b, a_w_in=None, mlstm_layer=None, tm=512):
    n = x_packed.shape[0] // PACK_ROWS
    row = pl.BlockSpec((tm, D_MODEL), lambda i: (i, 0))
    vec = pl.BlockSpec((1, D_MODEL), lambda i: (0, 0))
    packed = pl.BlockSpec((tm * PACK_ROWS, LANES), lambda i: (i, 0))
    in_specs = [packed, packed, vec, vec]
    out_specs = [row, row]
    out_shape = [jax.ShapeDtypeStruct((n, D_MODEL), F32), jax.ShapeDtypeStruct((n, D_MODEL), BF16)]
    args = [x_packed, y_packed, g[None, :], b[None, :]]
    if a_w_in is not None:
        in_specs.append(_gate_spec(mlstm_layer))
        out_specs.append(pl.BlockSpec((tm, LANES), lambda i: (i, 0)))
        out_shape.append(jax.ShapeDtypeStruct((n, LANES), F32))
        args.append(a_w_in)
    return pl.pallas_call(
        _add_ln_body,
        grid=(n // tm,),
        in_specs=in_specs,
        out_specs=out_specs,
        out_shape=out_shape,
        compiler_params=_params("parallel"),
        name="residual_ln",
    )(*args)


def _embed_body(xp_ref, xs_ref, wg_ref, xbo_ref, go_ref, *, prompt_tiles):
    x = jnp.where(pl.program_id(0) < prompt_tiles, xp_ref[...], xs_ref[...])
    xbo_ref[...] = x.astype(BF16)
    go_ref[...] = _gate_preacts(x, wg_ref)


def embed(x_p, x_s, a_w_in, tm=512):
    n = x_p.shape[0] + x_s.shape[0]
    prompt_tiles = x_p.shape[0] // tm
    assert prompt_tiles * tm == x_p.shape[0] and n % tm == 0
    row = pl.BlockSpec((tm, D_MODEL), lambda i: (i, 0))
    return pl.pallas_call(
        functools.partial(_embed_body, prompt_tiles=prompt_tiles),
        grid=(n // tm,),
        in_specs=[pl.BlockSpec((tm, D_MODEL), lambda i: (jnp.minimum(i, prompt_tiles - 1), 0)),
                  pl.BlockSpec((tm, D_MODEL), lambda i: (jnp.maximum(i - prompt_tiles, 0), 0)),
                  _gate_spec(0)],
        out_specs=[row, pl.BlockSpec((tm, LANES), lambda i: (i, 0))],
        out_shape=[jax.ShapeDtypeStruct((n, D_MODEL), BF16), jax.ShapeDtypeStruct((n, LANES), F32)],
        compiler_params=_params("parallel"),
        name="embed",
    )(x_p, x_s, a_w_in)


def _add_ln_split_body(x_ref, y_ref, g_ref, b_ref, op_ref, os_ref, *, prompt_tiles):
    i = pl.program_id(0)

    def run(o_ref):
        for r0 in range(0, o_ref.shape[0], LN_ROWS):
            o_ref[pl.ds(r0, LN_ROWS), :] = _ln(
                ALPHA * _load_packed(x_ref, LN_ROWS, r0) + _load_packed(y_ref, LN_ROWS, r0), g_ref[...], b_ref[...])

    @pl.when(i < prompt_tiles)
    def _():
        run(op_ref)

    @pl.when(i >= prompt_tiles)
    def _():
        run(os_ref)


def residual_ln_split(x_packed, y_packed, g, b, n_prompt, tm=512):
    n = x_packed.shape[0] // PACK_ROWS
    prompt_tiles = n_prompt // tm
    assert prompt_tiles * tm == n_prompt and n % tm == 0
    vec = pl.BlockSpec((1, D_MODEL), lambda i: (0, 0))
    packed = pl.BlockSpec((tm * PACK_ROWS, LANES), lambda i: (i, 0))
    return pl.pallas_call(
        functools.partial(_add_ln_split_body, prompt_tiles=prompt_tiles),
        grid=(n // tm,),
        in_specs=[packed, packed, vec, vec],
        out_specs=[pl.BlockSpec((tm, D_MODEL), lambda i: (jnp.minimum(i, prompt_tiles - 1), 0)),
                   pl.BlockSpec((tm, D_MODEL), lambda i: (jnp.maximum(i - prompt_tiles, 0), 0))],
        out_shape=[jax.ShapeDtypeStruct((n_prompt, D_MODEL), F32),
                   jax.ShapeDtypeStruct((n - n_prompt, D_MODEL), F32)],
        compiler_params=_params("arbitrary"),
        name="residual_ln_split",
    )(x_packed, y_packed, g[None, :], b[None, :])


ROUTER_E0 = N_GROUPS


def _route(logits, wt_ref, cls_ref):
    lane = lax.broadcasted_iota(jnp.int32, logits.shape, 1).astype(F32)
    big = float(LANES)
    rmax = lambda a: jnp.max(a, axis=-1, keepdims=True)
    rmin = lambda a: jnp.min(a, axis=-1, keepdims=True)
    is_g = lane < N_GROUPS
    gl = jnp.where(is_g, logits, -jnp.inf)
    gmax = rmax(gl)
    gsel = rmin(jnp.where(gl == gmax, lane, big))
    gsum = jnp.sum(jnp.where(is_g, jnp.exp(logits - gmax), 0.0), axis=-1, keepdims=True)
    g_w = 1.0 / gsum
    e_lo = ROUTER_E0 + EXP_PER_GROUP * gsel
    in_grp = (lane >= e_lo) & (lane < e_lo + EXP_PER_GROUP)
    el = jnp.where(in_grp, logits, -jnp.inf)
    e1 = rmax(el)
    i1 = rmin(jnp.where(el == e1, lane, big))
    el2 = jnp.where(lane == i1, -jnp.inf, el)
    e2 = rmax(el2)
    i2 = rmin(jnp.where(el2 == e2, lane, big))
    t = jnp.exp(e2 - e1)
    p1 = 1.0 / (1.0 + t)
    w1 = p1 * g_w
    w2 = (t * p1) * g_w
    a1 = i1 - e_lo
    a2 = i2 - e_lo
    first_low = a1 < a2
    lo = jnp.where(first_low, a1, a2)
    hi = jnp.where(first_low, a2, a1)
    w_lo = jnp.where(first_low, w1, w2)
    w_hi = jnp.where(first_low, w2, w1)
    off = jnp.where(lo == 0.0, 0.0, jnp.where(lo == 1.0, 3.0, 5.0))
    cls = gsel * float(len(PAIRS)) + off + (hi - lo - 1.0)
    wt_ref[...] = jnp.where(lane < LANES // 2, w_lo, w_hi)
    cls_ref[...] = jnp.broadcast_to(cls, logits.shape).astype(jnp.int32)


def _expert_body(ea_ref, eb_ref, valid_ref, x_ref, wt_ref, wga_ref, wua_ref, wda_ref,
                 wgb_ref, wub_ref, wdb_ref, y_ref):
    i = pl.program_id(0)

    @pl.when(valid_ref[i] == 0)
    def _():
        y_ref[...] = jnp.zeros_like(y_ref)

    @pl.when(valid_ref[i] != 0)
    def _():
        x = _load_packed(x_ref, MOE_TILE).astype(BF16)
        dot = functools.partial(jnp.dot, preferred_element_type=F32)

        ga, ua = dot(x, wga_ref[...]), dot(x, wua_ref[...])
        gb, ub = dot(x, wgb_ref[...]), dot(x, wub_ref[...])
        ya = dot((jax.nn.silu(ga) * ua).astype(BF16), wda_ref[...])
        yb = dot((jax.nn.silu(gb) * ub).astype(BF16), wdb_ref[...])
        w_lo = wt_ref[:, 0:1]
        w_hi = wt_ref[:, LANES // 2:LANES // 2 + 1]
        _store_packed(y_ref, w_lo * ya + w_hi * yb)


def experts(xs, wts, tile_ea, tile_eb, tile_valid, w_gate, w_up, w_down):
    wspec = lambda shape, which: pl.BlockSpec(
        (None,) + shape, lambda i, ea, eb, va: ((ea, eb)[which][i], 0, 0))
    up = (D_MODEL, D_EXPERT)
    down = (D_EXPERT, D_MODEL)
    packed = pl.BlockSpec((MOE_TILE * PACK_ROWS, LANES), lambda i, ea, eb, va: (i, 0))
    grid_spec = pltpu.PrefetchScalarGridSpec(
        num_scalar_prefetch=3,
        grid=(MOE_TILES,),
        in_specs=[packed,
                  pl.BlockSpec((MOE_TILE, LANES), lambda i, ea, eb, va: (i, 0)),
                  wspec(up, 0), wspec(up, 0), wspec(down, 0),
                  wspec(up, 1), wspec(up, 1), wspec(down, 1)],
        out_specs=packed,
    )
    return pl.pallas_call(
        _expert_body,
        grid_spec=grid_spec,
        out_shape=jax.ShapeDtypeStruct((MOE_ROWS * PACK_ROWS, LANES), jnp.int32),
        compiler_params=_params("arbitrary"),
        name="experts",
    )(tile_ea, tile_eb, tile_valid, xs, wts, w_gate, w_up, w_down, w_gate, w_up, w_down)


def sc_dispatch(xp, wts, pos, fill_rows):
    n = pos.shape[0]
    n_fill = fill_rows.shape[0]
    info = plsc.get_sparse_core_info()
    n_workers = info.num_cores * info.num_subcores
    assert n_workers == SC_WORKERS
    per_worker = n // n_workers
    steps = per_worker // SC_TOKENS
    fill_steps = n_fill // n_workers // SC_FILL
    assert steps * SC_TOKENS * n_workers == n and steps % 2 == 0
    assert fill_steps * SC_FILL * n_workers == n_fill
    mesh = plsc.VectorSubcoreMesh(core_axis_name="core", subcore_axis_name="subcore")
    dma = pltpu.SemaphoreType.DMA

    @functools.partial(
        pl.kernel,
        out_type=[jax.ShapeDtypeStruct((n + n_fill, PACK_ROWS, LANES), xp.dtype),
                  jax.ShapeDtypeStruct((n + n_fill, LANES), wts.dtype)],
        mesh=mesh,
        scratch_types=[pltpu.VMEM((steps, SC_TOKENS), jnp.int32),
                       pltpu.VMEM((fill_steps, SC_FILL), jnp.int32),
                       pltpu.VMEM((2, SC_TOKENS, PACK_ROWS, LANES), xp.dtype),
                       pltpu.VMEM((2, SC_TOKENS, LANES), wts.dtype),
                       dma((2,)), dma((2,)), dma((2,)), dma((2,))],
    )
    def scatter(x_hbm, w_hbm, pos_hbm, fill_hbm, zx_hbm, zw_hbm, xs_hbm, ws_hbm,
                pos_v, fill_v, xbuf, wbuf, read_x, read_w, write_x, write_w):
        worker = lax.axis_index("subcore") * info.num_cores + lax.axis_index("core")
        base = worker * per_worker
        pltpu.sync_copy(pos_hbm.at[worker], pos_v)
        pltpu.sync_copy(fill_hbm.at[worker], fill_v)

        def reads(s, b):
            rows = pl.ds(base + s * SC_TOKENS, SC_TOKENS)
            return (pltpu.make_async_copy(x_hbm.at[rows], xbuf.at[b], read_x.at[b]),
                    pltpu.make_async_copy(w_hbm.at[rows], wbuf.at[b], read_w.at[b]))

        def writes(s, b):
            return (pltpu.make_async_copy(xbuf.at[b], xs_hbm.at[pos_v.at[s]], write_x.at[b]),
                    pltpu.make_async_copy(wbuf.at[b], ws_hbm.at[pos_v.at[s]], write_w.at[b]))

        def start(copies):
            for c in copies:
                c.start()

        def wait(copies):
            for c in copies:
                c.wait()

        start(reads(0, 0))

        @pl.loop(0, steps, step=2)
        def _(s):
            wait(reads(s, 0))
            start(reads(s + 1, 1))
            start(writes(s, 0))
            wait(reads(s + 1, 1))
            wait(writes(s, 0))

            @pl.when(s + 2 < steps)
            def _():
                start(reads(s + 2, 0))

            start(writes(s + 1, 1))
            wait(writes(s + 1, 1))

        zx = xbuf.at[0, pl.ds(0, SC_FILL)]
        zw = wbuf.at[0, pl.ds(0, SC_FILL)]
        pltpu.sync_copy(zx_hbm, zx)
        pltpu.sync_copy(zw_hbm, zw)
        fills = [(pltpu.make_async_copy(zx, xs_hbm.at[fill_v.at[s]], write_x.at[0]),
                  pltpu.make_async_copy(zw, ws_hbm.at[fill_v.at[s]], write_w.at[0])) for s in range(fill_steps)]
        for f in fills:
            start(f)
        for f in fills:
            wait(f)

    xs, ws = scatter(xp.reshape(n, PACK_ROWS, LANES), wts,
                     pos.reshape(n_workers, steps, SC_TOKENS), fill_rows.reshape(n_workers, fill_steps, SC_FILL),
                     jnp.zeros((SC_FILL, PACK_ROWS, LANES), xp.dtype), jnp.zeros((SC_FILL, LANES), wts.dtype))
    return xs.reshape((n + n_fill) * PACK_ROWS, LANES), ws


def sc_gather_tokens(x, idx):
    n = x.shape[0] // PACK_ROWS
    p = idx.shape[0]
    info = plsc.get_sparse_core_info()
    n_workers = info.num_cores * info.num_subcores
    assert n_workers == SC_WORKERS
    per_worker = p // n_workers
    steps = per_worker // SC_TOKENS
    assert steps * SC_TOKENS * n_workers == p and steps % 2 == 0
    mesh = plsc.VectorSubcoreMesh(core_axis_name="core", subcore_axis_name="subcore")
    dma = pltpu.SemaphoreType.DMA

    @functools.partial(
        pl.kernel,
        out_type=jax.ShapeDtypeStruct((p, PACK_ROWS, LANES), x.dtype),
        mesh=mesh,
        scratch_types=[pltpu.VMEM((steps, SC_TOKENS), jnp.int32),
                       pltpu.VMEM((2, SC_TOKENS, PACK_ROWS, LANES), x.dtype),
                       dma((2,)), dma((2,))],
    )
    def gather(x_hbm, i_hbm, o_hbm, idx_v, buf, read_sem, write_sem):
        worker = lax.axis_index("subcore") * info.num_cores + lax.axis_index("core")
        base = worker * per_worker
        pltpu.sync_copy(i_hbm.at[worker], idx_v)

        def read(s, b):
            return pltpu.make_async_copy(x_hbm.at[idx_v.at[s]], buf.at[b], read_sem.at[b])

        def write(s, b):
            return pltpu.make_async_copy(buf.at[b], o_hbm.at[pl.ds(base + s * SC_TOKENS, SC_TOKENS)], write_sem.at[b])

        read(0, 0).start()

        @pl.loop(0, steps, step=2)
        def _(s):
            read(s, 0).wait()
            read(s + 1, 1).start()
            write(s, 0).start()
            read(s + 1, 1).wait()
            write(s, 0).wait()

            @pl.when(s + 2 < steps)
            def _():
                read(s + 2, 0).start()

            write(s + 1, 1).start()
            write(s + 1, 1).wait()

    out = gather(x.reshape(n, PACK_ROWS, LANES), idx.reshape(n_workers, steps, SC_TOKENS))
    return out.reshape(p * PACK_ROWS, LANES)


_CLASS_EA = [g * EXP_PER_GROUP + lo for g in range(N_GROUPS) for lo, hi in PAIRS]
_CLASS_EB = [g * EXP_PER_GROUP + hi for g in range(N_GROUPS) for lo, hi in PAIRS]


def dispatch_plan(cls):
    onehot = (cls[:, None] == jnp.arange(N_CLASSES, dtype=jnp.int32)[None, :]).astype(jnp.int32)
    csum = jnp.cumsum(onehot, axis=0)
    rank = jnp.sum(csum * onehot, axis=1) - 1
    counts = csum[-1]
    tiles_per = (counts + MOE_TILE - 1) // MOE_TILE
    tile_end = jnp.cumsum(tiles_per)
    tile_start = tile_end - tiles_per
    pos = jnp.sum(onehot * tile_start[None, :], axis=1) * MOE_TILE + rank
    n = cls.shape[0]
    pad = tiles_per * MOE_TILE - counts
    pad_end = jnp.cumsum(pad)
    k = jnp.arange(MOE_ROWS - n, dtype=jnp.int32)
    k_cls = jnp.sum((k[:, None] >= pad_end[None, :]).astype(jnp.int32), axis=1)
    k_hot = (k_cls[:, None] == jnp.arange(N_CLASSES, dtype=jnp.int32)[None, :]).astype(jnp.int32)
    pad_row0 = tile_start * MOE_TILE + counts - (pad_end - pad)
    fill_rows = k + jnp.where(k_cls < N_CLASSES, jnp.sum(k_hot * pad_row0[None, :], axis=1),
                              tile_end[-1] * MOE_TILE - pad_end[-1])
    t = jnp.arange(MOE_TILES, dtype=jnp.int32)
    tile_cls = jnp.sum((t[:, None] >= tile_end[None, :]).astype(jnp.int32), axis=1)
    valid = (tile_cls < N_CLASSES).astype(jnp.int32)
    last_cls = jnp.max(jnp.where(counts > 0, jnp.arange(N_CLASSES, dtype=jnp.int32), 0))
    tile_cls = jnp.where(valid == 1, tile_cls, last_cls)
    ea = jnp.asarray(_CLASS_EA, jnp.int32)[tile_cls]
    eb = jnp.asarray(_CLASS_EB, jnp.int32)[tile_cls]
    return pos, fill_rows, ea, eb, valid


def moe(xp, wts, cls, e_w_gate, e_w_up, w_down, e_w_down, layer):
    pos, fill_rows, ea, eb, valid = dispatch_plan(cls[:, 0])
    xs, ws = sc_dispatch(xp, wts, pos, fill_rows)
    w_gate, w_up = cast_layer(e_w_gate, layer, after=pos), cast_layer(e_w_up, layer, after=pos)
    ys = experts(xs, ws, ea, eb, valid, w_gate, w_up, w_down)
    y = sc_gather_tokens(ys, pos)
    if layer + 1 == DEPTH:
        return y, None
    return lax.optimization_barrier((y, cast_layer(e_w_down, layer + 1, after=ys)))


def _rows(a):
    return a.reshape(-1, a.shape[-1])


def kernel(x_prompt, x_sample, state_C, state_n, state_m, cache_k, cache_v, a_w_in, a_b_gate, a_norm, a_w_out, b_w_in, b_rel, b_w_out, ln1_g, ln1_b, ln2_g, ln2_b, r_w_group, r_b_group, r_w_expert, r_b_expert, e_w_gate, e_w_up, e_w_down):
    hk = A_HEADS * A_DK
    n_main = 2 * hk + 2 * D_MODEL
    assert n_main == GATE_COL_BLOCK * LANES and a_w_in.shape[-1] == n_main + 2 * A_HEADS
    x, x_tail = _rows(x_prompt), _rows(x_sample)
    xb, g = embed(x, x_tail, a_w_in)
    w_down = cast_layer(e_w_down, 0)
    outs = {k: [] for k in ("Cp", "np", "mp", "kp", "vp", "Cs", "ns", "ms", "ks", "vs")}
    for layer in range(DEPTH):
        j = layer // 2
        if layer % 2 == 0:
            proj = matmul(xb, a_w_in, j, n_main, BF16)
            zc = jnp.zeros((BATCH, A_HEADS, A_DK, A_DV), F32)
            zn = jnp.zeros((BATCH, A_HEADS, A_DK), F32)
            zm = jnp.zeros((BATCH, A_HEADS), F32)
            h_p, c_p, n_p, m_p = mlstm(proj, g, a_b_gate[j], a_norm[j], zc, zn, zm,
                                       batch=BATCH, seq=SEQ, L=MLSTM_CHUNK, row0=0)
            h_s, c_s, n_s, m_s = mlstm(proj, g, a_b_gate[j], a_norm[j],
                                       state_C[j], state_n[j], state_m[j],
                                       batch=DEC_BATCH, seq=DEC_SEQ, L=DEC_SEQ, row0=N_PROMPT)
            outs["Cp"].append(c_p); outs["np"].append(n_p); outs["mp"].append(m_p)
            outs["Cs"].append(c_s); outs["ns"].append(n_s); outs["ms"].append(m_s)
            w_out = a_w_out[j]
        else:
            qkv = matmul(xb, b_w_in, j, 3 * D_MODEL, BF16)
            h_p, k_p, v_p = attn_prompt(qkv, b_rel[j])
            h_s, k_s, v_s = attn_sample(qkv, cache_k, cache_v, j, b_rel[j])
            heads = lambda a: a.reshape(a.shape[:2] + (B_HEADS, B_DH))
            outs["kp"].append(heads(k_p)); outs["vp"].append(heads(v_p))
            outs["ks"].append(heads(k_s)); outs["vs"].append(heads(v_s))
            w_out = b_w_out[j]
        xp, wts, cls = mix_ln_router(h_p, h_s, w_out.astype(BF16), x, ln1_g[layer], ln1_b[layer],
                                     r_w_group[layer], r_b_group[layer], r_w_expert[layer], r_b_expert[layer],
                                     x_tail=x_tail if layer == 0 else None)
        y, w_down = moe(xp, wts, cls, e_w_gate, e_w_up, w_down, e_w_down, layer)
        if layer < DEPTH - 1:
            if (layer + 1) % 2 == 0:
                x, xb, g = residual_ln(xp, y, ln2_g[layer], ln2_b[layer], a_w_in, (layer + 1) // 2)
            else:
                x, xb = residual_ln(xp, y, ln2_g[layer], ln2_b[layer])
        else:
            y_p, y_s = residual_ln_split(xp, y, ln2_g[layer], ln2_b[layer], N_PROMPT)
    st = lambda k: jnp.stack(outs[k])
    return (y_p.reshape(BATCH, SEQ, D_MODEL), y_s.reshape(DEC_BATCH, DEC_SEQ, D_MODEL),
            st("Cp"), st("np"), st("mp"), st("kp"), st("vp"),
            st("Cs"), st("ns"), st("ms"), st("ks"), st("vs"))
```
